```python
import math
import jax, jax.numpy as jnp
from jax import lax
import numpy as np

D_MODEL = 1024
BATCH = 2
SEQ = 8192
DEPTH = 4
DEC_BATCH = 128
DEC_SEQ = 8
PAST_LEN = 8192
PAGE_SIZE = 128

N_MIXERS = 3
N_A = (DEPTH + 2) // 3
N_B = (DEPTH + 1) // 3
N_C = DEPTH // 3
NORM_EPS = 1e-6

A_HEADS = 4
A_DK = D_MODEL // 2 // A_HEADS
A_DV = D_MODEL // A_HEADS
A_CHUNK = 64
A_Q_END = A_HEADS * A_DK
A_K_END = 2 * A_Q_END
A_V_END = A_K_END + A_HEADS * A_DV
A_O_END = A_V_END + A_HEADS * A_DV
A_I_END = A_O_END + A_HEADS
A_IN = A_I_END + A_HEADS

B_HEADS = 16
B_KV_HEADS = 4
B_HD = D_MODEL // B_HEADS
B_GROUP = B_HEADS // B_KV_HEADS
B_IN = (B_HEADS + 2 * B_KV_HEADS) * B_HD
WINDOW = 128
ROPE_THETA = 500000.0
ROPE_DIM = B_HD // 4

C_WIDTH = D_MODEL
C_BLOCKS = 4
C_BW = C_WIDTH // C_BLOCKS
CONV_W = 4
LRU_C = 8.0

D_FF = 4 * D_MODEL

kernel_name = 'hybrid_mlstm_swa_rglru_step'


def rmsnorm(x, g):
    xf = x.astype(jnp.float32)
    y = xf * lax.rsqrt(jnp.mean(xf * xf, axis=-1, keepdims=True) + NORM_EPS)
    return (y * g.astype(jnp.float32)).astype(x.dtype)


def sqrelu_mlp(x, w_up, w_down):
    h = jax.nn.relu(x @ w_up)
    return (h * h) @ w_down


def mlstm_scan(q, k, v, ig, lf, c0, n0, m0):
    bsz, s = q.shape[0], q.shape[1]
    L = math.gcd(s, A_CHUNK)
    nc = s // L

    def to_chunks(t):
        return jnp.moveaxis(t.reshape((bsz, nc, L) + t.shape[2:]), 1, 0)

    causal = jnp.tril(jnp.ones((L, L), dtype=bool))

    def step(carry, inp):
        c, n, m = carry
        qb, kb, vb, ib, fb = inp
        bt = jnp.swapaxes(jnp.cumsum(fb, axis=1), 1, 2)
        it = jnp.swapaxes(ib, 1, 2)
        dmat = bt[..., :, None] - bt[..., None, :] + it[..., None, :]
        dmat = jnp.where(causal, dmat, -jnp.inf)
        inter = bt + m[..., None]
        m_t = jnp.maximum(jnp.max(dmat, axis=-1), inter)
        w = jnp.exp(dmat - m_t[..., None])
        w_inter = jnp.swapaxes(jnp.exp(inter - m_t), 1, 2)
        s_qk = jnp.einsum('bthd,bshd->bhts', qb, kb) * w
        num = jnp.einsum('bhts,bshv->bthv', s_qk, vb) + w_inter[..., None] * jnp.einsum('bhvk,bthk->bthv', c, qb)
        den = jnp.swapaxes(jnp.sum(s_qk, axis=-1), 1, 2) + w_inter * jnp.einsum('bhk,bthk->bth', n, qb)
        h = num / jnp.maximum(jnp.abs(den), jnp.exp(-jnp.swapaxes(m_t, 1, 2)))[..., None]
        m_new = m_t[..., -1]
        decay_src = jnp.exp(bt[..., -1:] - bt + it - m_new[..., None])
        carry_scale = jnp.exp(inter[..., -1] - m_new)
        c_new = carry_scale[..., None, None] * c + jnp.einsum('bhs,bshv,bshk->bhvk', decay_src, vb, kb)
        n_new = carry_scale[..., None] * n + jnp.einsum('bhs,bshk->bhk', decay_src, kb)
        return (c_new, n_new, m_new), h

    (c, n, m), hs = lax.scan(step, (c0, n0, m0), tuple(map(to_chunks, (q, k, v, ig, lf))))
    hs = jnp.moveaxis(hs, 0, 1).reshape(bsz, s, A_HEADS, A_DV)
    return hs, c, n, m


def mlstm_layer(x, c0, n0, m0, w_in, b_i, b_f, g_head, w_out):
    bsz, s, _ = x.shape
    f32 = jnp.float32
    proj = x @ w_in
    q = proj[..., :A_Q_END].reshape(bsz, s, A_HEADS, A_DK).astype(f32)
    k = proj[..., A_Q_END:A_K_END].reshape(bsz, s, A_HEADS, A_DK).astype(f32) * (A_DK ** -0.5)
    v = proj[..., A_K_END:A_V_END].reshape(bsz, s, A_HEADS, A_DV).astype(f32)
    o = jax.nn.sigmoid(proj[..., A_V_END:A_O_END].astype(f32))
    ig = (proj[..., A_O_END:A_I_END] + b_i).astype(f32)
    lf = jax.nn.log_sigmoid((proj[..., A_I_END:A_IN] + b_f).astype(f32))
    h, c, n, m = mlstm_scan(q, k, v, ig, lf, c0.astype(f32), n0.astype(f32), m0.astype(f32))
    h = h * lax.rsqrt(jnp.mean(h * h, axis=-1, keepdims=True) + NORM_EPS) * g_head.astype(f32)
    h = o * h.reshape(bsz, s, A_HEADS * A_DV)
    return h.astype(x.dtype) @ w_out, c, n, m


def rope_partial(x, pos):
    inv = ROPE_THETA ** (-jnp.arange(0, ROPE_DIM, 2, dtype=jnp.float32) / ROPE_DIM)
    ang = pos.astype(jnp.float32)[:, None] * inv[None, :]
    cos = jnp.cos(ang)[None, :, None, :]
    sin = jnp.sin(ang)[None, :, None, :]
    xr = x[..., :ROPE_DIM].astype(jnp.float32)
    x1, x2 = xr[..., :ROPE_DIM // 2], xr[..., ROPE_DIM // 2:]
    rot = jnp.concatenate([x1 * cos - x2 * sin, x2 * cos + x1 * sin], axis=-1)
    return jnp.concatenate([rot.astype(x.dtype), x[..., ROPE_DIM:]], axis=-1)


def swa_qkv(x, pos, w_qkv, b_qkv):
    bsz, s, _ = x.shape
    proj = x @ w_qkv + b_qkv
    q = rope_partial(proj[..., :B_HEADS * B_HD].reshape(bsz, s, B_HEADS, B_HD), pos)
    k = rope_partial(proj[..., B_HEADS * B_HD:(B_HEADS + B_KV_HEADS) * B_HD].reshape(bsz, s, B_KV_HEADS, B_HD), pos)
    v = proj[..., (B_HEADS + B_KV_HEADS) * B_HD:].reshape(bsz, s, B_KV_HEADS, B_HD)
    return q, k, v


def sink_attention(q, k, v, q_pos, k_pos, sinks):
    scores = jnp.einsum('bnqhgd,bnshd->bnhgqs', q, k).astype(jnp.float32) * (B_HD ** -0.5)
    diff = q_pos[:, :, None] - k_pos[:, None, :]
    allowed = (diff >= 0) & (diff < WINDOW) & (k_pos[:, None, :] >= 0)
    scores = jnp.where(allowed[None, :, None, None], scores, -jnp.inf)
    sink = jnp.broadcast_to(sinks.astype(jnp.float32).reshape(B_KV_HEADS, B_GROUP)[None, None, :, :, None, None],
                            scores.shape[:-1] + (1,))
    p = jax.nn.softmax(jnp.concatenate([scores, sink], axis=-1), axis=-1)[..., :-1]
    return jnp.einsum('bnhgqs,bnshd->bnqhgd', p.astype(v.dtype), v)


def swa_prompt(x, w_qkv, b_qkv, sinks, w_out, b_out):
    bsz, s, _ = x.shape
    pos = jnp.arange(s, dtype=jnp.int32)
    q, k, v = swa_qkv(x, pos, w_qkv, b_qkv)
    nb = s // WINDOW
    qb = q.reshape(bsz, nb, WINDOW, B_KV_HEADS, B_GROUP, B_HD)
    pad = jnp.zeros((bsz, WINDOW, B_KV_HEADS, B_HD), k.dtype)
    kp = jnp.concatenate([pad, k], axis=1).reshape(bsz, nb + 1, WINDOW, B_KV_HEADS, B_HD)
    vp = jnp.concatenate([pad, v], axis=1).reshape(bsz, nb + 1, WINDOW, B_KV_HEADS, B_HD)
    kb = jnp.concatenate([kp[:, :-1], kp[:, 1:]], axis=2)
    vb = jnp.concatenate([vp[:, :-1], vp[:, 1:]], axis=2)
    q_pos = pos.reshape(nb, WINDOW)
    k_pos = (jnp.arange(nb, dtype=jnp.int32)[:, None] - 1) * WINDOW + jnp.arange(2 * WINDOW, dtype=jnp.int32)[None, :]
    o = sink_attention(qb, kb, vb, q_pos, k_pos, sinks).reshape(bsz, s, B_HEADS * B_HD)
    buf = min(WINDOW, s)
    return o @ w_out + b_out, k[:, s - buf:], v[:, s - buf:]


def swa_sample(x, k_cache, v_cache, w_qkv, b_qkv, sinks, w_out, b_out):
    bsz, s, _ = x.shape
    pos = PAST_LEN + jnp.arange(s, dtype=jnp.int32)
    q, k, v = swa_qkv(x, pos, w_qkv, b_qkv)
    buf = k_cache.shape[1]
    kall = jnp.concatenate([k_cache.astype(k.dtype), k], axis=1)
    vall = jnp.concatenate([v_cache.astype(v.dtype), v], axis=1)
    k_pos = jnp.concatenate([PAST_LEN - buf + jnp.arange(buf, dtype=jnp.int32), pos])
    qb = q.reshape(bsz, 1, s, B_KV_HEADS, B_GROUP, B_HD)
    o = sink_attention(qb, kall[:, None], vall[:, None], pos[None], k_pos[None], sinks).reshape(bsz, s, B_HEADS * B_HD)
    return o @ w_out + b_out, kall[:, -buf:], vall[:, -buf:]


def lru_combine(left, right):
    a1, b1 = left
    a2, b2 = right
    return a1 * a2, a2 * b1 + b2


def rglru_layer(x, h0, conv0, w_in, w_conv, b_conv, w_a, b_a, w_x, b_x, lam, w_out):
    bsz, s, _ = x.shape
    f32 = jnp.float32
    proj = x @ w_in
    xb, gate = proj[..., :C_WIDTH], proj[..., C_WIDTH:]
    xp = jnp.concatenate([conv0.astype(xb.dtype), xb], axis=1)
    u = b_conv
    for j in range(CONV_W):
        u = u + w_conv[j] * xp[:, j:j + s]
    new_conv = xp[:, s:]
    ub = u.reshape(bsz, s, C_BLOCKS, C_BW)
    r = jax.nn.sigmoid((jnp.einsum('bsnc,ncd->bsnd', ub, w_a).reshape(bsz, s, C_WIDTH) + b_a).astype(f32))
    gi = jax.nn.sigmoid((jnp.einsum('bsnc,ncd->bsnd', ub, w_x).reshape(bsz, s, C_WIDTH) + b_x).astype(f32))
    log_a = -LRU_C * r * jax.nn.softplus(-lam.astype(f32))
    a = jnp.exp(log_a)
    bterm = jnp.sqrt(-jnp.expm1(2.0 * log_a)) * gi * u.astype(f32)
    acum, bcum = lax.associative_scan(lru_combine, (a, bterm), axis=1)
    h = acum * h0.astype(f32)[:, None] + bcum
    y = (h.astype(x.dtype) * jax.nn.gelu(gate)) @ w_out
    return y, h[:, -1], new_conv


def trunk(x, prompt, c_in, n_in, m_in, k_in, v_in, h_in, conv_in, p):
    bsz = x.shape[0]
    f32 = jnp.float32
    out_c, out_n, out_m, out_k, out_v, out_h, out_conv = [], [], [], [], [], [], []
    for i in range(DEPTH):
        kind = i % N_MIXERS
        j = i // N_MIXERS
        hn = rmsnorm(x, p['norm_mix'][i])
        if kind == 0:
            if prompt:
                c0 = jnp.zeros((bsz, A_HEADS, A_DV, A_DK), f32)
                n0 = jnp.zeros((bsz, A_HEADS, A_DK), f32)
                m0 = jnp.zeros((bsz, A_HEADS), f32)
            else:
                c0, n0, m0 = c_in[j], n_in[j], m_in[j]
            y, c, n, m = mlstm_layer(hn, c0, n0, m0, p['w_mlstm_in'][j], p['b_mlstm_i'][j], p['b_mlstm_f'][j],
                                     p['g_mlstm_head'][j], p['w_mlstm_out'][j])
            out_c.append(c); out_n.append(n); out_m.append(m)
        elif kind == 1:
            if prompt:
                y, kb, vb = swa_prompt(hn, p['w_swa_qkv'][j], p['b_swa_qkv'][j], p['swa_sinks'][j],
                                       p['w_swa_out'][j], p['b_swa_out'][j])
            else:
                y, kb, vb = swa_sample(hn, k_in[j], v_in[j], p['w_swa_qkv'][j], p['b_swa_qkv'][j],
                                       p['swa_sinks'][j], p['w_swa_out'][j], p['b_swa_out'][j])
            out_k.append(kb); out_v.append(vb)
        else:
            if prompt:
                h0 = jnp.zeros((bsz, C_WIDTH), f32)
                conv0 = jnp.zeros((bsz, CONV_W - 1, C_WIDTH), x.dtype)
            else:
                h0, conv0 = h_in[j], conv_in[j]
            y, h, cv = rglru_layer(hn, h0, conv0, p['w_rg_in'][j], p['w_rg_conv'][j], p['b_rg_conv'][j],
                                   p['w_rg_a'][j], p['b_rg_a'][j], p['w_rg_x'][j], p['b_rg_x'][j],
                                   p['rg_lambda'][j], p['w_rg_out'][j])
            out_h.append(h); out_conv.append(cv)
        x = x + y
        x = x + sqrelu_mlp(rmsnorm(x, p['norm_mlp'][i]), p['w_mlp_up'][i], p['w_mlp_down'][i])
    return (rmsnorm(x, p['norm_final']), jnp.stack(out_c), jnp.stack(out_n), jnp.stack(out_m),
            jnp.stack(out_k), jnp.stack(out_v), jnp.stack(out_h), jnp.stack(out_conv))


def setup_inputs(seed: int = 0) -> dict:
    key = jax.random.key(seed)
    ks = iter(jax.random.split(key, 48))
    f32 = jnp.float32

    def nrm(shape, scale):
        return scale * jax.random.normal(next(ks), shape, f32)

    swa_buf = min(WINDOW, PAST_LEN)
    a0 = jax.random.uniform(next(ks), (N_C, C_WIDTH), f32, 0.9, 0.999)
    return {
        'x_prompt': nrm((BATCH, SEQ, D_MODEL), 1.0),
        'x_sample': nrm((DEC_BATCH, DEC_SEQ, D_MODEL), 1.0),
        'state_mlstm_c': nrm((N_A, DEC_BATCH, A_HEADS, A_DV, A_DK), 0.1),
        'state_mlstm_n': nrm((N_A, DEC_BATCH, A_HEADS, A_DK), 0.1),
        'state_mlstm_m': nrm((N_A, DEC_BATCH, A_HEADS), 1.0),
        'cache_swa_k': nrm((N_B, DEC_BATCH, swa_buf, B_KV_HEADS, B_HD), 1.0),
        'cache_swa_v': nrm((N_B, DEC_BATCH, swa_buf, B_KV_HEADS, B_HD), 1.0),
        'state_rglru_h': nrm((N_C, DEC_BATCH, C_WIDTH), 0.5),
        'state_rglru_conv': nrm((N_C, DEC_BATCH, CONV_W - 1, C_WIDTH), 1.0),
        'norm_mix': 1.0 + nrm((DEPTH, D_MODEL), 0.02),
        'norm_mlp': 1.0 + nrm((DEPTH, D_MODEL), 0.02),
        'norm_final': 1.0 + nrm((D_MODEL,), 0.02),
        'w_mlp_up': nrm((DEPTH, D_MODEL, D_FF), D_MODEL ** -0.5),
        'w_mlp_down': nrm((DEPTH, D_FF, D_MODEL), D_FF ** -0.5),
        'w_mlstm_in': nrm((N_A, D_MODEL, A_IN), D_MODEL ** -0.5),
        'b_mlstm_i': nrm((N_A, A_HEADS), 0.1),
        'b_mlstm_f': 3.0 + nrm((N_A, A_HEADS), 0.1),
        'g_mlstm_head': 1.0 + nrm((N_A, A_HEADS, A_DV), 0.02),
        'w_mlstm_out': nrm((N_A, A_HEADS * A_DV, D_MODEL), (A_HEADS * A_DV) ** -0.5),
        'w_swa_qkv': nrm((N_B, D_MODEL, B_IN), D_MODEL ** -0.5),
        'b_swa_qkv': nrm((N_B, B_IN), 0.02),
        'swa_sinks': nrm((N_B, B_HEADS), 0.5),
        'w_swa_out': nrm((N_B, B_HEADS * B_HD, D_MODEL), (B_HEADS * B_HD) ** -0.5),
        'b_swa_out': nrm((N_B, D_MODEL), 0.02),
        'w_rg_in': nrm((N_C, D_MODEL, 2 * C_WIDTH), D_MODEL ** -0.5),
        'w_rg_conv': nrm((N_C, CONV_W, C_WIDTH), CONV_W ** -0.5),
        'b_rg_conv': nrm((N_C, C_WIDTH), 0.02),
        'w_rg_a': nrm((N_C, C_BLOCKS, C_BW, C_BW), C_BW ** -0.5),
        'b_rg_a': nrm((N_C, C_WIDTH), 0.02),
        'w_rg_x': nrm((N_C, C_BLOCKS, C_BW, C_BW), C_BW ** -0.5),
        'b_rg_x': nrm((N_C, C_WIDTH), 0.02),
        'rg_lambda': jnp.log(a0) - jnp.log1p(-a0),
        'w_rg_out': nrm((N_C, C_WIDTH, D_MODEL), C_WIDTH ** -0.5),
    }


def reference(x_prompt, x_sample, state_mlstm_c, state_mlstm_n, state_mlstm_m, cache_swa_k, cache_swa_v,
              state_rglru_h, state_rglru_conv, norm_mix, norm_mlp, norm_final, w_mlp_up, w_mlp_down,
              w_mlstm_in, b_mlstm_i, b_mlstm_f, g_mlstm_head, w_mlstm_out, w_swa_qkv, b_swa_qkv, swa_sinks,
              w_swa_out, b_swa_out, w_rg_in, w_rg_conv, b_rg_conv, w_rg_a, b_rg_a, w_rg_x, b_rg_x, rg_lambda,
              w_rg_out):
    p = {'norm_mix': norm_mix, 'norm_mlp': norm_mlp, 'norm_final': norm_final,
         'w_mlp_up': w_mlp_up, 'w_mlp_down': w_mlp_down,
         'w_mlstm_in': w_mlstm_in, 'b_mlstm_i': b_mlstm_i, 'b_mlstm_f': b_mlstm_f,
         'g_mlstm_head': g_mlstm_head, 'w_mlstm_out': w_mlstm_out,
         'w_swa_qkv': w_swa_qkv, 'b_swa_qkv': b_swa_qkv, 'swa_sinks': swa_sinks,
         'w_swa_out': w_swa_out, 'b_swa_out': b_swa_out,
         'w_rg_in': w_rg_in, 'w_rg_conv': w_rg_conv, 'b_rg_conv': b_rg_conv, 'w_rg_a': w_rg_a,
         'b_rg_a': b_rg_a, 'w_rg_x': w_rg_x, 'b_rg_x': b_rg_x, 'rg_lambda': rg_lambda, 'w_rg_out': w_rg_out}
    y_p, c_p, n_p, m_p, k_p, v_p, h_p, cv_p = trunk(x_prompt, True, None, None, None, None, None, None, None, p)
    y_s, c_s, n_s, m_s, k_s, v_s, h_s, cv_s = trunk(x_sample, False, state_mlstm_c, state_mlstm_n, state_mlstm_m,
                                                    cache_swa_k, cache_swa_v, state_rglru_h, state_rglru_conv, p)
    return (y_p, y_s, c_p, n_p, m_p, c_s, n_s, m_s, k_p, v_p, k_s, v_s, h_p, cv_p, h_s, cv_s)
```

```python
import functools
import math

import jax
import jax.numpy as jnp
from jax import lax
from jax.experimental import pallas as pl
from jax.experimental.pallas import tpu as pltpu

f32 = jnp.float32
bf16 = jnp.bfloat16

D_MODEL = 1024
BATCH = 2
SEQ = 8192
DEPTH = 4
DEC_BATCH = 128
DEC_SEQ = 8
PAST_LEN = 8192
N_MIXERS = 3
NORM_EPS = 1e-6

A_HEADS = 4
A_DK = 128
A_DV = 256
A_MAIN = 2 * A_HEADS * A_DK + 2 * A_HEADS * A_DV
A_IN = A_MAIN + 2 * A_HEADS

B_HEADS = 16
B_KV_HEADS = 4
B_HD = 64
B_GROUP = 4
B_Q = B_HEADS * B_HD
B_KV = B_KV_HEADS * B_HD
B_IN = B_Q + 2 * B_KV
WINDOW = 128
ROPE_THETA = 500000.0
ROPE_DIM = 16

C_WIDTH = 1024
C_BLOCKS = 4
C_BW = 256
CONV_W = 4
LRU_C = 8.0
D_FF = 4096

N_P = BATCH * SEQ
N_S = DEC_BATCH * DEC_SEQ
N_TOK = N_P + N_S

LANES = 128
SUBLANES = 8
VMEM_LIMIT = 56 * 1024 * 1024

TM = 512
MLSTM_CHUNK = 256
MLSTM_SB = 16
SWA_SB = 8
RG_T = 512
FF_CHUNK = 512


def _dot(a, b):
    return jnp.dot(a, b, preferred_element_type=f32)


def _dot_nt(a, b):
    return lax.dot_general(a, b, (((1,), (1,)), ((), ())), preferred_element_type=f32)


def _dot_exact(a, b):
    return jnp.dot(a, b, preferred_element_type=f32, precision=lax.Precision.HIGHEST)


def _idiv(x, d):
    assert d & (d - 1) == 0
    return x >> (d.bit_length() - 1)


def _imod(x, d):
    assert d & (d - 1) == 0
    return x & (d - 1)


def _rms_bf16(x, g):
    y = x * lax.rsqrt(jnp.mean(x * x, axis=-1, keepdims=True) + NORM_EPS)
    return (y * g).astype(bf16)


def _sigmoid(x):
    return 1.0 / (1.0 + jnp.exp(-x))


def _softplus(x):
    return jnp.maximum(x, 0.0) + jnp.log1p(jnp.exp(-jnp.abs(x)))


def _gelu_tanh(x):
    return 0.5 * x * (1.0 + jnp.tanh(math.sqrt(2.0 / math.pi) * (x + 0.044715 * (x * x * x))))


def _params(*sem):
    return pltpu.CompilerParams(dimension_semantics=sem, vmem_limit_bytes=VMEM_LIMIT)


def _const_spec(shape):
    nd = len(shape)
    return pl.BlockSpec(shape, lambda *_: (0,) * nd)


def _proj_mlstm_kernel(x_ref, g_ref, w_ref, wg_ref, bg_ref, main_ref, gate_ref):
    xn = _rms_bf16(x_ref[...], g_ref[...])
    for c in range(0, A_MAIN, 512):
        main_ref[:, c:c + 512] = _dot(xn, w_ref[:, c:c + 512])
    gp = _dot(xn, wg_ref[...]) + bg_ref[...]
    lane = lax.broadcasted_iota(jnp.int32, gp.shape, 1)
    gate_ref[...] = jnp.where(lane >= A_HEADS, -_softplus(-gp), gp)


def _proj_mlstm(x, g, w_main, w_gate, b_gate):
    return pl.pallas_call(
        _proj_mlstm_kernel,
        grid=(N_TOK // TM,),
        in_specs=[pl.BlockSpec((TM, D_MODEL), lambda i: (i, 0)),
                  _const_spec((1, D_MODEL)),
                  _const_spec((D_MODEL, A_MAIN)),
                  _const_spec((D_MODEL, LANES)),
                  _const_spec((1, LANES))],
        out_specs=[pl.BlockSpec((TM, A_MAIN), lambda i: (i, 0)),
                   pl.BlockSpec((TM, LANES), lambda i: (i, 0))],
        out_shape=[jax.ShapeDtypeStruct((N_TOK, A_MAIN), f32),
                   jax.ShapeDtypeStruct((N_TOK, LANES), f32)],
        compiler_params=_params("parallel"),
        name="proj_mlstm",
    )(x, g, w_main, w_gate, b_gate)


def _proj_swa_kernel(x_ref, g_ref, w_ref, b_ref, rc_ref, o_ref):
    i = pl.program_id(0)
    xn = _rms_bf16(x_ref[...], g_ref[...])
    row = lax.broadcasted_iota(jnp.int32, (TM, 1), 0) + i * TM
    pos = jnp.where(row < N_P, row & (SEQ - 1), PAST_LEN + (row & (DEC_SEQ - 1))).astype(f32)
    ang = pos * rc_ref[0:1, :]
    cos = jnp.cos(ang)
    sin = jnp.sin(ang)
    sin_lo = sin * rc_ref[1:2, :]
    sin_hi = sin * rc_ref[2:3, :]
    half = ROPE_DIM // 2
    for c in range(0, B_IN, LANES):
        p = _dot(xn, w_ref[:, c:c + LANES]) + b_ref[:, c:c + LANES]
        if c < B_Q + B_KV:
            p = (p * cos + pltpu.roll(p, LANES - half, 1) * sin_lo + pltpu.roll(p, half, 1) * sin_hi)
        o_ref[:, c:c + LANES] = p


def _proj_swa(x, g, w, b, rope_consts):
    return pl.pallas_call(
        _proj_swa_kernel,
        grid=(N_TOK // TM,),
        in_specs=[pl.BlockSpec((TM, D_MODEL), lambda i: (i, 0)),
                  _const_spec((1, D_MODEL)),
                  _const_spec((D_MODEL, B_IN)),
                  _const_spec((1, B_IN)),
                  _const_spec((SUBLANES, LANES))],
        out_specs=pl.BlockSpec((TM, B_IN), lambda i: (i, 0)),
        out_shape=jax.ShapeDtypeStruct((N_TOK, B_IN), f32),
        compiler_params=_params("parallel"),
        name="proj_swa",
    )(x, g, w, b, rope_consts)


def _proj_rg_kernel(x_ref, g_ref, w_ref, o_ref):
    xn = _rms_bf16(x_ref[...], g_ref[...])
    for c in range(0, 2 * C_WIDTH, 512):
        o_ref[:, c:c + 512] = _dot(xn, w_ref[:, c:c + 512])


def _proj_rg(x, g, w):
    return pl.pallas_call(
        _proj_rg_kernel,
        grid=(N_TOK // TM,),
        in_specs=[pl.BlockSpec((TM, D_MODEL), lambda i: (i, 0)),
                  _const_spec((1, D_MODEL)),
                  _const_spec((D_MODEL, 2 * C_WIDTH))],
        out_specs=pl.BlockSpec((TM, 2 * C_WIDTH), lambda i: (i, 0)),
        out_shape=jax.ShapeDtypeStruct((N_TOK, 2 * C_WIDTH), f32),
        compiler_params=_params("parallel"),
        name="proj_rg",
    )(x, g, w)


def _mlstm_masks(t, seg):
    r = lax.broadcasted_iota(jnp.int32, (t, t), 0)
    c = lax.broadcasted_iota(jnp.int32, (t, t), 1)
    if seg == t:
        same = None
        lower, upper = r >= c, r <= c
    else:
        same = _idiv(r, seg) == _idiv(c, seg)
        lower, upper = same & (r >= c), same & (r <= c)
    return r, c, same, lower, upper


def _mlstm_intra(q, k, v, btc, btr, itr, m_col, lower):
    dmat = jnp.where(lower, btc - btr + itr, -jnp.inf)
    inter = btc + m_col
    m_t = jnp.maximum(jnp.max(dmat, axis=1, keepdims=True), inter)
    w = jnp.exp(dmat - m_t)
    w_inter = jnp.exp(inter - m_t)
    s = _dot_nt(q, k) * w
    num = _dot(s.astype(bf16), v)
    den = jnp.sum(s, axis=1, keepdims=True)
    return num, den, m_t, w_inter, inter


def _mlstm_head_out(num, den, m_t, o_pre, g_head):
    h = num / jnp.maximum(jnp.abs(den), jnp.exp(-m_t))
    h = h * lax.rsqrt(jnp.mean(h * h, axis=-1, keepdims=True) + NORM_EPS) * g_head
    return (_sigmoid(o_pre) * h).astype(bf16)


def _mlstm_prompt_kernel(main_ref, gc_ref, gr_ref, gh_ref, hs_ref, c_ref, n_ref, mt_ref, c_s, n_s, m_s):
    t = MLSTM_CHUNK
    j = pl.program_id(1)

    @pl.when(j == 0)
    def _():
        c_s[...] = jnp.zeros_like(c_s)
        n_s[...] = jnp.zeros_like(n_s)
        m_s[...] = jnp.zeros_like(m_s)

    _, _, _, lower, upper = _mlstm_masks(t, t)
    gc = gc_ref[...]
    gr = gr_ref[...]
    btc_all = _dot_exact(lower.astype(f32), gc)
    btr_all = _dot_exact(gr, upper.astype(f32))
    lane = lax.broadcasted_iota(jnp.int32, (t, LANES), 1)
    mt_all = jnp.zeros((t, LANES), f32)
    for h in range(A_HEADS):
        q = main_ref[:, h * A_DK:(h + 1) * A_DK]
        k = main_ref[:, (A_HEADS + h) * A_DK:(A_HEADS + h + 1) * A_DK] * (A_DK ** -0.5)
        v = main_ref[:, 2 * A_HEADS * A_DK + h * A_DV:2 * A_HEADS * A_DK + (h + 1) * A_DV]
        o_pre = main_ref[:, 2 * A_HEADS * A_DK + (A_HEADS + h) * A_DV:2 * A_HEADS * A_DK + (A_HEADS + h + 1) * A_DV]
        qb, kb, vb = q.astype(bf16), k.astype(bf16), v.astype(bf16)
        btc = btc_all[:, A_HEADS + h:A_HEADS + h + 1]
        itc = gc[:, h:h + 1]
        btr = btr_all[A_HEADS + h:A_HEADS + h + 1, :]
        itr = gr[h:h + 1, :]
        m_in = m_s[0:1, h:h + 1]
        num, den, m_t, w_inter, inter = _mlstm_intra(qb, kb, vb, btc, btr, itr, m_in, lower)
        c_old = c_s[h]
        n_old = n_s[h:h + 1, :]
        num = num + w_inter * _dot_nt(qb, c_old.astype(bf16))
        den = den + w_inter * jnp.sum(q * n_old, axis=1, keepdims=True)
        hs_ref[:, h * A_DV:(h + 1) * A_DV] = _mlstm_head_out(num, den, m_t, o_pre, gh_ref[h:h + 1, :])
        m_new = m_t[t - 1:t, :]
        decay = jnp.exp(btc[t - 1:t, :] - btc + itc - m_new)
        scale = jnp.exp(inter[t - 1:t, :] - m_new)
        vdt = (decay * v).T.astype(bf16)
        c_s[h] = scale * c_old + _dot(vdt, kb)
        n_s[h:h + 1, :] = scale * n_old + jnp.sum(decay * k, axis=0, keepdims=True)
        mt_all = jnp.where(lane == h, m_t, mt_all)
    mt_ref[...] = mt_all
    m_s[0:1, :] = mt_all[t - 1:t, :]

    @pl.when(j == pl.num_programs(1) - 1)
    def _():
        c_ref[0] = c_s[...]
        n_ref[0] = n_s[...]


def _mlstm_prompt(main, gates, gates_t, g_head):
    nc = SEQ // MLSTM_CHUNK
    t = MLSTM_CHUNK
    return pl.pallas_call(
        _mlstm_prompt_kernel,
        grid=(BATCH, nc),
        in_specs=[pl.BlockSpec((t, A_MAIN), lambda b, j: (b * nc + j, 0)),
                  pl.BlockSpec((t, LANES), lambda b, j: (b * nc + j, 0)),
                  pl.BlockSpec((SUBLANES, t), lambda b, j: (0, b * nc + j)),
                  _const_spec((A_HEADS, A_DV))],
        out_specs=[pl.BlockSpec((t, A_HEADS * A_DV), lambda b, j: (b * nc + j, 0)),
                   pl.BlockSpec((1, A_HEADS, A_DV, A_DK), lambda b, j: (b, 0, 0, 0)),
                   pl.BlockSpec((1, A_HEADS, A_DK), lambda b, j: (b, 0, 0)),
                   pl.BlockSpec((t, LANES), lambda b, j: (b * nc + j, 0))],
        out_shape=[jax.ShapeDtypeStruct((N_P, A_HEADS * A_DV), bf16),
                   jax.ShapeDtypeStruct((BATCH, A_HEADS, A_DV, A_DK), f32),
                   jax.ShapeDtypeStruct((BATCH, A_HEADS, A_DK), f32),
                   jax.ShapeDtypeStruct((N_P, LANES), f32)],
        scratch_shapes=[pltpu.VMEM((A_HEADS, A_DV, A_DK), f32),
                        pltpu.VMEM((A_HEADS, A_DK), f32),
                        pltpu.VMEM((SUBLANES, LANES), f32)],
        compiler_params=_params("parallel", "arbitrary"),
        name="mlstm_prompt",
    )(main, gates, gates_t, g_head)


def _mlstm_sample_kernel(main_ref, gc_ref, gr_ref, m0_ref, c0_ref, n0_ref, gh_ref,
                         hs_ref, c_ref, n_ref, mt_ref, qc_s, nt_s, vdt_s, kb_s, dk_s, sc_s):
    t = MLSTM_SB * DEC_SEQ
    seg = DEC_SEQ
    r, c, same, lower, upper = _mlstm_masks(t, seg)
    seg_last = (same & (_imod(c, seg) == seg - 1)).astype(f32)
    gc = gc_ref[...]
    gr = gr_ref[...]
    btc_all = _dot_exact(lower.astype(f32), gc)
    btr_all = _dot_exact(gr, upper.astype(f32))
    lane = lax.broadcasted_iota(jnp.int32, (t, LANES), 1)
    tok_lane_seg = _idiv(lax.broadcasted_iota(jnp.int32, (A_DV, t), 1), seg)
    mt_all = jnp.zeros((t, LANES), f32)
    for h in range(A_HEADS):
        qcol = h * A_DK
        q = main_ref[:, qcol:qcol + A_DK]
        k = main_ref[:, (A_HEADS + h) * A_DK:(A_HEADS + h + 1) * A_DK] * (A_DK ** -0.5)
        v = main_ref[:, 2 * A_HEADS * A_DK + h * A_DV:2 * A_HEADS * A_DK + (h + 1) * A_DV]
        o_pre = main_ref[:, 2 * A_HEADS * A_DK + (A_HEADS + h) * A_DV:2 * A_HEADS * A_DK + (A_HEADS + h + 1) * A_DV]
        qb, kb, vb = q.astype(bf16), k.astype(bf16), v.astype(bf16)
        btc = btc_all[:, A_HEADS + h:A_HEADS + h + 1]
        itc = gc[:, h:h + 1]
        btr = btr_all[A_HEADS + h:A_HEADS + h + 1, :]
        itr = gr[h:h + 1, :]
        m_in = m0_ref[:, h:h + 1]
        num, den, m_t, w_inter, inter = _mlstm_intra(qb, kb, vb, btc, btr, itr, m_in, lower)

        def inter_body(b, carry, h=h, qcol=qcol):
            rows = pl.ds(pl.multiple_of(b * seg, seg), seg)
            qrow = main_ref[rows, qcol:qcol + A_DK].astype(bf16)
            qc_s[rows, :] = _dot_nt(qrow, c0_ref[b, h].astype(bf16))
            nt_s[rows, :] = jnp.broadcast_to(n0_ref[b, h:h + 1, :], (seg, A_DK))
            return carry

        lax.fori_loop(0, MLSTM_SB, inter_body, 0)
        num = num + w_inter * qc_s[...]
        den = den + w_inter * jnp.sum(q * nt_s[...], axis=1, keepdims=True)
        hs_ref[:, h * A_DV:(h + 1) * A_DV] = _mlstm_head_out(num, den, m_t, o_pre, gh_ref[h:h + 1, :])

        packed = jnp.where(lane == 0, m_t, jnp.where(lane == 1, btc, inter))
        last = _dot_exact(seg_last, packed)
        m_new, bt_last, inter_last = last[:, 0:1], last[:, 1:2], last[:, 2:3]
        decay = jnp.exp(bt_last - btc + itc - m_new)
        sc_s[...] = jnp.broadcast_to(jnp.exp(inter_last - m_new), (t, LANES))
        vdt_s[...] = (decay * v).T
        kb_s[...] = kb
        dk_s[...] = decay * k

        def upd_body(b, carry, h=h):
            rows = pl.ds(pl.multiple_of(b * seg, seg), seg)
            scale = sc_s[rows, :][0:1, 0:1]
            vdt_b = jnp.where(tok_lane_seg == b, vdt_s[...], 0.0).astype(bf16)
            c_ref[b, h] = scale * c0_ref[b, h] + _dot(vdt_b, kb_s[...])
            n_ref[b, h:h + 1, :] = scale * n0_ref[b, h:h + 1, :] + jnp.sum(dk_s[rows, :], axis=0, keepdims=True)
            return carry

        lax.fori_loop(0, MLSTM_SB, upd_body, 0)
        mt_all = jnp.where(lane == h, m_t, mt_all)
    mt_ref[...] = mt_all


def _mlstm_sample(main, gates, gates_t, m0_tok, c0, n0, g_head):
    t = MLSTM_SB * DEC_SEQ
    off = N_P // t
    return pl.pallas_call(
        _mlstm_sample_kernel,
        grid=(DEC_BATCH // MLSTM_SB,),
        in_specs=[pl.BlockSpec((t, A_MAIN), lambda i: (off + i, 0)),
                  pl.BlockSpec((t, LANES), lambda i: (off + i, 0)),
                  pl.BlockSpec((SUBLANES, t), lambda i: (0, off + i)),
                  pl.BlockSpec((t, LANES), lambda i: (i, 0)),
                  pl.BlockSpec((MLSTM_SB, A_HEADS, A_DV, A_DK), lambda i: (i, 0, 0, 0)),
                  pl.BlockSpec((MLSTM_SB, A_HEADS, A_DK), lambda i: (i, 0, 0)),
                  _const_spec((A_HEADS, A_DV))],
        out_specs=[pl.BlockSpec((t, A_HEADS * A_DV), lambda i: (i, 0)),
                   pl.BlockSpec((MLSTM_SB, A_HEADS, A_DV, A_DK), lambda i: (i, 0, 0, 0)),
                   pl.BlockSpec((MLSTM_SB, A_HEADS, A_DK), lambda i: (i, 0, 0)),
                   pl.BlockSpec((t, LANES), lambda i: (i, 0))],
        out_shape=[jax.ShapeDtypeStruct((N_S, A_HEADS * A_DV), bf16),
                   jax.ShapeDtypeStruct((DEC_BATCH, A_HEADS, A_DV, A_DK), f32),
                   jax.ShapeDtypeStruct((DEC_BATCH, A_HEADS, A_DK), f32),
                   jax.ShapeDtypeStruct((N_S, LANES), f32)],
        scratch_shapes=[pltpu.VMEM((t, A_DV), f32),
                        pltpu.VMEM((t, A_DK), f32),
                        pltpu.VMEM((A_DV, t), f32),
                        pltpu.VMEM((t, A_DK), bf16),
                        pltpu.VMEM((t, A_DK), f32),
                        pltpu.VMEM((t, LANES), f32)],
        compiler_params=_params("parallel"),
        name="mlstm_sample",
    )(main, gates, gates_t, m0_tok, c0, n0, g_head)


def _swa_softmax_pv(s, allowed, sink_col, vb):
    s = jnp.where(allowed, s * (B_HD ** -0.5), -jnp.inf)
    m = jnp.maximum(jnp.max(s, axis=1, keepdims=True), sink_col)
    p = jnp.exp(s - m)
    den = jnp.sum(p, axis=1, keepdims=True) + jnp.exp(sink_col - m)
    return _dot((p / den).astype(bf16), vb)


def _swa_prompt_kernel(sink_ref, q_ref, kp_ref, ko_ref, vp_ref, vo_ref, o_ref):
    n = pl.program_id(1)
    w = WINDOW
    kb = jnp.concatenate([kp_ref[...], ko_ref[...]], axis=0).astype(bf16)
    vb = jnp.concatenate([vp_ref[...], vo_ref[...]], axis=0).astype(bf16)
    rows = B_GROUP * w
    t = _imod(lax.broadcasted_iota(jnp.int32, (rows, 2 * w), 0), w)
    jj = lax.broadcasted_iota(jnp.int32, (rows, 2 * w), 1)
    allowed = (jj > t) & (jj <= t + w) & ((n > 0) | (jj >= w))
    lane_head = _idiv(lax.broadcasted_iota(jnp.int32, (1, B_KV), 1), B_HD)
    grp = _idiv(lax.broadcasted_iota(jnp.int32, (rows, 1), 0), w)
    outs = [jnp.zeros((w, B_KV), f32) for _ in range(B_GROUP)]
    for h in range(B_KV_HEADS):
        hm = lane_head == h
        qh = jnp.concatenate([jnp.where(hm, q_ref[:, g * B_KV:(g + 1) * B_KV], 0.0) for g in range(B_GROUP)],
                             axis=0).astype(bf16)
        sink_col = jnp.zeros((rows, 1), f32)
        for g in range(B_GROUP):
            sink_col = jnp.where(grp == g, sink_ref[h * B_GROUP + g], sink_col)
        r = _swa_softmax_pv(_dot_nt(qh, kb), allowed, sink_col, vb)
        for g in range(B_GROUP):
            outs[g] = outs[g] + jnp.where(hm, r[g * w:(g + 1) * w, :], 0.0)
    for g in range(B_GROUP):
        o_ref[:, g * B_KV:(g + 1) * B_KV] = outs[g].astype(bf16)


def _swa_prompt(proj, sinks):
    nb = SEQ // WINDOW
    w = WINDOW
    kcol = B_Q // B_KV
    vcol = kcol + 1
    prev = lambda b, n: b * nb + jnp.maximum(n - 1, 0)
    return pl.pallas_call(
        _swa_prompt_kernel,
        grid=(BATCH, nb),
        in_specs=[pl.BlockSpec(memory_space=pltpu.SMEM),
                  pl.BlockSpec((w, B_Q), lambda b, n: (b * nb + n, 0)),
                  pl.BlockSpec((w, B_KV), lambda b, n: (prev(b, n), kcol)),
                  pl.BlockSpec((w, B_KV), lambda b, n: (b * nb + n, kcol)),
                  pl.BlockSpec((w, B_KV), lambda b, n: (prev(b, n), vcol)),
                  pl.BlockSpec((w, B_KV), lambda b, n: (b * nb + n, vcol))],
        out_specs=pl.BlockSpec((w, B_Q), lambda b, n: (b * nb + n, 0)),
        out_shape=jax.ShapeDtypeStruct((N_P, B_Q), bf16),
        compiler_params=_params("parallel", "arbitrary"),
        name="swa_prompt",
    )(sinks, proj, proj, proj, proj, proj)


def _swa_sample_kernel(sink_ref, q_ref, kn_ref, vn_ref, kc_ref, vc_ref, o_ref, ko_ref, vo_ref, kp_s, vp_s):
    s_len = DEC_SEQ
    buf = WINDOW
    rows = B_HEADS * s_len
    keys = 2 * buf
    ri = lax.broadcasted_iota(jnp.int32, (rows, keys), 0)
    jj = lax.broadcasted_iota(jnp.int32, (rows, keys), 1)
    t = _imod(ri, s_len)
    allowed = ((jj < buf) & (jj > t)) | ((jj >= buf) & (jj - buf <= t))
    lane_head = _idiv(lax.broadcasted_iota(jnp.int32, (1, B_KV), 1), B_HD)
    hg = _idiv(lax.broadcasted_iota(jnp.int32, (rows, 1), 0), s_len)
    sink_col = jnp.zeros((rows, 1), f32)
    for i in range(B_HEADS):
        sink_col = jnp.where(hg == i, sink_ref[i], sink_col)
    kp_s[buf + s_len:, :] = jnp.zeros((keys - buf - s_len, B_KV), f32)
    vp_s[buf + s_len:, :] = jnp.zeros((keys - buf - s_len, B_KV), f32)

    def body(e, carry):
        rs = pl.ds(pl.multiple_of(e * s_len, s_len), s_len)
        kn = kn_ref[rs, :]
        vn = vn_ref[rs, :]
        kc = kc_ref[e]
        vc = vc_ref[e]
        kp_s[0:buf, :] = kc
        vp_s[0:buf, :] = vc
        kp_s[buf:buf + s_len, :] = kn
        vp_s[buf:buf + s_len, :] = vn
        ko_ref[e, 0:buf - s_len, :] = kc[s_len:, :]
        ko_ref[e, buf - s_len:, :] = kn
        vo_ref[e, 0:buf - s_len, :] = vc[s_len:, :]
        vo_ref[e, buf - s_len:, :] = vn
        qe = q_ref[rs, :]
        qbig = jnp.concatenate(
            [jnp.where(lane_head == h, qe[:, g * B_KV:(g + 1) * B_KV], 0.0)
             for h in range(B_KV_HEADS) for g in range(B_GROUP)], axis=0).astype(bf16)
        r = _swa_softmax_pv(_dot_nt(qbig, kp_s[...].astype(bf16)), allowed, sink_col, vp_s[...].astype(bf16))
        for g in range(B_GROUP):
            og = jnp.zeros((s_len, B_KV), f32)
            for h in range(B_KV_HEADS):
                blk = (h * B_GROUP + g) * s_len
                og = og + jnp.where(lane_head == h, r[blk:blk + s_len, :], 0.0)
            o_ref[rs, g * B_KV:(g + 1) * B_KV] = og.astype(bf16)
        return carry

    lax.fori_loop(0, SWA_SB, body, 0)


def _swa_sample(proj, sinks, k_cache, v_cache):
    t = SWA_SB * DEC_SEQ
    off = N_P // t
    kcol = B_Q // B_KV
    return pl.pallas_call(
        _swa_sample_kernel,
        grid=(DEC_BATCH // SWA_SB,),
        in_specs=[pl.BlockSpec(memory_space=pltpu.SMEM),
                  pl.BlockSpec((t, B_Q), lambda i: (off + i, 0)),
                  pl.BlockSpec((t, B_KV), lambda i: (off + i, kcol)),
                  pl.BlockSpec((t, B_KV), lambda i: (off + i, kcol + 1)),
                  pl.BlockSpec((SWA_SB, WINDOW, B_KV), lambda i: (i, 0, 0)),
                  pl.BlockSpec((SWA_SB, WINDOW, B_KV), lambda i: (i, 0, 0))],
        out_specs=[pl.BlockSpec((t, B_Q), lambda i: (i, 0)),
                   pl.BlockSpec((SWA_SB, WINDOW, B_KV), lambda i: (i, 0, 0)),
                   pl.BlockSpec((SWA_SB, WINDOW, B_KV), lambda i: (i, 0, 0))],
        out_shape=[jax.ShapeDtypeStruct((N_S, B_Q), bf16),
                   jax.ShapeDtypeStruct((DEC_BATCH, WINDOW, B_KV), f32),
                   jax.ShapeDtypeStruct((DEC_BATCH, WINDOW, B_KV), f32)],
        scratch_shapes=[pltpu.VMEM((2 * WINDOW, B_KV), f32),
                        pltpu.VMEM((2 * WINDOW, B_KV), f32)],
        compiler_params=_params("parallel"),
        name="swa_sample",
    )(sinks, proj, proj, proj, k_cache, v_cache)


def _rg_conv_group(x8, p8, wc_ref, bc_ref):
    row = lax.broadcasted_iota(jnp.int32, x8.shape, 0)
    u = bc_ref[...] + wc_ref[CONV_W - 1:CONV_W, :] * x8
    for d in range(1, CONV_W):
        sh = jnp.where(row >= d, pltpu.roll(x8, d, 0), pltpu.roll(p8, d, 0))
        u = u + wc_ref[CONV_W - 1 - d:CONV_W - d, :] * sh
    return u


def _rg_scan_group(a8, b8, carry):
    row = lax.broadcasted_iota(jnp.int32, a8.shape, 0)
    for d in (1, 2, 4):
        keep = row >= d
        b8 = jnp.where(keep, a8 * pltpu.roll(b8, d, 0) + b8, b8)
        a8 = jnp.where(keep, a8 * pltpu.roll(a8, d, 0), a8)
    return a8 * carry + b8


def _rg_gates(u, gate, wa_ref, ba_ref, wx_ref, bx_ref, lam_ref):
    ub = u.astype(bf16)
    ra = jnp.concatenate([_dot(ub[:, n * C_BW:(n + 1) * C_BW], wa_ref[n]) for n in range(C_BLOCKS)], axis=1)
    rx = jnp.concatenate([_dot(ub[:, n * C_BW:(n + 1) * C_BW], wx_ref[n]) for n in range(C_BLOCKS)], axis=1)
    r = _sigmoid(ra + ba_ref[...])
    gi = _sigmoid(rx + bx_ref[...])
    log_a = -LRU_C * r * _softplus(-lam_ref[...])
    a = jnp.exp(log_a)
    bterm = jnp.sqrt(-jnp.tanh(log_a) * (a * a + 1.0)) * gi * u
    return a, bterm, _gelu_tanh(gate)


def _rg_prompt_kernel(p_ref, wc_ref, bc_ref, wa_ref, ba_ref, wx_ref, bx_ref, lam_ref,
                      y_ref, h_ref, u_s, a_s, b_s, xc_s, hc_s):
    j = pl.program_id(1)
    ng = RG_T // SUBLANES

    @pl.when(j == 0)
    def _():
        xc_s[...] = jnp.zeros_like(xc_s)
        hc_s[...] = jnp.zeros_like(hc_s)

    def conv_body(gidx, p8):
        rows = pl.ds(pl.multiple_of(gidx * SUBLANES, SUBLANES), SUBLANES)
        x8 = p_ref[rows, 0:C_WIDTH]
        u_s[rows, :] = _rg_conv_group(x8, p8, wc_ref, bc_ref)
        return x8

    xc_s[...] = lax.fori_loop(0, ng, conv_body, xc_s[...])
    a, bterm, gg = _rg_gates(u_s[...], p_ref[:, C_WIDTH:], wa_ref, ba_ref, wx_ref, bx_ref, lam_ref)
    a_s[...] = a
    b_s[...] = bterm

    def scan_body(gidx, carry):
        rows = pl.ds(pl.multiple_of(gidx * SUBLANES, SUBLANES), SUBLANES)
        h8 = _rg_scan_group(a_s[rows, :], b_s[rows, :], carry)
        u_s[rows, :] = h8
        return h8[SUBLANES - 1:SUBLANES, :]

    h_last = lax.fori_loop(0, ng, scan_body, hc_s[...])
    hc_s[...] = h_last
    y_ref[...] = (u_s[...] * gg).astype(bf16)

    @pl.when(j == pl.num_programs(1) - 1)
    def _():
        h_ref[0] = h_last


def _rg_weight_specs():
    return [_const_spec((CONV_W, C_WIDTH)), _const_spec((1, C_WIDTH)),
            _const_spec((C_BLOCKS, C_BW, C_BW)), _const_spec((1, C_WIDTH)),
            _const_spec((C_BLOCKS, C_BW, C_BW)), _const_spec((1, C_WIDTH)),
            _const_spec((1, C_WIDTH))]


def _rg_prompt(proj, weights):
    nt = SEQ // RG_T
    return pl.pallas_call(
        _rg_prompt_kernel,
        grid=(BATCH, nt),
        in_specs=[pl.BlockSpec((RG_T, 2 * C_WIDTH), lambda b, j: (b * nt + j, 0))] + _rg_weight_specs(),
        out_specs=[pl.BlockSpec((RG_T, C_WIDTH), lambda b, j: (b * nt + j, 0)),
                   pl.BlockSpec((1, 1, C_WIDTH), lambda b, j: (b, 0, 0))],
        out_shape=[jax.ShapeDtypeStruct((N_P, C_WIDTH), bf16),
                   jax.ShapeDtypeStruct((BATCH, 1, C_WIDTH), f32)],
        scratch_shapes=[pltpu.VMEM((RG_T, C_WIDTH), f32),
                        pltpu.VMEM((RG_T, C_WIDTH), f32),
                        pltpu.VMEM((RG_T, C_WIDTH), f32),
                        pltpu.VMEM((SUBLANES, C_WIDTH), f32),
                        pltpu.VMEM((1, C_WIDTH), f32)],
        compiler_params=_params("parallel", "arbitrary"),
        name="rg_prompt",
    )(proj, *weights)


def _rg_sample_kernel(p_ref, cv_ref, h0_ref, wc_ref, bc_ref, wa_ref, ba_ref, wx_ref, bx_ref, lam_ref,
                      y_ref, h_ref, u_s, a_s, b_s):
    def conv_body(gidx, carry):
        rows = pl.ds(pl.multiple_of(gidx * SUBLANES, SUBLANES), SUBLANES)
        u_s[rows, :] = _rg_conv_group(p_ref[rows, 0:C_WIDTH], cv_ref[rows, :], wc_ref, bc_ref)
        return carry

    lax.fori_loop(0, DEC_BATCH, conv_body, 0)
    a, bterm, gg = _rg_gates(u_s[...], p_ref[:, C_WIDTH:], wa_ref, ba_ref, wx_ref, bx_ref, lam_ref)
    a_s[...] = a
    b_s[...] = bterm

    def scan_body(gidx, carry):
        rows = pl.ds(pl.multiple_of(gidx * SUBLANES, SUBLANES), SUBLANES)
        h8 = _rg_scan_group(a_s[rows, :], b_s[rows, :], h0_ref[pl.ds(gidx, 1), :])
        u_s[rows, :] = h8
        h_ref[pl.ds(gidx, 1), :] = h8[SUBLANES - 1:SUBLANES, :]
        return carry

    lax.fori_loop(0, DEC_BATCH, scan_body, 0)
    y_ref[...] = (u_s[...] * gg).astype(bf16)


def _rg_sample(proj, conv_pad, h0, weights):
    off = N_P // N_S
    return pl.pallas_call(
        _rg_sample_kernel,
        grid=(1,),
        in_specs=[pl.BlockSpec((N_S, 2 * C_WIDTH), lambda i: (off, 0)),
                  _const_spec((N_S, C_WIDTH)),
                  _const_spec((DEC_BATCH, C_WIDTH))] + _rg_weight_specs(),
        out_specs=[_const_spec((N_S, C_WIDTH)), _const_spec((DEC_BATCH, C_WIDTH))],
        out_shape=[jax.ShapeDtypeStruct((N_S, C_WIDTH), bf16),
                   jax.ShapeDtypeStruct((DEC_BATCH, C_WIDTH), f32)],
        scratch_shapes=[pltpu.VMEM((N_S, C_WIDTH), f32),
                        pltpu.VMEM((N_S, C_WIDTH), f32),
                        pltpu.VMEM((N_S, C_WIDTH), f32)],
        compiler_params=_params("arbitrary"),
        name="rg_sample",
    )(proj, conv_pad, h0, *weights)


def _out_mlp_kernel(x_ref, ap_ref, as_ref, wo_ref, bo_ref, g_ref, wup_ref, wdn_ref, gf_ref, o_ref,
                    x1_s, xn_s, *, final):
    i = pl.program_id(0)
    n_p_tiles = N_P // TM

    @pl.when(i < n_p_tiles)
    def _():
        x1_s[...] = x_ref[...] + _dot(ap_ref[...], wo_ref[...]) + bo_ref[...]

    @pl.when(i >= n_p_tiles)
    def _():
        x1_s[...] = x_ref[...] + _dot(as_ref[...], wo_ref[...]) + bo_ref[...]

    xn_s[...] = _rms_bf16(x1_s[...], g_ref[...])
    for c in range(0, D_FF, FF_CHUNK):
        hmid = jnp.maximum(_dot(xn_s[...], wup_ref[:, c:c + FF_CHUNK]), 0.0)
        x1_s[...] += _dot((hmid * hmid).astype(bf16), wdn_ref[c:c + FF_CHUNK, :])
    if final:
        x1 = x1_s[...]
        y = x1 * lax.rsqrt(jnp.mean(x1 * x1, axis=-1, keepdims=True) + NORM_EPS)
        o_ref[...] = y * gf_ref[...]
    else:
        o_ref[...] = x1_s[...]


def _out_mlp(x, a_p, a_s, wo, bo, g, w_up, w_down, g_final, final):
    n_p_tiles = N_P // TM
    single = pl.Buffered(1)
    wspec = lambda shape: pl.BlockSpec(shape, lambda i: (0, 0), pipeline_mode=single)
    return pl.pallas_call(
        functools.partial(_out_mlp_kernel, final=final),
        grid=(N_TOK // TM,),
        in_specs=[pl.BlockSpec((TM, D_MODEL), lambda i: (i, 0)),
                  pl.BlockSpec((TM, D_MODEL), lambda i: (jnp.minimum(i, n_p_tiles - 1), 0)),
                  pl.BlockSpec((TM, D_MODEL), lambda i: (jnp.maximum(i - n_p_tiles, 0), 0)),
                  wspec((D_MODEL, D_MODEL)),
                  wspec((1, D_MODEL)),
                  wspec((1, D_MODEL)),
                  wspec((D_MODEL, D_FF)),
                  wspec((D_FF, D_MODEL)),
                  wspec((1, D_MODEL))],
        out_specs=pl.BlockSpec((TM, D_MODEL), lambda i: (i, 0)),
        out_shape=jax.ShapeDtypeStruct((N_TOK, D_MODEL), f32),
        scratch_shapes=[pltpu.VMEM((TM, D_MODEL), f32),
                        pltpu.VMEM((TM, D_MODEL), bf16)],
        compiler_params=_params("parallel"),
        name="out_mlp",
    )(x, a_p, a_s, wo, bo, g, w_up, w_down, g_final)


def _rope_consts():
    half = ROPE_DIM // 2
    inv = ROPE_THETA ** (-jnp.arange(0, ROPE_DIM, 2, dtype=f32) / ROPE_DIM)
    lane = jnp.arange(LANES) % B_HD
    inv_lane = jnp.where(lane < ROPE_DIM, inv[lane % half], 0.0)
    lo = jnp.where(lane < half, -1.0, 0.0)
    hi = jnp.where((lane >= half) & (lane < ROPE_DIM), 1.0, 0.0)
    return jnp.zeros((SUBLANES, LANES), f32).at[0].set(inv_lane).at[1].set(lo).at[2].set(hi)


def _q_perm():
    g, h, d = jnp.meshgrid(jnp.arange(B_GROUP), jnp.arange(B_KV_HEADS), jnp.arange(B_HD), indexing="ij")
    return ((h * B_GROUP + g) * B_HD + d).reshape(-1)


def kernel(x_prompt, x_sample, state_mlstm_c, state_mlstm_n, state_mlstm_m, cache_swa_k, cache_swa_v,
           state_rglru_h, state_rglru_conv, norm_mix, norm_mlp, norm_final, w_mlp_up, w_mlp_down,
           w_mlstm_in, b_mlstm_i, b_mlstm_f, g_mlstm_head, w_mlstm_out, w_swa_qkv, b_swa_qkv, swa_sinks,
           w_swa_out, b_swa_out, w_rg_in, w_rg_conv, b_rg_conv, w_rg_a, b_rg_a, w_rg_x, b_rg_x, rg_lambda,
           w_rg_out):
    x = jnp.concatenate([x_prompt.reshape(N_P, D_MODEL), x_sample.reshape(N_S, D_MODEL)], axis=0)
    zero_bias = jnp.zeros((1, D_MODEL), f32)
    row = lambda v: v.reshape(1, -1).astype(f32)
    outs = {k: [] for k in ("c_p", "n_p", "m_p", "c_s", "n_s", "m_s", "k_p", "v_p", "k_s", "v_s",
                            "h_p", "cv_p", "h_s", "cv_s")}
    for i in range(DEPTH):
        kind, j = i % N_MIXERS, i // N_MIXERS
        g_mix = row(norm_mix[i])
        if kind == 0:
            w_in = w_mlstm_in[j]
            w_gate = jnp.pad(w_in[:, A_MAIN:], ((0, 0), (0, LANES - 2 * A_HEADS))).astype(bf16)
            b_gate = jnp.pad(jnp.concatenate([b_mlstm_i[j], b_mlstm_f[j]]), (0, LANES - 2 * A_HEADS)).reshape(1, LANES)
            main, gates = _proj_mlstm(x, g_mix, w_in[:, :A_MAIN].astype(bf16), w_gate, b_gate.astype(f32))
            gates_t = gates[:, :SUBLANES].T
            g_head = g_mlstm_head[j].astype(f32)
            a_p, c_p, n_p, mt_p = _mlstm_prompt(main, gates, gates_t, g_head)
            m0_tok = jnp.pad(jnp.repeat(state_mlstm_m[j].astype(f32), DEC_SEQ, axis=0),
                             ((0, 0), (0, LANES - A_HEADS)))
            a_s, c_s, n_s, mt_s = _mlstm_sample(main, gates, gates_t, m0_tok, state_mlstm_c[j].astype(f32),
                                                state_mlstm_n[j].astype(f32), g_head)
            outs["c_p"].append(c_p); outs["n_p"].append(n_p)
            outs["m_p"].append(mt_p[SEQ - 1::SEQ, :A_HEADS])
            outs["c_s"].append(c_s); outs["n_s"].append(n_s)
            outs["m_s"].append(mt_s[DEC_SEQ - 1::DEC_SEQ, :A_HEADS])
            wo, bo = w_mlstm_out[j], zero_bias
        elif kind == 1:
            perm = _q_perm()
            col_perm = jnp.concatenate([perm, jnp.arange(B_Q, B_IN)])
            proj = _proj_swa(x, g_mix, w_swa_qkv[j][:, col_perm].astype(bf16), row(b_swa_qkv[j][col_perm]),
                             _rope_consts())
            sinks = swa_sinks[j].astype(f32)
            a_p = _swa_prompt(proj, sinks)
            buf = cache_swa_k.shape[2]
            a_s, k_s, v_s = _swa_sample(proj, sinks,
                                        cache_swa_k[j].astype(f32).reshape(DEC_BATCH, buf, B_KV),
                                        cache_swa_v[j].astype(f32).reshape(DEC_BATCH, buf, B_KV))
            kv_p = proj[:N_P, B_Q:].reshape(BATCH, SEQ, 2, B_KV_HEADS, B_HD)[:, SEQ - WINDOW:]
            outs["k_p"].append(kv_p[:, :, 0]); outs["v_p"].append(kv_p[:, :, 1])
            outs["k_s"].append(k_s.reshape(DEC_BATCH, buf, B_KV_HEADS, B_HD))
            outs["v_s"].append(v_s.reshape(DEC_BATCH, buf, B_KV_HEADS, B_HD))
            wo, bo = w_swa_out[j][perm, :], row(b_swa_out[j])
        else:
            proj = _proj_rg(x, g_mix, w_rg_in[j].astype(bf16))
            weights = (w_rg_conv[j].astype(f32), row(b_rg_conv[j]), w_rg_a[j].astype(bf16), row(b_rg_a[j]),
                       w_rg_x[j].astype(bf16), row(b_rg_x[j]), row(rg_lambda[j]))
            a_p, h_p = _rg_prompt(proj, weights)
            conv_pad = jnp.pad(state_rglru_conv[j].astype(f32),
                               ((0, 0), (SUBLANES - (CONV_W - 1), 0), (0, 0))).reshape(N_S, C_WIDTH)
            a_s, h_s = _rg_sample(proj, conv_pad, state_rglru_h[j].astype(f32), weights)
            xb_p = proj[:N_P, :C_WIDTH].reshape(BATCH, SEQ, C_WIDTH)
            xb_s = proj[N_P:, :C_WIDTH].reshape(DEC_BATCH, DEC_SEQ, C_WIDTH)
            outs["h_p"].append(h_p.reshape(BATCH, C_WIDTH)); outs["cv_p"].append(xb_p[:, SEQ - (CONV_W - 1):])
            outs["h_s"].append(h_s); outs["cv_s"].append(xb_s[:, DEC_SEQ - (CONV_W - 1):])
            wo, bo = w_rg_out[j], zero_bias
        x = _out_mlp(x, a_p, a_s, wo.astype(bf16), bo, row(norm_mlp[i]), w_mlp_up[i].astype(bf16),
                     w_mlp_down[i].astype(bf16), row(norm_final), final=(i == DEPTH - 1))
    st = {k: jnp.stack(v) for k, v in outs.items()}
    y_p = x[:N_P].reshape(BATCH, SEQ, D_MODEL)
    y_s = x[N_P:].reshape(DEC_BATCH, DEC_SEQ, D_MODEL)
    return (y_p, y_s, st["c_p"], st["n_p"], st["m_p"], st["c_s"], st["n_s"], st["m_s"],
            st["k_p"], st["v_p"], st["k_s"], st["v_s"], st["h_p"], st["cv_p"], st["h_s"], st["cv_s"])
```

```python
import functools
import math

import jax
import jax.numpy as jnp
from jax import lax
from jax.experimental import pallas as pl
from jax.experimental.pallas import tpu as pltpu

f32 = jnp.float32
bf16 = jnp.bfloat16

D_MODEL = 1024
BATCH = 2
SEQ = 8192
DEPTH = 4
DEC_BATCH = 128
DEC_SEQ = 8
PAST_LEN = 8192
N_MIXERS = 3
NORM_EPS = 1e-6

A_HEADS = 4
A_DK = 128
A_DV = 256
A_QK = A_HEADS * A_DK
A_MAIN = 2 * A_QK + 2 * A_HEADS * A_DV
N_A = (DEPTH + 2) // 3

B_HEADS = 16
B_KV_HEADS = 4
B_HD = 64
B_GROUP = 4
B_Q = B_HEADS * B_HD
B_KV = B_KV_HEADS * B_HD
B_IN = B_Q + 2 * B_KV
WINDOW = 128
ROPE_THETA = 500000.0
ROPE_DIM = 16

C_WIDTH = 1024
C_BLOCKS = 4
C_BW = 256
CONV_W = 4
LRU_C = 8.0
D_FF = 4096

N_P = BATCH * SEQ
N_S = DEC_BATCH * DEC_SEQ

LANES = 128
SUBLANES = 8
VMEM_LIMIT = 56 * 1024 * 1024

TM = 512
NP_TILES = N_P // TM
NS_TILES = N_S // TM
MLSTM_CHUNK = 256
MLSTM_SB = 16
MLSTM_ST = MLSTM_SB * DEC_SEQ
MLSTM_UNROLL = 4
SWA_SB = 8
RG_T = 512
FF_CHUNK = 512


def _dot(a, b):
    return jnp.dot(a, b, preferred_element_type=f32)


def _dot_nt(a, b):
    return lax.dot_general(a, b, (((1,), (1,)), ((), ())), preferred_element_type=f32)


def _dot_exact(a, b):
    return jnp.dot(a, b, preferred_element_type=f32, precision=lax.Precision.HIGHEST)


def _idiv(x, d):
    assert d & (d - 1) == 0
    return x >> (d.bit_length() - 1)


def _imod(x, d):
    assert d & (d - 1) == 0
    return x & (d - 1)


def _rms_bf16(x, g):
    y = x * lax.rsqrt(jnp.mean(x * x, axis=-1, keepdims=True) + NORM_EPS)
    return (y * g).astype(bf16)


def _sigmoid(x):
    return 1.0 / (1.0 + jnp.exp(-x))


def _softplus(x):
    return jnp.maximum(x, 0.0) + jnp.log1p(jnp.exp(-jnp.abs(x)))


def _gelu_tanh(x):
    return 0.5 * x * (1.0 + jnp.tanh(math.sqrt(2.0 / math.pi) * (x + 0.044715 * (x * x * x))))


def _params(*sem):
    return pltpu.CompilerParams(dimension_semantics=sem, vmem_limit_bytes=VMEM_LIMIT)


def _const_spec(shape):
    nd = len(shape)
    return pl.BlockSpec(shape, lambda *_: (0,) * nd)


def _p_spec(width):
    return pl.BlockSpec((TM, width), lambda i: (jnp.minimum(i, NP_TILES - 1), 0))


def _s_spec(width):
    return pl.BlockSpec((TM, width), lambda i: (jnp.maximum(i - NP_TILES, 0), 0))


def _for_each_group(body, p_refs, s_refs):
    i = pl.program_id(0)

    @pl.when(i < NP_TILES)
    def _():
        body(*p_refs)

    @pl.when(i >= NP_TILES)
    def _():
        body(*s_refs)


def _proj_mlstm_kernel(xp_ref, xs_ref, g_ref, w_ref, wg_ref, bg_ref, mp_ref, ms_ref, gp_ref, gs_ref):
    def body(x_ref, main_ref, gate_ref):
        xn = _rms_bf16(x_ref[...], g_ref[...])
        for c in range(0, A_MAIN, 512):
            main_ref[:, c:c + 512] = _dot(xn, w_ref[:, c:c + 512])
        gp = _dot(xn, wg_ref[...]) + bg_ref[...]
        lane = lax.broadcasted_iota(jnp.int32, gp.shape, 1)
        gate_ref[...] = jnp.where(lane >= A_HEADS, -_softplus(-gp), gp)

    _for_each_group(body, (xp_ref, mp_ref, gp_ref), (xs_ref, ms_ref, gs_ref))


def _proj_mlstm(x_p, x_s, g, w_main, w_gate, b_gate):
    return pl.pallas_call(
        _proj_mlstm_kernel,
        grid=(NP_TILES + NS_TILES,),
        in_specs=[_p_spec(D_MODEL), _s_spec(D_MODEL),
                  _const_spec((1, D_MODEL)),
                  _const_spec((D_MODEL, A_MAIN)),
                  _const_spec((D_MODEL, LANES)),
                  _const_spec((1, LANES))],
        out_specs=[_p_spec(A_MAIN), _s_spec(A_MAIN), _p_spec(LANES), _s_spec(LANES)],
        out_shape=[jax.ShapeDtypeStruct((N_P, A_MAIN), f32), jax.ShapeDtypeStruct((N_S, A_MAIN), f32),
                   jax.ShapeDtypeStruct((N_P, LANES), f32), jax.ShapeDtypeStruct((N_S, LANES), f32)],
        compiler_params=_params("arbitrary"),
        name="proj_mlstm",
    )(x_p, x_s, g, w_main, w_gate, b_gate)


def _proj_swa_kernel(xp_ref, xs_ref, g_ref, w_ref, b_ref, tab_ref, rc_ref, op_ref, os_ref):
    half = ROPE_DIM // 2

    def body(x_ref, o_ref):
        xn = _rms_bf16(x_ref[...], g_ref[...])
        cos = tab_ref[:, 0:LANES]
        sin = tab_ref[:, LANES:2 * LANES]
        sin_lo = sin * rc_ref[0:1, :]
        sin_hi = sin * rc_ref[1:2, :]
        for c in range(0, B_IN, 512):
            p = _dot(xn, w_ref[:, c:c + 512]) + b_ref[:, c:c + 512]
            for l in range(0, 512, LANES):
                pl_ = p[:, l:l + LANES]
                if c + l < B_Q + B_KV:
                    pl_ = (pl_ * cos + pltpu.roll(pl_, LANES - half, 1) * sin_lo
                           + pltpu.roll(pl_, half, 1) * sin_hi)
                o_ref[:, c + l:c + l + LANES] = pl_

    _for_each_group(body, (xp_ref, op_ref), (xs_ref, os_ref))


def _proj_swa(x_p, x_s, g, w, b, rope_tab, rope_consts):
    seq_tiles = SEQ // TM
    tab_idx = lambda i: (jnp.where(i < NP_TILES, i % seq_tiles, seq_tiles), 0)
    return pl.pallas_call(
        _proj_swa_kernel,
        grid=(NP_TILES + NS_TILES,),
        in_specs=[_p_spec(D_MODEL), _s_spec(D_MODEL),
                  _const_spec((1, D_MODEL)),
                  _const_spec((D_MODEL, B_IN)),
                  _const_spec((1, B_IN)),
                  pl.BlockSpec((TM, 2 * LANES), tab_idx),
                  _const_spec((SUBLANES, LANES))],
        out_specs=[_p_spec(B_IN), _s_spec(B_IN)],
        out_shape=[jax.ShapeDtypeStruct((N_P, B_IN), f32), jax.ShapeDtypeStruct((N_S, B_IN), f32)],
        compiler_params=_params("arbitrary"),
        name="proj_swa",
    )(x_p, x_s, g, w, b, rope_tab, rope_consts)


def _proj_rg_kernel(xp_ref, xs_ref, g_ref, w_ref, op_ref, os_ref):
    def body(x_ref, o_ref):
        xn = _rms_bf16(x_ref[...], g_ref[...])
        for c in range(0, 2 * C_WIDTH, 512):
            o_ref[:, c:c + 512] = _dot(xn, w_ref[:, c:c + 512])

    _for_each_group(body, (xp_ref, op_ref), (xs_ref, os_ref))


def _proj_rg(x_p, x_s, g, w):
    return pl.pallas_call(
        _proj_rg_kernel,
        grid=(NP_TILES + NS_TILES,),
        in_specs=[_p_spec(D_MODEL), _s_spec(D_MODEL),
                  _const_spec((1, D_MODEL)),
                  _const_spec((D_MODEL, 2 * C_WIDTH))],
        out_specs=[_p_spec(2 * C_WIDTH), _s_spec(2 * C_WIDTH)],
        out_shape=[jax.ShapeDtypeStruct((N_P, 2 * C_WIDTH), f32), jax.ShapeDtypeStruct((N_S, 2 * C_WIDTH), f32)],
        compiler_params=_params("arbitrary"),
        name="proj_rg",
    )(x_p, x_s, g, w)


def _mlstm_masks(t, seg):
    r = lax.broadcasted_iota(jnp.int32, (t, t), 0)
    c = lax.broadcasted_iota(jnp.int32, (t, t), 1)
    if seg == t:
        same = None
        lower, upper = r >= c, r <= c
    else:
        same = _idiv(r, seg) == _idiv(c, seg)
        lower, upper = same & (r >= c), same & (r <= c)
    return r, c, same, lower, upper


def _mlstm_intra(q, k, v, btc, btr, itr, m_col, lower):
    dmat = jnp.where(lower, btc - btr + itr, -jnp.inf)
    inter = btc + m_col
    m_t = jnp.maximum(jnp.max(dmat, axis=1, keepdims=True), inter)
    w = jnp.exp(dmat - m_t)
    w_inter = jnp.exp(inter - m_t)
    s = _dot_nt(q, k) * w
    num = _dot(s.astype(bf16), v)
    den = jnp.sum(s, axis=1, keepdims=True)
    return num, den, m_t, w_inter, inter


def _mlstm_head_out(num, den, m_t, o_pre, g_head):
    h = num / jnp.maximum(jnp.abs(den), jnp.exp(-m_t))
    h = h * lax.rsqrt(jnp.mean(h * h, axis=-1, keepdims=True) + NORM_EPS) * g_head
    return (_sigmoid(o_pre) * h).astype(bf16)


def _mlstm_cols(main_ref, h):
    q = main_ref[:, h * A_DK:(h + 1) * A_DK]
    k = main_ref[:, A_QK + h * A_DK:A_QK + (h + 1) * A_DK] * (A_DK ** -0.5)
    v = main_ref[:, 2 * A_QK + h * A_DV:2 * A_QK + (h + 1) * A_DV]
    o_pre = main_ref[:, 2 * A_QK + (A_HEADS + h) * A_DV:2 * A_QK + (A_HEADS + h + 1) * A_DV]
    return q, k, v, o_pre


def _mlstm_prompt_kernel(main_ref, gc_ref, gr_ref, gh_ref, hs_ref, c_ref, n_ref, mt_ref, c_s, n_s, m_s):
    t = MLSTM_CHUNK
    j = pl.program_id(1)

    @pl.when(j == 0)
    def _():
        c_s[...] = jnp.zeros_like(c_s)
        n_s[...] = jnp.zeros_like(n_s)
        m_s[...] = jnp.zeros_like(m_s)

    _, _, _, lower, upper = _mlstm_masks(t, t)
    gc = gc_ref[...]
    gr = gr_ref[...]
    btc_all = _dot_exact(lower.astype(f32), gc)
    btr_all = _dot_exact(gr, upper.astype(f32))
    lane = lax.broadcasted_iota(jnp.int32, (t, LANES), 1)
    mt_all = jnp.zeros((t, LANES), f32)
    for h in range(A_HEADS):
        q, k, v, o_pre = _mlstm_cols(main_ref, h)
        qb, kb, vb = q.astype(bf16), k.astype(bf16), v.astype(bf16)
        btc = btc_all[:, A_HEADS + h:A_HEADS + h + 1]
        itc = gc[:, h:h + 1]
        btr = btr_all[A_HEADS + h:A_HEADS + h + 1, :]
        itr = gr[h:h + 1, :]
        m_in = m_s[0:1, h:h + 1]
        num, den, m_t, w_inter, inter = _mlstm_intra(qb, kb, vb, btc, btr, itr, m_in, lower)
        c_old = c_s[h]
        n_old = n_s[h:h + 1, :]
        num = num + w_inter * _dot_nt(qb, c_old.astype(bf16))
        den = den + w_inter * jnp.sum(q * n_old, axis=1, keepdims=True)
        hs_ref[:, h * A_DV:(h + 1) * A_DV] = _mlstm_head_out(num, den, m_t, o_pre, gh_ref[h:h + 1, :])
        m_new = m_t[t - 1:t, :]
        decay = jnp.exp(btc[t - 1:t, :] - btc + itc - m_new)
        scale = jnp.exp(inter[t - 1:t, :] - m_new)
        vdt = (decay * v).T.astype(bf16)
        c_s[h] = scale * c_old + _dot(vdt, kb)
        n_s[h:h + 1, :] = scale * n_old + jnp.sum(decay * k, axis=0, keepdims=True)
        mt_all = jnp.where(lane == h, m_t, mt_all)
    mt_ref[...] = mt_all
    m_s[0:1, :] = mt_all[t - 1:t, :]

    @pl.when(j == pl.num_programs(1) - 1)
    def _():
        c_ref[0] = c_s[...]
        n_ref[0] = n_s[...]


def _mlstm_prompt(main, gates, gates_t, g_head):
    nc = SEQ // MLSTM_CHUNK
    t = MLSTM_CHUNK
    return pl.pallas_call(
        _mlstm_prompt_kernel,
        grid=(BATCH, nc),
        in_specs=[pl.BlockSpec((t, A_MAIN), lambda b, j: (b * nc + j, 0)),
                  pl.BlockSpec((t, LANES), lambda b, j: (b * nc + j, 0)),
                  pl.BlockSpec((SUBLANES, t), lambda b, j: (0, b * nc + j)),
                  _const_spec((A_HEADS, A_DV))],
        out_specs=[pl.BlockSpec((t, A_HEADS * A_DV), lambda b, j: (b * nc + j, 0)),
                   pl.BlockSpec((1, A_HEADS, A_DV, A_DK), lambda b, j: (b, 0, 0, 0)),
                   pl.BlockSpec((1, A_HEADS, A_DK), lambda b, j: (b, 0, 0)),
                   pl.BlockSpec((t, LANES), lambda b, j: (b * nc + j, 0))],
        out_shape=[jax.ShapeDtypeStruct((N_P, A_HEADS * A_DV), bf16),
                   jax.ShapeDtypeStruct((BATCH, A_HEADS, A_DV, A_DK), f32),
                   jax.ShapeDtypeStruct((BATCH, A_HEADS, A_DK), f32),
                   jax.ShapeDtypeStruct((N_P, LANES), f32)],
        scratch_shapes=[pltpu.VMEM((A_HEADS, A_DV, A_DK), f32),
                        pltpu.VMEM((A_HEADS, A_DK), f32),
                        pltpu.VMEM((SUBLANES, LANES), f32)],
        compiler_params=_params("parallel", "arbitrary"),
        name="mlstm_prompt",
    )(main, gates, gates_t, g_head)


def _mlstm_sample_compute(main_ref, gc_ref, gr_ref, m0_ref, c0_ref, n0_ref, gh_ref,
                          hs_ref, n_ref, mt_ref, vdt_ref, kb_ref, sc_ref, acc_s, nt_s):
    t = MLSTM_ST
    seg = DEC_SEQ
    _, c, same, lower, upper = _mlstm_masks(t, seg)
    seg_last = (same & (_imod(c, seg) == seg - 1)).astype(f32)
    same_f = same.astype(f32)
    gc = gc_ref[...]
    gr = gr_ref[...]
    btc_all = _dot_exact(lower.astype(f32), gc)
    btr_all = _dot_exact(gr, upper.astype(f32))
    lane = lax.broadcasted_iota(jnp.int32, (t, LANES), 1)
    tok_seg = _idiv(lax.broadcasted_iota(jnp.int32, (A_DV, t), 1), seg)
    mt_all = jnp.zeros((t, LANES), f32)
    sc_all = jnp.zeros((t, LANES), f32)
    for h in range(A_HEADS):
        q, k, v, o_pre = _mlstm_cols(main_ref, h)
        qb, kb, vb = q.astype(bf16), k.astype(bf16), v.astype(bf16)
        btc = btc_all[:, A_HEADS + h:A_HEADS + h + 1]
        itc = gc[:, h:h + 1]
        btr = btr_all[A_HEADS + h:A_HEADS + h + 1, :]
        itr = gr[h:h + 1, :]
        m_in = m0_ref[:, h:h + 1]
        num, den, m_t, w_inter, inter = _mlstm_intra(qb, kb, vb, btc, btr, itr, m_in, lower)

        def inter_body(b, carry, h=h, qb=qb):
            rows = pl.ds(pl.multiple_of(b * seg, seg), seg)
            r = _dot_nt(c0_ref[b, h].astype(bf16), qb)
            acc_s[...] = jnp.where(tok_seg == b, r, acc_s[...])
            nt_s[rows, :] = jnp.broadcast_to(n0_ref[b, h:h + 1, :], (seg, A_DK))
            return carry

        acc_s[...] = jnp.zeros_like(acc_s)
        lax.fori_loop(0, MLSTM_SB, inter_body, 0, unroll=MLSTM_UNROLL)
        num = num + w_inter * acc_s[...].T
        den = den + w_inter * jnp.sum(q * nt_s[...], axis=1, keepdims=True)
        hs_ref[:, h * A_DV:(h + 1) * A_DV] = _mlstm_head_out(num, den, m_t, o_pre, gh_ref[h:h + 1, :])

        packed = jnp.where(lane == 0, m_t, jnp.where(lane == 1, btc, inter))
        last = _dot_exact(seg_last, packed)
        m_new, bt_last, inter_last = last[:, 0:1], last[:, 1:2], last[:, 2:3]
        decay = jnp.exp(bt_last - btc + itc - m_new)
        scale = jnp.exp(inter_last - m_new)
        vdt_ref[h] = (decay * v).T.astype(bf16)
        kb_ref[:, h * A_DK:(h + 1) * A_DK] = kb
        sc_all = jnp.where(lane == h, scale, sc_all)
        mt_all = jnp.where(lane == h, m_t, mt_all)
        n_new_tok = scale * nt_s[...] + _dot_exact(same_f, decay * k)
        for b in range(MLSTM_SB):
            n_ref[b, h:h + 1, :] = n_new_tok[b * seg:b * seg + 1, :]
    mt_ref[...] = mt_all
    sc_ref[...] = sc_all


def _mlstm_sample_update(c_in_ref, c_out_ref, vdt_ref, kb_ref, sc_ref):
    seg = DEC_SEQ
    lane_seg = _idiv(lax.broadcasted_iota(jnp.int32, (1, MLSTM_ST), 1), seg)
    for h in range(A_HEADS):
        def upd_body(b, carry, h=h):
            rows = pl.ds(pl.multiple_of(b * seg, seg), seg)
            scale = sc_ref[rows, :][0:1, h:h + 1]
            onehot = jnp.where(lane_seg == b, 1.0, 0.0).astype(bf16)
            upd = _dot(vdt_ref[h] * onehot, kb_ref[:, h * A_DK:(h + 1) * A_DK])
            c_out_ref[b, h] = scale * c_in_ref[b, h] + upd
            return carry

        lax.fori_loop(0, MLSTM_SB, upd_body, 0, unroll=MLSTM_UNROLL)


def _mlstm_sample_first_kernel(main_ref, gc_ref, gr_ref, m0_ref, c0_ref, n0_ref, gh_ref,
                               hs_ref, n_ref, mt_ref, vdt_ref, kb_ref, sc_ref, acc_s, nt_s):
    _mlstm_sample_compute(main_ref, gc_ref, gr_ref, m0_ref, c0_ref, n0_ref, gh_ref,
                          hs_ref, n_ref, mt_ref, vdt_ref, kb_ref, sc_ref, acc_s, nt_s)


def _mlstm_sample_second_kernel(main_ref, gc_ref, gr_ref, m0_ref, c0_ref, n0_ref, gh_ref, vdt0_ref, kb0_ref, sc0_ref,
                                hs_ref, n_ref, mt_ref, c_ref, vdt_s, kb_s, sc_s, acc_s, nt_s):
    layer = pl.program_id(0)

    @pl.when(layer == 0)
    def _():
        _mlstm_sample_update(c0_ref, c_ref, vdt0_ref, kb0_ref, sc0_ref)

    @pl.when(layer == 1)
    def _():
        _mlstm_sample_compute(main_ref, gc_ref, gr_ref, m0_ref, c0_ref, n0_ref, gh_ref,
                              hs_ref, n_ref, mt_ref, vdt_s, kb_s, sc_s, acc_s, nt_s)
        _mlstm_sample_update(c0_ref, c_ref, vdt_s, kb_s, sc_s)


def _mlstm_sample_scratch():
    t = MLSTM_ST
    return [pltpu.VMEM((A_DV, t), f32), pltpu.VMEM((t, A_DK), f32)]


def _mlstm_sample_first(main, gates, gates_t, m0_tok, c_all, n0, g_head):
    t = MLSTM_ST
    return pl.pallas_call(
        _mlstm_sample_first_kernel,
        grid=(DEC_BATCH // MLSTM_SB,),
        in_specs=[pl.BlockSpec((t, A_MAIN), lambda i: (i, 0)),
                  pl.BlockSpec((t, LANES), lambda i: (i, 0)),
                  pl.BlockSpec((SUBLANES, t), lambda i: (0, i)),
                  pl.BlockSpec((t, LANES), lambda i: (i, 0)),
                  pl.BlockSpec((None, MLSTM_SB, A_HEADS, A_DV, A_DK), lambda i: (0, i, 0, 0, 0)),
                  pl.BlockSpec((MLSTM_SB, A_HEADS, A_DK), lambda i: (i, 0, 0)),
                  _const_spec((A_HEADS, A_DV))],
        out_specs=[pl.BlockSpec((t, A_HEADS * A_DV), lambda i: (i, 0)),
                   pl.BlockSpec((MLSTM_SB, A_HEADS, A_DK), lambda i: (i, 0, 0)),
                   pl.BlockSpec((t, LANES), lambda i: (i, 0)),
                   pl.BlockSpec((A_HEADS, A_DV, t), lambda i: (0, 0, i)),
                   pl.BlockSpec((t, A_QK), lambda i: (i, 0)),
                   pl.BlockSpec((t, LANES), lambda i: (i, 0))],
        out_shape=[jax.ShapeDtypeStruct((N_S, A_HEADS * A_DV), bf16),
                   jax.ShapeDtypeStruct((DEC_BATCH, A_HEADS, A_DK), f32),
                   jax.ShapeDtypeStruct((N_S, LANES), f32),
                   jax.ShapeDtypeStruct((A_HEADS, A_DV, N_S), bf16),
                   jax.ShapeDtypeStruct((N_S, A_QK), bf16),
                   jax.ShapeDtypeStruct((N_S, LANES), f32)],
        scratch_shapes=_mlstm_sample_scratch(),
        compiler_params=_params("parallel"),
        name="mlstm_sample_first",
    )(main, gates, gates_t, m0_tok, c_all, n0, g_head)


def _mlstm_sample_second(main, gates, gates_t, m0_tok, c_all, n0, g_head, vdt0, kb0, sc0):
    t = MLSTM_ST
    own = lambda l, i: i * l
    first = lambda l, i: i * (1 - l)
    return pl.pallas_call(
        _mlstm_sample_second_kernel,
        grid=(N_A, DEC_BATCH // MLSTM_SB),
        in_specs=[pl.BlockSpec((t, A_MAIN), lambda l, i: (own(l, i), 0)),
                  pl.BlockSpec((t, LANES), lambda l, i: (own(l, i), 0)),
                  pl.BlockSpec((SUBLANES, t), lambda l, i: (0, own(l, i))),
                  pl.BlockSpec((t, LANES), lambda l, i: (own(l, i), 0)),
                  pl.BlockSpec((None, MLSTM_SB, A_HEADS, A_DV, A_DK), lambda l, i: (l, i, 0, 0, 0)),
                  pl.BlockSpec((MLSTM_SB, A_HEADS, A_DK), lambda l, i: (own(l, i), 0, 0)),
                  _const_spec((A_HEADS, A_DV)),
                  pl.BlockSpec((A_HEADS, A_DV, t), lambda l, i: (0, 0, first(l, i))),
                  pl.BlockSpec((t, A_QK), lambda l, i: (first(l, i), 0)),
                  pl.BlockSpec((t, LANES), lambda l, i: (first(l, i), 0))],
        out_specs=[pl.BlockSpec((t, A_HEADS * A_DV), lambda l, i: (own(l, i), 0)),
                   pl.BlockSpec((MLSTM_SB, A_HEADS, A_DK), lambda l, i: (own(l, i), 0, 0)),
                   pl.BlockSpec((t, LANES), lambda l, i: (own(l, i), 0)),
                   pl.BlockSpec((None, MLSTM_SB, A_HEADS, A_DV, A_DK), lambda l, i: (l, i, 0, 0, 0))],
        out_shape=[jax.ShapeDtypeStruct((N_S, A_HEADS * A_DV), bf16),
                   jax.ShapeDtypeStruct((DEC_BATCH, A_HEADS, A_DK), f32),
                   jax.ShapeDtypeStruct((N_S, LANES), f32),
                   jax.ShapeDtypeStruct((N_A, DEC_BATCH, A_HEADS, A_DV, A_DK), f32)],
        scratch_shapes=[pltpu.VMEM((A_HEADS, A_DV, t), bf16),
                        pltpu.VMEM((t, A_QK), bf16),
                        pltpu.VMEM((t, LANES), f32)] + _mlstm_sample_scratch(),
        compiler_params=_params("arbitrary", "arbitrary"),
        name="mlstm_sample_second",
    )(main, gates, gates_t, m0_tok, c_all, n0, g_head, vdt0, kb0, sc0)


def _swa_softmax_pv(s, allowed, sink_col, vb):
    s = jnp.where(allowed, s * (B_HD ** -0.5), -jnp.inf)
    m = jnp.maximum(jnp.max(s, axis=1, keepdims=True), sink_col)
    p = jnp.exp(s - m)
    den = jnp.sum(p, axis=1, keepdims=True) + jnp.exp(sink_col - m)
    return _dot((p / den).astype(bf16), vb)


def _swa_prompt_kernel(sink_ref, q_ref, kp_ref, ko_ref, vp_ref, vo_ref, o_ref):
    n = pl.program_id(1)
    w = WINDOW
    kb = jnp.concatenate([kp_ref[...], ko_ref[...]], axis=0).astype(bf16)
    vb = jnp.concatenate([vp_ref[...], vo_ref[...]], axis=0).astype(bf16)
    rows = B_GROUP * w
    t = _imod(lax.broadcasted_iota(jnp.int32, (rows, 2 * w), 0), w)
    jj = lax.broadcasted_iota(jnp.int32, (rows, 2 * w), 1)
    allowed = (jj > t) & (jj <= t + w) & ((n > 0) | (jj >= w))
    lane_head = _idiv(lax.broadcasted_iota(jnp.int32, (1, B_KV), 1), B_HD)
    grp = _idiv(lax.broadcasted_iota(jnp.int32, (rows, 1), 0), w)
    outs = [jnp.zeros((w, B_KV), f32) for _ in range(B_GROUP)]
    for h in range(B_KV_HEADS):
        hm = lane_head == h
        qh = jnp.concatenate([jnp.where(hm, q_ref[:, g * B_KV:(g + 1) * B_KV], 0.0) for g in range(B_GROUP)],
                             axis=0).astype(bf16)
        sink_col = jnp.zeros((rows, 1), f32)
        for g in range(B_GROUP):
            sink_col = jnp.where(grp == g, sink_ref[h * B_GROUP + g], sink_col)
        r = _swa_softmax_pv(_dot_nt(qh, kb), allowed, sink_col, vb)
        for g in range(B_GROUP):
            outs[g] = outs[g] + jnp.where(hm, r[g * w:(g + 1) * w, :], 0.0)
    for g in range(B_GROUP):
        o_ref[:, g * B_KV:(g + 1) * B_KV] = outs[g].astype(bf16)


def _swa_prompt(proj, sinks):
    nb = SEQ // WINDOW
    w = WINDOW
    kcol = B_Q // B_KV
    vcol = kcol + 1
    prev = lambda b, n: b * nb + jnp.maximum(n - 1, 0)
    return pl.pallas_call(
        _swa_prompt_kernel,
        grid=(BATCH, nb),
        in_specs=[pl.BlockSpec(memory_space=pltpu.SMEM),
                  pl.BlockSpec((w, B_Q), lambda b, n: (b * nb + n, 0)),
                  pl.BlockSpec((w, B_KV), lambda b, n: (prev(b, n), kcol)),
                  pl.BlockSpec((w, B_KV), lambda b, n: (b * nb + n, kcol)),
                  pl.BlockSpec((w, B_KV), lambda b, n: (prev(b, n), vcol)),
                  pl.BlockSpec((w, B_KV), lambda b, n: (b * nb + n, vcol))],
        out_specs=pl.BlockSpec((w, B_Q), lambda b, n: (b * nb + n, 0)),
        out_shape=jax.ShapeDtypeStruct((N_P, B_Q), bf16),
        compiler_params=_params("parallel", "arbitrary"),
        name="swa_prompt",
    )(sinks, proj, proj, proj, proj, proj)


def _swa_sample_kernel(sink_ref, q_ref, kn_ref, vn_ref, kc_ref, vc_ref, o_ref, ko_ref, vo_ref):
    s_len = DEC_SEQ
    buf = WINDOW
    rows = B_HEADS * s_len
    keys = 2 * buf
    ri = lax.broadcasted_iota(jnp.int32, (rows, keys), 0)
    jj = lax.broadcasted_iota(jnp.int32, (rows, keys), 1)
    t = _imod(ri, s_len)
    allowed = ((jj < buf) & (jj > t)) | ((jj >= buf) & (jj - buf <= t))
    lane_head = _idiv(lax.broadcasted_iota(jnp.int32, (1, B_KV), 1), B_HD)
    hg = _idiv(lax.broadcasted_iota(jnp.int32, (rows, 1), 0), s_len)
    sink_col = jnp.zeros((rows, 1), f32)
    for i in range(B_HEADS):
        sink_col = jnp.where(hg == i, sink_ref[i], sink_col)
    pad = jnp.zeros((keys - buf - s_len, B_KV), f32)

    def body(e, carry):
        rs = pl.ds(pl.multiple_of(e * s_len, s_len), s_len)
        kn = kn_ref[rs, :]
        vn = vn_ref[rs, :]
        kc = kc_ref[e]
        vc = vc_ref[e]
        kpad = jnp.concatenate([kc, kn, pad], axis=0).astype(bf16)
        vpad = jnp.concatenate([vc, vn, pad], axis=0).astype(bf16)
        ko_ref[e, 0:buf - s_len, :] = kc[s_len:, :]
        ko_ref[e, buf - s_len:, :] = kn
        vo_ref[e, 0:buf - s_len, :] = vc[s_len:, :]
        vo_ref[e, buf - s_len:, :] = vn
        qe = q_ref[rs, :]
        qbig = jnp.concatenate(
            [jnp.where(lane_head == h, qe[:, g * B_KV:(g + 1) * B_KV], 0.0)
             for h in range(B_KV_HEADS) for g in range(B_GROUP)], axis=0).astype(bf16)
        r = _swa_softmax_pv(_dot_nt(qbig, kpad), allowed, sink_col, vpad)
        for g in range(B_GROUP):
            og = jnp.zeros((s_len, B_KV), f32)
            for h in range(B_KV_HEADS):
                blk = (h * B_GROUP + g) * s_len
                og = og + jnp.where(lane_head == h, r[blk:blk + s_len, :], 0.0)
            o_ref[rs, g * B_KV:(g + 1) * B_KV] = og.astype(bf16)
        return carry

    lax.fori_loop(0, SWA_SB, body, 0, unroll=4)


def _swa_sample(proj, sinks, k_cache, v_cache):
    t = SWA_SB * DEC_SEQ
    kcol = B_Q // B_KV
    return pl.pallas_call(
        _swa_sample_kernel,
        grid=(DEC_BATCH // SWA_SB,),
        in_specs=[pl.BlockSpec(memory_space=pltpu.SMEM),
                  pl.BlockSpec((t, B_Q), lambda i: (i, 0)),
                  pl.BlockSpec((t, B_KV), lambda i: (i, kcol)),
                  pl.BlockSpec((t, B_KV), lambda i: (i, kcol + 1)),
                  pl.BlockSpec((SWA_SB, WINDOW, B_KV), lambda i: (i, 0, 0)),
                  pl.BlockSpec((SWA_SB, WINDOW, B_KV), lambda i: (i, 0, 0))],
        out_specs=[pl.BlockSpec((t, B_Q), lambda i: (i, 0)),
                   pl.BlockSpec((SWA_SB, WINDOW, B_KV), lambda i: (i, 0, 0)),
                   pl.BlockSpec((SWA_SB, WINDOW, B_KV), lambda i: (i, 0, 0))],
        out_shape=[jax.ShapeDtypeStruct((N_S, B_Q), bf16),
                   jax.ShapeDtypeStruct((DEC_BATCH, WINDOW, B_KV), f32),
                   jax.ShapeDtypeStruct((DEC_BATCH, WINDOW, B_KV), f32)],
        compiler_params=_params("parallel"),
        name="swa_sample",
    )(sinks, proj, proj, proj, k_cache, v_cache)


def _rg_conv_group(x8, p8, wc_ref, bc_ref):
    row = lax.broadcasted_iota(jnp.int32, x8.shape, 0)
    u = bc_ref[...] + wc_ref[CONV_W - 1:CONV_W, :] * x8
    for d in range(1, CONV_W):
        sh = jnp.where(row >= d, pltpu.roll(x8, d, 0), pltpu.roll(p8, d, 0))
        u = u + wc_ref[CONV_W - 1 - d:CONV_W - d, :] * sh
    return u


def _rg_scan_group(a8, b8, carry):
    row = lax.broadcasted_iota(jnp.int32, a8.shape, 0)
    for d in (1, 2, 4):
        keep = row >= d
        b8 = jnp.where(keep, a8 * pltpu.roll(b8, d, 0) + b8, b8)
        a8 = jnp.where(keep, a8 * pltpu.roll(a8, d, 0), a8)
    return a8 * carry + b8


def _rg_gates(u, gate, wa_ref, ba_ref, wx_ref, bx_ref, lam_ref):
    ub = u.astype(bf16)
    ra = jnp.concatenate([_dot(ub[:, n * C_BW:(n + 1) * C_BW], wa_ref[n]) for n in range(C_BLOCKS)], axis=1)
    rx = jnp.concatenate([_dot(ub[:, n * C_BW:(n + 1) * C_BW], wx_ref[n]) for n in range(C_BLOCKS)], axis=1)
    r = _sigmoid(ra + ba_ref[...])
    gi = _sigmoid(rx + bx_ref[...])
    log_a = -LRU_C * r * _softplus(-lam_ref[...])
    a = jnp.exp(log_a)
    bterm = jnp.sqrt(-jnp.tanh(log_a) * (a * a + 1.0)) * gi * u
    return a, bterm, _gelu_tanh(gate)


def _rg_prompt_kernel(p_ref, wc_ref, bc_ref, wa_ref, ba_ref, wx_ref, bx_ref, lam_ref,
                      y_ref, h_ref, u_s, a_s, b_s, xc_s, hc_s):
    j = pl.program_id(1)
    ng = RG_T // SUBLANES

    @pl.when(j == 0)
    def _():
        xc_s[...] = jnp.zeros_like(xc_s)
        hc_s[...] = jnp.zeros_like(hc_s)

    def conv_body(gidx, p8):
        rows = pl.ds(pl.multiple_of(gidx * SUBLANES, SUBLANES), SUBLANES)
        x8 = p_ref[rows, 0:C_WIDTH]
        u_s[rows, :] = _rg_conv_group(x8, p8, wc_ref, bc_ref)
        return x8

    xc_s[...] = lax.fori_loop(0, ng, conv_body, xc_s[...])
    a, bterm, gg = _rg_gates(u_s[...], p_ref[:, C_WIDTH:], wa_ref, ba_ref, wx_ref, bx_ref, lam_ref)
    a_s[...] = a
    b_s[...] = bterm

    def scan_body(gidx, carry):
        rows = pl.ds(pl.multiple_of(gidx * SUBLANES, SUBLANES), SUBLANES)
        h8 = _rg_scan_group(a_s[rows, :], b_s[rows, :], carry)
        u_s[rows, :] = h8
        return h8[SUBLANES - 1:SUBLANES, :]

    h_last = lax.fori_loop(0, ng, scan_body, hc_s[...])
    hc_s[...] = h_last
    y_ref[...] = (u_s[...] * gg).astype(bf16)

    @pl.when(j == pl.num_programs(1) - 1)
    def _():
        h_ref[0] = h_last


def _rg_weight_specs():
    return [_const_spec((CONV_W, C_WIDTH)), _const_spec((1, C_WIDTH)),
            _const_spec((C_BLOCKS, C_BW, C_BW)), _const_spec((1, C_WIDTH)),
            _const_spec((C_BLOCKS, C_BW, C_BW)), _const_spec((1, C_WIDTH)),
            _const_spec((1, C_WIDTH))]


def _rg_prompt(proj, weights):
    nt = SEQ // RG_T
    return pl.pallas_call(
        _rg_prompt_kernel,
        grid=(BATCH, nt),
        in_specs=[pl.BlockSpec((RG_T, 2 * C_WIDTH), lambda b, j: (b * nt + j, 0))] + _rg_weight_specs(),
        out_specs=[pl.BlockSpec((RG_T, C_WIDTH), lambda b, j: (b * nt + j, 0)),
                   pl.BlockSpec((1, 1, C_WIDTH), lambda b, j: (b, 0, 0))],
        out_shape=[jax.ShapeDtypeStruct((N_P, C_WIDTH), bf16),
                   jax.ShapeDtypeStruct((BATCH, 1, C_WIDTH), f32)],
        scratch_shapes=[pltpu.VMEM((RG_T, C_WIDTH), f32),
                        pltpu.VMEM((RG_T, C_WIDTH), f32),
                        pltpu.VMEM((RG_T, C_WIDTH), f32),
                        pltpu.VMEM((SUBLANES, C_WIDTH), f32),
                        pltpu.VMEM((1, C_WIDTH), f32)],
        compiler_params=_params("parallel", "arbitrary"),
        name="rg_prompt",
    )(proj, *weights)


def _rg_sample_kernel(p_ref, cv_ref, h0_ref, wc_ref, bc_ref, wa_ref, ba_ref, wx_ref, bx_ref, lam_ref,
                      y_ref, h_ref, u_s, a_s, b_s):
    def conv_body(gidx, carry):
        rows = pl.ds(pl.multiple_of(gidx * SUBLANES, SUBLANES), SUBLANES)
        u_s[rows, :] = _rg_conv_group(p_ref[rows, 0:C_WIDTH], cv_ref[rows, :], wc_ref, bc_ref)
        return carry

    lax.fori_loop(0, DEC_BATCH, conv_body, 0)
    a, bterm, gg = _rg_gates(u_s[...], p_ref[:, C_WIDTH:], wa_ref, ba_ref, wx_ref, bx_ref, lam_ref)
    a_s[...] = a
    b_s[...] = bterm

    def scan_body(gidx, carry):
        rows = pl.ds(pl.multiple_of(gidx * SUBLANES, SUBLANES), SUBLANES)
        h8 = _rg_scan_group(a_s[rows, :], b_s[rows, :], h0_ref[pl.ds(gidx, 1), :])
        u_s[rows, :] = h8
        h_ref[pl.ds(gidx, 1), :] = h8[SUBLANES - 1:SUBLANES, :]
        return carry

    lax.fori_loop(0, DEC_BATCH, scan_body, 0)
    y_ref[...] = (u_s[...] * gg).astype(bf16)


def _rg_sample(proj, conv_pad, h0, weights):
    return pl.pallas_call(
        _rg_sample_kernel,
        grid=(1,),
        in_specs=[_const_spec((N_S, 2 * C_WIDTH)),
                  _const_spec((N_S, C_WIDTH)),
                  _const_spec((DEC_BATCH, C_WIDTH))] + _rg_weight_specs(),
        out_specs=[_const_spec((N_S, C_WIDTH)), _const_spec((DEC_BATCH, C_WIDTH))],
        out_shape=[jax.ShapeDtypeStruct((N_S, C_WIDTH), bf16),
                   jax.ShapeDtypeStruct((DEC_BATCH, C_WIDTH), f32)],
        scratch_shapes=[pltpu.VMEM((N_S, C_WIDTH), f32),
                        pltpu.VMEM((N_S, C_WIDTH), f32),
                        pltpu.VMEM((N_S, C_WIDTH), f32)],
        compiler_params=_params("arbitrary"),
        name="rg_sample",
    )(proj, conv_pad, h0, *weights)


def _out_mlp_kernel(xp_ref, xs_ref, ap_ref, as_ref, wo_ref, bo_ref, g_ref, wup_ref, wdn_ref, gf_ref,
                    op_ref, os_ref, x1_s, xn_s, *, final):
    def body(x_ref, a_ref, o_ref):
        x1_s[...] = x_ref[...] + _dot(a_ref[...], wo_ref[...]) + bo_ref[...]
        xn_s[...] = _rms_bf16(x1_s[...], g_ref[...])
        for c in range(0, D_FF, FF_CHUNK):
            hmid = jnp.maximum(_dot(xn_s[...], wup_ref[:, c:c + FF_CHUNK]), 0.0)
            x1_s[...] += _dot((hmid * hmid).astype(bf16), wdn_ref[c:c + FF_CHUNK, :])
        if final:
            x1 = x1_s[...]
            y = x1 * lax.rsqrt(jnp.mean(x1 * x1, axis=-1, keepdims=True) + NORM_EPS)
            o_ref[...] = y * gf_ref[...]
        else:
            o_ref[...] = x1_s[...]

    _for_each_group(body, (xp_ref, ap_ref, op_ref), (xs_ref, as_ref, os_ref))


def _out_mlp(x_p, x_s, a_p, a_s, wo, bo, g, w_up_all, w_down_all, layer, g_final, final):
    single = pl.Buffered(1)
    wspec = lambda shape: pl.BlockSpec(shape, lambda i: (0, 0), pipeline_mode=single)
    lspec = lambda shape: pl.BlockSpec((None,) + shape, lambda i: (layer, 0, 0), pipeline_mode=single)
    return pl.pallas_call(
        functools.partial(_out_mlp_kernel, final=final),
        grid=(NP_TILES + NS_TILES,),
        in_specs=[_p_spec(D_MODEL), _s_spec(D_MODEL), _p_spec(D_MODEL), _s_spec(D_MODEL),
                  wspec((D_MODEL, D_MODEL)),
                  wspec((1, D_MODEL)),
                  wspec((1, D_MODEL)),
                  lspec((D_MODEL, D_FF)),
                  lspec((D_FF, D_MODEL)),
                  wspec((1, D_MODEL))],
        out_specs=[_p_spec(D_MODEL), _s_spec(D_MODEL)],
        out_shape=[jax.ShapeDtypeStruct((N_P, D_MODEL), f32), jax.ShapeDtypeStruct((N_S, D_MODEL), f32)],
        scratch_shapes=[pltpu.VMEM((TM, D_MODEL), f32),
                        pltpu.VMEM((TM, D_MODEL), bf16)],
        compiler_params=_params("arbitrary"),
        name="out_mlp",
    )(x_p, x_s, a_p, a_s, wo, bo, g, w_up_all, w_down_all, g_final)


def _rope_tables():
    half = ROPE_DIM // 2
    inv = ROPE_THETA ** (-jnp.arange(0, ROPE_DIM, 2, dtype=f32) / ROPE_DIM)
    lane = jnp.arange(LANES) % B_HD
    inv_lane = jnp.where(lane < ROPE_DIM, inv[lane % half], 0.0)
    pos = jnp.concatenate([jnp.arange(SEQ, dtype=jnp.int32), PAST_LEN + jnp.arange(TM, dtype=jnp.int32) % DEC_SEQ])
    ang = pos.astype(f32)[:, None] * inv_lane[None, :]
    tab = jnp.concatenate([jnp.cos(ang), jnp.sin(ang)], axis=1)
    lo = jnp.where(lane < half, -1.0, 0.0)
    hi = jnp.where((lane >= half) & (lane < ROPE_DIM), 1.0, 0.0)
    consts = jnp.zeros((SUBLANES, LANES), f32).at[0].set(lo).at[1].set(hi)
    return tab, consts


def _q_perm():
    g, h, d = jnp.meshgrid(jnp.arange(B_GROUP), jnp.arange(B_KV_HEADS), jnp.arange(B_HD), indexing="ij")
    return ((h * B_GROUP + g) * B_HD + d).reshape(-1)


def _last_rows(arr, n_seq, seq_len, n_rows, col0, col1):
    return jnp.stack([arr[(s + 1) * seq_len - n_rows:(s + 1) * seq_len, col0:col1] for s in range(n_seq)])


def kernel(x_prompt, x_sample, state_mlstm_c, state_mlstm_n, state_mlstm_m, cache_swa_k, cache_swa_v,
           state_rglru_h, state_rglru_conv, norm_mix, norm_mlp, norm_final, w_mlp_up, w_mlp_down,
           w_mlstm_in, b_mlstm_i, b_mlstm_f, g_mlstm_head, w_mlstm_out, w_swa_qkv, b_swa_qkv, swa_sinks,
           w_swa_out, b_swa_out, w_rg_in, w_rg_conv, b_rg_conv, w_rg_a, b_rg_a, w_rg_x, b_rg_x, rg_lambda,
           w_rg_out):
    assert N_A == 2
    x_p = x_prompt.reshape(N_P, D_MODEL)
    x_s = x_sample.reshape(N_S, D_MODEL)
    zero_bias = jnp.zeros((1, D_MODEL), f32)
    row = lambda v: v.reshape(1, -1).astype(f32)
    w_up_all = w_mlp_up.astype(bf16)
    w_down_all = w_mlp_down.astype(bf16)
    state_c = state_mlstm_c.astype(f32)
    outs = {k: [] for k in ("c_p", "n_p", "m_p", "n_s", "m_s", "k_p", "v_p", "k_s", "v_s",
                            "h_p", "cv_p", "h_s", "cv_s")}
    deferred = None
    c_s_all = None
    for i in range(DEPTH):
        kind, j = i % N_MIXERS, i // N_MIXERS
        g_mix = row(norm_mix[i])
        if kind == 0:
            w_in = w_mlstm_in[j]
            n_gate = 2 * A_HEADS
            w_gate = jnp.pad(w_in[:, A_MAIN:], ((0, 0), (0, LANES - n_gate))).astype(bf16)
            b_gate = jnp.pad(jnp.concatenate([b_mlstm_i[j], b_mlstm_f[j]]), (0, LANES - n_gate)).reshape(1, LANES)
            main_p, main_s, gates_p, gates_s = _proj_mlstm(x_p, x_s, g_mix, w_in[:, :A_MAIN].astype(bf16), w_gate,
                                                           b_gate.astype(f32))
            g_head = g_mlstm_head[j].astype(f32)
            a_p, c_p, n_p, mt_p = _mlstm_prompt(main_p, gates_p, gates_p[:, :SUBLANES].T, g_head)
            m0_tok = jnp.pad(jnp.repeat(state_mlstm_m[j].astype(f32), DEC_SEQ, axis=0),
                             ((0, 0), (0, LANES - A_HEADS)))
            n0 = state_mlstm_n[j].astype(f32)
            gates_s_t = gates_s[:, :SUBLANES].T
            if j == 0:
                a_s, n_s, mt_s, vdt0, kb0, sc0 = _mlstm_sample_first(main_s, gates_s, gates_s_t, m0_tok, state_c, n0,
                                                                     g_head)
                deferred = (vdt0, kb0, sc0)
            else:
                a_s, n_s, mt_s, c_s_all = _mlstm_sample_second(main_s, gates_s, gates_s_t, m0_tok, state_c, n0,
                                                               g_head, *deferred)
            outs["c_p"].append(c_p); outs["n_p"].append(n_p)
            outs["m_p"].append(_last_rows(mt_p, BATCH, SEQ, 1, 0, A_HEADS).reshape(BATCH, A_HEADS))
            outs["n_s"].append(n_s)
            outs["m_s"].append(mt_s[DEC_SEQ - 1::DEC_SEQ, :A_HEADS])
            wo, bo = w_mlstm_out[j], zero_bias
        elif kind == 1:
            perm = _q_perm()
            col_perm = jnp.concatenate([perm, jnp.arange(B_Q, B_IN)])
            rope_tab, rope_consts = _rope_tables()
            proj_p, proj_s = _proj_swa(x_p, x_s, g_mix, w_swa_qkv[j][:, col_perm].astype(bf16),
                                       row(b_swa_qkv[j][col_perm]), rope_tab, rope_consts)
            sinks = swa_sinks[j].astype(f32)
            a_p = _swa_prompt(proj_p, sinks)
            buf = cache_swa_k.shape[2]
            a_s, k_s, v_s = _swa_sample(proj_s, sinks,
                                        cache_swa_k[j].astype(f32).reshape(DEC_BATCH, buf, B_KV),
                                        cache_swa_v[j].astype(f32).reshape(DEC_BATCH, buf, B_KV))
            outs["k_p"].append(_last_rows(proj_p, BATCH, SEQ, WINDOW, B_Q, B_Q + B_KV)
                               .reshape(BATCH, WINDOW, B_KV_HEADS, B_HD))
            outs["v_p"].append(_last_rows(proj_p, BATCH, SEQ, WINDOW, B_Q + B_KV, B_IN)
                               .reshape(BATCH, WINDOW, B_KV_HEADS, B_HD))
            outs["k_s"].append(k_s.reshape(DEC_BATCH, buf, B_KV_HEADS, B_HD))
            outs["v_s"].append(v_s.reshape(DEC_BATCH, buf, B_KV_HEADS, B_HD))
            wo, bo = w_swa_out[j][perm, :], row(b_swa_out[j])
        else:
            proj_p, proj_s = _proj_rg(x_p, x_s, g_mix, w_rg_in[j].astype(bf16))
            weights = (w_rg_conv[j].astype(f32), row(b_rg_conv[j]), w_rg_a[j].astype(bf16), row(b_rg_a[j]),
                       w_rg_x[j].astype(bf16), row(b_rg_x[j]), row(rg_lambda[j]))
            a_p, h_p = _rg_prompt(proj_p, weights)
            conv_pad = jnp.pad(state_rglru_conv[j].astype(f32),
                               ((0, 0), (SUBLANES - (CONV_W - 1), 0), (0, 0))).reshape(N_S, C_WIDTH)
            a_s, h_s = _rg_sample(proj_s, conv_pad, state_rglru_h[j].astype(f32), weights)
            outs["h_p"].append(h_p.reshape(BATCH, C_WIDTH))
            outs["cv_p"].append(_last_rows(proj_p, BATCH, SEQ, CONV_W - 1, 0, C_WIDTH))
            outs["h_s"].append(h_s)
            outs["cv_s"].append(proj_s[:, :C_WIDTH].reshape(DEC_BATCH, DEC_SEQ, C_WIDTH)[:, DEC_SEQ - (CONV_W - 1):])
            wo, bo = w_rg_out[j], zero_bias
        x_p, x_s = _out_mlp(x_p, x_s, a_p, a_s, wo.astype(bf16), bo, row(norm_mlp[i]), w_up_all, w_down_all, i,
                            row(norm_final), final=(i == DEPTH - 1))
    st = {k: jnp.stack(v) for k, v in outs.items()}
    y_p = x_p.reshape(BATCH, SEQ, D_MODEL)
    y_s = x_s.reshape(DEC_BATCH, DEC_SEQ, D_MODEL)
    return (y_p, y_s, st["c_p"], st["n_p"], st["m_p"], c_s_all, st["n_s"], st["m_s"],
            st["k_p"], st["v_p"], st["k_s"], st["v_s"], st["h_p"], st["cv_p"], st["h_s"], st["cv_s"])
```

```python
import functools
import math

import jax
import jax.numpy as jnp
from jax import lax
from jax.experimental import pallas as pl
from jax.experimental.pallas import tpu as pltpu

f32 = jnp.float32
bf16 = jnp.bfloat16

D_MODEL = 1024
BATCH = 2
SEQ = 8192
DEPTH = 4
DEC_BATCH = 128
DEC_SEQ = 8
PAST_LEN = 8192
N_MIXERS = 3
NORM_EPS = 1e-6

A_HEADS = 4
A_DK = 128
A_DV = 256
A_QK = A_HEADS * A_DK
A_MAIN = 2 * A_QK + 2 * A_HEADS * A_DV
N_A = (DEPTH + 2) // 3

B_HEADS = 16
B_KV_HEADS = 4
B_HD = 64
B_GROUP = 4
B_Q = B_HEADS * B_HD
B_KV = B_KV_HEADS * B_HD
B_IN = B_Q + 2 * B_KV
WINDOW = 128
ROPE_THETA = 500000.0
ROPE_DIM = 16

C_WIDTH = 1024
C_BLOCKS = 4
C_BW = 256
CONV_W = 4
LRU_C = 8.0
D_FF = 4096

N_P = BATCH * SEQ
N_S = DEC_BATCH * DEC_SEQ

LANES = 128
SUBLANES = 8
VMEM_LIMIT = 56 * 1024 * 1024

TM = 512
NP_TILES = N_P // TM
NS_TILES = N_S // TM
MLSTM_CHUNK = 256
MLSTM_SB = 16
MLSTM_ST = MLSTM_SB * DEC_SEQ
MLSTM_UNROLL = 4
SWA_SB = 8
RG_T = 512
FF_CHUNK = 512


def _dot(a, b):
    return jnp.dot(a, b, preferred_element_type=f32)


def _dot_nt(a, b):
    return lax.dot_general(a, b, (((1,), (1,)), ((), ())), preferred_element_type=f32)


def _dot_exact(a, b):
    return jnp.dot(a, b, preferred_element_type=f32, precision=lax.Precision.HIGHEST)


def _idiv(x, d):
    assert d & (d - 1) == 0
    return x >> (d.bit_length() - 1)


def _imod(x, d):
    assert d & (d - 1) == 0
    return x & (d - 1)


def _rms_bf16(x, g):
    y = x * lax.rsqrt(jnp.mean(x * x, axis=-1, keepdims=True) + NORM_EPS)
    return (y * g).astype(bf16)


def _sigmoid(x):
    return 0.5 * jnp.tanh(0.5 * x) + 0.5


def _sqrt_nonneg(v):
    return jnp.where(v > 0.0, v * lax.rsqrt(v), 0.0)


def _softplus(x):
    return jnp.maximum(x, 0.0) + jnp.log1p(jnp.exp(-jnp.abs(x)))


def _gelu_tanh(x):
    return 0.5 * x * (1.0 + jnp.tanh(math.sqrt(2.0 / math.pi) * (x + 0.044715 * (x * x * x))))


def _params(*sem):
    return pltpu.CompilerParams(dimension_semantics=sem, vmem_limit_bytes=VMEM_LIMIT)


def _const_spec(shape):
    nd = len(shape)
    return pl.BlockSpec(shape, lambda *_: (0,) * nd)


def _p_spec(width):
    return pl.BlockSpec((TM, width), lambda i: (jnp.minimum(i, NP_TILES - 1), 0))


def _s_spec(width):
    return pl.BlockSpec((TM, width), lambda i: (jnp.maximum(i - NP_TILES, 0), 0))


def _for_each_group(body, p_refs, s_refs):
    i = pl.program_id(0)

    @pl.when(i < NP_TILES)
    def _():
        body(*p_refs)

    @pl.when(i >= NP_TILES)
    def _():
        body(*s_refs)


def _proj_mlstm_kernel(xp_ref, xs_ref, g_ref, w_ref, wg_ref, bg_ref, mp_ref, ms_ref, gp_ref, gs_ref):
    def body(x_ref, main_ref, gate_ref):
        xn = _rms_bf16(x_ref[...], g_ref[...])
        for c in range(0, A_MAIN, 512):
            main_ref[:, c:c + 512] = _dot(xn, w_ref[:, c:c + 512])
        gp = _dot(xn, wg_ref[...]) + bg_ref[...]
        lane = lax.broadcasted_iota(jnp.int32, gp.shape, 1)
        gate_ref[...] = jnp.where(lane >= A_HEADS, -_softplus(-gp), gp)

    _for_each_group(body, (xp_ref, mp_ref, gp_ref), (xs_ref, ms_ref, gs_ref))


def _proj_mlstm(x_p, x_s, g, w_main, w_gate, b_gate):
    return pl.pallas_call(
        _proj_mlstm_kernel,
        grid=(NP_TILES + NS_TILES,),
        in_specs=[_p_spec(D_MODEL), _s_spec(D_MODEL),
                  _const_spec((1, D_MODEL)),
                  _const_spec((D_MODEL, A_MAIN)),
                  _const_spec((D_MODEL, LANES)),
                  _const_spec((1, LANES))],
        out_specs=[_p_spec(A_MAIN), _s_spec(A_MAIN), _p_spec(LANES), _s_spec(LANES)],
        out_shape=[jax.ShapeDtypeStruct((N_P, A_MAIN), f32), jax.ShapeDtypeStruct((N_S, A_MAIN), f32),
                   jax.ShapeDtypeStruct((N_P, LANES), f32), jax.ShapeDtypeStruct((N_S, LANES), f32)],
        compiler_params=_params("arbitrary"),
        name="proj_mlstm",
    )(x_p, x_s, g, w_main, w_gate, b_gate)


def _proj_swa_kernel(xp_ref, xs_ref, g_ref, w_ref, b_ref, tab_ref, rc_ref, op_ref, os_ref):
    half = ROPE_DIM // 2

    def body(x_ref, o_ref):
        xn = _rms_bf16(x_ref[...], g_ref[...])
        cos = tab_ref[:, 0:LANES]
        sin = tab_ref[:, LANES:2 * LANES]
        sin_lo = sin * rc_ref[0:1, :]
        sin_hi = sin * rc_ref[1:2, :]
        for c in range(0, B_IN, 512):
            p = _dot(xn, w_ref[:, c:c + 512]) + b_ref[:, c:c + 512]
            for l in range(0, 512, LANES):
                pl_ = p[:, l:l + LANES]
                if c + l < B_Q + B_KV:
                    pl_ = (pl_ * cos + pltpu.roll(pl_, LANES - half, 1) * sin_lo
                           + pltpu.roll(pl_, half, 1) * sin_hi)
                o_ref[:, c + l:c + l + LANES] = pl_

    _for_each_group(body, (xp_ref, op_ref), (xs_ref, os_ref))


def _proj_swa(x_p, x_s, g, w, b, rope_tab, rope_consts):
    seq_tiles = SEQ // TM
    tab_idx = lambda i: (jnp.where(i < NP_TILES, i % seq_tiles, seq_tiles), 0)
    return pl.pallas_call(
        _proj_swa_kernel,
        grid=(NP_TILES + NS_TILES,),
        in_specs=[_p_spec(D_MODEL), _s_spec(D_MODEL),
                  _const_spec((1, D_MODEL)),
                  _const_spec((D_MODEL, B_IN)),
                  _const_spec((1, B_IN)),
                  pl.BlockSpec((TM, 2 * LANES), tab_idx),
                  _const_spec((SUBLANES, LANES))],
        out_specs=[_p_spec(B_IN), _s_spec(B_IN)],
        out_shape=[jax.ShapeDtypeStruct((N_P, B_IN), f32), jax.ShapeDtypeStruct((N_S, B_IN), f32)],
        compiler_params=_params("arbitrary"),
        name="proj_swa",
    )(x_p, x_s, g, w, b, rope_tab, rope_consts)


def _proj_rg_kernel(xp_ref, xs_ref, g_ref, w_ref, op_ref, os_ref):
    def body(x_ref, o_ref):
        xn = _rms_bf16(x_ref[...], g_ref[...])
        for c in range(0, 2 * C_WIDTH, 512):
            o_ref[:, c:c + 512] = _dot(xn, w_ref[:, c:c + 512])

    _for_each_group(body, (xp_ref, op_ref), (xs_ref, os_ref))


def _proj_rg(x_p, x_s, g, w):
    return pl.pallas_call(
        _proj_rg_kernel,
        grid=(NP_TILES + NS_TILES,),
        in_specs=[_p_spec(D_MODEL), _s_spec(D_MODEL),
                  _const_spec((1, D_MODEL)),
                  _const_spec((D_MODEL, 2 * C_WIDTH))],
        out_specs=[_p_spec(2 * C_WIDTH), _s_spec(2 * C_WIDTH)],
        out_shape=[jax.ShapeDtypeStruct((N_P, 2 * C_WIDTH), f32), jax.ShapeDtypeStruct((N_S, 2 * C_WIDTH), f32)],
        compiler_params=_params("arbitrary"),
        name="proj_rg",
    )(x_p, x_s, g, w)


def _mlstm_masks(t, seg):
    r = lax.broadcasted_iota(jnp.int32, (t, t), 0)
    c = lax.broadcasted_iota(jnp.int32, (t, t), 1)
    if seg == t:
        same = None
        lower, upper = r >= c, r <= c
    else:
        same = _idiv(r, seg) == _idiv(c, seg)
        lower, upper = same & (r >= c), same & (r <= c)
    return r, c, same, lower, upper


def _mlstm_intra(q, k, v, btc, btr, itr, m_col, lower):
    dmat = jnp.where(lower, btc - btr + itr, -jnp.inf)
    inter = btc + m_col
    m_t = jnp.maximum(jnp.max(dmat, axis=1, keepdims=True), inter)
    w = jnp.exp(dmat - m_t)
    w_inter = jnp.exp(inter - m_t)
    s = _dot_nt(q, k) * w
    num = _dot(s.astype(bf16), v)
    den = jnp.sum(s, axis=1, keepdims=True)
    return num, den, m_t, w_inter, inter


def _mlstm_head_out(num, den, m_t, o_pre, g_head):
    h = num / jnp.maximum(jnp.abs(den), jnp.exp(-m_t))
    h = h * lax.rsqrt(jnp.mean(h * h, axis=-1, keepdims=True) + NORM_EPS) * g_head
    return (_sigmoid(o_pre) * h).astype(bf16)


def _mlstm_cols(main_ref, h):
    q = main_ref[:, h * A_DK:(h + 1) * A_DK]
    k = main_ref[:, A_QK + h * A_DK:A_QK + (h + 1) * A_DK] * (A_DK ** -0.5)
    v = main_ref[:, 2 * A_QK + h * A_DV:2 * A_QK + (h + 1) * A_DV]
    o_pre = main_ref[:, 2 * A_QK + (A_HEADS + h) * A_DV:2 * A_QK + (A_HEADS + h + 1) * A_DV]
    return q, k, v, o_pre


def _mlstm_prompt_kernel(main_ref, gc_ref, gr_ref, gh_ref, hs_ref, c_ref, n_ref, mt_ref, c_s, n_s, m_s):
    t = MLSTM_CHUNK
    j = pl.program_id(1)

    @pl.when(j == 0)
    def _():
        c_s[...] = jnp.zeros_like(c_s)
        n_s[...] = jnp.zeros_like(n_s)
        m_s[...] = jnp.zeros_like(m_s)

    _, _, _, lower, upper = _mlstm_masks(t, t)
    lower_f = lower.astype(f32)
    gc = gc_ref[...]
    gr = gr_ref[...]
    btr_all = _dot_exact(gr, upper.astype(f32))
    lane = lax.broadcasted_iota(jnp.int32, (t, LANES), 1)
    mt_all = jnp.zeros((t, LANES), f32)

    def rep(col):
        return jnp.broadcast_to(col, (t, LANES))

    def wide(col, n):
        return jnp.concatenate([col] * n, axis=1)

    def row_sum(x):
        acc = x[:, 0:LANES]
        for l in range(LANES, x.shape[1], LANES):
            acc = acc + x[:, l:l + LANES]
        return jnp.broadcast_to(jnp.sum(acc, axis=1, keepdims=True), acc.shape)

    heads = range(A_HEADS)
    stack = lambda xs: jnp.concatenate(list(xs), axis=0)
    head = lambda x, h: x[h * t:(h + 1) * t]
    cols = [_mlstm_cols(main_ref, h) for h in heads]
    qs, ks, vs = [c[0] for c in cols], [c[1] for c in cols], [c[2] for c in cols]
    qbs, kbs, vbs = [q.astype(bf16) for q in qs], [k.astype(bf16) for k in ks], [v.astype(bf16) for v in vs]
    cts = [c_s[h] for h in heads]
    ns = [n_s[h:h + 1, :] for h in heads]
    btrs = [btr_all[A_HEADS + h:A_HEADS + h + 1, :] for h in heads]
    itrs = [gr[h:h + 1, :] for h in heads]

    lf_rep = jnp.concatenate([rep(gc[:, A_HEADS + h:A_HEADS + h + 1]) for h in heads], axis=1)
    btc_all = _dot_exact(lower_f, lf_rep)
    btc = stack(btc_all[:, h * LANES:(h + 1) * LANES] for h in heads)
    itc = stack(rep(gc[:, h:h + 1]) for h in heads)
    m_in = stack(rep(m_s[0:1, h:h + 1]) for h in heads)
    dmat = stack(jnp.where(lower, wide(head(btc, h), t // LANES) + (itrs[h] - btrs[h]), -jnp.inf) for h in heads)
    inter = btc + m_in
    m_t = jnp.maximum(jnp.broadcast_to(jnp.max(dmat, axis=1, keepdims=True), inter.shape), inter)
    w = jnp.exp(dmat - wide(m_t, t // LANES))
    w_inter = jnp.exp(inter - m_t)
    s = stack(_dot_nt(qbs[h], kbs[h]) for h in heads) * w
    sb = s.astype(bf16)
    num = (stack(_dot(head(sb, h), vbs[h]) for h in heads)
           + wide(w_inter, A_DV // LANES) * stack(_dot(qbs[h], cts[h].astype(bf16)) for h in heads))
    den = row_sum(s) + w_inter * row_sum(stack(qs[h] * ns[h] for h in heads))
    inv = 1.0 / jnp.maximum(jnp.abs(den), jnp.exp(-m_t))
    norm = inv * lax.rsqrt(inv * inv * (row_sum(num * num) * (1.0 / A_DV)) + NORM_EPS)
    out = _sigmoid(stack(c[3] for c in cols)) * (num * wide(norm, A_DV // LANES))
    for h in heads:
        hs_ref[:, h * A_DV:(h + 1) * A_DV] = (head(out, h) * gh_ref[h:h + 1, :]).astype(bf16)

    m_news = [head(m_t, h)[t - 1:t, 0:1] for h in heads]
    last = lambda x: stack(rep(head(x, h)[t - 1:t, 0:1]) for h in heads)
    decay_c = jnp.exp(last(btc) - btc + itc - last(m_t))
    for h in heads:
        scale = jnp.exp(head(inter, h)[t - 1:t, 0:1] - m_news[h])
        decay_r = jnp.exp(btrs[h][:, t - 1:t] - btrs[h] + itrs[h] - m_news[h])
        c_s[h] = scale * cts[h] + _dot((ks[h].T * decay_r).astype(bf16), vbs[h])
        n_s[h:h + 1, :] = scale * ns[h] + jnp.sum(head(decay_c, h) * ks[h], axis=0, keepdims=True)
        mt_all = jnp.where(lane == h, head(m_t, h), mt_all)
    mt_ref[...] = mt_all
    m_s[0:1, :] = mt_all[t - 1:t, :]

    @pl.when(j == pl.num_programs(1) - 1)
    def _():
        for h in range(A_HEADS):
            c_ref[0, h] = c_s[h].T
        n_ref[0] = n_s[...]


def _mlstm_prompt(main, gates, gates_t, g_head):
    nc = SEQ // MLSTM_CHUNK
    t = MLSTM_CHUNK
    return pl.pallas_call(
        _mlstm_prompt_kernel,
        grid=(BATCH, nc),
        in_specs=[pl.BlockSpec((t, A_MAIN), lambda b, j: (b * nc + j, 0)),
                  pl.BlockSpec((t, LANES), lambda b, j: (b * nc + j, 0)),
                  pl.BlockSpec((SUBLANES, t), lambda b, j: (0, b * nc + j)),
                  _const_spec((A_HEADS, A_DV))],
        out_specs=[pl.BlockSpec((t, A_HEADS * A_DV), lambda b, j: (b * nc + j, 0)),
                   pl.BlockSpec((1, A_HEADS, A_DV, A_DK), lambda b, j: (b, 0, 0, 0)),
                   pl.BlockSpec((1, A_HEADS, A_DK), lambda b, j: (b, 0, 0)),
                   pl.BlockSpec((t, LANES), lambda b, j: (b * nc + j, 0))],
        out_shape=[jax.ShapeDtypeStruct((N_P, A_HEADS * A_DV), bf16),
                   jax.ShapeDtypeStruct((BATCH, A_HEADS, A_DV, A_DK), f32),
                   jax.ShapeDtypeStruct((BATCH, A_HEADS, A_DK), f32),
                   jax.ShapeDtypeStruct((N_P, LANES), f32)],
        scratch_shapes=[pltpu.VMEM((A_HEADS, A_DK, A_DV), f32),
                        pltpu.VMEM((A_HEADS, A_DK), f32),
                        pltpu.VMEM((SUBLANES, LANES), f32)],
        compiler_params=_params("parallel", "arbitrary"),
        name="mlstm_prompt",
    )(main, gates, gates_t, g_head)


def _mlstm_sample_compute(main_ref, gc_ref, gr_ref, m0_ref, c0_ref, n0_ref, gh_ref,
                          hs_ref, n_ref, mt_ref, vdt_ref, kb_ref, sc_ref, acc_s, nt_s):
    t = MLSTM_ST
    seg = DEC_SEQ
    _, c, same, lower, upper = _mlstm_masks(t, seg)
    seg_last = (same & (_imod(c, seg) == seg - 1)).astype(f32)
    same_f = same.astype(f32)
    gc = gc_ref[...]
    gr = gr_ref[...]
    btc_all = _dot_exact(lower.astype(f32), gc)
    btr_all = _dot_exact(gr, upper.astype(f32))
    lane = lax.broadcasted_iota(jnp.int32, (t, LANES), 1)
    tok_seg = _idiv(lax.broadcasted_iota(jnp.int32, (A_DV, t), 1), seg)
    mt_all = jnp.zeros((t, LANES), f32)
    sc_all = jnp.zeros((t, LANES), f32)
    for h in range(A_HEADS):
        q, k, v, o_pre = _mlstm_cols(main_ref, h)
        qb, kb, vb = q.astype(bf16), k.astype(bf16), v.astype(bf16)
        btc = btc_all[:, A_HEADS + h:A_HEADS + h + 1]
        itc = gc[:, h:h + 1]
        btr = btr_all[A_HEADS + h:A_HEADS + h + 1, :]
        itr = gr[h:h + 1, :]
        m_in = m0_ref[:, h:h + 1]
        num, den, m_t, w_inter, inter = _mlstm_intra(qb, kb, vb, btc, btr, itr, m_in, lower)

        def inter_body(b, carry, h=h, qb=qb):
            rows = pl.ds(pl.multiple_of(b * seg, seg), seg)
            r = _dot_nt(c0_ref[b, h].astype(bf16), qb)
            acc_s[...] = jnp.where(tok_seg == b, r, acc_s[...])
            nt_s[rows, :] = jnp.broadcast_to(n0_ref[b, h:h + 1, :], (seg, A_DK))
            return carry

        acc_s[...] = jnp.zeros_like(acc_s)
        lax.fori_loop(0, MLSTM_SB, inter_body, 0, unroll=MLSTM_UNROLL)
        num = num + w_inter * acc_s[...].T
        den = den + w_inter * jnp.sum(q * nt_s[...], axis=1, keepdims=True)
        hs_ref[:, h * A_DV:(h + 1) * A_DV] = _mlstm_head_out(num, den, m_t, o_pre, gh_ref[h:h + 1, :])

        packed = jnp.where(lane == 0, m_t, jnp.where(lane == 1, btc, inter))
        last = _dot_exact(seg_last, packed)
        m_new, bt_last, inter_last = last[:, 0:1], last[:, 1:2], last[:, 2:3]
        decay = jnp.exp(bt_last - btc + itc - m_new)
        scale = jnp.exp(inter_last - m_new)
        vdt_ref[h] = (decay * v).T.astype(bf16)
        kb_ref[:, h * A_DK:(h + 1) * A_DK] = kb
        sc_all = jnp.where(lane == h, scale, sc_all)
        mt_all = jnp.where(lane == h, m_t, mt_all)
        n_new_tok = scale * nt_s[...] + _dot_exact(same_f, decay * k)
        for b in range(MLSTM_SB):
            n_ref[b, h:h + 1, :] = n_new_tok[b * seg:b * seg + 1, :]
    mt_ref[...] = mt_all
    sc_ref[...] = sc_all


def _mlstm_sample_update(c_in_ref, c_out_ref, vdt_ref, kb_ref, sc_ref):
    seg = DEC_SEQ
    lane_seg = _idiv(lax.broadcasted_iota(jnp.int32, (1, MLSTM_ST), 1), seg)
    for h in range(A_HEADS):
        def upd_body(b, carry, h=h):
            rows = pl.ds(pl.multiple_of(b * seg, seg), seg)
            scale = sc_ref[rows, :][0:1, h:h + 1]
            onehot = jnp.where(lane_seg == b, 1.0, 0.0).astype(bf16)
            upd = _dot(vdt_ref[h] * onehot, kb_ref[:, h * A_DK:(h + 1) * A_DK])
            c_out_ref[b, h] = scale * c_in_ref[b, h] + upd
            return carry

        lax.fori_loop(0, MLSTM_SB, upd_body, 0, unroll=MLSTM_UNROLL)


def _mlstm_sample_first_kernel(main_ref, gc_ref, gr_ref, m0_ref, c0_ref, n0_ref, gh_ref,
                               hs_ref, n_ref, mt_ref, vdt_ref, kb_ref, sc_ref, acc_s, nt_s):
    _mlstm_sample_compute(main_ref, gc_ref, gr_ref, m0_ref, c0_ref, n0_ref, gh_ref,
                          hs_ref, n_ref, mt_ref, vdt_ref, kb_ref, sc_ref, acc_s, nt_s)


def _mlstm_sample_second_kernel(main_ref, gc_ref, gr_ref, m0_ref, c0_ref, n0_ref, gh_ref, vdt0_ref, kb0_ref, sc0_ref,
                                hs_ref, n_ref, mt_ref, c_ref, vdt_s, kb_s, sc_s, acc_s, nt_s):
    layer = pl.program_id(0)

    @pl.when(layer == 0)
    def _():
        _mlstm_sample_update(c0_ref, c_ref, vdt0_ref, kb0_ref, sc0_ref)

    @pl.when(layer == 1)
    def _():
        _mlstm_sample_compute(main_ref, gc_ref, gr_ref, m0_ref, c0_ref, n0_ref, gh_ref,
                              hs_ref, n_ref, mt_ref, vdt_s, kb_s, sc_s, acc_s, nt_s)
        _mlstm_sample_update(c0_ref, c_ref, vdt_s, kb_s, sc_s)


def _mlstm_sample_scratch():
    t = MLSTM_ST
    return [pltpu.VMEM((A_DV, t), f32), pltpu.VMEM((t, A_DK), f32)]


def _mlstm_sample_first(main, gates, gates_t, m0_tok, c_all, n0, g_head):
    t = MLSTM_ST
    return pl.pallas_call(
        _mlstm_sample_first_kernel,
        grid=(DEC_BATCH // MLSTM_SB,),
        in_specs=[pl.BlockSpec((t, A_MAIN), lambda i: (i, 0)),
                  pl.BlockSpec((t, LANES), lambda i: (i, 0)),
                  pl.BlockSpec((SUBLANES, t), lambda i: (0, i)),
                  pl.BlockSpec((t, LANES), lambda i: (i, 0)),
                  pl.BlockSpec((None, MLSTM_SB, A_HEADS, A_DV, A_DK), lambda i: (0, i, 0, 0, 0)),
                  pl.BlockSpec((MLSTM_SB, A_HEADS, A_DK), lambda i: (i, 0, 0)),
                  _const_spec((A_HEADS, A_DV))],
        out_specs=[pl.BlockSpec((t, A_HEADS * A_DV), lambda i: (i, 0)),
                   pl.BlockSpec((MLSTM_SB, A_HEADS, A_DK), lambda i: (i, 0, 0)),
                   pl.BlockSpec((t, LANES), lambda i: (i, 0)),
                   pl.BlockSpec((A_HEADS, A_DV, t), lambda i: (0, 0, i)),
                   pl.BlockSpec((t, A_QK), lambda i: (i, 0)),
                   pl.BlockSpec((t, LANES), lambda i: (i, 0))],
        out_shape=[jax.ShapeDtypeStruct((N_S, A_HEADS * A_DV), bf16),
                   jax.ShapeDtypeStruct((DEC_BATCH, A_HEADS, A_DK), f32),
                   jax.ShapeDtypeStruct((N_S, LANES), f32),
                   jax.ShapeDtypeStruct((A_HEADS, A_DV, N_S), bf16),
                   jax.ShapeDtypeStruct((N_S, A_QK), bf16),
                   jax.ShapeDtypeStruct((N_S, LANES), f32)],
        scratch_shapes=_mlstm_sample_scratch(),
        compiler_params=_params("parallel"),
        name="mlstm_sample_first",
    )(main, gates, gates_t, m0_tok, c_all, n0, g_head)


def _mlstm_sample_second(main, gates, gates_t, m0_tok, c_all, n0, g_head, vdt0, kb0, sc0):
    t = MLSTM_ST
    own = lambda l, i: i * l
    first = lambda l, i: i * (1 - l)
    return pl.pallas_call(
        _mlstm_sample_second_kernel,
        grid=(N_A, DEC_BATCH // MLSTM_SB),
        in_specs=[pl.BlockSpec((t, A_MAIN), lambda l, i: (own(l, i), 0)),
                  pl.BlockSpec((t, LANES), lambda l, i: (own(l, i), 0)),
                  pl.BlockSpec((SUBLANES, t), lambda l, i: (0, own(l, i))),
                  pl.BlockSpec((t, LANES), lambda l, i: (own(l, i), 0)),
                  pl.BlockSpec((None, MLSTM_SB, A_HEADS, A_DV, A_DK), lambda l, i: (l, i, 0, 0, 0)),
                  pl.BlockSpec((MLSTM_SB, A_HEADS, A_DK), lambda l, i: (own(l, i), 0, 0)),
                  _const_spec((A_HEADS, A_DV)),
                  pl.BlockSpec((A_HEADS, A_DV, t), lambda l, i: (0, 0, first(l, i))),
                  pl.BlockSpec((t, A_QK), lambda l, i: (first(l, i), 0)),
                  pl.BlockSpec((t, LANES), lambda l, i: (first(l, i), 0))],
        out_specs=[pl.BlockSpec((t, A_HEADS * A_DV), lambda l, i: (own(l, i), 0)),
                   pl.BlockSpec((MLSTM_SB, A_HEADS, A_DK), lambda l, i: (own(l, i), 0, 0)),
                   pl.BlockSpec((t, LANES), lambda l, i: (own(l, i), 0)),
                   pl.BlockSpec((None, MLSTM_SB, A_HEADS, A_DV, A_DK), lambda l, i: (l, i, 0, 0, 0))],
        out_shape=[jax.ShapeDtypeStruct((N_S, A_HEADS * A_DV), bf16),
                   jax.ShapeDtypeStruct((DEC_BATCH, A_HEADS, A_DK), f32),
                   jax.ShapeDtypeStruct((N_S, LANES), f32),
                   jax.ShapeDtypeStruct((N_A, DEC_BATCH, A_HEADS, A_DV, A_DK), f32)],
        scratch_shapes=[pltpu.VMEM((A_HEADS, A_DV, t), bf16),
                        pltpu.VMEM((t, A_QK), bf16),
                        pltpu.VMEM((t, LANES), f32)] + _mlstm_sample_scratch(),
        compiler_params=_params("arbitrary", "arbitrary"),
        name="mlstm_sample_second",
    )(main, gates, gates_t, m0_tok, c_all, n0, g_head, vdt0, kb0, sc0)


def _swa_softmax_pv(s, allowed, sink_col, vb):
    s = jnp.where(allowed, s * (B_HD ** -0.5), -jnp.inf)
    m = jnp.maximum(jnp.max(s, axis=1, keepdims=True), sink_col)
    p = jnp.exp(s - m)
    den = jnp.sum(p, axis=1, keepdims=True) + jnp.exp(sink_col - m)
    return _dot((p / den).astype(bf16), vb)


def _swa_prompt_kernel(sink_ref, q_ref, kp_ref, ko_ref, vp_ref, vo_ref, o_ref):
    n = pl.program_id(1)
    w = WINDOW
    kb = jnp.concatenate([kp_ref[...], ko_ref[...]], axis=0).astype(bf16)
    vb = jnp.concatenate([vp_ref[...], vo_ref[...]], axis=0).astype(bf16)
    rows = B_GROUP * w
    t = _imod(lax.broadcasted_iota(jnp.int32, (rows, 2 * w), 0), w)
    jj = lax.broadcasted_iota(jnp.int32, (rows, 2 * w), 1)
    bias = jnp.where((jj > t) & (jj <= t + w) & ((n > 0) | (jj >= w)), 0.0, -jnp.inf)
    lane_head = _idiv(lax.broadcasted_iota(jnp.int32, (1, B_KV), 1), B_HD)
    grp = _idiv(lax.broadcasted_iota(jnp.int32, (rows, 1), 0), w)
    outs = [jnp.zeros((w, B_KV), f32) for _ in range(B_GROUP)]
    for h in range(B_KV_HEADS):
        hm = lane_head == h
        qscale = jnp.where(hm, B_HD ** -0.5, 0.0)
        qh = jnp.concatenate([q_ref[:, g * B_KV:(g + 1) * B_KV] * qscale for g in range(B_GROUP)],
                             axis=0).astype(bf16)
        sink_col = jnp.zeros((rows, 1), f32)
        for g in range(B_GROUP):
            sink_col = jnp.where(grp == g, sink_ref[h * B_GROUP + g], sink_col)
        s = _dot_nt(qh, kb) + bias
        m = jnp.maximum(jnp.max(s, axis=1, keepdims=True), sink_col)
        p = jnp.exp(s - m)
        den = jnp.sum(p, axis=1, keepdims=True) + jnp.exp(sink_col - m)
        r = _dot((p * (1.0 / den)).astype(bf16), vb)
        for g in range(B_GROUP):
            outs[g] = jnp.where(hm, r[g * w:(g + 1) * w, :], outs[g])
    for g in range(B_GROUP):
        o_ref[:, g * B_KV:(g + 1) * B_KV] = outs[g].astype(bf16)


def _swa_prompt(proj, sinks):
    nb = SEQ // WINDOW
    w = WINDOW
    kcol = B_Q // B_KV
    vcol = kcol + 1
    prev = lambda b, n: b * nb + jnp.maximum(n - 1, 0)
    return pl.pallas_call(
        _swa_prompt_kernel,
        grid=(BATCH, nb),
        in_specs=[pl.BlockSpec(memory_space=pltpu.SMEM),
                  pl.BlockSpec((w, B_Q), lambda b, n: (b * nb + n, 0)),
                  pl.BlockSpec((w, B_KV), lambda b, n: (prev(b, n), kcol)),
                  pl.BlockSpec((w, B_KV), lambda b, n: (b * nb + n, kcol)),
                  pl.BlockSpec((w, B_KV), lambda b, n: (prev(b, n), vcol)),
                  pl.BlockSpec((w, B_KV), lambda b, n: (b * nb + n, vcol))],
        out_specs=pl.BlockSpec((w, B_Q), lambda b, n: (b * nb + n, 0)),
        out_shape=jax.ShapeDtypeStruct((N_P, B_Q), bf16),
        compiler_params=_params("parallel", "arbitrary"),
        name="swa_prompt",
    )(sinks, proj, proj, proj, proj, proj)


def _swa_sample_kernel(sink_ref, q_ref, kn_ref, vn_ref, kc_ref, vc_ref, o_ref, ko_ref, vo_ref):
    s_len = DEC_SEQ
    buf = WINDOW
    rows = B_HEADS * s_len
    keys = 2 * buf
    ri = lax.broadcasted_iota(jnp.int32, (rows, keys), 0)
    jj = lax.broadcasted_iota(jnp.int32, (rows, keys), 1)
    t = _imod(ri, s_len)
    allowed = ((jj < buf) & (jj > t)) | ((jj >= buf) & (jj - buf <= t))
    lane_head = _idiv(lax.broadcasted_iota(jnp.int32, (1, B_KV), 1), B_HD)
    hg = _idiv(lax.broadcasted_iota(jnp.int32, (rows, 1), 0), s_len)
    sink_col = jnp.zeros((rows, 1), f32)
    for i in range(B_HEADS):
        sink_col = jnp.where(hg == i, sink_ref[i], sink_col)
    pad = jnp.zeros((keys - buf - s_len, B_KV), f32)

    def body(e, carry):
        rs = pl.ds(pl.multiple_of(e * s_len, s_len), s_len)
        kn = kn_ref[rs, :]
        vn = vn_ref[rs, :]
        kc = kc_ref[e]
        vc = vc_ref[e]
        kpad = jnp.concatenate([kc, kn, pad], axis=0).astype(bf16)
        vpad = jnp.concatenate([vc, vn, pad], axis=0).astype(bf16)
        ko_ref[e, 0:buf - s_len, :] = kc[s_len:, :]
        ko_ref[e, buf - s_len:, :] = kn
        vo_ref[e, 0:buf - s_len, :] = vc[s_len:, :]
        vo_ref[e, buf - s_len:, :] = vn
        qe = q_ref[rs, :]
        qbig = jnp.concatenate(
            [jnp.where(lane_head == h, qe[:, g * B_KV:(g + 1) * B_KV], 0.0)
             for h in range(B_KV_HEADS) for g in range(B_GROUP)], axis=0).astype(bf16)
        r = _swa_softmax_pv(_dot_nt(qbig, kpad), allowed, sink_col, vpad)
        for g in range(B_GROUP):
            og = jnp.zeros((s_len, B_KV), f32)
            for h in range(B_KV_HEADS):
                blk = (h * B_GROUP + g) * s_len
                og = og + jnp.where(lane_head == h, r[blk:blk + s_len, :], 0.0)
            o_ref[rs, g * B_KV:(g + 1) * B_KV] = og.astype(bf16)
        return carry

    lax.fori_loop(0, SWA_SB, body, 0, unroll=4)


def _swa_sample(proj, sinks, k_cache, v_cache):
    t = SWA_SB * DEC_SEQ
    kcol = B_Q // B_KV
    return pl.pallas_call(
        _swa_sample_kernel,
        grid=(DEC_BATCH // SWA_SB,),
        in_specs=[pl.BlockSpec(memory_space=pltpu.SMEM),
                  pl.BlockSpec((t, B_Q), lambda i: (i, 0)),
                  pl.BlockSpec((t, B_KV), lambda i: (i, kcol)),
                  pl.BlockSpec((t, B_KV), lambda i: (i, kcol + 1)),
                  pl.BlockSpec((SWA_SB, WINDOW, B_KV), lambda i: (i, 0, 0)),
                  pl.BlockSpec((SWA_SB, WINDOW, B_KV), lambda i: (i, 0, 0))],
        out_specs=[pl.BlockSpec((t, B_Q), lambda i: (i, 0)),
                   pl.BlockSpec((SWA_SB, WINDOW, B_KV), lambda i: (i, 0, 0)),
                   pl.BlockSpec((SWA_SB, WINDOW, B_KV), lambda i: (i, 0, 0))],
        out_shape=[jax.ShapeDtypeStruct((N_S, B_Q), bf16),
                   jax.ShapeDtypeStruct((DEC_BATCH, WINDOW, B_KV), f32),
                   jax.ShapeDtypeStruct((DEC_BATCH, WINDOW, B_KV), f32)],
        compiler_params=_params("parallel"),
        name="swa_sample",
    )(sinks, proj, proj, proj, k_cache, v_cache)


def _rg_conv_group(x8, p8, wc_ref, bc_ref):
    row = lax.broadcasted_iota(jnp.int32, x8.shape, 0)
    u = bc_ref[...] + wc_ref[CONV_W - 1:CONV_W, :] * x8
    for d in range(1, CONV_W):
        sh = pltpu.roll(jnp.where(row >= SUBLANES - d, p8, x8), d, 0)
        u = u + wc_ref[CONV_W - 1 - d:CONV_W - d, :] * sh
    return u


def _rg_scan_group(a8, b8, carry):
    row = lax.broadcasted_iota(jnp.int32, a8.shape, 0)
    for d in (1, 2, 4):
        keep = row >= d
        b8 = jnp.where(keep, a8 * pltpu.roll(b8, d, 0) + b8, b8)
        a8 = jnp.where(keep, a8 * pltpu.roll(a8, d, 0), a8)
    return a8 * carry + b8


def _rg_gate_rows(ra, rx, neg_rate, u):
    z = _sigmoid(ra) * neg_rate
    a = jnp.exp(-z)
    return a, _sqrt_nonneg(jnp.tanh(z) * (a * a + 1.0)) * _sigmoid(rx) * u


def _rg_gates(u, gate, wa_ref, ba_ref, wx_ref, bx_ref, lam_ref):
    ub = u.astype(bf16)
    ra = jnp.concatenate([_dot(ub[:, n * C_BW:(n + 1) * C_BW], wa_ref[n]) for n in range(C_BLOCKS)], axis=1)
    rx = jnp.concatenate([_dot(ub[:, n * C_BW:(n + 1) * C_BW], wx_ref[n]) for n in range(C_BLOCKS)], axis=1)
    a, bterm = _rg_gate_rows(ra + ba_ref[...], rx + bx_ref[...], LRU_C * _softplus(-lam_ref[...]), u)
    return a, bterm, _gelu_tanh(gate)


def _rg_prompt_kernel(p_ref, wc_ref, bc_ref, wa_ref, ba_ref, wx_ref, bx_ref, lam_ref,
                      y_ref, h_ref, u_s, a_s, b_s, xc_s, hc_s):
    j = pl.program_id(1)
    ng = RG_T // SUBLANES
    out_rows = 2 * SUBLANES

    @pl.when(j == 0)
    def _():
        xc_s[...] = jnp.zeros_like(xc_s)
        hc_s[...] = jnp.zeros_like(hc_s)

    def group(gidx, n=SUBLANES):
        return pl.ds(pl.multiple_of(gidx * n, n), n)

    def conv_body(gidx, p8):
        x8 = p_ref[group(gidx), 0:C_WIDTH]
        u_s[group(gidx), :] = _rg_conv_group(x8, p8, wc_ref, bc_ref)
        return x8

    xc_s[...] = lax.fori_loop(0, ng, conv_body, xc_s[...], unroll=2)
    ub = u_s[...].astype(bf16)
    for n in range(C_BLOCKS):
        cols = slice(n * C_BW, (n + 1) * C_BW)
        a_s[:, cols] = _dot(ub[:, cols], wa_ref[n])
        b_s[:, cols] = _dot(ub[:, cols], wx_ref[n])
    neg_rate = LRU_C * _softplus(-lam_ref[...])

    def gate_body(gidx, carry):
        rows = group(gidx)
        a, bterm = _rg_gate_rows(a_s[rows, :] + ba_ref[...], b_s[rows, :] + bx_ref[...], neg_rate, u_s[rows, :])
        a_s[rows, :] = a
        b_s[rows, :] = bterm
        return carry

    lax.fori_loop(0, ng, gate_body, 0, unroll=2)

    def scan_body(gidx, carry):
        rows = group(gidx, out_rows)
        lo = pl.ds(pl.multiple_of(gidx * out_rows, out_rows), SUBLANES)
        hi = pl.ds(pl.multiple_of(gidx * out_rows + SUBLANES, SUBLANES), SUBLANES)
        h_lo = _rg_scan_group(a_s[lo, :], b_s[lo, :], carry)
        h_hi = _rg_scan_group(a_s[hi, :], b_s[hi, :], h_lo[SUBLANES - 1:SUBLANES, :])
        h16 = jnp.concatenate([h_lo, h_hi], axis=0)
        y_ref[rows, :] = (h16 * _gelu_tanh(p_ref[rows, C_WIDTH:])).astype(bf16)
        return h_hi[SUBLANES - 1:SUBLANES, :]

    h_last = lax.fori_loop(0, RG_T // out_rows, scan_body, hc_s[...])
    hc_s[...] = h_last

    @pl.when(j == pl.num_programs(1) - 1)
    def _():
        h_ref[0] = h_last


def _rg_weight_specs():
    return [_const_spec((CONV_W, C_WIDTH)), _const_spec((1, C_WIDTH)),
            _const_spec((C_BLOCKS, C_BW, C_BW)), _const_spec((1, C_WIDTH)),
            _const_spec((C_BLOCKS, C_BW, C_BW)), _const_spec((1, C_WIDTH)),
            _const_spec((1, C_WIDTH))]


def _rg_prompt(proj, weights):
    nt = SEQ // RG_T
    return pl.pallas_call(
        _rg_prompt_kernel,
        grid=(BATCH, nt),
        in_specs=[pl.BlockSpec((RG_T, 2 * C_WIDTH), lambda b, j: (b * nt + j, 0))] + _rg_weight_specs(),
        out_specs=[pl.BlockSpec((RG_T, C_WIDTH), lambda b, j: (b * nt + j, 0)),
                   pl.BlockSpec((1, 1, C_WIDTH), lambda b, j: (b, 0, 0))],
        out_shape=[jax.ShapeDtypeStruct((N_P, C_WIDTH), bf16),
                   jax.ShapeDtypeStruct((BATCH, 1, C_WIDTH), f32)],
        scratch_shapes=[pltpu.VMEM((RG_T, C_WIDTH), f32),
                        pltpu.VMEM((RG_T, C_WIDTH), f32),
                        pltpu.VMEM((RG_T, C_WIDTH), f32),
                        pltpu.VMEM((SUBLANES, C_WIDTH), f32),
                        pltpu.VMEM((1, C_WIDTH), f32)],
        compiler_params=_params("parallel", "arbitrary"),
        name="rg_prompt",
    )(proj, *weights)


def _rg_sample_kernel(p_ref, cv_ref, h0_ref, wc_ref, bc_ref, wa_ref, ba_ref, wx_ref, bx_ref, lam_ref,
                      y_ref, h_ref, u_s, a_s, b_s):
    def conv_body(gidx, carry):
        rows = pl.ds(pl.multiple_of(gidx * SUBLANES, SUBLANES), SUBLANES)
        u_s[rows, :] = _rg_conv_group(p_ref[rows, 0:C_WIDTH], cv_ref[rows, :], wc_ref, bc_ref)
        return carry

    lax.fori_loop(0, DEC_BATCH, conv_body, 0)
    a, bterm, gg = _rg_gates(u_s[...], p_ref[:, C_WIDTH:], wa_ref, ba_ref, wx_ref, bx_ref, lam_ref)
    a_s[...] = a
    b_s[...] = bterm

    def scan_body(gidx, carry):
        rows = pl.ds(pl.multiple_of(gidx * SUBLANES, SUBLANES), SUBLANES)
        h8 = _rg_scan_group(a_s[rows, :], b_s[rows, :], h0_ref[pl.ds(gidx, 1), :])
        u_s[rows, :] = h8
        h_ref[pl.ds(gidx, 1), :] = h8[SUBLANES - 1:SUBLANES, :]
        return carry

    lax.fori_loop(0, DEC_BATCH, scan_body, 0)
    y_ref[...] = (u_s[...] * gg).astype(bf16)


def _rg_sample(proj, conv_pad, h0, weights):
    return pl.pallas_call(
        _rg_sample_kernel,
        grid=(1,),
        in_specs=[_const_spec((N_S, 2 * C_WIDTH)),
                  _const_spec((N_S, C_WIDTH)),
                  _const_spec((DEC_BATCH, C_WIDTH))] + _rg_weight_specs(),
        out_specs=[_const_spec((N_S, C_WIDTH)), _const_spec((DEC_BATCH, C_WIDTH))],
        out_shape=[jax.ShapeDtypeStruct((N_S, C_WIDTH), bf16),
                   jax.ShapeDtypeStruct((DEC_BATCH, C_WIDTH), f32)],
        scratch_shapes=[pltpu.VMEM((N_S, C_WIDTH), f32),
                        pltpu.VMEM((N_S, C_WIDTH), f32),
                        pltpu.VMEM((N_S, C_WIDTH), f32)],
        compiler_params=_params("arbitrary"),
        name="rg_sample",
    )(proj, conv_pad, h0, *weights)


def _out_mlp_kernel(xp_ref, xs_ref, ap_ref, as_ref, wo_ref, bo_ref, g_ref, wup_ref, wdn_ref, gf_ref,
                    op_ref, os_ref, x1_s, xn_s, *, final):
    def body(x_ref, a_ref, o_ref):
        x1_s[...] = x_ref[...] + _dot(a_ref[...], wo_ref[...]) + bo_ref[...]
        xn_s[...] = _rms_bf16(x1_s[...], g_ref[...])
        for c in range(0, D_FF, FF_CHUNK):
            hmid = jnp.maximum(_dot(xn_s[...], wup_ref[:, c:c + FF_CHUNK]), 0.0)
            x1_s[...] += _dot((hmid * hmid).astype(bf16), wdn_ref[c:c + FF_CHUNK, :])
        if final:
            x1 = x1_s[...]
            y = x1 * lax.rsqrt(jnp.mean(x1 * x1, axis=-1, keepdims=True) + NORM_EPS)
            o_ref[...] = y * gf_ref[...]
        else:
            o_ref[...] = x1_s[...]

    _for_each_group(body, (xp_ref, ap_ref, op_ref), (xs_ref, as_ref, os_ref))


def _out_mlp(x_p, x_s, a_p, a_s, wo, bo, g, w_up_all, w_down_all, layer, g_final, final):
    single = pl.Buffered(1)
    wspec = lambda shape: pl.BlockSpec(shape, lambda i: (0, 0), pipeline_mode=single)
    lspec = lambda shape: pl.BlockSpec((None,) + shape, lambda i: (layer, 0, 0), pipeline_mode=single)
    return pl.pallas_call(
        functools.partial(_out_mlp_kernel, final=final),
        grid=(NP_TILES + NS_TILES,),
        in_specs=[_p_spec(D_MODEL), _s_spec(D_MODEL), _p_spec(D_MODEL), _s_spec(D_MODEL),
                  wspec((D_MODEL, D_MODEL)),
                  wspec((1, D_MODEL)),
                  wspec((1, D_MODEL)),
                  lspec((D_MODEL, D_FF)),
                  lspec((D_FF, D_MODEL)),
                  wspec((1, D_MODEL))],
        out_specs=[_p_spec(D_MODEL), _s_spec(D_MODEL)],
        out_shape=[jax.ShapeDtypeStruct((N_P, D_MODEL), f32), jax.ShapeDtypeStruct((N_S, D_MODEL), f32)],
        scratch_shapes=[pltpu.VMEM((TM, D_MODEL), f32),
                        pltpu.VMEM((TM, D_MODEL), bf16)],
        compiler_params=_params("arbitrary"),
        name="out_mlp",
    )(x_p, x_s, a_p, a_s, wo, bo, g, w_up_all, w_down_all, g_final)


def _rope_tables():
    half = ROPE_DIM // 2
    inv = ROPE_THETA ** (-jnp.arange(0, ROPE_DIM, 2, dtype=f32) / ROPE_DIM)
    lane = jnp.arange(LANES) % B_HD
    inv_lane = jnp.where(lane < ROPE_DIM, inv[lane % half], 0.0)
    pos = jnp.concatenate([jnp.arange(SEQ, dtype=jnp.int32), PAST_LEN + jnp.arange(TM, dtype=jnp.int32) % DEC_SEQ])
    ang = pos.astype(f32)[:, None] * inv_lane[None, :]
    tab = jnp.concatenate([jnp.cos(ang), jnp.sin(ang)], axis=1)
    lo = jnp.where(lane < half, -1.0, 0.0)
    hi = jnp.where((lane >= half) & (lane < ROPE_DIM), 1.0, 0.0)
    consts = jnp.zeros((SUBLANES, LANES), f32).at[0].set(lo).at[1].set(hi)
    return tab, consts


def _q_perm():
    g, h, d = jnp.meshgrid(jnp.arange(B_GROUP), jnp.arange(B_KV_HEADS), jnp.arange(B_HD), indexing="ij")
    return ((h * B_GROUP + g) * B_HD + d).reshape(-1)


def _last_rows(arr, n_seq, seq_len, n_rows, col0, col1):
    return jnp.stack([arr[(s + 1) * seq_len - n_rows:(s + 1) * seq_len, col0:col1] for s in range(n_seq)])


def kernel(x_prompt, x_sample, state_mlstm_c, state_mlstm_n, state_mlstm_m, cache_swa_k, cache_swa_v,
           state_rglru_h, state_rglru_conv, norm_mix, norm_mlp, norm_final, w_mlp_up, w_mlp_down,
           w_mlstm_in, b_mlstm_i, b_mlstm_f, g_mlstm_head, w_mlstm_out, w_swa_qkv, b_swa_qkv, swa_sinks,
           w_swa_out, b_swa_out, w_rg_in, w_rg_conv, b_rg_conv, w_rg_a, b_rg_a, w_rg_x, b_rg_x, rg_lambda,
           w_rg_out):
    assert N_A == 2
    x_p = x_prompt.reshape(N_P, D_MODEL)
    x_s = x_sample.reshape(N_S, D_MODEL)
    zero_bias = jnp.zeros((1, D_MODEL), f32)
    row = lambda v: v.reshape(1, -1).astype(f32)
    w_up_all = w_mlp_up.astype(bf16)
    w_down_all = w_mlp_down.astype(bf16)
    state_c = state_mlstm_c.astype(f32)
    outs = {k: [] for k in ("c_p", "n_p", "m_p", "n_s", "m_s", "k_p", "v_p", "k_s", "v_s",
                            "h_p", "cv_p", "h_s", "cv_s")}
    deferred = None
    c_s_all = None
    for i in range(DEPTH):
        kind, j = i % N_MIXERS, i // N_MIXERS
        g_mix = row(norm_mix[i])
        if kind == 0:
            w_in = w_mlstm_in[j]
            n_gate = 2 * A_HEADS
            w_gate = jnp.pad(w_in[:, A_MAIN:], ((0, 0), (0, LANES - n_gate))).astype(bf16)
            b_gate = jnp.pad(jnp.concatenate([b_mlstm_i[j], b_mlstm_f[j]]), (0, LANES - n_gate)).reshape(1, LANES)
            main_p, main_s, gates_p, gates_s = _proj_mlstm(x_p, x_s, g_mix, w_in[:, :A_MAIN].astype(bf16), w_gate,
                                                           b_gate.astype(f32))
            g_head = g_mlstm_head[j].astype(f32)
            a_p, c_p, n_p, mt_p = _mlstm_prompt(main_p, gates_p, gates_p[:, :SUBLANES].T, g_head)
            m0_tok = jnp.pad(jnp.repeat(state_mlstm_m[j].astype(f32), DEC_SEQ, axis=0),
                             ((0, 0), (0, LANES - A_HEADS)))
            n0 = state_mlstm_n[j].astype(f32)
            gates_s_t = gates_s[:, :SUBLANES].T
            if j == 0:
                a_s, n_s, mt_s, vdt0, kb0, sc0 = _mlstm_sample_first(main_s, gates_s, gates_s_t, m0_tok, state_c, n0,
                                                                     g_head)
                deferred = (vdt0, kb0, sc0)
            else:
                a_s, n_s, mt_s, c_s_all = _mlstm_sample_second(main_s, gates_s, gates_s_t, m0_tok, state_c, n0,
                                                               g_head, *deferred)
            outs["c_p"].append(c_p); outs["n_p"].append(n_p)
            outs["m_p"].append(_last_rows(mt_p, BATCH, SEQ, 1, 0, A_HEADS).reshape(BATCH, A_HEADS))
            outs["n_s"].append(n_s)
            outs["m_s"].append(mt_s[DEC_SEQ - 1::DEC_SEQ, :A_HEADS])
            wo, bo = w_mlstm_out[j], zero_bias
        elif kind == 1:
            perm = _q_perm()
            col_perm = jnp.concatenate([perm, jnp.arange(B_Q, B_IN)])
            rope_tab, rope_consts = _rope_tables()
            proj_p, proj_s = _proj_swa(x_p, x_s, g_mix, w_swa_qkv[j][:, col_perm].astype(bf16),
                                       row(b_swa_qkv[j][col_perm]), rope_tab, rope_consts)
            sinks = swa_sinks[j].astype(f32)
            a_p = _swa_prompt(proj_p, sinks)
            buf = cache_swa_k.shape[2]
            a_s, k_s, v_s = _swa_sample(proj_s, sinks,
                                        cache_swa_k[j].astype(f32).reshape(DEC_BATCH, buf, B_KV),
                                        cache_swa_v[j].astype(f32).reshape(DEC_BATCH, buf, B_KV))
            outs["k_p"].append(_last_rows(proj_p, BATCH, SEQ, WINDOW, B_Q, B_Q + B_KV)
                               .reshape(BATCH, WINDOW, B_KV_HEADS, B_HD))
            outs["v_p"].append(_last_rows(proj_p, BATCH, SEQ, WINDOW, B_Q + B_KV, B_IN)
                               .reshape(BATCH, WINDOW, B_KV_HEADS, B_HD))
            outs["k_s"].append(k_s.reshape(DEC_BATCH, buf, B_KV_HEADS, B_HD))
            outs["v_s"].append(v_s.reshape(DEC_BATCH, buf, B_KV_HEADS, B_HD))
            wo, bo = w_swa_out[j][perm, :], row(b_swa_out[j])
        else:
            proj_p, proj_s = _proj_rg(x_p, x_s, g_mix, w_rg_in[j].astype(bf16))
            weights = (w_rg_conv[j].astype(f32), row(b_rg_conv[j]), w_rg_a[j].astype(bf16), row(b_rg_a[j]),
                       w_rg_x[j].astype(bf16), row(b_rg_x[j]), row(rg_lambda[j]))
            a_p, h_p = _rg_prompt(proj_p, weights)
            conv_pad = jnp.pad(state_rglru_conv[j].astype(f32),
                               ((0, 0), (SUBLANES - (CONV_W - 1), 0), (0, 0))).reshape(N_S, C_WIDTH)
            a_s, h_s = _rg_sample(proj_s, conv_pad, state_rglru_h[j].astype(f32), weights)
            outs["h_p"].append(h_p.reshape(BATCH, C_WIDTH))
            outs["cv_p"].append(_last_rows(proj_p, BATCH, SEQ, CONV_W - 1, 0, C_WIDTH))
            outs["h_s"].append(h_s)
            outs["cv_s"].append(proj_s[:, :C_WIDTH].reshape(DEC_BATCH, DEC_SEQ, C_WIDTH)[:, DEC_SEQ - (CONV_W - 1):])
            wo, bo = w_rg_out[j], zero_bias
        x_p, x_s = _out_mlp(x_p, x_s, a_p, a_s, wo.astype(bf16), bo, row(norm_mlp[i]), w_up_all, w_down_all, i,
                            row(norm_final), final=(i == DEPTH - 1))
    st = {k: jnp.stack(v) for k, v in outs.items()}
    y_p = x_p.reshape(BATCH, SEQ, D_MODEL)
    y_s = x_s.reshape(DEC_BATCH, DEC_SEQ, D_MODEL)
    return (y_p, y_s, st["c_p"], st["n_p"], st["m_p"], c_s_all, st["n_s"], st["m_s"],
            st["k_p"], st["v_p"], st["k_s"], st["v_s"], st["h_p"], st["cv_p"], st["h_s"], st["cv_s"])
```

```python
import functools
import math

import jax
import jax.numpy as jnp
from jax import lax
from jax.experimental import pallas as pl
from jax.experimental.pallas import tpu as pltpu

f32 = jnp.float32
bf16 = jnp.bfloat16

D_MODEL = 1024
BATCH = 2
SEQ = 8192
DEPTH = 4
DEC_BATCH = 128
DEC_SEQ = 8
PAST_LEN = 8192
N_MIXERS = 3
NORM_EPS = 1e-6

A_HEADS = 4
A_DK = 128
A_DV = 256
A_QK = A_HEADS * A_DK
A_MAIN = 2 * A_QK + 2 * A_HEADS * A_DV
N_A = (DEPTH + 2) // 3

B_HEADS = 16
B_KV_HEADS = 4
B_HD = 64
B_GROUP = 4
B_Q = B_HEADS * B_HD
B_KV = B_KV_HEADS * B_HD
B_IN = B_Q + 2 * B_KV
WINDOW = 128
ROPE_THETA = 500000.0
ROPE_DIM = 16

C_WIDTH = 1024
C_BLOCKS = 4
C_BW = 256
CONV_W = 4
LRU_C = 8.0
D_FF = 4096

N_P = BATCH * SEQ
N_S = DEC_BATCH * DEC_SEQ

LANES = 128
SUBLANES = 8
VMEM_LIMIT = 56 * 1024 * 1024

TM = 512
NP_TILES = N_P // TM
NS_TILES = N_S // TM
MLSTM_CHUNK = 256
MLSTM_SB = 16
MLSTM_ST = MLSTM_SB * DEC_SEQ
MLSTM_UNROLL = 8
SWA_SB = 8
SWA_TILE = 512
RG_T = 512
FF_CHUNK = 512


def _dot(a, b):
    return jnp.dot(a, b, preferred_element_type=f32)


def _dot_nt(a, b):
    return lax.dot_general(a, b, (((1,), (1,)), ((), ())), preferred_element_type=f32)


def _dot_exact(a, b):
    return jnp.dot(a, b, preferred_element_type=f32, precision=lax.Precision.HIGHEST)


def _idiv(x, d):
    assert d & (d - 1) == 0
    return x >> (d.bit_length() - 1)


def _imod(x, d):
    assert d & (d - 1) == 0
    return x & (d - 1)


def _rms_bf16(x, g):
    y = x * lax.rsqrt(jnp.mean(x * x, axis=-1, keepdims=True) + NORM_EPS)
    return (y * g).astype(bf16)


def _sigmoid(x):
    return 0.5 * jnp.tanh(0.5 * x) + 0.5


def _sqrt_nonneg(v):
    return jnp.where(v > 0.0, v * lax.rsqrt(v), 0.0)


def _softplus(x):
    return jnp.maximum(x, 0.0) + jnp.log1p(jnp.exp(-jnp.abs(x)))


def _gelu_tanh(x):
    return 0.5 * x * (1.0 + jnp.tanh(math.sqrt(2.0 / math.pi) * (x + 0.044715 * (x * x * x))))


def _params(*sem):
    return pltpu.CompilerParams(dimension_semantics=sem, vmem_limit_bytes=VMEM_LIMIT)


def _const_spec(shape):
    nd = len(shape)
    return pl.BlockSpec(shape, lambda *_: (0,) * nd)


def _p_spec(width):
    return pl.BlockSpec((TM, width), lambda i: (jnp.minimum(i, NP_TILES - 1), 0))


def _s_spec(width):
    return pl.BlockSpec((TM, width), lambda i: (jnp.maximum(i - NP_TILES, 0), 0))


def _for_each_group(body, p_refs, s_refs):
    i = pl.program_id(0)

    @pl.when(i < NP_TILES)
    def _():
        body(*p_refs)

    @pl.when(i >= NP_TILES)
    def _():
        body(*s_refs)


def _proj_mlstm_kernel(xp_ref, xs_ref, g_ref, w_ref, wg_ref, bg_ref, mp_ref, ms_ref, gp_ref, gs_ref):
    def body(x_ref, main_ref, gate_ref):
        xn = _rms_bf16(x_ref[...], g_ref[...])
        for c in range(0, A_MAIN, 512):
            main_ref[:, c:c + 512] = _dot(xn, w_ref[:, c:c + 512])
        gp = _dot(xn, wg_ref[...]) + bg_ref[...]
        lane = lax.broadcasted_iota(jnp.int32, gp.shape, 1)
        gate_ref[...] = jnp.where(lane >= A_HEADS, -_softplus(-gp), gp)

    _for_each_group(body, (xp_ref, mp_ref, gp_ref), (xs_ref, ms_ref, gs_ref))


def _proj_mlstm(x_p, x_s, g, w_main, w_gate, b_gate):
    return pl.pallas_call(
        _proj_mlstm_kernel,
        grid=(NP_TILES + NS_TILES,),
        in_specs=[_p_spec(D_MODEL), _s_spec(D_MODEL),
                  _const_spec((1, D_MODEL)),
                  _const_spec((D_MODEL, A_MAIN)),
                  _const_spec((D_MODEL, LANES)),
                  _const_spec((1, LANES))],
        out_specs=[_p_spec(A_MAIN), _s_spec(A_MAIN), _p_spec(LANES), _s_spec(LANES)],
        out_shape=[jax.ShapeDtypeStruct((N_P, A_MAIN), f32), jax.ShapeDtypeStruct((N_S, A_MAIN), f32),
                   jax.ShapeDtypeStruct((N_P, LANES), f32), jax.ShapeDtypeStruct((N_S, LANES), f32)],
        compiler_params=_params("arbitrary"),
        name="proj_mlstm",
    )(x_p, x_s, g, w_main, w_gate, b_gate)


def _proj_swa_kernel(xp_ref, xs_ref, g_ref, w_ref, b_ref, tab_ref, rc_ref, op_ref, os_ref):
    half = ROPE_DIM // 2

    def body(x_ref, o_ref):
        xn = _rms_bf16(x_ref[...], g_ref[...])
        cos = tab_ref[:, 0:LANES]
        sin = tab_ref[:, LANES:2 * LANES]
        sin_lo = sin * rc_ref[0:1, :]
        sin_hi = sin * rc_ref[1:2, :]
        for c in range(0, B_IN, 512):
            p = _dot(xn, w_ref[:, c:c + 512]) + b_ref[:, c:c + 512]
            for l in range(0, 512, LANES):
                pl_ = p[:, l:l + LANES]
                if c + l < B_Q + B_KV:
                    pl_ = (pl_ * cos + pltpu.roll(pl_, LANES - half, 1) * sin_lo
                           + pltpu.roll(pl_, half, 1) * sin_hi)
                o_ref[:, c + l:c + l + LANES] = pl_

    _for_each_group(body, (xp_ref, op_ref), (xs_ref, os_ref))


def _proj_swa(x_p, x_s, g, w, b, rope_tab, rope_consts):
    seq_tiles = SEQ // TM
    tab_idx = lambda i: (jnp.where(i < NP_TILES, i % seq_tiles, seq_tiles), 0)
    return pl.pallas_call(
        _proj_swa_kernel,
        grid=(NP_TILES + NS_TILES,),
        in_specs=[_p_spec(D_MODEL), _s_spec(D_MODEL),
                  _const_spec((1, D_MODEL)),
                  _const_spec((D_MODEL, B_IN)),
                  _const_spec((1, B_IN)),
                  pl.BlockSpec((TM, 2 * LANES), tab_idx),
                  _const_spec((SUBLANES, LANES))],
        out_specs=[_p_spec(B_IN), _s_spec(B_IN)],
        out_shape=[jax.ShapeDtypeStruct((N_P, B_IN), f32), jax.ShapeDtypeStruct((N_S, B_IN), f32)],
        compiler_params=_params("arbitrary"),
        name="proj_swa",
    )(x_p, x_s, g, w, b, rope_tab, rope_consts)


def _proj_rg_kernel(xp_ref, xs_ref, g_ref, w_ref, op_ref, os_ref):
    def body(x_ref, o_ref):
        xn = _rms_bf16(x_ref[...], g_ref[...])
        for c in range(0, 2 * C_WIDTH, 512):
            o_ref[:, c:c + 512] = _dot(xn, w_ref[:, c:c + 512])

    _for_each_group(body, (xp_ref, op_ref), (xs_ref, os_ref))


def _proj_rg(x_p, x_s, g, w):
    return pl.pallas_call(
        _proj_rg_kernel,
        grid=(NP_TILES + NS_TILES,),
        in_specs=[_p_spec(D_MODEL), _s_spec(D_MODEL),
                  _const_spec((1, D_MODEL)),
                  _const_spec((D_MODEL, 2 * C_WIDTH))],
        out_specs=[_p_spec(2 * C_WIDTH), _s_spec(2 * C_WIDTH)],
        out_shape=[jax.ShapeDtypeStruct((N_P, 2 * C_WIDTH), f32), jax.ShapeDtypeStruct((N_S, 2 * C_WIDTH), f32)],
        compiler_params=_params("arbitrary"),
        name="proj_rg",
    )(x_p, x_s, g, w)


def _mlstm_masks(t, seg):
    r = lax.broadcasted_iota(jnp.int32, (t, t), 0)
    c = lax.broadcasted_iota(jnp.int32, (t, t), 1)
    if seg == t:
        same = None
        lower, upper = r >= c, r <= c
    else:
        same = _idiv(r, seg) == _idiv(c, seg)
        lower, upper = same & (r >= c), same & (r <= c)
    return r, c, same, lower, upper


def _wide(col, n):
    return jnp.concatenate([col] * n, axis=1)


def _row_sum(x):
    acc = x[:, 0:LANES]
    for l in range(LANES, x.shape[1], LANES):
        acc = acc + x[:, l:l + LANES]
    return jnp.broadcast_to(jnp.sum(acc, axis=1, keepdims=True), acc.shape)


def _mlstm_cols(main_ref, h):
    q = main_ref[:, h * A_DK:(h + 1) * A_DK]
    k = main_ref[:, A_QK + h * A_DK:A_QK + (h + 1) * A_DK] * (A_DK ** -0.5)
    v = main_ref[:, 2 * A_QK + h * A_DV:2 * A_QK + (h + 1) * A_DV]
    o_pre = main_ref[:, 2 * A_QK + (A_HEADS + h) * A_DV:2 * A_QK + (A_HEADS + h + 1) * A_DV]
    return q, k, v, o_pre


def _mlstm_prompt_chunk(main_ref, gc_ref, gr_ref, gh_ref, hs_ref, c_ref, n_ref, mt_ref, c_s, n_s, m_s):
    t = MLSTM_CHUNK
    j = pl.program_id(0)

    @pl.when(j == 0)
    def _():
        c_s[...] = jnp.zeros_like(c_s)
        n_s[...] = jnp.zeros_like(n_s)
        m_s[...] = jnp.zeros_like(m_s)

    _, _, _, lower, upper = _mlstm_masks(t, t)
    lower_f = lower.astype(f32)
    gc = gc_ref[...]
    gr = gr_ref[...]
    btr_all = _dot_exact(gr, upper.astype(f32))
    lane = lax.broadcasted_iota(jnp.int32, (t, LANES), 1)
    mt_all = jnp.zeros((t, LANES), f32)

    def rep(col):
        return jnp.broadcast_to(col, (t, LANES))

    wide, row_sum = _wide, _row_sum

    heads = range(A_HEADS)
    stack = lambda xs: jnp.concatenate(list(xs), axis=0)
    head = lambda x, h: x[h * t:(h + 1) * t]
    cols = [_mlstm_cols(main_ref, h) for h in heads]
    qs, ks, vs = [c[0] for c in cols], [c[1] for c in cols], [c[2] for c in cols]
    qbs, kbs, vbs = [q.astype(bf16) for q in qs], [k.astype(bf16) for k in ks], [v.astype(bf16) for v in vs]
    cts = [c_s[h] for h in heads]
    ns = [n_s[h:h + 1, :] for h in heads]
    btrs = [btr_all[A_HEADS + h:A_HEADS + h + 1, :] for h in heads]
    itrs = [gr[h:h + 1, :] for h in heads]

    lf_rep = jnp.concatenate([rep(gc[:, A_HEADS + h:A_HEADS + h + 1]) for h in heads], axis=1)
    btc_all = _dot_exact(lower_f, lf_rep)
    btc = stack(btc_all[:, h * LANES:(h + 1) * LANES] for h in heads)
    itc = stack(rep(gc[:, h:h + 1]) for h in heads)
    m_in = stack(rep(m_s[0:1, h:h + 1]) for h in heads)
    dmat = stack(jnp.where(lower, wide(head(btc, h), t // LANES) + (itrs[h] - btrs[h]), -jnp.inf) for h in heads)
    inter = btc + m_in
    m_t = jnp.maximum(jnp.broadcast_to(jnp.max(dmat, axis=1, keepdims=True), inter.shape), inter)
    w = jnp.exp(dmat - wide(m_t, t // LANES))
    w_inter = jnp.exp(inter - m_t)
    s = stack(_dot_nt(qbs[h], kbs[h]) for h in heads) * w
    sb = s.astype(bf16)
    num = (stack(_dot(head(sb, h), vbs[h]) for h in heads)
           + wide(w_inter, A_DV // LANES) * stack(_dot(qbs[h], cts[h].astype(bf16)) for h in heads))
    den = row_sum(s) + w_inter * row_sum(stack(qs[h] * ns[h] for h in heads))
    inv = 1.0 / jnp.maximum(jnp.abs(den), jnp.exp(-m_t))
    norm = inv * lax.rsqrt(inv * inv * (row_sum(num * num) * (1.0 / A_DV)) + NORM_EPS)
    out = _sigmoid(stack(c[3] for c in cols)) * (num * wide(norm, A_DV // LANES))
    for h in heads:
        hs_ref[:, h * A_DV:(h + 1) * A_DV] = (head(out, h) * gh_ref[h:h + 1, :]).astype(bf16)

    m_news = [head(m_t, h)[t - 1:t, 0:1] for h in heads]
    last = lambda x: stack(rep(head(x, h)[t - 1:t, 0:1]) for h in heads)
    decay_c = jnp.exp(last(btc) - btc + itc - last(m_t))
    for h in heads:
        scale = jnp.exp(head(inter, h)[t - 1:t, 0:1] - m_news[h])
        decay_r = jnp.exp(btrs[h][:, t - 1:t] - btrs[h] + itrs[h] - m_news[h])
        c_s[h] = scale * cts[h] + _dot((ks[h].T * decay_r).astype(bf16), vbs[h])
        n_s[h:h + 1, :] = scale * ns[h] + jnp.sum(head(decay_c, h) * ks[h], axis=0, keepdims=True)
        mt_all = jnp.where(lane == h, head(m_t, h), mt_all)
    mt_ref[...] = mt_all
    m_s[0:1, :] = mt_all[t - 1:t, :]

    @pl.when(j == pl.num_programs(0) - 1)
    def _():
        for h in range(A_HEADS):
            c_ref[h] = c_s[h].T
        n_ref[...] = n_s[...]


def _mlstm_prompt_kernel(main_ref, gc_ref, gr_ref, gh_ref, hs_ref, c_ref, n_ref, mt_ref, c_s, n_s, m_s):
    for b in range(BATCH):
        _mlstm_prompt_chunk(main_ref.at[b], gc_ref.at[b], gr_ref.at[b], gh_ref, hs_ref.at[b], c_ref.at[b],
                            n_ref.at[b], mt_ref.at[b], c_s.at[b], n_s.at[b], m_s.at[b])


def _mlstm_prompt(main, gates, gates_t, g_head):
    t = MLSTM_CHUNK
    return pl.pallas_call(
        _mlstm_prompt_kernel,
        grid=(SEQ // t,),
        in_specs=[pl.BlockSpec((BATCH, t, A_MAIN), lambda j: (0, j, 0)),
                  pl.BlockSpec((BATCH, t, LANES), lambda j: (0, j, 0)),
                  pl.BlockSpec((BATCH, SUBLANES, t), lambda j: (0, 0, j)),
                  _const_spec((A_HEADS, A_DV))],
        out_specs=[pl.BlockSpec((BATCH, t, A_HEADS * A_DV), lambda j: (0, j, 0)),
                   _const_spec((BATCH, A_HEADS, A_DV, A_DK)),
                   _const_spec((BATCH, A_HEADS, A_DK)),
                   pl.BlockSpec((BATCH, t, LANES), lambda j: (0, j, 0))],
        out_shape=[jax.ShapeDtypeStruct((BATCH, SEQ, A_HEADS * A_DV), bf16),
                   jax.ShapeDtypeStruct((BATCH, A_HEADS, A_DV, A_DK), f32),
                   jax.ShapeDtypeStruct((BATCH, A_HEADS, A_DK), f32),
                   jax.ShapeDtypeStruct((BATCH, SEQ, LANES), f32)],
        scratch_shapes=[pltpu.VMEM((BATCH, A_HEADS, A_DK, A_DV), f32),
                        pltpu.VMEM((BATCH, A_HEADS, A_DK), f32),
                        pltpu.VMEM((BATCH, SUBLANES, LANES), f32)],
        compiler_params=_params("arbitrary"),
        name="mlstm_prompt",
    )(main, gates, gates_t, g_head)


def _mlstm_sample_compute(main_ref, gc_ref, gr_ref, m0_ref, c0_ref, n0_ref, gh_ref,
                          hs_ref, n_ref, mt_ref, vdt_ref, kb_ref, sc_ref, acc_s, nt_s):
    t = MLSTM_ST
    seg = DEC_SEQ
    _, c, same, lower, upper = _mlstm_masks(t, seg)
    seg_last = (same & (_imod(c, seg) == seg - 1)).astype(f32)
    same_f = same.astype(f32)
    gc = gc_ref[...]
    gr = gr_ref[...]
    btr_all = _dot_exact(gr, upper.astype(f32))
    lane = lax.broadcasted_iota(jnp.int32, (t, LANES), 1)
    tok_seg = _idiv(lax.broadcasted_iota(jnp.int32, (A_DV, t), 1), seg)

    heads = range(A_HEADS)
    rep = lambda col: jnp.broadcast_to(col, (t, LANES))
    stack = lambda xs: jnp.concatenate(list(xs), axis=0)
    lanes = lambda xs: jnp.concatenate(list(xs), axis=1)
    head = lambda x, h: x[h * t:(h + 1) * t]
    cols = [_mlstm_cols(main_ref, h) for h in heads]
    qs, ks, vs = [c_[0] for c_ in cols], [c_[1] for c_ in cols], [c_[2] for c_ in cols]
    qbs, kbs, vbs = [q.astype(bf16) for q in qs], [k.astype(bf16) for k in ks], [v.astype(bf16) for v in vs]

    btc_all = _dot_exact(lower.astype(f32), lanes(rep(gc[:, A_HEADS + h:A_HEADS + h + 1]) for h in heads))
    btc = stack(btc_all[:, h * LANES:(h + 1) * LANES] for h in heads)
    itc = stack(rep(gc[:, h:h + 1]) for h in heads)
    m_in = stack(rep(m0_ref[:, h:h + 1]) for h in heads)
    dmat = stack(jnp.where(lower, head(btc, h) + (gr[h:h + 1, :] - btr_all[A_HEADS + h:A_HEADS + h + 1, :]), -jnp.inf)
                 for h in heads)
    inter = btc + m_in
    m_t = jnp.maximum(jnp.broadcast_to(jnp.max(dmat, axis=1, keepdims=True), inter.shape), inter)
    w = jnp.exp(dmat - m_t)
    w_inter = jnp.exp(inter - m_t)
    s = stack(_dot_nt(qbs[h], kbs[h]) for h in heads) * w
    sb = s.astype(bf16)
    num = stack(_dot(head(sb, h), vbs[h]) for h in heads)

    for h in heads:
        def inter_body(b, carry, h=h):
            rows = pl.ds(pl.multiple_of(h * t + b * seg, seg), seg)
            r = _dot_nt(c0_ref[b, h].astype(bf16), qbs[h])
            acc_s[h] = jnp.where(tok_seg == b, r, acc_s[h])
            nt_s[rows, :] = jnp.broadcast_to(n0_ref[b, h:h + 1, :], (seg, A_DK))
            return carry

        acc_s[h] = jnp.zeros((A_DV, t), f32)
        lax.fori_loop(0, MLSTM_SB, inter_body, 0, unroll=MLSTM_UNROLL)
    n_tok = nt_s[...]
    num = num + _wide(w_inter, A_DV // LANES) * stack(acc_s[h].T for h in heads)
    den = _row_sum(s) + w_inter * _row_sum(stack(qs) * n_tok)
    inv = 1.0 / jnp.maximum(jnp.abs(den), jnp.exp(-m_t))
    norm = inv * lax.rsqrt(inv * inv * (_row_sum(num * num) * (1.0 / A_DV)) + NORM_EPS)
    out = _sigmoid(stack(c_[3] for c_ in cols)) * (num * _wide(norm, A_DV // LANES))
    for h in heads:
        hs_ref[:, h * A_DV:(h + 1) * A_DV] = (head(out, h) * gh_ref[h:h + 1, :]).astype(bf16)

    last = _dot_exact(seg_last, lanes([head(x, h) for x in (m_t, btc, inter) for h in heads]))
    pick = lambda i: stack(last[:, (i * A_HEADS + h) * LANES:(i * A_HEADS + h + 1) * LANES] for h in heads)
    m_new, bt_last, inter_last = pick(0), pick(1), pick(2)
    decay = jnp.exp(bt_last - btc + itc - m_new)
    scale = jnp.exp(inter_last - m_new)
    seg_sum = _dot_exact(same_f, lanes(head(decay, h) * ks[h] for h in heads))
    n_new = scale * n_tok + stack(seg_sum[:, h * A_DK:(h + 1) * A_DK] for h in heads)
    mt_all = jnp.zeros((t, LANES), f32)
    sc_all = jnp.zeros((t, LANES), f32)
    for h in heads:
        vdt_ref[h] = (_wide(head(decay, h), A_DV // LANES) * vs[h]).T.astype(bf16)
        kb_ref[:, h * A_DK:(h + 1) * A_DK] = kbs[h]
        sc_all = jnp.where(lane == h, head(scale, h), sc_all)
        mt_all = jnp.where(lane == h, head(m_t, h), mt_all)
    for b in range(MLSTM_SB):
        n_ref[b] = stack(head(n_new, h)[b * seg:b * seg + 1, :] for h in heads)
    mt_ref[...] = mt_all
    sc_ref[...] = sc_all


def _mlstm_sample_update(c_in_ref, c_out_ref, vdt_ref, kb_ref, sc_ref):
    seg = DEC_SEQ
    lane_seg = _idiv(lax.broadcasted_iota(jnp.int32, (1, MLSTM_ST), 1), seg)
    for h in range(A_HEADS):
        def upd_body(b, carry, h=h):
            rows = pl.ds(pl.multiple_of(b * seg, seg), seg)
            scale = sc_ref[rows, :][0:1, h:h + 1]
            onehot = jnp.where(lane_seg == b, 1.0, 0.0).astype(bf16)
            upd = _dot(vdt_ref[h] * onehot, kb_ref[:, h * A_DK:(h + 1) * A_DK])
            c_out_ref[b, h] = scale * c_in_ref[b, h] + upd
            return carry

        lax.fori_loop(0, MLSTM_SB, upd_body, 0, unroll=MLSTM_UNROLL)


def _mlstm_sample_first_kernel(main_ref, gc_ref, gr_ref, m0_ref, c0_ref, n0_ref, gh_ref,
                               hs_ref, n_ref, mt_ref, vdt_ref, kb_ref, sc_ref, acc_s, nt_s):
    _mlstm_sample_compute(main_ref, gc_ref, gr_ref, m0_ref, c0_ref, n0_ref, gh_ref,
                          hs_ref, n_ref, mt_ref, vdt_ref, kb_ref, sc_ref, acc_s, nt_s)


def _mlstm_sample_second_kernel(main_ref, gc_ref, gr_ref, m0_ref, c0_ref, n0_ref, gh_ref, vdt0_ref, kb0_ref, sc0_ref,
                                hs_ref, n_ref, mt_ref, c_ref, vdt_s, kb_s, sc_s, acc_s, nt_s):
    layer = pl.program_id(0)

    @pl.when(layer == 0)
    def _():
        _mlstm_sample_update(c0_ref, c_ref, vdt0_ref, kb0_ref, sc0_ref)

    @pl.when(layer == 1)
    def _():
        _mlstm_sample_compute(main_ref, gc_ref, gr_ref, m0_ref, c0_ref, n0_ref, gh_ref,
                              hs_ref, n_ref, mt_ref, vdt_s, kb_s, sc_s, acc_s, nt_s)
        _mlstm_sample_update(c0_ref, c_ref, vdt_s, kb_s, sc_s)


def _mlstm_sample_scratch():
    t = MLSTM_ST
    return [pltpu.VMEM((A_HEADS, A_DV, t), f32), pltpu.VMEM((A_HEADS * t, A_DK), f32)]


def _mlstm_sample_first(main, gates, gates_t, m0_tok, c_all, n0, g_head):
    t = MLSTM_ST
    return pl.pallas_call(
        _mlstm_sample_first_kernel,
        grid=(DEC_BATCH // MLSTM_SB,),
        in_specs=[pl.BlockSpec((t, A_MAIN), lambda i: (i, 0)),
                  pl.BlockSpec((t, LANES), lambda i: (i, 0)),
                  pl.BlockSpec((SUBLANES, t), lambda i: (0, i)),
                  pl.BlockSpec((t, LANES), lambda i: (i, 0)),
                  pl.BlockSpec((None, MLSTM_SB, A_HEADS, A_DV, A_DK), lambda i: (0, i, 0, 0, 0)),
                  pl.BlockSpec((MLSTM_SB, A_HEADS, A_DK), lambda i: (i, 0, 0)),
                  _const_spec((A_HEADS, A_DV))],
        out_specs=[pl.BlockSpec((t, A_HEADS * A_DV), lambda i: (i, 0)),
                   pl.BlockSpec((MLSTM_SB, A_HEADS, A_DK), lambda i: (i, 0, 0)),
                   pl.BlockSpec((t, LANES), lambda i: (i, 0)),
                   pl.BlockSpec((A_HEADS, A_DV, t), lambda i: (0, 0, i)),
                   pl.BlockSpec((t, A_QK), lambda i: (i, 0)),
                   pl.BlockSpec((t, LANES), lambda i: (i, 0))],
        out_shape=[jax.ShapeDtypeStruct((N_S, A_HEADS * A_DV), bf16),
                   jax.ShapeDtypeStruct((DEC_BATCH, A_HEADS, A_DK), f32),
                   jax.ShapeDtypeStruct((N_S, LANES), f32),
                   jax.ShapeDtypeStruct((A_HEADS, A_DV, N_S), bf16),
                   jax.ShapeDtypeStruct((N_S, A_QK), bf16),
                   jax.ShapeDtypeStruct((N_S, LANES), f32)],
        scratch_shapes=_mlstm_sample_scratch(),
        compiler_params=_params("parallel"),
        name="mlstm_sample_first",
    )(main, gates, gates_t, m0_tok, c_all, n0, g_head)


def _mlstm_sample_second(main, gates, gates_t, m0_tok, c_all, n0, g_head, vdt0, kb0, sc0):
    t = MLSTM_ST
    own = lambda l, i: i * l
    first = lambda l, i: i * (1 - l)
    return pl.pallas_call(
        _mlstm_sample_second_kernel,
        grid=(N_A, DEC_BATCH // MLSTM_SB),
        in_specs=[pl.BlockSpec((t, A_MAIN), lambda l, i: (own(l, i), 0)),
                  pl.BlockSpec((t, LANES), lambda l, i: (own(l, i), 0)),
                  pl.BlockSpec((SUBLANES, t), lambda l, i: (0, own(l, i))),
                  pl.BlockSpec((t, LANES), lambda l, i: (own(l, i), 0)),
                  pl.BlockSpec((None, MLSTM_SB, A_HEADS, A_DV, A_DK), lambda l, i: (l, i, 0, 0, 0)),
                  pl.BlockSpec((MLSTM_SB, A_HEADS, A_DK), lambda l, i: (own(l, i), 0, 0)),
                  _const_spec((A_HEADS, A_DV)),
                  pl.BlockSpec((A_HEADS, A_DV, t), lambda l, i: (0, 0, first(l, i))),
                  pl.BlockSpec((t, A_QK), lambda l, i: (first(l, i), 0)),
                  pl.BlockSpec((t, LANES), lambda l, i: (first(l, i), 0))],
        out_specs=[pl.BlockSpec((t, A_HEADS * A_DV), lambda l, i: (own(l, i), 0)),
                   pl.BlockSpec((MLSTM_SB, A_HEADS, A_DK), lambda l, i: (own(l, i), 0, 0)),
                   pl.BlockSpec((t, LANES), lambda l, i: (own(l, i), 0)),
                   pl.BlockSpec((None, MLSTM_SB, A_HEADS, A_DV, A_DK), lambda l, i: (l, i, 0, 0, 0))],
        out_shape=[jax.ShapeDtypeStruct((N_S, A_HEADS * A_DV), bf16),
                   jax.ShapeDtypeStruct((DEC_BATCH, A_HEADS, A_DK), f32),
                   jax.ShapeDtypeStruct((N_S, LANES), f32),
                   jax.ShapeDtypeStruct((N_A, DEC_BATCH, A_HEADS, A_DV, A_DK), f32)],
        scratch_shapes=[pltpu.VMEM((A_HEADS, A_DV, t), bf16),
                        pltpu.VMEM((t, A_QK), bf16),
                        pltpu.VMEM((t, LANES), f32)] + _mlstm_sample_scratch(),
        compiler_params=_params("arbitrary", "arbitrary"),
        name="mlstm_sample_second",
    )(main, gates, gates_t, m0_tok, c_all, n0, g_head, vdt0, kb0, sc0)


def _swa_softmax_pv(s, allowed, sink_col, vb):
    s = jnp.where(allowed, s * (B_HD ** -0.5), -jnp.inf)
    m = jnp.maximum(jnp.max(s, axis=1, keepdims=True), sink_col)
    p = jnp.exp(s - m)
    den = jnp.sum(p, axis=1, keepdims=True) + jnp.exp(sink_col - m)
    return _dot((p / den).astype(bf16), vb)


def _swa_window_block(q_groups, kb, vb, sink_ref, has_prev):
    w = WINDOW
    rows = B_GROUP * w
    t = _imod(lax.broadcasted_iota(jnp.int32, (rows, 2 * w), 0), w)
    jj = lax.broadcasted_iota(jnp.int32, (rows, 2 * w), 1)
    bias = jnp.where((jj > t) & (jj <= t + w) & (has_prev | (jj >= w)), 0.0, -jnp.inf)
    lane_head = _idiv(lax.broadcasted_iota(jnp.int32, (1, B_KV), 1), B_HD)
    grp = _idiv(lax.broadcasted_iota(jnp.int32, (rows, 1), 0), w)
    outs = [jnp.zeros((w, B_KV), f32) for _ in range(B_GROUP)]
    for h in range(B_KV_HEADS):
        hm = lane_head == h
        qscale = jnp.where(hm, B_HD ** -0.5, 0.0)
        qh = jnp.concatenate([q * qscale for q in q_groups], axis=0).astype(bf16)
        sink_col = jnp.zeros((rows, 1), f32)
        for g in range(B_GROUP):
            sink_col = jnp.where(grp == g, sink_ref[h * B_GROUP + g], sink_col)
        s = _dot_nt(qh, kb) + bias
        m = jnp.maximum(jnp.max(s, axis=1, keepdims=True), sink_col)
        p = jnp.exp(s - m)
        den = jnp.sum(p, axis=1, keepdims=True) + jnp.exp(sink_col - m)
        r = _dot((p * (1.0 / den)).astype(bf16), vb)
        for g in range(B_GROUP):
            outs[g] = jnp.where(hm, r[g * w:(g + 1) * w, :], outs[g])
    return outs


def _swa_prompt_kernel(sink_ref, q_ref, ko_ref, vo_ref, kp_ref, vp_ref, a_ref):
    w = WINDOW
    tile_has_prev = pl.program_id(1) > 0
    for nb in range(SWA_TILE // w):
        r0 = nb * w
        if nb == 0:
            k_prev, v_prev, has_prev = kp_ref[...], vp_ref[...], tile_has_prev
        else:
            k_prev, v_prev, has_prev = ko_ref[r0 - w:r0, :], vo_ref[r0 - w:r0, :], True
        kb = jnp.concatenate([k_prev, ko_ref[r0:r0 + w, :]], axis=0).astype(bf16)
        vb = jnp.concatenate([v_prev, vo_ref[r0:r0 + w, :]], axis=0).astype(bf16)
        outs = _swa_window_block([q_ref[r0:r0 + w, g * B_KV:(g + 1) * B_KV] for g in range(B_GROUP)],
                                 kb, vb, sink_ref, has_prev)
        for g in range(B_GROUP):
            a_ref[r0:r0 + w, g * B_KV:(g + 1) * B_KV] = outs[g].astype(bf16)


def _swa_prompt(proj, sinks):
    nt = SEQ // SWA_TILE
    w = WINDOW
    per = SWA_TILE // w
    kcol = B_Q // B_KV
    vcol = kcol + 1
    tile = lambda b, n: b * nt + n
    prev = lambda b, n: jnp.maximum(tile(b, n) * per - 1, 0)
    return pl.pallas_call(
        _swa_prompt_kernel,
        grid=(BATCH, nt),
        in_specs=[pl.BlockSpec(memory_space=pltpu.SMEM),
                  pl.BlockSpec((SWA_TILE, B_Q), lambda b, n: (tile(b, n), 0)),
                  pl.BlockSpec((SWA_TILE, B_KV), lambda b, n: (tile(b, n), kcol)),
                  pl.BlockSpec((SWA_TILE, B_KV), lambda b, n: (tile(b, n), vcol)),
                  pl.BlockSpec((w, B_KV), lambda b, n: (prev(b, n), kcol)),
                  pl.BlockSpec((w, B_KV), lambda b, n: (prev(b, n), vcol))],
        out_specs=pl.BlockSpec((SWA_TILE, B_Q), lambda b, n: (tile(b, n), 0)),
        out_shape=jax.ShapeDtypeStruct((N_P, B_Q), bf16),
        compiler_params=_params("parallel", "arbitrary"),
        name="swa_prompt",
    )(sinks, proj, proj, proj, proj, proj)


def _swa_sample_kernel(sink_ref, q_ref, kn_ref, vn_ref, kc_ref, vc_ref, o_ref, ko_ref, vo_ref):
    s_len = DEC_SEQ
    buf = WINDOW
    rows = B_HEADS * s_len
    keys = 2 * buf
    ri = lax.broadcasted_iota(jnp.int32, (rows, keys), 0)
    jj = lax.broadcasted_iota(jnp.int32, (rows, keys), 1)
    t = _imod(ri, s_len)
    allowed = ((jj < buf) & (jj > t)) | ((jj >= buf) & (jj - buf <= t))
    lane_head = _idiv(lax.broadcasted_iota(jnp.int32, (1, B_KV), 1), B_HD)
    hg = _idiv(lax.broadcasted_iota(jnp.int32, (rows, 1), 0), s_len)
    sink_col = jnp.zeros((rows, 1), f32)
    for i in range(B_HEADS):
        sink_col = jnp.where(hg == i, sink_ref[i], sink_col)
    pad = jnp.zeros((keys - buf - s_len, B_KV), f32)

    def body(e, carry):
        rs = pl.ds(pl.multiple_of(e * s_len, s_len), s_len)
        kn = kn_ref[rs, :]
        vn = vn_ref[rs, :]
        kc = kc_ref[e]
        vc = vc_ref[e]
        kpad = jnp.concatenate([kc, kn, pad], axis=0).astype(bf16)
        vpad = jnp.concatenate([vc, vn, pad], axis=0).astype(bf16)
        ko_ref[e, 0:buf - s_len, :] = kc[s_len:, :]
        ko_ref[e, buf - s_len:, :] = kn
        vo_ref[e, 0:buf - s_len, :] = vc[s_len:, :]
        vo_ref[e, buf - s_len:, :] = vn
        qe = q_ref[rs, :]
        qbig = jnp.concatenate(
            [jnp.where(lane_head == h, qe[:, g * B_KV:(g + 1) * B_KV], 0.0)
             for h in range(B_KV_HEADS) for g in range(B_GROUP)], axis=0).astype(bf16)
        r = _swa_softmax_pv(_dot_nt(qbig, kpad), allowed, sink_col, vpad)
        for g in range(B_GROUP):
            og = jnp.zeros((s_len, B_KV), f32)
            for h in range(B_KV_HEADS):
                blk = (h * B_GROUP + g) * s_len
                og = og + jnp.where(lane_head == h, r[blk:blk + s_len, :], 0.0)
            o_ref[rs, g * B_KV:(g + 1) * B_KV] = og.astype(bf16)
        return carry

    lax.fori_loop(0, SWA_SB, body, 0, unroll=4)


def _swa_sample(proj, sinks, k_cache, v_cache):
    t = SWA_SB * DEC_SEQ
    kcol = B_Q // B_KV
    return pl.pallas_call(
        _swa_sample_kernel,
        grid=(DEC_BATCH // SWA_SB,),
        in_specs=[pl.BlockSpec(memory_space=pltpu.SMEM),
                  pl.BlockSpec((t, B_Q), lambda i: (i, 0)),
                  pl.BlockSpec((t, B_KV), lambda i: (i, kcol)),
                  pl.BlockSpec((t, B_KV), lambda i: (i, kcol + 1)),
                  pl.BlockSpec((SWA_SB, WINDOW, B_KV), lambda i: (i, 0, 0)),
                  pl.BlockSpec((SWA_SB, WINDOW, B_KV), lambda i: (i, 0, 0))],
        out_specs=[pl.BlockSpec((t, B_Q), lambda i: (i, 0)),
                   pl.BlockSpec((SWA_SB, WINDOW, B_KV), lambda i: (i, 0, 0)),
                   pl.BlockSpec((SWA_SB, WINDOW, B_KV), lambda i: (i, 0, 0))],
        out_shape=[jax.ShapeDtypeStruct((N_S, B_Q), bf16),
                   jax.ShapeDtypeStruct((DEC_BATCH, WINDOW, B_KV), f32),
                   jax.ShapeDtypeStruct((DEC_BATCH, WINDOW, B_KV), f32)],
        compiler_params=_params("parallel"),
        name="swa_sample",
    )(sinks, proj, proj, proj, k_cache, v_cache)


def _rg_conv_group(x8, p8, wc_ref, bc_ref):
    row = lax.broadcasted_iota(jnp.int32, x8.shape, 0)
    u = bc_ref[...] + wc_ref[CONV_W - 1:CONV_W, :] * x8
    for d in range(1, CONV_W):
        sh = pltpu.roll(jnp.where(row >= SUBLANES - d, p8, x8), d, 0)
        u = u + wc_ref[CONV_W - 1 - d:CONV_W - d, :] * sh
    return u


def _rg_scan_group(a8, b8, carry):
    row = lax.broadcasted_iota(jnp.int32, a8.shape, 0)
    for d in (1, 2, 4):
        keep = row >= d
        b8 = jnp.where(keep, a8 * pltpu.roll(b8, d, 0) + b8, b8)
        a8 = jnp.where(keep, a8 * pltpu.roll(a8, d, 0), a8)
    return a8 * carry + b8


def _rg_gate_rows(ra, rx, neg_rate, u):
    z = _sigmoid(ra) * neg_rate
    a = jnp.exp(-z)
    return a, _sqrt_nonneg(jnp.tanh(z) * (a * a + 1.0)) * _sigmoid(rx) * u


def _rg_gates(u, gate, wa_ref, ba_ref, wx_ref, bx_ref, lam_ref):
    ub = u.astype(bf16)
    ra = jnp.concatenate([_dot(ub[:, n * C_BW:(n + 1) * C_BW], wa_ref[n]) for n in range(C_BLOCKS)], axis=1)
    rx = jnp.concatenate([_dot(ub[:, n * C_BW:(n + 1) * C_BW], wx_ref[n]) for n in range(C_BLOCKS)], axis=1)
    a, bterm = _rg_gate_rows(ra + ba_ref[...], rx + bx_ref[...], LRU_C * _softplus(-lam_ref[...]), u)
    return a, bterm, _gelu_tanh(gate)


def _rg_prompt_kernel(p_ref, wc_ref, bc_ref, wa_ref, ba_ref, wx_ref, bx_ref, lam_ref,
                      y_ref, h_ref, u_s, a_s, b_s, xc_s, hc_s):
    j = pl.program_id(1)
    ng = RG_T // SUBLANES
    out_rows = 2 * SUBLANES

    @pl.when(j == 0)
    def _():
        xc_s[...] = jnp.zeros_like(xc_s)
        hc_s[...] = jnp.zeros_like(hc_s)

    def group(gidx, n=SUBLANES):
        return pl.ds(pl.multiple_of(gidx * n, n), n)

    def conv_body(gidx, p8):
        x8 = p_ref[group(gidx), 0:C_WIDTH]
        u_s[group(gidx), :] = _rg_conv_group(x8, p8, wc_ref, bc_ref)
        return x8

    xc_s[...] = lax.fori_loop(0, ng, conv_body, xc_s[...], unroll=2)
    ub = u_s[...].astype(bf16)
    for n in range(C_BLOCKS):
        cols = slice(n * C_BW, (n + 1) * C_BW)
        a_s[:, cols] = _dot(ub[:, cols], wa_ref[n])
        b_s[:, cols] = _dot(ub[:, cols], wx_ref[n])
    neg_rate = LRU_C * _softplus(-lam_ref[...])

    def gate_body(gidx, carry):
        rows = group(gidx)
        a, bterm = _rg_gate_rows(a_s[rows, :] + ba_ref[...], b_s[rows, :] + bx_ref[...], neg_rate, u_s[rows, :])
        a_s[rows, :] = a
        b_s[rows, :] = bterm
        return carry

    lax.fori_loop(0, ng, gate_body, 0, unroll=2)

    def scan_body(gidx, carry):
        rows = group(gidx, out_rows)
        lo = pl.ds(pl.multiple_of(gidx * out_rows, out_rows), SUBLANES)
        hi = pl.ds(pl.multiple_of(gidx * out_rows + SUBLANES, SUBLANES), SUBLANES)
        h_lo = _rg_scan_group(a_s[lo, :], b_s[lo, :], carry)
        h_hi = _rg_scan_group(a_s[hi, :], b_s[hi, :], h_lo[SUBLANES - 1:SUBLANES, :])
        h16 = jnp.concatenate([h_lo, h_hi], axis=0)
        y_ref[rows, :] = (h16 * _gelu_tanh(p_ref[rows, C_WIDTH:])).astype(bf16)
        return h_hi[SUBLANES - 1:SUBLANES, :]

    h_last = lax.fori_loop(0, RG_T // out_rows, scan_body, hc_s[...])
    hc_s[...] = h_last

    @pl.when(j == pl.num_programs(1) - 1)
    def _():
        h_ref[0] = h_last


def _rg_weight_specs():
    return [_const_spec((CONV_W, C_WIDTH)), _const_spec((1, C_WIDTH)),
            _const_spec((C_BLOCKS, C_BW, C_BW)), _const_spec((1, C_WIDTH)),
            _const_spec((C_BLOCKS, C_BW, C_BW)), _const_spec((1, C_WIDTH)),
            _const_spec((1, C_WIDTH))]


def _rg_prompt(proj, weights):
    nt = SEQ // RG_T
    return pl.pallas_call(
        _rg_prompt_kernel,
        grid=(BATCH, nt),
        in_specs=[pl.BlockSpec((RG_T, 2 * C_WIDTH), lambda b, j: (b * nt + j, 0))] + _rg_weight_specs(),
        out_specs=[pl.BlockSpec((RG_T, C_WIDTH), lambda b, j: (b * nt + j, 0)),
                   pl.BlockSpec((1, 1, C_WIDTH), lambda b, j: (b, 0, 0))],
        out_shape=[jax.ShapeDtypeStruct((N_P, C_WIDTH), bf16),
                   jax.ShapeDtypeStruct((BATCH, 1, C_WIDTH), f32)],
        scratch_shapes=[pltpu.VMEM((RG_T, C_WIDTH), f32),
                        pltpu.VMEM((RG_T, C_WIDTH), f32),
                        pltpu.VMEM((RG_T, C_WIDTH), f32),
                        pltpu.VMEM((SUBLANES, C_WIDTH), f32),
                        pltpu.VMEM((1, C_WIDTH), f32)],
        compiler_params=_params("parallel", "arbitrary"),
        name="rg_prompt",
    )(proj, *weights)


def _rg_sample_kernel(p_ref, cv_ref, h0_ref, wc_ref, bc_ref, wa_ref, ba_ref, wx_ref, bx_ref, lam_ref,
                      y_ref, h_ref, u_s, a_s, b_s):
    def conv_body(gidx, carry):
        rows = pl.ds(pl.multiple_of(gidx * SUBLANES, SUBLANES), SUBLANES)
        u_s[rows, :] = _rg_conv_group(p_ref[rows, 0:C_WIDTH], cv_ref[rows, :], wc_ref, bc_ref)
        return carry

    lax.fori_loop(0, DEC_BATCH, conv_body, 0)
    a, bterm, gg = _rg_gates(u_s[...], p_ref[:, C_WIDTH:], wa_ref, ba_ref, wx_ref, bx_ref, lam_ref)
    a_s[...] = a
    b_s[...] = bterm

    def scan_body(gidx, carry):
        rows = pl.ds(pl.multiple_of(gidx * SUBLANES, SUBLANES), SUBLANES)
        h8 = _rg_scan_group(a_s[rows, :], b_s[rows, :], h0_ref[pl.ds(gidx, 1), :])
        u_s[rows, :] = h8
        h_ref[pl.ds(gidx, 1), :] = h8[SUBLANES - 1:SUBLANES, :]
        return carry

    lax.fori_loop(0, DEC_BATCH, scan_body, 0)
    y_ref[...] = (u_s[...] * gg).astype(bf16)


def _rg_sample(proj, conv_pad, h0, weights):
    return pl.pallas_call(
        _rg_sample_kernel,
        grid=(1,),
        in_specs=[_const_spec((N_S, 2 * C_WIDTH)),
                  _const_spec((N_S, C_WIDTH)),
                  _const_spec((DEC_BATCH, C_WIDTH))] + _rg_weight_specs(),
        out_specs=[_const_spec((N_S, C_WIDTH)), _const_spec((DEC_BATCH, C_WIDTH))],
        out_shape=[jax.ShapeDtypeStruct((N_S, C_WIDTH), bf16),
                   jax.ShapeDtypeStruct((DEC_BATCH, C_WIDTH), f32)],
        scratch_shapes=[pltpu.VMEM((N_S, C_WIDTH), f32),
                        pltpu.VMEM((N_S, C_WIDTH), f32),
                        pltpu.VMEM((N_S, C_WIDTH), f32)],
        compiler_params=_params("arbitrary"),
        name="rg_sample",
    )(proj, conv_pad, h0, *weights)


def _mlp_tile(x_ref, a_ref, o_ref, wo_ref, bo_ref, g_ref, wup_ref, wdn_ref, gf_ref, x1_s, xn_s, final):
    x1_s[...] = x_ref[...] + _dot(a_ref[...], wo_ref[...]) + bo_ref[...]
    xn_s[...] = _rms_bf16(x1_s[...], g_ref[...])
    for c in range(0, D_FF, FF_CHUNK):
        hmid = jnp.maximum(_dot(xn_s[...], wup_ref[:, c:c + FF_CHUNK]), 0.0)
        x1_s[...] += _dot((hmid * hmid).astype(bf16), wdn_ref[c:c + FF_CHUNK, :])
    if final:
        x1 = x1_s[...]
        y = x1 * lax.rsqrt(jnp.mean(x1 * x1, axis=-1, keepdims=True) + NORM_EPS)
        o_ref[...] = y * gf_ref[...]
    else:
        o_ref[...] = x1_s[...]


def _out_mlp_kernel(xp_ref, xs_ref, ap_ref, as_ref, wo_ref, bo_ref, g_ref, wup_ref, wdn_ref, gf_ref,
                    op_ref, os_ref, x1_s, xn_s, *, final):
    def body(x_ref, a_ref, o_ref):
        _mlp_tile(x_ref, a_ref, o_ref, wo_ref, bo_ref, g_ref, wup_ref, wdn_ref, gf_ref, x1_s, xn_s, final)

    _for_each_group(body, (xp_ref, ap_ref, op_ref), (xs_ref, as_ref, os_ref))


def _out_mlp(x_p, x_s, a_p, a_s, wo, bo, g, w_up_all, w_down_all, layer, g_final, final):
    single = pl.Buffered(1)
    wspec = lambda shape: pl.BlockSpec(shape, lambda i: (0, 0), pipeline_mode=single)
    lspec = lambda shape: pl.BlockSpec((None,) + shape, lambda i: (layer, 0, 0), pipeline_mode=single)
    return pl.pallas_call(
        functools.partial(_out_mlp_kernel, final=final),
        grid=(NP_TILES + NS_TILES,),
        in_specs=[_p_spec(D_MODEL), _s_spec(D_MODEL), _p_spec(D_MODEL), _s_spec(D_MODEL),
                  wspec((D_MODEL, D_MODEL)),
                  wspec((1, D_MODEL)),
                  wspec((1, D_MODEL)),
                  lspec((D_MODEL, D_FF)),
                  lspec((D_FF, D_MODEL)),
                  wspec((1, D_MODEL))],
        out_specs=[_p_spec(D_MODEL), _s_spec(D_MODEL)],
        out_shape=[jax.ShapeDtypeStruct((N_P, D_MODEL), f32), jax.ShapeDtypeStruct((N_S, D_MODEL), f32)],
        scratch_shapes=[pltpu.VMEM((TM, D_MODEL), f32),
                        pltpu.VMEM((TM, D_MODEL), bf16)],
        compiler_params=_params("arbitrary"),
        name="out_mlp",
    )(x_p, x_s, a_p, a_s, wo, bo, g, w_up_all, w_down_all, g_final)


def _rope_tables():
    half = ROPE_DIM // 2
    inv = ROPE_THETA ** (-jnp.arange(0, ROPE_DIM, 2, dtype=f32) / ROPE_DIM)
    lane = jnp.arange(LANES) % B_HD
    inv_lane = jnp.where(lane < ROPE_DIM, inv[lane % half], 0.0)
    pos = jnp.concatenate([jnp.arange(SEQ, dtype=jnp.int32), PAST_LEN + jnp.arange(TM, dtype=jnp.int32) % DEC_SEQ])
    ang = pos.astype(f32)[:, None] * inv_lane[None, :]
    tab = jnp.concatenate([jnp.cos(ang), jnp.sin(ang)], axis=1)
    lo = jnp.where(lane < half, -1.0, 0.0)
    hi = jnp.where((lane >= half) & (lane < ROPE_DIM), 1.0, 0.0)
    consts = jnp.zeros((SUBLANES, LANES), f32).at[0].set(lo).at[1].set(hi)
    return tab, consts


def _q_perm():
    g, h, d = jnp.meshgrid(jnp.arange(B_GROUP), jnp.arange(B_KV_HEADS), jnp.arange(B_HD), indexing="ij")
    return ((h * B_GROUP + g) * B_HD + d).reshape(-1)


def _last_rows(arr, n_seq, seq_len, n_rows, col0, col1):
    return jnp.stack([arr[(s + 1) * seq_len - n_rows:(s + 1) * seq_len, col0:col1] for s in range(n_seq)])


def kernel(x_prompt, x_sample, state_mlstm_c, state_mlstm_n, state_mlstm_m, cache_swa_k, cache_swa_v,
           state_rglru_h, state_rglru_conv, norm_mix, norm_mlp, norm_final, w_mlp_up, w_mlp_down,
           w_mlstm_in, b_mlstm_i, b_mlstm_f, g_mlstm_head, w_mlstm_out, w_swa_qkv, b_swa_qkv, swa_sinks,
           w_swa_out, b_swa_out, w_rg_in, w_rg_conv, b_rg_conv, w_rg_a, b_rg_a, w_rg_x, b_rg_x, rg_lambda,
           w_rg_out):
    assert N_A == 2
    x_p = x_prompt.reshape(N_P, D_MODEL)
    x_s = x_sample.reshape(N_S, D_MODEL)
    zero_bias = jnp.zeros((1, D_MODEL), f32)
    row = lambda v: v.reshape(1, -1).astype(f32)
    w_up_all = w_mlp_up.astype(bf16)
    w_down_all = w_mlp_down.astype(bf16)
    state_c = state_mlstm_c.astype(f32)
    outs = {k: [] for k in ("c_p", "n_p", "m_p", "n_s", "m_s", "k_p", "v_p", "k_s", "v_s",
                            "h_p", "cv_p", "h_s", "cv_s")}
    deferred = None
    c_s_all = None
    for i in range(DEPTH):
        kind, j = i % N_MIXERS, i // N_MIXERS
        g_mix = row(norm_mix[i])
        if kind == 0:
            w_in = w_mlstm_in[j]
            n_gate = 2 * A_HEADS
            w_gate = jnp.pad(w_in[:, A_MAIN:], ((0, 0), (0, LANES - n_gate))).astype(bf16)
            b_gate = jnp.pad(jnp.concatenate([b_mlstm_i[j], b_mlstm_f[j]]), (0, LANES - n_gate)).reshape(1, LANES)
            main_p, main_s, gates_p, gates_s = _proj_mlstm(x_p, x_s, g_mix, w_in[:, :A_MAIN].astype(bf16), w_gate,
                                                           b_gate.astype(f32))
            g_head = g_mlstm_head[j].astype(f32)
            gates_p3 = gates_p.reshape(BATCH, SEQ, LANES)
            a_p, c_p, n_p, mt_p = _mlstm_prompt(main_p.reshape(BATCH, SEQ, A_MAIN), gates_p3,
                                                jnp.swapaxes(gates_p3[:, :, :SUBLANES], 1, 2), g_head)
            a_p = a_p.reshape(N_P, A_HEADS * A_DV)
            mt_p = mt_p.reshape(N_P, LANES)
            m0_tok = jnp.pad(jnp.repeat(state_mlstm_m[j].astype(f32), DEC_SEQ, axis=0),
                             ((0, 0), (0, LANES - A_HEADS)))
            n0 = state_mlstm_n[j].astype(f32)
            gates_s_t = gates_s[:, :SUBLANES].T
            if j == 0:
                a_s, n_s, mt_s, vdt0, kb0, sc0 = _mlstm_sample_first(main_s, gates_s, gates_s_t, m0_tok, state_c, n0,
                                                                     g_head)
                deferred = (vdt0, kb0, sc0)
            else:
                a_s, n_s, mt_s, c_s_all = _mlstm_sample_second(main_s, gates_s, gates_s_t, m0_tok, state_c, n0,
                                                               g_head, *deferred)
            outs["c_p"].append(c_p); outs["n_p"].append(n_p)
            outs["m_p"].append(_last_rows(mt_p, BATCH, SEQ, 1, 0, A_HEADS).reshape(BATCH, A_HEADS))
            outs["n_s"].append(n_s)
            outs["m_s"].append(mt_s[DEC_SEQ - 1::DEC_SEQ, :A_HEADS])
            wo, bo = w_mlstm_out[j], zero_bias
        elif kind == 1:
            perm = _q_perm()
            col_perm = jnp.concatenate([perm, jnp.arange(B_Q, B_IN)])
            rope_tab, rope_consts = _rope_tables()
            proj_p, proj_s = _proj_swa(x_p, x_s, g_mix, w_swa_qkv[j][:, col_perm].astype(bf16),
                                       row(b_swa_qkv[j][col_perm]), rope_tab, rope_consts)
            sinks = swa_sinks[j].astype(f32)
            a_p = _swa_prompt(proj_p, sinks)
            buf = cache_swa_k.shape[2]
            a_s, k_s, v_s = _swa_sample(proj_s, sinks,
                                        cache_swa_k[j].astype(f32).reshape(DEC_BATCH, buf, B_KV),
                                        cache_swa_v[j].astype(f32).reshape(DEC_BATCH, buf, B_KV))
            outs["k_p"].append(_last_rows(proj_p, BATCH, SEQ, WINDOW, B_Q, B_Q + B_KV)
                               .reshape(BATCH, WINDOW, B_KV_HEADS, B_HD))
            outs["v_p"].append(_last_rows(proj_p, BATCH, SEQ, WINDOW, B_Q + B_KV, B_IN)
                               .reshape(BATCH, WINDOW, B_KV_HEADS, B_HD))
            outs["k_s"].append(k_s.reshape(DEC_BATCH, buf, B_KV_HEADS, B_HD))
            outs["v_s"].append(v_s.reshape(DEC_BATCH, buf, B_KV_HEADS, B_HD))
            wo, bo = w_swa_out[j][perm, :], row(b_swa_out[j])
        else:
            proj_p, proj_s = _proj_rg(x_p, x_s, g_mix, w_rg_in[j].astype(bf16))
            weights = (w_rg_conv[j].astype(f32), row(b_rg_conv[j]), w_rg_a[j].astype(bf16), row(b_rg_a[j]),
                       w_rg_x[j].astype(bf16), row(b_rg_x[j]), row(rg_lambda[j]))
            a_p, h_p = _rg_prompt(proj_p, weights)
            conv_pad = jnp.pad(state_rglru_conv[j].astype(f32),
                               ((0, 0), (SUBLANES - (CONV_W - 1), 0), (0, 0))).reshape(N_S, C_WIDTH)
            a_s, h_s = _rg_sample(proj_s, conv_pad, state_rglru_h[j].astype(f32), weights)
            outs["h_p"].append(h_p.reshape(BATCH, C_WIDTH))
            outs["cv_p"].append(_last_rows(proj_p, BATCH, SEQ, CONV_W - 1, 0, C_WIDTH))
            outs["h_s"].append(h_s)
            outs["cv_s"].append(proj_s[:, :C_WIDTH].reshape(DEC_BATCH, DEC_SEQ, C_WIDTH)[:, DEC_SEQ - (CONV_W - 1):])
            wo, bo = w_rg_out[j], zero_bias
        x_p, x_s = _out_mlp(x_p, x_s, a_p, a_s, wo.astype(bf16), bo, row(norm_mlp[i]), w_up_all, w_down_all, i,
                            row(norm_final), final=(i == DEPTH - 1))
    st = {k: jnp.stack(v) for k, v in outs.items()}
    y_p = x_p.reshape(BATCH, SEQ, D_MODEL)
    y_s = x_s.reshape(DEC_BATCH, DEC_SEQ, D_MODEL)
    return (y_p, y_s, st["c_p"], st["n_p"], st["m_p"], c_s_all, st["n_s"], st["m_s"],
            st["k_p"], st["v_p"], st["k_s"], st["v_s"], st["h_p"], st["cv_p"], st["h_s"], st["cv_s"])
```

```python
import functools
import math

import jax
import jax.numpy as jnp
import numpy as np
from jax import lax
from jax.experimental import pallas as pl
from jax.experimental.pallas import tpu as pltpu

f32 = jnp.float32
bf16 = jnp.bfloat16

D_MODEL = 1024
BATCH = 2
SEQ = 8192
DEPTH = 4
DEC_BATCH = 128
DEC_SEQ = 8
PAST_LEN = 8192
N_MIXERS = 3
NORM_EPS = 1e-6

A_HEADS = 4
A_DK = 128
A_DV = 256
A_QK = A_HEADS * A_DK
A_MAIN = 2 * A_QK + 2 * A_HEADS * A_DV
N_A = (DEPTH + 2) // 3

B_HEADS = 16
B_KV_HEADS = 4
B_HD = 64
B_GROUP = 4
B_Q = B_HEADS * B_HD
B_KV = B_KV_HEADS * B_HD
B_IN = B_Q + 2 * B_KV
WINDOW = 128
ROPE_THETA = 500000.0
ROPE_DIM = 16

C_WIDTH = 1024
C_BLOCKS = 4
C_BW = 256
CONV_W = 4
LRU_C = 8.0
D_FF = 4096

N_P = BATCH * SEQ
N_S = DEC_BATCH * DEC_SEQ

LANES = 128
SUBLANES = 8
VMEM_LIMIT = 56 * 1024 * 1024

TM = 512
NP_TILES = N_P // TM
NS_TILES = N_S // TM
MLSTM_CHUNK = 256
MLSTM_SB = 16
MLSTM_ST = MLSTM_SB * DEC_SEQ
MLSTM_UNROLL = 8
SWA_SB = 8
SWA_TILE = 512
RG_T = 512
FF_CHUNK = 512


def _dot(a, b):
    return jnp.dot(a, b, preferred_element_type=f32)


def _dot_nt(a, b):
    return lax.dot_general(a, b, (((1,), (1,)), ((), ())), preferred_element_type=f32)


def _split3(x):
    hi = x.astype(bf16)
    r1 = x - hi.astype(f32)
    mid = r1.astype(bf16)
    lo = (r1 - mid.astype(f32)).astype(bf16)
    return hi, mid, lo


def _mask_dot(mask, x):
    m = jnp.where(mask, 1.0, 0.0).astype(bf16)
    return sum(_dot(m, part) for part in _split3(x))


def _dot_mask(x, mask):
    m = jnp.where(mask, 1.0, 0.0).astype(bf16)
    return sum(_dot(part, m) for part in _split3(x))


def _idiv(x, d):
    assert d & (d - 1) == 0
    return x >> (d.bit_length() - 1)


def _imod(x, d):
    assert d & (d - 1) == 0
    return x & (d - 1)


def _rms_bf16(x, g):
    y = x * lax.rsqrt(jnp.mean(x * x, axis=-1, keepdims=True) + NORM_EPS)
    return (y * g).astype(bf16)


def _sigmoid(x):
    return 0.5 * jnp.tanh(0.5 * x) + 0.5


def _sqrt_nonneg(v):
    return jnp.where(v > 0.0, v * lax.rsqrt(v), 0.0)


def _softplus(x):
    return jnp.maximum(x, 0.0) + jnp.log1p(jnp.exp(-jnp.abs(x)))


def _gelu_tanh(x):
    return 0.5 * x * (1.0 + jnp.tanh(math.sqrt(2.0 / math.pi) * (x + 0.044715 * (x * x * x))))


def _params(*sem):
    return pltpu.CompilerParams(dimension_semantics=sem, vmem_limit_bytes=VMEM_LIMIT)


def _const_spec(shape):
    nd = len(shape)
    return pl.BlockSpec(shape, lambda *_: (0,) * nd)


def _p_spec(width):
    return pl.BlockSpec((TM, width), lambda i: (jnp.minimum(i, NP_TILES - 1), 0))


def _s_spec(width):
    return pl.BlockSpec((TM, width), lambda i: (jnp.maximum(i - NP_TILES, 0), 0))


def _for_each_group(body, p_refs, s_refs):
    i = pl.program_id(0)

    @pl.when(i < NP_TILES)
    def _():
        body(*p_refs)

    @pl.when(i >= NP_TILES)
    def _():
        body(*s_refs)


def _proj_mlstm_kernel(xp_ref, xs_ref, g_ref, w_ref, wg_ref, bg_ref, wgt_ref, bgt_ref,
                       mp_ref, ms_ref, gp_ref, gs_ref, gtp_ref, gts_ref):
    def gate_act(pre, idx):
        return jnp.where(idx >= A_HEADS, -_softplus(-pre), pre)

    def body(x_ref, main_ref, gate_ref, gate_t_ref):
        xn = _rms_bf16(x_ref[...], g_ref[...])
        for c in range(0, A_MAIN, 512):
            main_ref[:, c:c + 512] = _dot(xn, w_ref[:, c:c + 512])
        gp = _dot(xn, wg_ref[...]) + bg_ref[...]
        gate_ref[...] = gate_act(gp, lax.broadcasted_iota(jnp.int32, gp.shape, 1))
        gt = _dot_nt(wgt_ref[...], xn)[0:SUBLANES, :] + bgt_ref[...]
        gate_t_ref[...] = gate_act(gt, lax.broadcasted_iota(jnp.int32, gt.shape, 0))

    _for_each_group(body, (xp_ref, mp_ref, gp_ref, gtp_ref), (xs_ref, ms_ref, gs_ref, gts_ref))


def _proj_mlstm(x_p, x_s, g, w_main_all, w_gate_all, b_gate, w_gate_t_all, b_gate_t, layer):
    lspec = lambda shape: pl.BlockSpec((None,) + shape, lambda i: (layer, 0, 0))
    return pl.pallas_call(
        _proj_mlstm_kernel,
        grid=(NP_TILES + NS_TILES,),
        in_specs=[_p_spec(D_MODEL), _s_spec(D_MODEL),
                  _const_spec((1, D_MODEL)),
                  lspec((D_MODEL, A_MAIN)),
                  lspec((D_MODEL, LANES)),
                  _const_spec((1, LANES)),
                  lspec((2 * SUBLANES, D_MODEL)),
                  _const_spec((SUBLANES, 1))],
        out_specs=[_p_spec(A_MAIN), _s_spec(A_MAIN), _p_spec(LANES), _s_spec(LANES),
                   pl.BlockSpec((SUBLANES, TM), lambda i: (0, jnp.minimum(i, NP_TILES - 1))),
                   pl.BlockSpec((SUBLANES, TM), lambda i: (0, jnp.maximum(i - NP_TILES, 0)))],
        out_shape=[jax.ShapeDtypeStruct((N_P, A_MAIN), f32), jax.ShapeDtypeStruct((N_S, A_MAIN), f32),
                   jax.ShapeDtypeStruct((N_P, LANES), f32), jax.ShapeDtypeStruct((N_S, LANES), f32),
                   jax.ShapeDtypeStruct((SUBLANES, N_P), f32), jax.ShapeDtypeStruct((SUBLANES, N_S), f32)],
        compiler_params=_params("arbitrary"),
        name="proj_mlstm",
    )(x_p, x_s, g, w_main_all, w_gate_all, b_gate, w_gate_t_all, b_gate_t)


def _proj_swa_kernel(xp_ref, xs_ref, g_ref, w_ref, b_ref, tab_ref, rc_ref, op_ref, os_ref):
    half = ROPE_DIM // 2

    def body(x_ref, o_ref):
        xn = _rms_bf16(x_ref[...], g_ref[...])
        cos = tab_ref[:, 0:LANES]
        sin = tab_ref[:, LANES:2 * LANES]
        sin_lo = sin * rc_ref[0:1, :]
        sin_hi = sin * rc_ref[1:2, :]
        for c in range(0, B_IN, 512):
            p = _dot(xn, w_ref[:, c:c + 512]) + b_ref[:, c:c + 512]
            for l in range(0, 512, LANES):
                pl_ = p[:, l:l + LANES]
                if c + l < B_Q + B_KV:
                    pl_ = (pl_ * cos + pltpu.roll(pl_, LANES - half, 1) * sin_lo
                           + pltpu.roll(pl_, half, 1) * sin_hi)
                o_ref[:, c + l:c + l + LANES] = pl_

    _for_each_group(body, (xp_ref, op_ref), (xs_ref, os_ref))


def _proj_swa(x_p, x_s, g, w, b, rope_tab, rope_consts):
    seq_tiles = SEQ // TM
    tab_idx = lambda i: (jnp.where(i < NP_TILES, i % seq_tiles, seq_tiles), 0)
    return pl.pallas_call(
        _proj_swa_kernel,
        grid=(NP_TILES + NS_TILES,),
        in_specs=[_p_spec(D_MODEL), _s_spec(D_MODEL),
                  _const_spec((1, D_MODEL)),
                  _const_spec((D_MODEL, B_IN)),
                  _const_spec((1, B_IN)),
                  pl.BlockSpec((TM, 2 * LANES), tab_idx),
                  _const_spec((SUBLANES, LANES))],
        out_specs=[_p_spec(B_IN), _s_spec(B_IN)],
        out_shape=[jax.ShapeDtypeStruct((N_P, B_IN), f32), jax.ShapeDtypeStruct((N_S, B_IN), f32)],
        compiler_params=_params("arbitrary"),
        name="proj_swa",
    )(x_p, x_s, g, w, b, rope_tab, rope_consts)


def _proj_rg_kernel(xp_ref, xs_ref, g_ref, w_ref, op_ref, os_ref):
    def body(x_ref, o_ref):
        xn = _rms_bf16(x_ref[...], g_ref[...])
        for c in range(0, 2 * C_WIDTH, 512):
            o_ref[:, c:c + 512] = _dot(xn, w_ref[:, c:c + 512])

    _for_each_group(body, (xp_ref, op_ref), (xs_ref, os_ref))


def _proj_rg(x_p, x_s, g, w):
    return pl.pallas_call(
        _proj_rg_kernel,
        grid=(NP_TILES + NS_TILES,),
        in_specs=[_p_spec(D_MODEL), _s_spec(D_MODEL),
                  _const_spec((1, D_MODEL)),
                  _const_spec((D_MODEL, 2 * C_WIDTH))],
        out_specs=[_p_spec(2 * C_WIDTH), _s_spec(2 * C_WIDTH)],
        out_shape=[jax.ShapeDtypeStruct((N_P, 2 * C_WIDTH), f32), jax.ShapeDtypeStruct((N_S, 2 * C_WIDTH), f32)],
        compiler_params=_params("arbitrary"),
        name="proj_rg",
    )(x_p, x_s, g, w)


def _mlstm_masks(t, seg):
    r = lax.broadcasted_iota(jnp.int32, (t, t), 0)
    c = lax.broadcasted_iota(jnp.int32, (t, t), 1)
    if seg == t:
        same = None
        lower, upper = r >= c, r <= c
    else:
        same = _idiv(r, seg) == _idiv(c, seg)
        lower, upper = same & (r >= c), same & (r <= c)
    return r, c, same, lower, upper


def _wide(col, n):
    return jnp.concatenate([col] * n, axis=1)


def _row_sum(x):
    acc = x[:, 0:LANES]
    for l in range(LANES, x.shape[1], LANES):
        acc = acc + x[:, l:l + LANES]
    return jnp.broadcast_to(jnp.sum(acc, axis=1, keepdims=True), acc.shape)


def _mlstm_cols(main_ref, h):
    q = main_ref[:, h * A_DK:(h + 1) * A_DK]
    k = main_ref[:, A_QK + h * A_DK:A_QK + (h + 1) * A_DK] * (A_DK ** -0.5)
    v = main_ref[:, 2 * A_QK + h * A_DV:2 * A_QK + (h + 1) * A_DV]
    o_pre = main_ref[:, 2 * A_QK + (A_HEADS + h) * A_DV:2 * A_QK + (A_HEADS + h + 1) * A_DV]
    return q, k, v, o_pre


def _mlstm_prompt_chunk(main_ref, gc_ref, gr_ref, gh_ref, hs_ref, c_ref, n_ref, mt_ref, c_s, n_s, m_s):
    t = MLSTM_CHUNK
    j = pl.program_id(0)

    @pl.when(j == 0)
    def _():
        c_s[...] = jnp.zeros_like(c_s)
        n_s[...] = jnp.zeros_like(n_s)
        m_s[...] = jnp.zeros_like(m_s)

    _, _, _, lower, upper = _mlstm_masks(t, t)
    gc = gc_ref[...]
    gr = gr_ref[...]
    btr_all = _dot_mask(gr, upper)
    lane = lax.broadcasted_iota(jnp.int32, (t, LANES), 1)
    mt_all = jnp.zeros((t, LANES), f32)

    def rep(col):
        return jnp.broadcast_to(col, (t, LANES))

    wide, row_sum = _wide, _row_sum

    heads = range(A_HEADS)
    stack = lambda xs: jnp.concatenate(list(xs), axis=0)
    head = lambda x, h: x[h * t:(h + 1) * t]
    cols = [_mlstm_cols(main_ref, h) for h in heads]
    qs, ks, vs = [c[0] for c in cols], [c[1] for c in cols], [c[2] for c in cols]
    qbs, kbs, vbs = [q.astype(bf16) for q in qs], [k.astype(bf16) for k in ks], [v.astype(bf16) for v in vs]
    cts = [c_s[h] for h in heads]
    ns = [n_s[h:h + 1, :] for h in heads]
    btrs = [btr_all[A_HEADS + h:A_HEADS + h + 1, :] for h in heads]
    itrs = [gr[h:h + 1, :] for h in heads]

    lf_rep = jnp.concatenate([rep(gc[:, A_HEADS + h:A_HEADS + h + 1]) for h in heads], axis=1)
    btc_all = _mask_dot(lower, lf_rep)
    btc = stack(btc_all[:, h * LANES:(h + 1) * LANES] for h in heads)
    itc = stack(rep(gc[:, h:h + 1]) for h in heads)
    m_in = stack(rep(m_s[0:1, h:h + 1]) for h in heads)
    dmat = stack(jnp.where(lower, wide(head(btc, h), t // LANES) + (itrs[h] - btrs[h]), -jnp.inf) for h in heads)
    inter = btc + m_in
    m_t = jnp.maximum(jnp.broadcast_to(jnp.max(dmat, axis=1, keepdims=True), inter.shape), inter)
    w = jnp.exp(dmat - wide(m_t, t // LANES))
    w_inter = jnp.exp(inter - m_t)
    s = stack(_dot_nt(qbs[h], kbs[h]) for h in heads) * w
    sb = s.astype(bf16)
    num = (stack(_dot(head(sb, h), vbs[h]) for h in heads)
           + wide(w_inter, A_DV // LANES) * stack(_dot(qbs[h], cts[h].astype(bf16)) for h in heads))
    den = row_sum(s) + w_inter * row_sum(stack(qs[h] * ns[h] for h in heads))
    inv = 1.0 / jnp.maximum(jnp.abs(den), jnp.exp(-m_t))
    norm = inv * lax.rsqrt(inv * inv * (row_sum(num * num) * (1.0 / A_DV)) + NORM_EPS)
    out = _sigmoid(stack(c[3] for c in cols)) * (num * wide(norm, A_DV // LANES))
    for h in heads:
        hs_ref[:, h * A_DV:(h + 1) * A_DV] = (head(out, h) * gh_ref[h:h + 1, :]).astype(bf16)

    m_news = [head(m_t, h)[t - 1:t, 0:1] for h in heads]
    last = lambda x: stack(rep(head(x, h)[t - 1:t, 0:1]) for h in heads)
    decay_c = jnp.exp(last(btc) - btc + itc - last(m_t))
    for h in heads:
        scale = jnp.exp(head(inter, h)[t - 1:t, 0:1] - m_news[h])
        decay_r = jnp.exp(btrs[h][:, t - 1:t] - btrs[h] + itrs[h] - m_news[h])
        c_s[h] = scale * cts[h] + _dot((ks[h].T * decay_r).astype(bf16), vbs[h])
        n_s[h:h + 1, :] = scale * ns[h] + jnp.sum(head(decay_c, h) * ks[h], axis=0, keepdims=True)
        mt_all = jnp.where(lane == h, head(m_t, h), mt_all)
    mt_ref[...] = mt_all
    m_s[0:1, :] = mt_all[t - 1:t, :]

    @pl.when(j == pl.num_programs(0) - 1)
    def _():
        for h in range(A_HEADS):
            c_ref[h] = c_s[h].T
        n_ref[...] = n_s[...]


def _mlstm_prompt_kernel(main_ref, gc_ref, *rest):
    gr_refs, (gh_ref, hs_ref, c_ref, n_ref, mt_ref, c_s, n_s, m_s) = rest[:BATCH], rest[BATCH:]
    for b in range(BATCH):
        _mlstm_prompt_chunk(main_ref.at[b], gc_ref.at[b], gr_refs[b], gh_ref, hs_ref.at[b], c_ref.at[b],
                            n_ref.at[b], mt_ref.at[b], c_s.at[b], n_s.at[b], m_s.at[b])


def _mlstm_prompt(main, gates, gates_t, g_head):
    t = MLSTM_CHUNK
    nc = SEQ // t
    return pl.pallas_call(
        _mlstm_prompt_kernel,
        grid=(nc,),
        in_specs=[pl.BlockSpec((BATCH, t, A_MAIN), lambda j: (0, j, 0)),
                  pl.BlockSpec((BATCH, t, LANES), lambda j: (0, j, 0))]
                 + [pl.BlockSpec((SUBLANES, t), functools.partial(lambda j, b: (0, b * nc + j), b=b))
                    for b in range(BATCH)]
                 + [_const_spec((A_HEADS, A_DV))],
        out_specs=[pl.BlockSpec((BATCH, t, A_HEADS * A_DV), lambda j: (0, j, 0)),
                   _const_spec((BATCH, A_HEADS, A_DV, A_DK)),
                   _const_spec((BATCH, A_HEADS, A_DK)),
                   pl.BlockSpec((BATCH, t, LANES), lambda j: (0, j, 0))],
        out_shape=[jax.ShapeDtypeStruct((BATCH, SEQ, A_HEADS * A_DV), bf16),
                   jax.ShapeDtypeStruct((BATCH, A_HEADS, A_DV, A_DK), f32),
                   jax.ShapeDtypeStruct((BATCH, A_HEADS, A_DK), f32),
                   jax.ShapeDtypeStruct((BATCH, SEQ, LANES), f32)],
        scratch_shapes=[pltpu.VMEM((BATCH, A_HEADS, A_DK, A_DV), f32),
                        pltpu.VMEM((BATCH, A_HEADS, A_DK), f32),
                        pltpu.VMEM((BATCH, SUBLANES, LANES), f32)],
        compiler_params=_params("arbitrary"),
        name="mlstm_prompt",
    )(main, gates, *([gates_t] * BATCH), g_head)


def _mlstm_sample_compute(main_ref, gc_ref, gr_ref, m0_ref, c0_ref, n0_ref, gh_ref,
                          hs_ref, n_ref, mt_ref, vdt_ref, kb_ref, sc_ref, acc_s, nt_s):
    t = MLSTM_ST
    seg = DEC_SEQ
    _, c, same, lower, upper = _mlstm_masks(t, seg)
    seg_last = same & (_imod(c, seg) == seg - 1)
    gc = gc_ref[...]
    gr = gr_ref[...]
    btr_all = _dot_mask(gr, upper)
    lane = lax.broadcasted_iota(jnp.int32, (t, LANES), 1)
    tok_seg = _idiv(lax.broadcasted_iota(jnp.int32, (A_DV, t), 1), seg)

    heads = range(A_HEADS)
    rep = lambda col: jnp.broadcast_to(col, (t, LANES))
    stack = lambda xs: jnp.concatenate(list(xs), axis=0)
    lanes = lambda xs: jnp.concatenate(list(xs), axis=1)
    head = lambda x, h: x[h * t:(h + 1) * t]
    cols = [_mlstm_cols(main_ref, h) for h in heads]
    qs, ks, vs = [c_[0] for c_ in cols], [c_[1] for c_ in cols], [c_[2] for c_ in cols]
    qbs, kbs, vbs = [q.astype(bf16) for q in qs], [k.astype(bf16) for k in ks], [v.astype(bf16) for v in vs]

    btc_all = _mask_dot(lower, lanes(rep(gc[:, A_HEADS + h:A_HEADS + h + 1]) for h in heads))
    btc = stack(btc_all[:, h * LANES:(h + 1) * LANES] for h in heads)
    itc = stack(rep(gc[:, h:h + 1]) for h in heads)
    m_in = stack(rep(m0_ref[:, h:h + 1]) for h in heads)
    dmat = stack(jnp.where(lower, head(btc, h) + (gr[h:h + 1, :] - btr_all[A_HEADS + h:A_HEADS + h + 1, :]), -jnp.inf)
                 for h in heads)
    inter = btc + m_in
    m_t = jnp.maximum(jnp.broadcast_to(jnp.max(dmat, axis=1, keepdims=True), inter.shape), inter)
    w = jnp.exp(dmat - m_t)
    w_inter = jnp.exp(inter - m_t)
    s = stack(_dot_nt(qbs[h], kbs[h]) for h in heads) * w
    sb = s.astype(bf16)
    num = stack(_dot(head(sb, h), vbs[h]) for h in heads)

    for h in heads:
        def inter_body(b, carry, h=h):
            rows = pl.ds(pl.multiple_of(h * t + b * seg, seg), seg)
            r = _dot_nt(c0_ref[b, h].astype(bf16), qbs[h])
            acc_s[h] = jnp.where(tok_seg == b, r, acc_s[h])
            nt_s[rows, :] = jnp.broadcast_to(n0_ref[b, h:h + 1, :], (seg, A_DK))
            return carry

        acc_s[h] = jnp.zeros((A_DV, t), f32)
        lax.fori_loop(0, MLSTM_SB, inter_body, 0, unroll=MLSTM_UNROLL)
    n_tok = nt_s[...]
    num = num + _wide(w_inter, A_DV // LANES) * stack(acc_s[h].T for h in heads)
    den = _row_sum(s) + w_inter * _row_sum(stack(qs) * n_tok)
    inv = 1.0 / jnp.maximum(jnp.abs(den), jnp.exp(-m_t))
    norm = inv * lax.rsqrt(inv * inv * (_row_sum(num * num) * (1.0 / A_DV)) + NORM_EPS)
    out = _sigmoid(stack(c_[3] for c_ in cols)) * (num * _wide(norm, A_DV // LANES))
    for h in heads:
        hs_ref[:, h * A_DV:(h + 1) * A_DV] = (head(out, h) * gh_ref[h:h + 1, :]).astype(bf16)

    last = _mask_dot(seg_last, lanes([head(x, h) for x in (m_t, btc, inter) for h in heads]))
    pick = lambda i: stack(last[:, (i * A_HEADS + h) * LANES:(i * A_HEADS + h + 1) * LANES] for h in heads)
    m_new, bt_last, inter_last = pick(0), pick(1), pick(2)
    decay = jnp.exp(bt_last - btc + itc - m_new)
    scale = jnp.exp(inter_last - m_new)
    seg_sum = _mask_dot(same, lanes(head(decay, h) * ks[h] for h in heads))
    n_new = scale * n_tok + stack(seg_sum[:, h * A_DK:(h + 1) * A_DK] for h in heads)
    mt_all = jnp.zeros((t, LANES), f32)
    sc_all = jnp.zeros((t, LANES), f32)
    for h in heads:
        vdt_ref[h] = (_wide(head(decay, h), A_DV // LANES) * vs[h]).T.astype(bf16)
        kb_ref[:, h * A_DK:(h + 1) * A_DK] = kbs[h]
        sc_all = jnp.where(lane == h, head(scale, h), sc_all)
        mt_all = jnp.where(lane == h, head(m_t, h), mt_all)
    for b in range(MLSTM_SB):
        n_ref[b] = stack(head(n_new, h)[b * seg:b * seg + 1, :] for h in heads)
    mt_ref[...] = mt_all
    sc_ref[...] = sc_all


def _mlstm_sample_update(c_in_ref, c_out_ref, vdt_ref, kb_ref, sc_ref):
    seg = DEC_SEQ
    lane_seg = _idiv(lax.broadcasted_iota(jnp.int32, (1, MLSTM_ST), 1), seg)
    for h in range(A_HEADS):
        def upd_body(b, carry, h=h):
            rows = pl.ds(pl.multiple_of(b * seg, seg), seg)
            scale = sc_ref[rows, :][0:1, h:h + 1]
            onehot = jnp.where(lane_seg == b, 1.0, 0.0).astype(bf16)
            upd = _dot(vdt_ref[h] * onehot, kb_ref[:, h * A_DK:(h + 1) * A_DK])
            c_out_ref[b, h] = scale * c_in_ref[b, h] + upd
            return carry

        lax.fori_loop(0, MLSTM_SB, upd_body, 0, unroll=MLSTM_UNROLL)


def _mlstm_sample_first_kernel(main_ref, gc_ref, gr_ref, m0_ref, c0_ref, n0_ref, gh_ref,
                               hs_ref, n_ref, mt_ref, vdt_ref, kb_ref, sc_ref, acc_s, nt_s):
    _mlstm_sample_compute(main_ref, gc_ref, gr_ref, m0_ref, c0_ref, n0_ref, gh_ref,
                          hs_ref, n_ref, mt_ref, vdt_ref, kb_ref, sc_ref, acc_s, nt_s)


def _mlstm_sample_second_kernel(main_ref, gc_ref, gr_ref, m0_ref, c0_ref, n0_ref, gh_ref, vdt0_ref, kb0_ref, sc0_ref,
                                hs_ref, n_ref, mt_ref, c_ref, vdt_s, kb_s, sc_s, acc_s, nt_s):
    layer = pl.program_id(0)

    @pl.when(layer == 0)
    def _():
        _mlstm_sample_update(c0_ref, c_ref, vdt0_ref, kb0_ref, sc0_ref)

    @pl.when(layer == 1)
    def _():
        _mlstm_sample_compute(main_ref, gc_ref, gr_ref, m0_ref, c0_ref, n0_ref, gh_ref,
                              hs_ref, n_ref, mt_ref, vdt_s, kb_s, sc_s, acc_s, nt_s)
        _mlstm_sample_update(c0_ref, c_ref, vdt_s, kb_s, sc_s)


def _mlstm_sample_scratch():
    t = MLSTM_ST
    return [pltpu.VMEM((A_HEADS, A_DV, t), f32), pltpu.VMEM((A_HEADS * t, A_DK), f32)]


def _mlstm_sample_first(main, gates, gates_t, m0_tok, c_all, n0, g_head):
    t = MLSTM_ST
    return pl.pallas_call(
        _mlstm_sample_first_kernel,
        grid=(DEC_BATCH // MLSTM_SB,),
        in_specs=[pl.BlockSpec((t, A_MAIN), lambda i: (i, 0)),
                  pl.BlockSpec((t, LANES), lambda i: (i, 0)),
                  pl.BlockSpec((SUBLANES, t), lambda i: (0, i)),
                  pl.BlockSpec((t, LANES), lambda i: (i, 0)),
                  pl.BlockSpec((None, MLSTM_SB, A_HEADS, A_DV, A_DK), lambda i: (0, i, 0, 0, 0)),
                  pl.BlockSpec((MLSTM_SB, A_HEADS, A_DK), lambda i: (i, 0, 0)),
                  _const_spec((A_HEADS, A_DV))],
        out_specs=[pl.BlockSpec((t, A_HEADS * A_DV), lambda i: (i, 0)),
                   pl.BlockSpec((MLSTM_SB, A_HEADS, A_DK), lambda i: (i, 0, 0)),
                   pl.BlockSpec((t, LANES), lambda i: (i, 0)),
                   pl.BlockSpec((A_HEADS, A_DV, t), lambda i: (0, 0, i)),
                   pl.BlockSpec((t, A_QK), lambda i: (i, 0)),
                   pl.BlockSpec((t, LANES), lambda i: (i, 0))],
        out_shape=[jax.ShapeDtypeStruct((N_S, A_HEADS * A_DV), bf16),
                   jax.ShapeDtypeStruct((DEC_BATCH, A_HEADS, A_DK), f32),
                   jax.ShapeDtypeStruct((N_S, LANES), f32),
                   jax.ShapeDtypeStruct((A_HEADS, A_DV, N_S), bf16),
                   jax.ShapeDtypeStruct((N_S, A_QK), bf16),
                   jax.ShapeDtypeStruct((N_S, LANES), f32)],
        scratch_shapes=_mlstm_sample_scratch(),
        compiler_params=_params("parallel"),
        name="mlstm_sample_first",
    )(main, gates, gates_t, m0_tok, c_all, n0, g_head)


def _mlstm_sample_second(main, gates, gates_t, m0_tok, c_all, n0, g_head, vdt0, kb0, sc0):
    t = MLSTM_ST
    own = lambda l, i: i * l
    first = lambda l, i: i * (1 - l)
    return pl.pallas_call(
        _mlstm_sample_second_kernel,
        grid=(N_A, DEC_BATCH // MLSTM_SB),
        in_specs=[pl.BlockSpec((t, A_MAIN), lambda l, i: (own(l, i), 0)),
                  pl.BlockSpec((t, LANES), lambda l, i: (own(l, i), 0)),
                  pl.BlockSpec((SUBLANES, t), lambda l, i: (0, own(l, i))),
                  pl.BlockSpec((t, LANES), lambda l, i: (own(l, i), 0)),
                  pl.BlockSpec((None, MLSTM_SB, A_HEADS, A_DV, A_DK), lambda l, i: (l, i, 0, 0, 0)),
                  pl.BlockSpec((MLSTM_SB, A_HEADS, A_DK), lambda l, i: (own(l, i), 0, 0)),
                  _const_spec((A_HEADS, A_DV)),
                  pl.BlockSpec((A_HEADS, A_DV, t), lambda l, i: (0, 0, first(l, i))),
                  pl.BlockSpec((t, A_QK), lambda l, i: (first(l, i), 0)),
                  pl.BlockSpec((t, LANES), lambda l, i: (first(l, i), 0))],
        out_specs=[pl.BlockSpec((t, A_HEADS * A_DV), lambda l, i: (own(l, i), 0)),
                   pl.BlockSpec((MLSTM_SB, A_HEADS, A_DK), lambda l, i: (own(l, i), 0, 0)),
                   pl.BlockSpec((t, LANES), lambda l, i: (own(l, i), 0)),
                   pl.BlockSpec((None, MLSTM_SB, A_HEADS, A_DV, A_DK), lambda l, i: (l, i, 0, 0, 0))],
        out_shape=[jax.ShapeDtypeStruct((N_S, A_HEADS * A_DV), bf16),
                   jax.ShapeDtypeStruct((DEC_BATCH, A_HEADS, A_DK), f32),
                   jax.ShapeDtypeStruct((N_S, LANES), f32),
                   jax.ShapeDtypeStruct((N_A, DEC_BATCH, A_HEADS, A_DV, A_DK), f32)],
        scratch_shapes=[pltpu.VMEM((A_HEADS, A_DV, t), bf16),
                        pltpu.VMEM((t, A_QK), bf16),
                        pltpu.VMEM((t, LANES), f32)] + _mlstm_sample_scratch(),
        compiler_params=_params("arbitrary", "arbitrary"),
        name="mlstm_sample_second",
    )(main, gates, gates_t, m0_tok, c_all, n0, g_head, vdt0, kb0, sc0)


def _swa_softmax_pv(s, allowed, sink_col, vb):
    s = jnp.where(allowed, s * (B_HD ** -0.5), -jnp.inf)
    m = jnp.maximum(jnp.max(s, axis=1, keepdims=True), sink_col)
    p = jnp.exp(s - m)
    den = jnp.sum(p, axis=1, keepdims=True) + jnp.exp(sink_col - m)
    return _dot((p / den).astype(bf16), vb)


def _swa_window_block(q_groups, kb, vb, sink_ref, has_prev):
    w = WINDOW
    rows = B_GROUP * w
    t = _imod(lax.broadcasted_iota(jnp.int32, (rows, 2 * w), 0), w)
    jj = lax.broadcasted_iota(jnp.int32, (rows, 2 * w), 1)
    bias = jnp.where((jj > t) & (jj <= t + w) & (has_prev | (jj >= w)), 0.0, -jnp.inf)
    lane_head = _idiv(lax.broadcasted_iota(jnp.int32, (1, B_KV), 1), B_HD)
    grp = _idiv(lax.broadcasted_iota(jnp.int32, (rows, 1), 0), w)
    outs = [jnp.zeros((w, B_KV), f32) for _ in range(B_GROUP)]
    for h in range(B_KV_HEADS):
        hm = lane_head == h
        qscale = jnp.where(hm, B_HD ** -0.5, 0.0)
        qh = jnp.concatenate([q * qscale for q in q_groups], axis=0).astype(bf16)
        sink_col = jnp.zeros((rows, 1), f32)
        for g in range(B_GROUP):
            sink_col = jnp.where(grp == g, sink_ref[h * B_GROUP + g], sink_col)
        s = _dot_nt(qh, kb) + bias
        m = jnp.maximum(jnp.max(s, axis=1, keepdims=True), sink_col)
        p = jnp.exp(s - m)
        den = jnp.sum(p, axis=1, keepdims=True) + jnp.exp(sink_col - m)
        r = _dot((p * (1.0 / den)).astype(bf16), vb)
        for g in range(B_GROUP):
            outs[g] = jnp.where(hm, r[g * w:(g + 1) * w, :], outs[g])
    return outs


def _swa_prompt_kernel(sink_ref, q_ref, ko_ref, vo_ref, kp_ref, vp_ref, a_ref):
    w = WINDOW
    tile_has_prev = pl.program_id(1) > 0
    for nb in range(SWA_TILE // w):
        r0 = nb * w
        if nb == 0:
            k_prev, v_prev, has_prev = kp_ref[...], vp_ref[...], tile_has_prev
        else:
            k_prev, v_prev, has_prev = ko_ref[r0 - w:r0, :], vo_ref[r0 - w:r0, :], True
        kb = jnp.concatenate([k_prev, ko_ref[r0:r0 + w, :]], axis=0).astype(bf16)
        vb = jnp.concatenate([v_prev, vo_ref[r0:r0 + w, :]], axis=0).astype(bf16)
        outs = _swa_window_block([q_ref[r0:r0 + w, g * B_KV:(g + 1) * B_KV] for g in range(B_GROUP)],
                                 kb, vb, sink_ref, has_prev)
        for g in range(B_GROUP):
            a_ref[r0:r0 + w, g * B_KV:(g + 1) * B_KV] = outs[g].astype(bf16)


def _swa_prompt(proj, sinks):
    nt = SEQ // SWA_TILE
    w = WINDOW
    per = SWA_TILE // w
    kcol = B_Q // B_KV
    vcol = kcol + 1
    tile = lambda b, n: b * nt + n
    prev = lambda b, n: jnp.maximum(tile(b, n) * per - 1, 0)
    return pl.pallas_call(
        _swa_prompt_kernel,
        grid=(BATCH, nt),
        in_specs=[pl.BlockSpec(memory_space=pltpu.SMEM),
                  pl.BlockSpec((SWA_TILE, B_Q), lambda b, n: (tile(b, n), 0)),
                  pl.BlockSpec((SWA_TILE, B_KV), lambda b, n: (tile(b, n), kcol)),
                  pl.BlockSpec((SWA_TILE, B_KV), lambda b, n: (tile(b, n), vcol)),
                  pl.BlockSpec((w, B_KV), lambda b, n: (prev(b, n), kcol)),
                  pl.BlockSpec((w, B_KV), lambda b, n: (prev(b, n), vcol))],
        out_specs=pl.BlockSpec((SWA_TILE, B_Q), lambda b, n: (tile(b, n), 0)),
        out_shape=jax.ShapeDtypeStruct((N_P, B_Q), bf16),
        compiler_params=_params("parallel", "arbitrary"),
        name="swa_prompt",
    )(sinks, proj, proj, proj, proj, proj)


def _swa_sample_kernel(sink_ref, q_ref, kn_ref, vn_ref, kc_ref, vc_ref, o_ref, ko_ref, vo_ref):
    s_len = DEC_SEQ
    buf = WINDOW
    rows = B_HEADS * s_len
    keys = 2 * buf
    ri = lax.broadcasted_iota(jnp.int32, (rows, keys), 0)
    jj = lax.broadcasted_iota(jnp.int32, (rows, keys), 1)
    t = _imod(ri, s_len)
    allowed = ((jj < buf) & (jj > t)) | ((jj >= buf) & (jj - buf <= t))
    lane_head = _idiv(lax.broadcasted_iota(jnp.int32, (1, B_KV), 1), B_HD)
    hg = _idiv(lax.broadcasted_iota(jnp.int32, (rows, 1), 0), s_len)
    sink_col = jnp.zeros((rows, 1), f32)
    for i in range(B_HEADS):
        sink_col = jnp.where(hg == i, sink_ref[i], sink_col)
    pad = jnp.zeros((keys - buf - s_len, B_KV), f32)

    def body(e, carry):
        rs = pl.ds(pl.multiple_of(e * s_len, s_len), s_len)
        kn = kn_ref[rs, :]
        vn = vn_ref[rs, :]
        kc = kc_ref[e]
        vc = vc_ref[e]
        kpad = jnp.concatenate([kc, kn, pad], axis=0).astype(bf16)
        vpad = jnp.concatenate([vc, vn, pad], axis=0).astype(bf16)
        ko_ref[e, 0:buf - s_len, :] = kc[s_len:, :]
        ko_ref[e, buf - s_len:, :] = kn
        vo_ref[e, 0:buf - s_len, :] = vc[s_len:, :]
        vo_ref[e, buf - s_len:, :] = vn
        qe = q_ref[rs, :]
        qbig = jnp.concatenate(
            [jnp.where(lane_head == h, qe[:, g * B_KV:(g + 1) * B_KV], 0.0)
             for h in range(B_KV_HEADS) for g in range(B_GROUP)], axis=0).astype(bf16)
        r = _swa_softmax_pv(_dot_nt(qbig, kpad), allowed, sink_col, vpad)
        for g in range(B_GROUP):
            og = jnp.zeros((s_len, B_KV), f32)
            for h in range(B_KV_HEADS):
                blk = (h * B_GROUP + g) * s_len
                og = og + jnp.where(lane_head == h, r[blk:blk + s_len, :], 0.0)
            o_ref[rs, g * B_KV:(g + 1) * B_KV] = og.astype(bf16)
        return carry

    lax.fori_loop(0, SWA_SB, body, 0, unroll=4)


def _swa_sample(proj, sinks, k_cache, v_cache):
    t = SWA_SB * DEC_SEQ
    kcol = B_Q // B_KV
    return pl.pallas_call(
        _swa_sample_kernel,
        grid=(DEC_BATCH // SWA_SB,),
        in_specs=[pl.BlockSpec(memory_space=pltpu.SMEM),
                  pl.BlockSpec((t, B_Q), lambda i: (i, 0)),
                  pl.BlockSpec((t, B_KV), lambda i: (i, kcol)),
                  pl.BlockSpec((t, B_KV), lambda i: (i, kcol + 1)),
                  pl.BlockSpec((SWA_SB, WINDOW, B_KV), lambda i: (i, 0, 0)),
                  pl.BlockSpec((SWA_SB, WINDOW, B_KV), lambda i: (i, 0, 0))],
        out_specs=[pl.BlockSpec((t, B_Q), lambda i: (i, 0)),
                   pl.BlockSpec((SWA_SB, WINDOW, B_KV), lambda i: (i, 0, 0)),
                   pl.BlockSpec((SWA_SB, WINDOW, B_KV), lambda i: (i, 0, 0))],
        out_shape=[jax.ShapeDtypeStruct((N_S, B_Q), bf16),
                   jax.ShapeDtypeStruct((DEC_BATCH, WINDOW, B_KV), f32),
                   jax.ShapeDtypeStruct((DEC_BATCH, WINDOW, B_KV), f32)],
        compiler_params=_params("parallel"),
        name="swa_sample",
    )(sinks, proj, proj, proj, k_cache, v_cache)


def _rg_conv_group(x8, p8, wc_ref, bc_ref):
    row = lax.broadcasted_iota(jnp.int32, x8.shape, 0)
    u = bc_ref[...] + wc_ref[CONV_W - 1:CONV_W, :] * x8
    for d in range(1, CONV_W):
        sh = pltpu.roll(jnp.where(row >= SUBLANES - d, p8, x8), d, 0)
        u = u + wc_ref[CONV_W - 1 - d:CONV_W - d, :] * sh
    return u


def _rg_scan_group(a8, b8, carry):
    row = lax.broadcasted_iota(jnp.int32, a8.shape, 0)
    for d in (1, 2, 4):
        keep = row >= d
        b8 = jnp.where(keep, a8 * pltpu.roll(b8, d, 0) + b8, b8)
        a8 = jnp.where(keep, a8 * pltpu.roll(a8, d, 0), a8)
    return a8 * carry + b8


def _rg_gate_rows(ra, rx, neg_rate, u):
    z = _sigmoid(ra) * neg_rate
    a = jnp.exp(-z)
    return a, _sqrt_nonneg(jnp.tanh(z) * (a * a + 1.0)) * _sigmoid(rx) * u


def _rg_gates(u, gate, wa_ref, ba_ref, wx_ref, bx_ref, lam_ref):
    ub = u.astype(bf16)
    ra = jnp.concatenate([_dot(ub[:, n * C_BW:(n + 1) * C_BW], wa_ref[n]) for n in range(C_BLOCKS)], axis=1)
    rx = jnp.concatenate([_dot(ub[:, n * C_BW:(n + 1) * C_BW], wx_ref[n]) for n in range(C_BLOCKS)], axis=1)
    a, bterm = _rg_gate_rows(ra + ba_ref[...], rx + bx_ref[...], LRU_C * _softplus(-lam_ref[...]), u)
    return a, bterm, _gelu_tanh(gate)


def _rg_prompt_kernel(p_ref, wc_ref, bc_ref, wa_ref, ba_ref, wx_ref, bx_ref, lam_ref,
                      y_ref, h_ref, u_s, a_s, b_s, xc_s, hc_s):
    j = pl.program_id(1)
    ng = RG_T // SUBLANES
    out_rows = 2 * SUBLANES

    @pl.when(j == 0)
    def _():
        xc_s[...] = jnp.zeros_like(xc_s)
        hc_s[...] = jnp.zeros_like(hc_s)

    def group(gidx, n=SUBLANES):
        return pl.ds(pl.multiple_of(gidx * n, n), n)

    def conv_body(gidx, p8):
        x8 = p_ref[group(gidx), 0:C_WIDTH]
        u_s[group(gidx), :] = _rg_conv_group(x8, p8, wc_ref, bc_ref)
        return x8

    xc_s[...] = lax.fori_loop(0, ng, conv_body, xc_s[...], unroll=2)
    ub = u_s[...].astype(bf16)
    for n in range(C_BLOCKS):
        cols = slice(n * C_BW, (n + 1) * C_BW)
        a_s[:, cols] = _dot(ub[:, cols], wa_ref[n])
        b_s[:, cols] = _dot(ub[:, cols], wx_ref[n])
    neg_rate = LRU_C * _softplus(-lam_ref[...])

    def gate_body(gidx, carry):
        rows = group(gidx)
        a, bterm = _rg_gate_rows(a_s[rows, :] + ba_ref[...], b_s[rows, :] + bx_ref[...], neg_rate, u_s[rows, :])
        a_s[rows, :] = a
        b_s[rows, :] = bterm
        return carry

    lax.fori_loop(0, ng, gate_body, 0, unroll=2)

    def scan_body(gidx, carry):
        rows = group(gidx, out_rows)
        lo = pl.ds(pl.multiple_of(gidx * out_rows, out_rows), SUBLANES)
        hi = pl.ds(pl.multiple_of(gidx * out_rows + SUBLANES, SUBLANES), SUBLANES)
        h_lo = _rg_scan_group(a_s[lo, :], b_s[lo, :], carry)
        h_hi = _rg_scan_group(a_s[hi, :], b_s[hi, :], h_lo[SUBLANES - 1:SUBLANES, :])
        h16 = jnp.concatenate([h_lo, h_hi], axis=0)
        y_ref[rows, :] = (h16 * _gelu_tanh(p_ref[rows, C_WIDTH:])).astype(bf16)
        return h_hi[SUBLANES - 1:SUBLANES, :]

    h_last = lax.fori_loop(0, RG_T // out_rows, scan_body, hc_s[...])
    hc_s[...] = h_last

    @pl.when(j == pl.num_programs(1) - 1)
    def _():
        h_ref[0] = h_last


def _rg_weight_specs():
    return [_const_spec((CONV_W, C_WIDTH)), _const_spec((1, C_WIDTH)),
            _const_spec((C_BLOCKS, C_BW, C_BW)), _const_spec((1, C_WIDTH)),
            _const_spec((C_BLOCKS, C_BW, C_BW)), _const_spec((1, C_WIDTH)),
            _const_spec((1, C_WIDTH))]


def _rg_prompt(proj, weights):
    nt = SEQ // RG_T
    return pl.pallas_call(
        _rg_prompt_kernel,
        grid=(BATCH, nt),
        in_specs=[pl.BlockSpec((RG_T, 2 * C_WIDTH), lambda b, j: (b * nt + j, 0))] + _rg_weight_specs(),
        out_specs=[pl.BlockSpec((RG_T, C_WIDTH), lambda b, j: (b * nt + j, 0)),
                   pl.BlockSpec((1, 1, C_WIDTH), lambda b, j: (b, 0, 0))],
        out_shape=[jax.ShapeDtypeStruct((N_P, C_WIDTH), bf16),
                   jax.ShapeDtypeStruct((BATCH, 1, C_WIDTH), f32)],
        scratch_shapes=[pltpu.VMEM((RG_T, C_WIDTH), f32),
                        pltpu.VMEM((RG_T, C_WIDTH), f32),
                        pltpu.VMEM((RG_T, C_WIDTH), f32),
                        pltpu.VMEM((SUBLANES, C_WIDTH), f32),
                        pltpu.VMEM((1, C_WIDTH), f32)],
        compiler_params=_params("parallel", "arbitrary"),
        name="rg_prompt",
    )(proj, *weights)


def _rg_sample_kernel(p_ref, cv_ref, h0_ref, wc_ref, bc_ref, wa_ref, ba_ref, wx_ref, bx_ref, lam_ref,
                      y_ref, h_ref, u_s, a_s, b_s):
    def conv_body(gidx, carry):
        rows = pl.ds(pl.multiple_of(gidx * SUBLANES, SUBLANES), SUBLANES)
        u_s[rows, :] = _rg_conv_group(p_ref[rows, 0:C_WIDTH], cv_ref[rows, :], wc_ref, bc_ref)
        return carry

    lax.fori_loop(0, DEC_BATCH, conv_body, 0)
    a, bterm, gg = _rg_gates(u_s[...], p_ref[:, C_WIDTH:], wa_ref, ba_ref, wx_ref, bx_ref, lam_ref)
    a_s[...] = a
    b_s[...] = bterm

    def scan_body(gidx, carry):
        rows = pl.ds(pl.multiple_of(gidx * SUBLANES, SUBLANES), SUBLANES)
        h8 = _rg_scan_group(a_s[rows, :], b_s[rows, :], h0_ref[pl.ds(gidx, 1), :])
        u_s[rows, :] = h8
        h_ref[pl.ds(gidx, 1), :] = h8[SUBLANES - 1:SUBLANES, :]
        return carry

    lax.fori_loop(0, DEC_BATCH, scan_body, 0)
    y_ref[...] = (u_s[...] * gg).astype(bf16)


def _rg_sample(proj, conv_pad, h0, weights):
    return pl.pallas_call(
        _rg_sample_kernel,
        grid=(1,),
        in_specs=[_const_spec((N_S, 2 * C_WIDTH)),
                  _const_spec((N_S, C_WIDTH)),
                  _const_spec((DEC_BATCH, C_WIDTH))] + _rg_weight_specs(),
        out_specs=[_const_spec((N_S, C_WIDTH)), _const_spec((DEC_BATCH, C_WIDTH))],
        out_shape=[jax.ShapeDtypeStruct((N_S, C_WIDTH), bf16),
                   jax.ShapeDtypeStruct((DEC_BATCH, C_WIDTH), f32)],
        scratch_shapes=[pltpu.VMEM((N_S, C_WIDTH), f32),
                        pltpu.VMEM((N_S, C_WIDTH), f32),
                        pltpu.VMEM((N_S, C_WIDTH), f32)],
        compiler_params=_params("arbitrary"),
        name="rg_sample",
    )(proj, conv_pad, h0, *weights)


def _mlp_tile(x_ref, a_ref, o_ref, wo_ref, bo_ref, g_ref, wup_ref, wdn_ref, gf_ref, x1_s, xn_s, final):
    x1_s[...] = x_ref[...] + _dot(a_ref[...], wo_ref[...]) + bo_ref[...]
    xn_s[...] = _rms_bf16(x1_s[...], g_ref[...])
    for c in range(0, D_FF, FF_CHUNK):
        hmid = jnp.maximum(_dot(xn_s[...], wup_ref[:, c:c + FF_CHUNK]), 0.0)
        x1_s[...] += _dot((hmid * hmid).astype(bf16), wdn_ref[c:c + FF_CHUNK, :])
    if final:
        x1 = x1_s[...]
        y = x1 * lax.rsqrt(jnp.mean(x1 * x1, axis=-1, keepdims=True) + NORM_EPS)
        o_ref[...] = y * gf_ref[...]
    else:
        o_ref[...] = x1_s[...]


def _out_mlp_kernel(xp_ref, xs_ref, ap_ref, as_ref, wo_ref, bo_ref, g_ref, wup_ref, wdn_ref, gf_ref,
                    op_ref, os_ref, x1_s, xn_s, *, final):
    def body(x_ref, a_ref, o_ref):
        _mlp_tile(x_ref, a_ref, o_ref, wo_ref, bo_ref, g_ref, wup_ref, wdn_ref, gf_ref, x1_s, xn_s, final)

    _for_each_group(body, (xp_ref, ap_ref, op_ref), (xs_ref, as_ref, os_ref))


def _out_mlp(x_p, x_s, a_p, a_s, wo, bo, g, w_up_all, w_down_all, layer, g_final, final):
    single = pl.Buffered(1)
    wspec = lambda shape: pl.BlockSpec(shape, lambda i: (0, 0), pipeline_mode=single)
    lspec = lambda shape: pl.BlockSpec((None,) + shape, lambda i: (layer, 0, 0), pipeline_mode=single)
    return pl.pallas_call(
        functools.partial(_out_mlp_kernel, final=final),
        grid=(NP_TILES + NS_TILES,),
        in_specs=[_p_spec(D_MODEL), _s_spec(D_MODEL), _p_spec(D_MODEL), _s_spec(D_MODEL),
                  wspec((D_MODEL, D_MODEL)),
                  wspec((1, D_MODEL)),
                  wspec((1, D_MODEL)),
                  lspec((D_MODEL, D_FF)),
                  lspec((D_FF, D_MODEL)),
                  wspec((1, D_MODEL))],
        out_specs=[_p_spec(D_MODEL), _s_spec(D_MODEL)],
        out_shape=[jax.ShapeDtypeStruct((N_P, D_MODEL), f32), jax.ShapeDtypeStruct((N_S, D_MODEL), f32)],
        scratch_shapes=[pltpu.VMEM((TM, D_MODEL), f32),
                        pltpu.VMEM((TM, D_MODEL), bf16)],
        compiler_params=_params("arbitrary"),
        name="out_mlp",
    )(x_p, x_s, a_p, a_s, wo, bo, g, w_up_all, w_down_all, g_final)


def _rope_tables():
    half = ROPE_DIM // 2
    inv = np.float32(ROPE_THETA) ** (-np.arange(0, ROPE_DIM, 2, dtype=np.float32) / np.float32(ROPE_DIM))
    lane = np.arange(LANES) % B_HD
    inv_lane = np.where(lane < ROPE_DIM, inv[lane % half], np.float32(0.0)).astype(np.float32)
    pos = np.concatenate([np.arange(SEQ), PAST_LEN + np.arange(TM) % DEC_SEQ]).astype(np.float32)
    ang = pos[:, None] * inv_lane[None, :]
    tab = np.concatenate([np.cos(ang), np.sin(ang)], axis=1).astype(np.float32)
    consts = np.zeros((SUBLANES, LANES), np.float32)
    consts[0] = np.where(lane < half, -1.0, 0.0)
    consts[1] = np.where((lane >= half) & (lane < ROPE_DIM), 1.0, 0.0)
    return jnp.asarray(tab), jnp.asarray(consts)


def _q_cols_to_group_major(w):
    lead = w.shape[:-1]
    return jnp.swapaxes(w.reshape(lead + (B_KV_HEADS, B_GROUP, B_HD)), -3, -2).reshape(lead + (B_Q,))


def _last_rows(arr, n_seq, seq_len, n_rows, col0, col1):
    return jnp.stack([arr[(s + 1) * seq_len - n_rows:(s + 1) * seq_len, col0:col1] for s in range(n_seq)])


def kernel(x_prompt, x_sample, state_mlstm_c, state_mlstm_n, state_mlstm_m, cache_swa_k, cache_swa_v,
           state_rglru_h, state_rglru_conv, norm_mix, norm_mlp, norm_final, w_mlp_up, w_mlp_down,
           w_mlstm_in, b_mlstm_i, b_mlstm_f, g_mlstm_head, w_mlstm_out, w_swa_qkv, b_swa_qkv, swa_sinks,
           w_swa_out, b_swa_out, w_rg_in, w_rg_conv, b_rg_conv, w_rg_a, b_rg_a, w_rg_x, b_rg_x, rg_lambda,
           w_rg_out):
    assert N_A == 2
    x_p = x_prompt.reshape(N_P, D_MODEL)
    x_s = x_sample.reshape(N_S, D_MODEL)
    zero_bias = jnp.zeros((1, D_MODEL), f32)
    row = lambda v: v.reshape(1, -1).astype(f32)
    w_up_all = w_mlp_up.astype(bf16)
    w_down_all = w_mlp_down.astype(bf16)
    state_c = state_mlstm_c.astype(f32)
    n_gate = 2 * A_HEADS
    assert n_gate == SUBLANES
    w_mlstm_main = w_mlstm_in[:, :, :A_MAIN].astype(bf16)
    w_gate_cols = w_mlstm_in[:, :, A_MAIN:]
    w_mlstm_gate = jnp.pad(w_gate_cols, ((0, 0), (0, 0), (0, LANES - n_gate))).astype(bf16)
    w_mlstm_gate_t = jnp.pad(jnp.swapaxes(w_gate_cols, 1, 2), ((0, 0), (0, n_gate), (0, 0))).astype(bf16)
    outs = {k: [] for k in ("c_p", "n_p", "m_p", "n_s", "m_s", "k_p", "v_p", "k_s", "v_s",
                            "h_p", "cv_p", "h_s", "cv_s")}
    deferred = None
    c_s_all = None
    for i in range(DEPTH):
        kind, j = i % N_MIXERS, i // N_MIXERS
        g_mix = row(norm_mix[i])
        if kind == 0:
            b_gate = jnp.concatenate([b_mlstm_i[j], b_mlstm_f[j]]).astype(f32)
            main_p, main_s, gates_p, gates_s, gates_p_t, gates_s_t = _proj_mlstm(
                x_p, x_s, g_mix, w_mlstm_main, w_mlstm_gate, jnp.pad(b_gate, (0, LANES - n_gate)).reshape(1, LANES),
                w_mlstm_gate_t, b_gate.reshape(n_gate, 1), j)
            g_head = g_mlstm_head[j].astype(f32)
            a_p, c_p, n_p, mt_p = _mlstm_prompt(main_p.reshape(BATCH, SEQ, A_MAIN),
                                                gates_p.reshape(BATCH, SEQ, LANES), gates_p_t, g_head)
            a_p = a_p.reshape(N_P, A_HEADS * A_DV)
            mt_p = mt_p.reshape(N_P, LANES)
            m0_tok = jnp.pad(jnp.repeat(state_mlstm_m[j].astype(f32), DEC_SEQ, axis=0),
                             ((0, 0), (0, LANES - A_HEADS)))
            n0 = state_mlstm_n[j].astype(f32)
            if j == 0:
                a_s, n_s, mt_s, vdt0, kb0, sc0 = _mlstm_sample_first(main_s, gates_s, gates_s_t, m0_tok, state_c, n0,
                                                                     g_head)
                deferred = (vdt0, kb0, sc0)
            else:
                a_s, n_s, mt_s, c_s_all = _mlstm_sample_second(main_s, gates_s, gates_s_t, m0_tok, state_c, n0,
                                                               g_head, *deferred)
            outs["c_p"].append(c_p); outs["n_p"].append(n_p)
            outs["m_p"].append(_last_rows(mt_p, BATCH, SEQ, 1, 0, A_HEADS).reshape(BATCH, A_HEADS))
            outs["n_s"].append(n_s)
            outs["m_s"].append(mt_s[DEC_SEQ - 1::DEC_SEQ, :A_HEADS])
            wo, bo = w_mlstm_out[j], zero_bias
        elif kind == 1:
            rope_tab, rope_consts = _rope_tables()
            w_qkv, b_qkv = w_swa_qkv[j], b_swa_qkv[j]
            w_qkv = jnp.concatenate([_q_cols_to_group_major(w_qkv[:, :B_Q]), w_qkv[:, B_Q:]], axis=1)
            b_qkv = jnp.concatenate([_q_cols_to_group_major(b_qkv[:B_Q]), b_qkv[B_Q:]])
            proj_p, proj_s = _proj_swa(x_p, x_s, g_mix, w_qkv.astype(bf16), row(b_qkv), rope_tab, rope_consts)
            sinks = swa_sinks[j].astype(f32)
            a_p = _swa_prompt(proj_p, sinks)
            buf = cache_swa_k.shape[2]
            a_s, k_s, v_s = _swa_sample(proj_s, sinks,
                                        cache_swa_k[j].astype(f32).reshape(DEC_BATCH, buf, B_KV),
                                        cache_swa_v[j].astype(f32).reshape(DEC_BATCH, buf, B_KV))
            outs["k_p"].append(_last_rows(proj_p, BATCH, SEQ, WINDOW, B_Q, B_Q + B_KV)
                               .reshape(BATCH, WINDOW, B_KV_HEADS, B_HD))
            outs["v_p"].append(_last_rows(proj_p, BATCH, SEQ, WINDOW, B_Q + B_KV, B_IN)
                               .reshape(BATCH, WINDOW, B_KV_HEADS, B_HD))
            outs["k_s"].append(k_s.reshape(DEC_BATCH, buf, B_KV_HEADS, B_HD))
            outs["v_s"].append(v_s.reshape(DEC_BATCH, buf, B_KV_HEADS, B_HD))
            wo = jnp.swapaxes(w_swa_out[j].reshape(B_KV_HEADS, B_GROUP, B_HD, D_MODEL), 0, 1).reshape(B_Q, D_MODEL)
            bo = row(b_swa_out[j])
        else:
            proj_p, proj_s = _proj_rg(x_p, x_s, g_mix, w_rg_in[j].astype(bf16))
            weights = (w_rg_conv[j].astype(f32), row(b_rg_conv[j]), w_rg_a[j].astype(bf16), row(b_rg_a[j]),
                       w_rg_x[j].astype(bf16), row(b_rg_x[j]), row(rg_lambda[j]))
            a_p, h_p = _rg_prompt(proj_p, weights)
            conv_pad = jnp.pad(state_rglru_conv[j].astype(f32),
                               ((0, 0), (SUBLANES - (CONV_W - 1), 0), (0, 0))).reshape(N_S, C_WIDTH)
            a_s, h_s = _rg_sample(proj_s, conv_pad, state_rglru_h[j].astype(f32), weights)
            outs["h_p"].append(h_p.reshape(BATCH, C_WIDTH))
            outs["cv_p"].append(_last_rows(proj_p, BATCH, SEQ, CONV_W - 1, 0, C_WIDTH))
            outs["h_s"].append(h_s)
            outs["cv_s"].append(proj_s[:, :C_WIDTH].reshape(DEC_BATCH, DEC_SEQ, C_WIDTH)[:, DEC_SEQ - (CONV_W - 1):])
            wo, bo = w_rg_out[j], zero_bias
        x_p, x_s = _out_mlp(x_p, x_s, a_p, a_s, wo.astype(bf16), bo, row(norm_mlp[i]), w_up_all, w_down_all, i,
                            row(norm_final), final=(i == DEPTH - 1))
    st = {k: jnp.stack(v) for k, v in outs.items()}
    y_p = x_p.reshape(BATCH, SEQ, D_MODEL)
    y_s = x_s.reshape(DEC_BATCH, DEC_SEQ, D_MODEL)
    return (y_p, y_s, st["c_p"], st["n_p"], st["m_p"], c_s_all, st["n_s"], st["m_s"],
            st["k_p"], st["v_p"], st["k_s"], st["v_s"], st["h_p"], st["cv_p"], st["h_s"], st["cv_s"])
```

```python
import functools
import math

import jax
import jax.numpy as jnp
import numpy as np
from jax import lax
from jax.experimental import pallas as pl
from jax.experimental.pallas import tpu as pltpu

f32 = jnp.float32
bf16 = jnp.bfloat16

D_MODEL = 1024
BATCH = 2
SEQ = 8192
DEPTH = 4
DEC_BATCH = 128
DEC_SEQ = 8
PAST_LEN = 8192
N_MIXERS = 3
NORM_EPS = 1e-6

A_HEADS = 4
A_DK = 128
A_DV = 256
A_QK = A_HEADS * A_DK
A_MAIN = 2 * A_QK + 2 * A_HEADS * A_DV
N_A = (DEPTH + 2) // 3

B_HEADS = 16
B_KV_HEADS = 4
B_HD = 64
B_GROUP = 4
B_Q = B_HEADS * B_HD
B_KV = B_KV_HEADS * B_HD
B_IN = B_Q + 2 * B_KV
WINDOW = 128
ROPE_THETA = 500000.0
ROPE_DIM = 16

C_WIDTH = 1024
C_BLOCKS = 4
C_BW = 256
CONV_W = 4
LRU_C = 8.0
D_FF = 4096

N_P = BATCH * SEQ
N_S = DEC_BATCH * DEC_SEQ

LANES = 128
SUBLANES = 8
VMEM_LIMIT = 56 * 1024 * 1024

TM = 512
NP_TILES = N_P // TM
NS_TILES = N_S // TM
MLSTM_CHUNK = 256
MLSTM_SB = 16
MLSTM_ST = MLSTM_SB * DEC_SEQ
MLSTM_UNROLL = 8
SWA_SB = 8
SWA_TILE = 512
RG_T = 512
FF_CHUNK = 512


def _dot(a, b):
    return jnp.dot(a, b, preferred_element_type=f32)


def _dot_nt(a, b):
    return lax.dot_general(a, b, (((1,), (1,)), ((), ())), preferred_element_type=f32)


def _split3(x):
    hi = x.astype(bf16)
    r1 = x - hi.astype(f32)
    mid = r1.astype(bf16)
    lo = (r1 - mid.astype(f32)).astype(bf16)
    return hi, mid, lo


def _mask_dot(mask, x):
    m = jnp.where(mask, 1.0, 0.0).astype(bf16)
    return sum(_dot(m, part) for part in _split3(x))


def _dot_mask(x, mask):
    m = jnp.where(mask, 1.0, 0.0).astype(bf16)
    return sum(_dot(part, m) for part in _split3(x))


def _idiv(x, d):
    assert d & (d - 1) == 0
    return x >> (d.bit_length() - 1)


def _imod(x, d):
    assert d & (d - 1) == 0
    return x & (d - 1)


def _rms_bf16(x, g):
    y = x * lax.rsqrt(jnp.mean(x * x, axis=-1, keepdims=True) + NORM_EPS)
    return (y * g).astype(bf16)


def _sigmoid(x):
    return 0.5 * jnp.tanh(0.5 * x) + 0.5


def _sqrt_nonneg(v):
    return jnp.where(v > 0.0, v * lax.rsqrt(v), 0.0)


def _softplus(x):
    return jnp.maximum(x, 0.0) + jnp.log1p(jnp.exp(-jnp.abs(x)))


def _gelu_tanh(x):
    return 0.5 * x * (1.0 + jnp.tanh(math.sqrt(2.0 / math.pi) * (x + 0.044715 * (x * x * x))))


def _params(*sem):
    return pltpu.CompilerParams(dimension_semantics=sem, vmem_limit_bytes=VMEM_LIMIT)


def _const_spec(shape):
    nd = len(shape)
    return pl.BlockSpec(shape, lambda *_: (0,) * nd)


def _p_spec(width):
    return pl.BlockSpec((TM, width), lambda i: (jnp.minimum(i, NP_TILES - 1), 0))


def _s_spec(width):
    return pl.BlockSpec((TM, width), lambda i: (jnp.maximum(i - NP_TILES, 0), 0))


def _for_each_group(body, p_refs, s_refs):
    i = pl.program_id(0)

    @pl.when(i < NP_TILES)
    def _():
        body(*p_refs)

    @pl.when(i >= NP_TILES)
    def _():
        body(*s_refs)


def _proj_mlstm_kernel(xp_ref, xs_ref, g_ref, w_ref, wg_ref, bg_ref, mp_ref, ms_ref, gp_ref, gs_ref, gtp_ref, gts_ref):
    def body(x_ref, main_ref, gate_ref, gate_t_ref):
        xn = _rms_bf16(x_ref[...], g_ref[...])
        for c in range(0, A_MAIN, 512):
            main_ref[:, c:c + 512] = _dot(xn, w_ref[:, c:c + 512])
        gp = _dot(xn, wg_ref[...]) + bg_ref[...]
        lane = lax.broadcasted_iota(jnp.int32, gp.shape, 1)
        gates = jnp.where(lane >= A_HEADS, -_softplus(-gp), gp)
        gate_ref[...] = gates
        gate_t_ref[...] = gates.T[0:SUBLANES, :]

    _for_each_group(body, (xp_ref, mp_ref, gp_ref, gtp_ref), (xs_ref, ms_ref, gs_ref, gts_ref))


def _proj_mlstm(x_p, x_s, g, w_main_all, w_gate_all, b_gate, layer):
    lspec = lambda shape: pl.BlockSpec((None,) + shape, lambda i: (layer, 0, 0))
    return pl.pallas_call(
        _proj_mlstm_kernel,
        grid=(NP_TILES + NS_TILES,),
        in_specs=[_p_spec(D_MODEL), _s_spec(D_MODEL),
                  _const_spec((1, D_MODEL)),
                  lspec((D_MODEL, A_MAIN)),
                  lspec((D_MODEL, LANES)),
                  _const_spec((1, LANES))],
        out_specs=[_p_spec(A_MAIN), _s_spec(A_MAIN), _p_spec(LANES), _s_spec(LANES),
                   pl.BlockSpec((SUBLANES, TM), lambda i: (0, jnp.minimum(i, NP_TILES - 1))),
                   pl.BlockSpec((SUBLANES, TM), lambda i: (0, jnp.maximum(i - NP_TILES, 0)))],
        out_shape=[jax.ShapeDtypeStruct((N_P, A_MAIN), f32), jax.ShapeDtypeStruct((N_S, A_MAIN), f32),
                   jax.ShapeDtypeStruct((N_P, LANES), f32), jax.ShapeDtypeStruct((N_S, LANES), f32),
                   jax.ShapeDtypeStruct((SUBLANES, N_P), f32), jax.ShapeDtypeStruct((SUBLANES, N_S), f32)],
        compiler_params=_params("arbitrary"),
        name="proj_mlstm",
    )(x_p, x_s, g, w_main_all, w_gate_all, b_gate)


def _proj_swa_kernel(xp_ref, xs_ref, g_ref, w_ref, b_ref, tab_ref, rc_ref, op_ref, os_ref):
    half = ROPE_DIM // 2

    def body(x_ref, o_ref):
        xn = _rms_bf16(x_ref[...], g_ref[...])
        cos = tab_ref[:, 0:LANES]
        sin = tab_ref[:, LANES:2 * LANES]
        sin_lo = sin * rc_ref[0:1, :]
        sin_hi = sin * rc_ref[1:2, :]
        for c in range(0, B_IN, 512):
            p = _dot(xn, w_ref[:, c:c + 512]) + b_ref[:, c:c + 512]
            for l in range(0, 512, LANES):
                pl_ = p[:, l:l + LANES]
                if c + l < B_Q + B_KV:
                    pl_ = (pl_ * cos + pltpu.roll(pl_, LANES - half, 1) * sin_lo
                           + pltpu.roll(pl_, half, 1) * sin_hi)
                o_ref[:, c + l:c + l + LANES] = pl_

    _for_each_group(body, (xp_ref, op_ref), (xs_ref, os_ref))


def _proj_swa(x_p, x_s, g, w, b, rope_tab, rope_consts):
    seq_tiles = SEQ // TM
    tab_idx = lambda i: (jnp.where(i < NP_TILES, i % seq_tiles, seq_tiles), 0)
    return pl.pallas_call(
        _proj_swa_kernel,
        grid=(NP_TILES + NS_TILES,),
        in_specs=[_p_spec(D_MODEL), _s_spec(D_MODEL),
                  _const_spec((1, D_MODEL)),
                  _const_spec((D_MODEL, B_IN)),
                  _const_spec((1, B_IN)),
                  pl.BlockSpec((TM, 2 * LANES), tab_idx),
                  _const_spec((SUBLANES, LANES))],
        out_specs=[_p_spec(B_IN), _s_spec(B_IN)],
        out_shape=[jax.ShapeDtypeStruct((N_P, B_IN), f32), jax.ShapeDtypeStruct((N_S, B_IN), f32)],
        compiler_params=_params("arbitrary"),
        name="proj_swa",
    )(x_p, x_s, g, w, b, rope_tab, rope_consts)


def _proj_rg_kernel(xp_ref, xs_ref, g_ref, w_ref, op_ref, os_ref):
    def body(x_ref, o_ref):
        xn = _rms_bf16(x_ref[...], g_ref[...])
        for c in range(0, 2 * C_WIDTH, 512):
            o_ref[:, c:c + 512] = _dot(xn, w_ref[:, c:c + 512])

    _for_each_group(body, (xp_ref, op_ref), (xs_ref, os_ref))


def _proj_rg(x_p, x_s, g, w):
    return pl.pallas_call(
        _proj_rg_kernel,
        grid=(NP_TILES + NS_TILES,),
        in_specs=[_p_spec(D_MODEL), _s_spec(D_MODEL),
                  _const_spec((1, D_MODEL)),
                  _const_spec((D_MODEL, 2 * C_WIDTH))],
        out_specs=[_p_spec(2 * C_WIDTH), _s_spec(2 * C_WIDTH)],
        out_shape=[jax.ShapeDtypeStruct((N_P, 2 * C_WIDTH), f32), jax.ShapeDtypeStruct((N_S, 2 * C_WIDTH), f32)],
        compiler_params=_params("arbitrary"),
        name="proj_rg",
    )(x_p, x_s, g, w)


def _mlstm_masks(t, seg):
    r = lax.broadcasted_iota(jnp.int32, (t, t), 0)
    c = lax.broadcasted_iota(jnp.int32, (t, t), 1)
    if seg == t:
        same = None
        lower, upper = r >= c, r <= c
    else:
        same = _idiv(r, seg) == _idiv(c, seg)
        lower, upper = same & (r >= c), same & (r <= c)
    return r, c, same, lower, upper


def _wide(col, n):
    return jnp.concatenate([col] * n, axis=1)


def _row_sum(x):
    acc = x[:, 0:LANES]
    for l in range(LANES, x.shape[1], LANES):
        acc = acc + x[:, l:l + LANES]
    return jnp.broadcast_to(jnp.sum(acc, axis=1, keepdims=True), acc.shape)


def _mlstm_cols(main_ref, h):
    q = main_ref[:, h * A_DK:(h + 1) * A_DK]
    k = main_ref[:, A_QK + h * A_DK:A_QK + (h + 1) * A_DK] * (A_DK ** -0.5)
    v = main_ref[:, 2 * A_QK + h * A_DV:2 * A_QK + (h + 1) * A_DV]
    o_pre = main_ref[:, 2 * A_QK + (A_HEADS + h) * A_DV:2 * A_QK + (A_HEADS + h + 1) * A_DV]
    return q, k, v, o_pre


def _mlstm_prompt_chunk(main_ref, gc_ref, gr_ref, gh_ref, hs_ref, c_ref, n_ref, mt_ref, c_s, n_s, m_s):
    t = MLSTM_CHUNK
    j = pl.program_id(0)

    @pl.when(j == 0)
    def _():
        c_s[...] = jnp.zeros_like(c_s)
        n_s[...] = jnp.zeros_like(n_s)
        m_s[...] = jnp.zeros_like(m_s)

    _, _, _, lower, upper = _mlstm_masks(t, t)
    gc = gc_ref[...]
    gr = gr_ref[...]
    btr_all = _dot_mask(gr, upper)
    lane = lax.broadcasted_iota(jnp.int32, (t, LANES), 1)
    mt_all = jnp.zeros((t, LANES), f32)

    def rep(col):
        return jnp.broadcast_to(col, (t, LANES))

    wide, row_sum = _wide, _row_sum

    heads = range(A_HEADS)
    stack = lambda xs: jnp.concatenate(list(xs), axis=0)
    head = lambda x, h: x[h * t:(h + 1) * t]
    cols = [_mlstm_cols(main_ref, h) for h in heads]
    qs, ks, vs = [c[0] for c in cols], [c[1] for c in cols], [c[2] for c in cols]
    qbs, kbs, vbs = [q.astype(bf16) for q in qs], [k.astype(bf16) for k in ks], [v.astype(bf16) for v in vs]
    cts = [c_s[h] for h in heads]
    ns = [n_s[h:h + 1, :] for h in heads]
    btrs = [btr_all[A_HEADS + h:A_HEADS + h + 1, :] for h in heads]
    itrs = [gr[h:h + 1, :] for h in heads]

    lf_rep = jnp.concatenate([rep(gc[:, A_HEADS + h:A_HEADS + h + 1]) for h in heads], axis=1)
    btc_all = _mask_dot(lower, lf_rep)
    btc = stack(btc_all[:, h * LANES:(h + 1) * LANES] for h in heads)
    itc = stack(rep(gc[:, h:h + 1]) for h in heads)
    m_in = stack(rep(m_s[0:1, h:h + 1]) for h in heads)
    dmat = stack(jnp.where(lower, wide(head(btc, h), t // LANES) + (itrs[h] - btrs[h]), -jnp.inf) for h in heads)
    inter = btc + m_in
    m_t = jnp.maximum(jnp.broadcast_to(jnp.max(dmat, axis=1, keepdims=True), inter.shape), inter)
    w = jnp.exp(dmat - wide(m_t, t // LANES))
    w_inter = jnp.exp(inter - m_t)
    s = stack(_dot_nt(qbs[h], kbs[h]) for h in heads) * w
    sb = s.astype(bf16)
    num = (stack(_dot(head(sb, h), vbs[h]) for h in heads)
           + wide(w_inter, A_DV // LANES) * stack(_dot(qbs[h], cts[h].astype(bf16)) for h in heads))
    den = row_sum(s) + w_inter * row_sum(stack(qs[h] * ns[h] for h in heads))
    inv = 1.0 / jnp.maximum(jnp.abs(den), jnp.exp(-m_t))
    norm = inv * lax.rsqrt(inv * inv * (row_sum(num * num) * (1.0 / A_DV)) + NORM_EPS)
    out = _sigmoid(stack(c[3] for c in cols)) * (num * wide(norm, A_DV // LANES))
    for h in heads:
        hs_ref[:, h * A_DV:(h + 1) * A_DV] = (head(out, h) * gh_ref[h:h + 1, :]).astype(bf16)

    m_news = [head(m_t, h)[t - 1:t, 0:1] for h in heads]
    last = lambda x: stack(rep(head(x, h)[t - 1:t, 0:1]) for h in heads)
    decay_c = jnp.exp(last(btc) - btc + itc - last(m_t))
    for h in heads:
        scale = jnp.exp(head(inter, h)[t - 1:t, 0:1] - m_news[h])
        decay_r = jnp.exp(btrs[h][:, t - 1:t] - btrs[h] + itrs[h] - m_news[h])
        c_s[h] = scale * cts[h] + _dot((ks[h].T * decay_r).astype(bf16), vbs[h])
        n_s[h:h + 1, :] = scale * ns[h] + jnp.sum(head(decay_c, h) * ks[h], axis=0, keepdims=True)
        mt_all = jnp.where(lane == h, head(m_t, h), mt_all)
    mt_ref[...] = mt_all
    m_s[0:1, :] = mt_all[t - 1:t, :]

    @pl.when(j == pl.num_programs(0) - 1)
    def _():
        for h in range(A_HEADS):
            c_ref[h] = c_s[h].T
        n_ref[...] = n_s[...]


def _mlstm_prompt_kernel(main_ref, gc_ref, *rest):
    gr_refs, (gh_ref, hs_ref, c_ref, n_ref, mt_ref, c_s, n_s, m_s) = rest[:BATCH], rest[BATCH:]
    for b in range(BATCH):
        _mlstm_prompt_chunk(main_ref.at[b], gc_ref.at[b], gr_refs[b], gh_ref, hs_ref.at[b], c_ref.at[b],
                            n_ref.at[b], mt_ref.at[b], c_s.at[b], n_s.at[b], m_s.at[b])


def _mlstm_prompt(main, gates, gates_t, g_head):
    t = MLSTM_CHUNK
    nc = SEQ // t
    return pl.pallas_call(
        _mlstm_prompt_kernel,
        grid=(nc,),
        in_specs=[pl.BlockSpec((BATCH, t, A_MAIN), lambda j: (0, j, 0)),
                  pl.BlockSpec((BATCH, t, LANES), lambda j: (0, j, 0))]
                 + [pl.BlockSpec((SUBLANES, t), functools.partial(lambda j, b: (0, b * nc + j), b=b))
                    for b in range(BATCH)]
                 + [_const_spec((A_HEADS, A_DV))],
        out_specs=[pl.BlockSpec((BATCH, t, A_HEADS * A_DV), lambda j: (0, j, 0)),
                   _const_spec((BATCH, A_HEADS, A_DV, A_DK)),
                   _const_spec((BATCH, A_HEADS, A_DK)),
                   pl.BlockSpec((BATCH, t, LANES), lambda j: (0, j, 0))],
        out_shape=[jax.ShapeDtypeStruct((BATCH, SEQ, A_HEADS * A_DV), bf16),
                   jax.ShapeDtypeStruct((BATCH, A_HEADS, A_DV, A_DK), f32),
                   jax.ShapeDtypeStruct((BATCH, A_HEADS, A_DK), f32),
                   jax.ShapeDtypeStruct((BATCH, SEQ, LANES), f32)],
        scratch_shapes=[pltpu.VMEM((BATCH, A_HEADS, A_DK, A_DV), f32),
                        pltpu.VMEM((BATCH, A_HEADS, A_DK), f32),
                        pltpu.VMEM((BATCH, SUBLANES, LANES), f32)],
        compiler_params=_params("arbitrary"),
        name="mlstm_prompt",
    )(main, gates, *([gates_t] * BATCH), g_head)


def _mlstm_sample_compute(main_ref, gc_ref, gr_ref, m0_ref, c0_ref, n0_ref, gh_ref,
                          hs_ref, n_ref, mt_ref, vdt_ref, kb_ref, sc_ref, acc_s, nt_s):
    t = MLSTM_ST
    seg = DEC_SEQ
    _, c, same, lower, upper = _mlstm_masks(t, seg)
    seg_last = same & (_imod(c, seg) == seg - 1)
    gc = gc_ref[...]
    gr = gr_ref[...]
    btr_all = _dot_mask(gr, upper)
    lane = lax.broadcasted_iota(jnp.int32, (t, LANES), 1)
    tok_seg = _idiv(lax.broadcasted_iota(jnp.int32, (A_DV, t), 1), seg)

    heads = range(A_HEADS)
    rep = lambda col: jnp.broadcast_to(col, (t, LANES))
    stack = lambda xs: jnp.concatenate(list(xs), axis=0)
    lanes = lambda xs: jnp.concatenate(list(xs), axis=1)
    head = lambda x, h: x[h * t:(h + 1) * t]
    cols = [_mlstm_cols(main_ref, h) for h in heads]
    qs, ks, vs = [c_[0] for c_ in cols], [c_[1] for c_ in cols], [c_[2] for c_ in cols]
    qbs, kbs, vbs = [q.astype(bf16) for q in qs], [k.astype(bf16) for k in ks], [v.astype(bf16) for v in vs]

    btc_all = _mask_dot(lower, lanes(rep(gc[:, A_HEADS + h:A_HEADS + h + 1]) for h in heads))
    btc = stack(btc_all[:, h * LANES:(h + 1) * LANES] for h in heads)
    itc = stack(rep(gc[:, h:h + 1]) for h in heads)
    m_in = stack(rep(m0_ref[:, h:h + 1]) for h in heads)
    dmat = stack(jnp.where(lower, head(btc, h) + (gr[h:h + 1, :] - btr_all[A_HEADS + h:A_HEADS + h + 1, :]), -jnp.inf)
                 for h in heads)
    inter = btc + m_in
    m_t = jnp.maximum(jnp.broadcast_to(jnp.max(dmat, axis=1, keepdims=True), inter.shape), inter)
    w = jnp.exp(dmat - m_t)
    w_inter = jnp.exp(inter - m_t)
    s = stack(_dot_nt(qbs[h], kbs[h]) for h in heads) * w
    sb = s.astype(bf16)
    num = stack(_dot(head(sb, h), vbs[h]) for h in heads)

    for h in heads:
        def inter_body(b, carry, h=h):
            rows = pl.ds(pl.multiple_of(h * t + b * seg, seg), seg)
            r = _dot_nt(c0_ref[b, h].astype(bf16), qbs[h])
            acc_s[h] = jnp.where(tok_seg == b, r, acc_s[h])
            nt_s[rows, :] = jnp.broadcast_to(n0_ref[b, h:h + 1, :], (seg, A_DK))
            return carry

        acc_s[h] = jnp.zeros((A_DV, t), f32)
        lax.fori_loop(0, MLSTM_SB, inter_body, 0, unroll=MLSTM_UNROLL)
    n_tok = nt_s[...]
    num = num + _wide(w_inter, A_DV // LANES) * stack(acc_s[h].T for h in heads)
    den = _row_sum(s) + w_inter * _row_sum(stack(qs) * n_tok)
    inv = 1.0 / jnp.maximum(jnp.abs(den), jnp.exp(-m_t))
    norm = inv * lax.rsqrt(inv * inv * (_row_sum(num * num) * (1.0 / A_DV)) + NORM_EPS)
    out = _sigmoid(stack(c_[3] for c_ in cols)) * (num * _wide(norm, A_DV // LANES))
    for h in heads:
        hs_ref[:, h * A_DV:(h + 1) * A_DV] = (head(out, h) * gh_ref[h:h + 1, :]).astype(bf16)

    last = _mask_dot(seg_last, lanes([head(x, h) for x in (m_t, btc, inter) for h in heads]))
    pick = lambda i: stack(last[:, (i * A_HEADS + h) * LANES:(i * A_HEADS + h + 1) * LANES] for h in heads)
    m_new, bt_last, inter_last = pick(0), pick(1), pick(2)
    decay = jnp.exp(bt_last - btc + itc - m_new)
    scale = jnp.exp(inter_last - m_new)
    seg_sum = _mask_dot(same, lanes(head(decay, h) * ks[h] for h in heads))
    n_new = scale * n_tok + stack(seg_sum[:, h * A_DK:(h + 1) * A_DK] for h in heads)
    mt_all = jnp.zeros((t, LANES), f32)
    sc_all = jnp.zeros((t, LANES), f32)
    for h in heads:
        vdt_ref[h] = (_wide(head(decay, h), A_DV // LANES) * vs[h]).T.astype(bf16)
        kb_ref[:, h * A_DK:(h + 1) * A_DK] = kbs[h]
        sc_all = jnp.where(lane == h, head(scale, h), sc_all)
        mt_all = jnp.where(lane == h, head(m_t, h), mt_all)
    for b in range(MLSTM_SB):
        n_ref[b] = stack(head(n_new, h)[b * seg:b * seg + 1, :] for h in heads)
    mt_ref[...] = mt_all
    sc_ref[...] = sc_all


def _mlstm_sample_update(c_in_ref, c_out_ref, vdt_ref, kb_ref, sc_ref):
    seg = DEC_SEQ
    lane_seg = _idiv(lax.broadcasted_iota(jnp.int32, (1, MLSTM_ST), 1), seg)
    for h in range(A_HEADS):
        def upd_body(b, carry, h=h):
            rows = pl.ds(pl.multiple_of(b * seg, seg), seg)
            scale = sc_ref[rows, :][0:1, h:h + 1]
            onehot = jnp.where(lane_seg == b, 1.0, 0.0).astype(bf16)
            upd = _dot(vdt_ref[h] * onehot, kb_ref[:, h * A_DK:(h + 1) * A_DK])
            c_out_ref[b, h] = scale * c_in_ref[b, h] + upd
            return carry

        lax.fori_loop(0, MLSTM_SB, upd_body, 0, unroll=MLSTM_UNROLL)


def _mlstm_sample_first_kernel(main_ref, gc_ref, gr_ref, m0_ref, c0_ref, n0_ref, gh_ref,
                               hs_ref, n_ref, mt_ref, vdt_ref, kb_ref, sc_ref, acc_s, nt_s):
    _mlstm_sample_compute(main_ref, gc_ref, gr_ref, m0_ref, c0_ref, n0_ref, gh_ref,
                          hs_ref, n_ref, mt_ref, vdt_ref, kb_ref, sc_ref, acc_s, nt_s)


def _mlstm_sample_second_kernel(main_ref, gc_ref, gr_ref, m0_ref, c0_ref, n0_ref, gh_ref, vdt0_ref, kb0_ref, sc0_ref,
                                hs_ref, n_ref, mt_ref, c_ref, vdt_s, kb_s, sc_s, acc_s, nt_s):
    layer = pl.program_id(0)

    @pl.when(layer == 0)
    def _():
        _mlstm_sample_update(c0_ref, c_ref, vdt0_ref, kb0_ref, sc0_ref)

    @pl.when(layer == 1)
    def _():
        _mlstm_sample_compute(main_ref, gc_ref, gr_ref, m0_ref, c0_ref, n0_ref, gh_ref,
                              hs_ref, n_ref, mt_ref, vdt_s, kb_s, sc_s, acc_s, nt_s)
        _mlstm_sample_update(c0_ref, c_ref, vdt_s, kb_s, sc_s)


def _mlstm_sample_scratch():
    t = MLSTM_ST
    return [pltpu.VMEM((A_HEADS, A_DV, t), f32), pltpu.VMEM((A_HEADS * t, A_DK), f32)]


def _mlstm_sample_first(main, gates, gates_t, m0_tok, c_all, n0, g_head):
    t = MLSTM_ST
    return pl.pallas_call(
        _mlstm_sample_first_kernel,
        grid=(DEC_BATCH // MLSTM_SB,),
        in_specs=[pl.BlockSpec((t, A_MAIN), lambda i: (i, 0)),
                  pl.BlockSpec((t, LANES), lambda i: (i, 0)),
                  pl.BlockSpec((SUBLANES, t), lambda i: (0, i)),
                  pl.BlockSpec((t, LANES), lambda i: (i, 0)),
                  pl.BlockSpec((None, MLSTM_SB, A_HEADS, A_DV, A_DK), lambda i: (0, i, 0, 0, 0)),
                  pl.BlockSpec((MLSTM_SB, A_HEADS, A_DK), lambda i: (i, 0, 0)),
                  _const_spec((A_HEADS, A_DV))],
        out_specs=[pl.BlockSpec((t, A_HEADS * A_DV), lambda i: (i, 0)),
                   pl.BlockSpec((MLSTM_SB, A_HEADS, A_DK), lambda i: (i, 0, 0)),
                   pl.BlockSpec((t, LANES), lambda i: (i, 0)),
                   pl.BlockSpec((A_HEADS, A_DV, t), lambda i: (0, 0, i)),
                   pl.BlockSpec((t, A_QK), lambda i: (i, 0)),
                   pl.BlockSpec((t, LANES), lambda i: (i, 0))],
        out_shape=[jax.ShapeDtypeStruct((N_S, A_HEADS * A_DV), bf16),
                   jax.ShapeDtypeStruct((DEC_BATCH, A_HEADS, A_DK), f32),
                   jax.ShapeDtypeStruct((N_S, LANES), f32),
                   jax.ShapeDtypeStruct((A_HEADS, A_DV, N_S), bf16),
                   jax.ShapeDtypeStruct((N_S, A_QK), bf16),
                   jax.ShapeDtypeStruct((N_S, LANES), f32)],
        scratch_shapes=_mlstm_sample_scratch(),
        compiler_params=_params("parallel"),
        name="mlstm_sample_first",
    )(main, gates, gates_t, m0_tok, c_all, n0, g_head)


def _mlstm_sample_second(main, gates, gates_t, m0_tok, c_all, n0, g_head, vdt0, kb0, sc0):
    t = MLSTM_ST
    own = lambda l, i: i * l
    first = lambda l, i: i * (1 - l)
    return pl.pallas_call(
        _mlstm_sample_second_kernel,
        grid=(N_A, DEC_BATCH // MLSTM_SB),
        in_specs=[pl.BlockSpec((t, A_MAIN), lambda l, i: (own(l, i), 0)),
                  pl.BlockSpec((t, LANES), lambda l, i: (own(l, i), 0)),
                  pl.BlockSpec((SUBLANES, t), lambda l, i: (0, own(l, i))),
                  pl.BlockSpec((t, LANES), lambda l, i: (own(l, i), 0)),
                  pl.BlockSpec((None, MLSTM_SB, A_HEADS, A_DV, A_DK), lambda l, i: (l, i, 0, 0, 0)),
                  pl.BlockSpec((MLSTM_SB, A_HEADS, A_DK), lambda l, i: (own(l, i), 0, 0)),
                  _const_spec((A_HEADS, A_DV)),
                  pl.BlockSpec((A_HEADS, A_DV, t), lambda l, i: (0, 0, first(l, i))),
                  pl.BlockSpec((t, A_QK), lambda l, i: (first(l, i), 0)),
                  pl.BlockSpec((t, LANES), lambda l, i: (first(l, i), 0))],
        out_specs=[pl.BlockSpec((t, A_HEADS * A_DV), lambda l, i: (own(l, i), 0)),
                   pl.BlockSpec((MLSTM_SB, A_HEADS, A_DK), lambda l, i: (own(l, i), 0, 0)),
                   pl.BlockSpec((t, LANES), lambda l, i: (own(l, i), 0)),
                   pl.BlockSpec((None, MLSTM_SB, A_HEADS, A_DV, A_DK), lambda l, i: (l, i, 0, 0, 0))],
        out_shape=[jax.ShapeDtypeStruct((N_S, A_HEADS * A_DV), bf16),
                   jax.ShapeDtypeStruct((DEC_BATCH, A_HEADS, A_DK), f32),
                   jax.ShapeDtypeStruct((N_S, LANES), f32),
                   jax.ShapeDtypeStruct((N_A, DEC_BATCH, A_HEADS, A_DV, A_DK), f32)],
        scratch_shapes=[pltpu.VMEM((A_HEADS, A_DV, t), bf16),
                        pltpu.VMEM((t, A_QK), bf16),
                        pltpu.VMEM((t, LANES), f32)] + _mlstm_sample_scratch(),
        compiler_params=_params("arbitrary", "arbitrary"),
        name="mlstm_sample_second",
    )(main, gates, gates_t, m0_tok, c_all, n0, g_head, vdt0, kb0, sc0)


def _swa_softmax_pv(s, allowed, sink_col, vb):
    s = jnp.where(allowed, s * (B_HD ** -0.5), -jnp.inf)
    m = jnp.maximum(jnp.max(s, axis=1, keepdims=True), sink_col)
    p = jnp.exp(s - m)
    den = jnp.sum(p, axis=1, keepdims=True) + jnp.exp(sink_col - m)
    return _dot((p / den).astype(bf16), vb)


def _swa_window_block(q_groups, kb, vb, sink_ref, has_prev):
    w = WINDOW
    rows = B_GROUP * w
    t = _imod(lax.broadcasted_iota(jnp.int32, (rows, 2 * w), 0), w)
    jj = lax.broadcasted_iota(jnp.int32, (rows, 2 * w), 1)
    bias = jnp.where((jj > t) & (jj <= t + w) & (has_prev | (jj >= w)), 0.0, -jnp.inf)
    lane_head = _idiv(lax.broadcasted_iota(jnp.int32, (1, B_KV), 1), B_HD)
    grp = _idiv(lax.broadcasted_iota(jnp.int32, (rows, 1), 0), w)
    outs = [jnp.zeros((w, B_KV), f32) for _ in range(B_GROUP)]
    for h in range(B_KV_HEADS):
        hm = lane_head == h
        qscale = jnp.where(hm, B_HD ** -0.5, 0.0)
        qh = jnp.concatenate([q * qscale for q in q_groups], axis=0).astype(bf16)
        sink_col = jnp.zeros((rows, 1), f32)
        for g in range(B_GROUP):
            sink_col = jnp.where(grp == g, sink_ref[h * B_GROUP + g], sink_col)
        s = _dot_nt(qh, kb) + bias
        m = jnp.maximum(jnp.max(s, axis=1, keepdims=True), sink_col)
        p = jnp.exp(s - m)
        den = jnp.sum(p, axis=1, keepdims=True) + jnp.exp(sink_col - m)
        r = _dot((p * (1.0 / den)).astype(bf16), vb)
        for g in range(B_GROUP):
            outs[g] = jnp.where(hm, r[g * w:(g + 1) * w, :], outs[g])
    return outs


def _swa_prompt_kernel(sink_ref, q_ref, ko_ref, vo_ref, kp_ref, vp_ref, a_ref):
    w = WINDOW
    tile_has_prev = pl.program_id(1) > 0
    for nb in range(SWA_TILE // w):
        r0 = nb * w
        if nb == 0:
            k_prev, v_prev, has_prev = kp_ref[...], vp_ref[...], tile_has_prev
        else:
            k_prev, v_prev, has_prev = ko_ref[r0 - w:r0, :], vo_ref[r0 - w:r0, :], True
        kb = jnp.concatenate([k_prev, ko_ref[r0:r0 + w, :]], axis=0).astype(bf16)
        vb = jnp.concatenate([v_prev, vo_ref[r0:r0 + w, :]], axis=0).astype(bf16)
        outs = _swa_window_block([q_ref[r0:r0 + w, g * B_KV:(g + 1) * B_KV] for g in range(B_GROUP)],
                                 kb, vb, sink_ref, has_prev)
        for g in range(B_GROUP):
            a_ref[r0:r0 + w, g * B_KV:(g + 1) * B_KV] = outs[g].astype(bf16)


def _swa_prompt(proj, sinks):
    nt = SEQ // SWA_TILE
    w = WINDOW
    per = SWA_TILE // w
    kcol = B_Q // B_KV
    vcol = kcol + 1
    tile = lambda b, n: b * nt + n
    prev = lambda b, n: jnp.maximum(tile(b, n) * per - 1, 0)
    return pl.pallas_call(
        _swa_prompt_kernel,
        grid=(BATCH, nt),
        in_specs=[pl.BlockSpec(memory_space=pltpu.SMEM),
                  pl.BlockSpec((SWA_TILE, B_Q), lambda b, n: (tile(b, n), 0)),
                  pl.BlockSpec((SWA_TILE, B_KV), lambda b, n: (tile(b, n), kcol)),
                  pl.BlockSpec((SWA_TILE, B_KV), lambda b, n: (tile(b, n), vcol)),
                  pl.BlockSpec((w, B_KV), lambda b, n: (prev(b, n), kcol)),
                  pl.BlockSpec((w, B_KV), lambda b, n: (prev(b, n), vcol))],
        out_specs=pl.BlockSpec((SWA_TILE, B_Q), lambda b, n: (tile(b, n), 0)),
        out_shape=jax.ShapeDtypeStruct((N_P, B_Q), bf16),
        compiler_params=_params("parallel", "arbitrary"),
        name="swa_prompt",
    )(sinks, proj, proj, proj, proj, proj)


def _swa_sample_kernel(sink_ref, q_ref, kn_ref, vn_ref, kc_ref, vc_ref, o_ref, ko_ref, vo_ref):
    s_len = DEC_SEQ
    buf = WINDOW
    rows = B_HEADS * s_len
    keys = 2 * buf
    ri = lax.broadcasted_iota(jnp.int32, (rows, keys), 0)
    jj = lax.broadcasted_iota(jnp.int32, (rows, keys), 1)
    t = _imod(ri, s_len)
    allowed = ((jj < buf) & (jj > t)) | ((jj >= buf) & (jj - buf <= t))
    lane_head = _idiv(lax.broadcasted_iota(jnp.int32, (1, B_KV), 1), B_HD)
    hg = _idiv(lax.broadcasted_iota(jnp.int32, (rows, 1), 0), s_len)
    sink_col = jnp.zeros((rows, 1), f32)
    for i in range(B_HEADS):
        sink_col = jnp.where(hg == i, sink_ref[i], sink_col)
    pad = jnp.zeros((keys - buf - s_len, B_KV), f32)

    def body(e, carry):
        rs = pl.ds(pl.multiple_of(e * s_len, s_len), s_len)
        kn = kn_ref[rs, :]
        vn = vn_ref[rs, :]
        kc = kc_ref[e]
        vc = vc_ref[e]
        kpad = jnp.concatenate([kc, kn, pad], axis=0).astype(bf16)
        vpad = jnp.concatenate([vc, vn, pad], axis=0).astype(bf16)
        ko_ref[e, 0:buf - s_len, :] = kc[s_len:, :]
        ko_ref[e, buf - s_len:, :] = kn
        vo_ref[e, 0:buf - s_len, :] = vc[s_len:, :]
        vo_ref[e, buf - s_len:, :] = vn
        qe = q_ref[rs, :]
        qbig = jnp.concatenate(
            [jnp.where(lane_head == h, qe[:, g * B_KV:(g + 1) * B_KV], 0.0)
             for h in range(B_KV_HEADS) for g in range(B_GROUP)], axis=0).astype(bf16)
        r = _swa_softmax_pv(_dot_nt(qbig, kpad), allowed, sink_col, vpad)
        for g in range(B_GROUP):
            og = jnp.zeros((s_len, B_KV), f32)
            for h in range(B_KV_HEADS):
                blk = (h * B_GROUP + g) * s_len
                og = og + jnp.where(lane_head == h, r[blk:blk + s_len, :], 0.0)
            o_ref[rs, g * B_KV:(g + 1) * B_KV] = og.astype(bf16)
        return carry

    lax.fori_loop(0, SWA_SB, body, 0, unroll=4)


def _swa_sample(proj, sinks, k_cache, v_cache):
    t = SWA_SB * DEC_SEQ
    kcol = B_Q // B_KV
    return pl.pallas_call(
        _swa_sample_kernel,
        grid=(DEC_BATCH // SWA_SB,),
        in_specs=[pl.BlockSpec(memory_space=pltpu.SMEM),
                  pl.BlockSpec((t, B_Q), lambda i: (i, 0)),
                  pl.BlockSpec((t, B_KV), lambda i: (i, kcol)),
                  pl.BlockSpec((t, B_KV), lambda i: (i, kcol + 1)),
                  pl.BlockSpec((SWA_SB, WINDOW, B_KV), lambda i: (i, 0, 0)),
                  pl.BlockSpec((SWA_SB, WINDOW, B_KV), lambda i: (i, 0, 0))],
        out_specs=[pl.BlockSpec((t, B_Q), lambda i: (i, 0)),
                   pl.BlockSpec((SWA_SB, WINDOW, B_KV), lambda i: (i, 0, 0)),
                   pl.BlockSpec((SWA_SB, WINDOW, B_KV), lambda i: (i, 0, 0))],
        out_shape=[jax.ShapeDtypeStruct((N_S, B_Q), bf16),
                   jax.ShapeDtypeStruct((DEC_BATCH, WINDOW, B_KV), f32),
                   jax.ShapeDtypeStruct((DEC_BATCH, WINDOW, B_KV), f32)],
        compiler_params=_params("parallel"),
        name="swa_sample",
    )(sinks, proj, proj, proj, k_cache, v_cache)


def _rg_conv_group(x8, p8, wc_ref, bc_ref):
    row = lax.broadcasted_iota(jnp.int32, x8.shape, 0)
    u = bc_ref[...] + wc_ref[CONV_W - 1] * x8
    for d in range(1, CONV_W):
        sh = pltpu.roll(jnp.where(row >= SUBLANES - d, p8, x8), d, 0)
        u = u + wc_ref[CONV_W - 1 - d] * sh
    return u


def _rg_scan_group(a8, b8, carry):
    row = lax.broadcasted_iota(jnp.int32, a8.shape, 0)
    for d in (1, 2, 4):
        keep = row >= d
        b8 = jnp.where(keep, a8 * pltpu.roll(b8, d, 0) + b8, b8)
        a8 = jnp.where(keep, a8 * pltpu.roll(a8, d, 0), a8)
    return a8 * carry + b8


def _rg_gate_rows(ra, rx, neg_rate, u):
    z = _sigmoid(ra) * neg_rate
    a = jnp.exp(-z)
    return a, _sqrt_nonneg(jnp.tanh(z) * (a * a + 1.0)) * _sigmoid(rx) * u


def _rg_gates(u, gate, wa_ref, ba_ref, wx_ref, bx_ref, lam_ref):
    ub = u.astype(bf16)
    ra = jnp.concatenate([_dot(ub[:, n * C_BW:(n + 1) * C_BW], wa_ref[n]) for n in range(C_BLOCKS)], axis=1)
    rx = jnp.concatenate([_dot(ub[:, n * C_BW:(n + 1) * C_BW], wx_ref[n]) for n in range(C_BLOCKS)], axis=1)
    a, bterm = _rg_gate_rows(ra + ba_ref[0:1, :], rx + bx_ref[0:1, :], LRU_C * _softplus(-lam_ref[0:1, :]), u)
    return a, bterm, _gelu_tanh(gate)


def _rg_prompt_kernel(p_ref, wc_ref, bc_ref, wa_ref, ba_ref, wx_ref, bx_ref, lam_ref,
                      y_ref, h_ref, u_s, a_s, b_s, xc_s, hc_s):
    j = pl.program_id(1)
    ng = RG_T // SUBLANES
    out_rows = 2 * SUBLANES

    @pl.when(j == 0)
    def _():
        xc_s[...] = jnp.zeros_like(xc_s)
        hc_s[...] = jnp.zeros_like(hc_s)

    def group(gidx, n=SUBLANES):
        return pl.ds(pl.multiple_of(gidx * n, n), n)

    def conv_body(gidx, p8):
        x8 = p_ref[group(gidx), 0:C_WIDTH]
        u_s[group(gidx), :] = _rg_conv_group(x8, p8, wc_ref, bc_ref)
        return x8

    xc_s[...] = lax.fori_loop(0, ng, conv_body, xc_s[...], unroll=2)
    ub = u_s[...].astype(bf16)
    for n in range(C_BLOCKS):
        cols = slice(n * C_BW, (n + 1) * C_BW)
        a_s[:, cols] = _dot(ub[:, cols], wa_ref[n])
        b_s[:, cols] = _dot(ub[:, cols], wx_ref[n])
    neg_rate = LRU_C * _softplus(-lam_ref[...])

    def gate_body(gidx, carry):
        rows = group(gidx)
        a, bterm = _rg_gate_rows(a_s[rows, :] + ba_ref[...], b_s[rows, :] + bx_ref[...], neg_rate, u_s[rows, :])
        a_s[rows, :] = a
        b_s[rows, :] = bterm
        return carry

    lax.fori_loop(0, ng, gate_body, 0, unroll=4)

    def scan_body(gidx, carry):
        rows = group(gidx, out_rows)
        lo = pl.ds(pl.multiple_of(gidx * out_rows, out_rows), SUBLANES)
        hi = pl.ds(pl.multiple_of(gidx * out_rows + SUBLANES, SUBLANES), SUBLANES)
        h_lo = _rg_scan_group(a_s[lo, :], b_s[lo, :], carry)
        h_hi = _rg_scan_group(a_s[hi, :], b_s[hi, :], h_lo[SUBLANES - 1:SUBLANES, :])
        h16 = jnp.concatenate([h_lo, h_hi], axis=0)
        y_ref[rows, :] = (h16 * _gelu_tanh(p_ref[rows, C_WIDTH:])).astype(bf16)
        return h_hi[SUBLANES - 1:SUBLANES, :]

    h_last = lax.fori_loop(0, RG_T // out_rows, scan_body, hc_s[...])
    hc_s[...] = h_last

    @pl.when(j == pl.num_programs(1) - 1)
    def _():
        h_ref[0] = h_last


def _rg_weight_specs():
    rows = (SUBLANES, C_WIDTH)
    return [_const_spec((CONV_W,) + rows), _const_spec(rows),
            _const_spec((C_BLOCKS, C_BW, C_BW)), _const_spec(rows),
            _const_spec((C_BLOCKS, C_BW, C_BW)), _const_spec(rows),
            _const_spec(rows)]


def _rg_prompt(proj, weights):
    nt = SEQ // RG_T
    return pl.pallas_call(
        _rg_prompt_kernel,
        grid=(BATCH, nt),
        in_specs=[pl.BlockSpec((RG_T, 2 * C_WIDTH), lambda b, j: (b * nt + j, 0))] + _rg_weight_specs(),
        out_specs=[pl.BlockSpec((RG_T, C_WIDTH), lambda b, j: (b * nt + j, 0)),
                   pl.BlockSpec((1, 1, C_WIDTH), lambda b, j: (b, 0, 0))],
        out_shape=[jax.ShapeDtypeStruct((N_P, C_WIDTH), bf16),
                   jax.ShapeDtypeStruct((BATCH, 1, C_WIDTH), f32)],
        scratch_shapes=[pltpu.VMEM((RG_T, C_WIDTH), f32),
                        pltpu.VMEM((RG_T, C_WIDTH), f32),
                        pltpu.VMEM((RG_T, C_WIDTH), f32),
                        pltpu.VMEM((SUBLANES, C_WIDTH), f32),
                        pltpu.VMEM((1, C_WIDTH), f32)],
        compiler_params=_params("parallel", "arbitrary"),
        name="rg_prompt",
    )(proj, *weights)


def _rg_sample_kernel(p_ref, cv_ref, h0_ref, wc_ref, bc_ref, wa_ref, ba_ref, wx_ref, bx_ref, lam_ref,
                      y_ref, h_ref, u_s, a_s, b_s):
    def conv_body(gidx, carry):
        rows = pl.ds(pl.multiple_of(gidx * SUBLANES, SUBLANES), SUBLANES)
        u_s[rows, :] = _rg_conv_group(p_ref[rows, 0:C_WIDTH], cv_ref[rows, :], wc_ref, bc_ref)
        return carry

    lax.fori_loop(0, DEC_BATCH, conv_body, 0)
    a, bterm, gg = _rg_gates(u_s[...], p_ref[:, C_WIDTH:], wa_ref, ba_ref, wx_ref, bx_ref, lam_ref)
    a_s[...] = a
    b_s[...] = bterm

    def scan_body(gidx, carry):
        rows = pl.ds(pl.multiple_of(gidx * SUBLANES, SUBLANES), SUBLANES)
        h8 = _rg_scan_group(a_s[rows, :], b_s[rows, :], h0_ref[pl.ds(gidx, 1), :])
        u_s[rows, :] = h8
        h_ref[pl.ds(gidx, 1), :] = h8[SUBLANES - 1:SUBLANES, :]
        return carry

    lax.fori_loop(0, DEC_BATCH, scan_body, 0)
    y_ref[...] = (u_s[...] * gg).astype(bf16)


def _rg_sample(proj, conv_pad, h0, weights):
    return pl.pallas_call(
        _rg_sample_kernel,
        grid=(1,),
        in_specs=[_const_spec((N_S, 2 * C_WIDTH)),
                  _const_spec((N_S, C_WIDTH)),
                  _const_spec((DEC_BATCH, C_WIDTH))] + _rg_weight_specs(),
        out_specs=[_const_spec((N_S, C_WIDTH)), _const_spec((DEC_BATCH, C_WIDTH))],
        out_shape=[jax.ShapeDtypeStruct((N_S, C_WIDTH), bf16),
                   jax.ShapeDtypeStruct((DEC_BATCH, C_WIDTH), f32)],
        scratch_shapes=[pltpu.VMEM((N_S, C_WIDTH), f32),
                        pltpu.VMEM((N_S, C_WIDTH), f32),
                        pltpu.VMEM((N_S, C_WIDTH), f32)],
        compiler_params=_params("arbitrary"),
        name="rg_sample",
    )(proj, conv_pad, h0, *weights)


def _mlp_tile(x_ref, a_ref, o_ref, wo_ref, bo_ref, g_ref, wup_ref, wdn_ref, gf_ref, x1_s, xn_s, final):
    x1_s[...] = x_ref[...] + _dot(a_ref[...], wo_ref[...]) + bo_ref[...]
    xn_s[...] = _rms_bf16(x1_s[...], g_ref[...])
    for c in range(0, D_FF, FF_CHUNK):
        hmid = jnp.maximum(_dot(xn_s[...], wup_ref[:, c:c + FF_CHUNK]), 0.0)
        x1_s[...] += _dot((hmid * hmid).astype(bf16), wdn_ref[c:c + FF_CHUNK, :])
    if final:
        x1 = x1_s[...]
        y = x1 * lax.rsqrt(jnp.mean(x1 * x1, axis=-1, keepdims=True) + NORM_EPS)
        o_ref[...] = y * gf_ref[...]
    else:
        o_ref[...] = x1_s[...]


def _out_mlp_kernel(xp_ref, xs_ref, ap_ref, as_ref, wo_ref, bo_ref, g_ref, wup_ref, wdn_ref, gf_ref,
                    op_ref, os_ref, x1_s, xn_s, *, final):
    def body(x_ref, a_ref, o_ref):
        _mlp_tile(x_ref, a_ref, o_ref, wo_ref, bo_ref, g_ref, wup_ref, wdn_ref, gf_ref, x1_s, xn_s, final)

    _for_each_group(body, (xp_ref, ap_ref, op_ref), (xs_ref, as_ref, os_ref))


def _out_mlp(x_p, x_s, a_p, a_s, wo, bo, g, w_up_all, w_down_all, layer, g_final, final):
    single = pl.Buffered(1)
    wspec = lambda shape: pl.BlockSpec(shape, lambda i: (0, 0), pipeline_mode=single)
    lspec = lambda shape: pl.BlockSpec((None,) + shape, lambda i: (layer, 0, 0), pipeline_mode=single)
    return pl.pallas_call(
        functools.partial(_out_mlp_kernel, final=final),
        grid=(NP_TILES + NS_TILES,),
        in_specs=[_p_spec(D_MODEL), _s_spec(D_MODEL), _p_spec(D_MODEL), _s_spec(D_MODEL),
                  wspec((D_MODEL, D_MODEL)),
                  wspec((1, D_MODEL)),
                  wspec((1, D_MODEL)),
                  lspec((D_MODEL, D_FF)),
                  lspec((D_FF, D_MODEL)),
                  wspec((1, D_MODEL))],
        out_specs=[_p_spec(D_MODEL), _s_spec(D_MODEL)],
        out_shape=[jax.ShapeDtypeStruct((N_P, D_MODEL), f32), jax.ShapeDtypeStruct((N_S, D_MODEL), f32)],
        scratch_shapes=[pltpu.VMEM((TM, D_MODEL), f32),
                        pltpu.VMEM((TM, D_MODEL), bf16)],
        compiler_params=_params("arbitrary"),
        name="out_mlp",
    )(x_p, x_s, a_p, a_s, wo, bo, g, w_up_all, w_down_all, g_final)


def _rope_tables():
    half = ROPE_DIM // 2
    inv = np.float32(ROPE_THETA) ** (-np.arange(0, ROPE_DIM, 2, dtype=np.float32) / np.float32(ROPE_DIM))
    lane = np.arange(LANES) % B_HD
    inv_lane = np.where(lane < ROPE_DIM, inv[lane % half], np.float32(0.0)).astype(np.float32)
    pos = np.concatenate([np.arange(SEQ), PAST_LEN + np.arange(TM) % DEC_SEQ]).astype(np.float32)
    ang = pos[:, None] * inv_lane[None, :]
    tab = np.concatenate([np.cos(ang), np.sin(ang)], axis=1).astype(np.float32)
    consts = np.zeros((SUBLANES, LANES), np.float32)
    consts[0] = np.where(lane < half, -1.0, 0.0)
    consts[1] = np.where((lane >= half) & (lane < ROPE_DIM), 1.0, 0.0)
    return jnp.asarray(tab), jnp.asarray(consts)


def _q_cols_to_group_major(w):
    lead = w.shape[:-1]
    return jnp.swapaxes(w.reshape(lead + (B_KV_HEADS, B_GROUP, B_HD)), -3, -2).reshape(lead + (B_Q,))


def _last_rows(arr, n_seq, seq_len, n_rows, col0, col1):
    return jnp.stack([arr[(s + 1) * seq_len - n_rows:(s + 1) * seq_len, col0:col1] for s in range(n_seq)])


def kernel(x_prompt, x_sample, state_mlstm_c, state_mlstm_n, state_mlstm_m, cache_swa_k, cache_swa_v,
           state_rglru_h, state_rglru_conv, norm_mix, norm_mlp, norm_final, w_mlp_up, w_mlp_down,
           w_mlstm_in, b_mlstm_i, b_mlstm_f, g_mlstm_head, w_mlstm_out, w_swa_qkv, b_swa_qkv, swa_sinks,
           w_swa_out, b_swa_out, w_rg_in, w_rg_conv, b_rg_conv, w_rg_a, b_rg_a, w_rg_x, b_rg_x, rg_lambda,
           w_rg_out):
    assert N_A == 2
    x_p = x_prompt.reshape(N_P, D_MODEL)
    x_s = x_sample.reshape(N_S, D_MODEL)
    zero_bias = jnp.zeros((1, D_MODEL), f32)
    row = lambda v: v.reshape(1, -1).astype(f32)
    w_up_all = w_mlp_up.astype(bf16)
    w_down_all = w_mlp_down.astype(bf16)
    state_c = state_mlstm_c.astype(f32)
    n_gate = 2 * A_HEADS
    assert n_gate == SUBLANES
    w_mlstm_main = w_mlstm_in[:, :, :A_MAIN].astype(bf16)
    w_gate_cols = w_mlstm_in[:, :, A_MAIN:]
    w_mlstm_gate = jnp.pad(w_gate_cols, ((0, 0), (0, 0), (0, LANES - n_gate))).astype(bf16)
    outs = {k: [] for k in ("c_p", "n_p", "m_p", "n_s", "m_s", "k_p", "v_p", "k_s", "v_s",
                            "h_p", "cv_p", "h_s", "cv_s")}
    deferred = None
    c_s_all = None
    for i in range(DEPTH):
        kind, j = i % N_MIXERS, i // N_MIXERS
        g_mix = row(norm_mix[i])
        if kind == 0:
            b_gate = jnp.concatenate([b_mlstm_i[j], b_mlstm_f[j]]).astype(f32)
            main_p, main_s, gates_p, gates_s, gates_p_t, gates_s_t = _proj_mlstm(
                x_p, x_s, g_mix, w_mlstm_main, w_mlstm_gate, jnp.pad(b_gate, (0, LANES - n_gate)).reshape(1, LANES), j)
            g_head = g_mlstm_head[j].astype(f32)
            a_p, c_p, n_p, mt_p = _mlstm_prompt(main_p.reshape(BATCH, SEQ, A_MAIN),
                                                gates_p.reshape(BATCH, SEQ, LANES), gates_p_t, g_head)
            a_p = a_p.reshape(N_P, A_HEADS * A_DV)
            mt_p = mt_p.reshape(N_P, LANES)
            m0_tok = jnp.pad(jnp.repeat(state_mlstm_m[j].astype(f32), DEC_SEQ, axis=0),
                             ((0, 0), (0, LANES - A_HEADS)))
            n0 = state_mlstm_n[j].astype(f32)
            if j == 0:
                a_s, n_s, mt_s, vdt0, kb0, sc0 = _mlstm_sample_first(main_s, gates_s, gates_s_t, m0_tok, state_c, n0,
                                                                     g_head)
                deferred = (vdt0, kb0, sc0)
            else:
                a_s, n_s, mt_s, c_s_all = _mlstm_sample_second(main_s, gates_s, gates_s_t, m0_tok, state_c, n0,
                                                               g_head, *deferred)
            outs["c_p"].append(c_p); outs["n_p"].append(n_p)
            outs["m_p"].append(_last_rows(mt_p, BATCH, SEQ, 1, 0, A_HEADS).reshape(BATCH, A_HEADS))
            outs["n_s"].append(n_s)
            outs["m_s"].append(mt_s[DEC_SEQ - 1::DEC_SEQ, :A_HEADS])
            wo, bo = w_mlstm_out[j], zero_bias
        elif kind == 1:
            rope_tab, rope_consts = _rope_tables()
            w_qkv, b_qkv = w_swa_qkv[j], b_swa_qkv[j]
            w_qkv = jnp.concatenate([_q_cols_to_group_major(w_qkv[:, :B_Q]), w_qkv[:, B_Q:]], axis=1)
            b_qkv = jnp.concatenate([_q_cols_to_group_major(b_qkv[:B_Q]), b_qkv[B_Q:]])
            proj_p, proj_s = _proj_swa(x_p, x_s, g_mix, w_qkv.astype(bf16), row(b_qkv), rope_tab, rope_consts)
            sinks = swa_sinks[j].astype(f32)
            a_p = _swa_prompt(proj_p, sinks)
            buf = cache_swa_k.shape[2]
            a_s, k_s, v_s = _swa_sample(proj_s, sinks,
                                        cache_swa_k[j].astype(f32).reshape(DEC_BATCH, buf, B_KV),
                                        cache_swa_v[j].astype(f32).reshape(DEC_BATCH, buf, B_KV))
            outs["k_p"].append(_last_rows(proj_p, BATCH, SEQ, WINDOW, B_Q, B_Q + B_KV)
                               .reshape(BATCH, WINDOW, B_KV_HEADS, B_HD))
            outs["v_p"].append(_last_rows(proj_p, BATCH, SEQ, WINDOW, B_Q + B_KV, B_IN)
                               .reshape(BATCH, WINDOW, B_KV_HEADS, B_HD))
            outs["k_s"].append(k_s.reshape(DEC_BATCH, buf, B_KV_HEADS, B_HD))
            outs["v_s"].append(v_s.reshape(DEC_BATCH, buf, B_KV_HEADS, B_HD))
            wo = jnp.swapaxes(w_swa_out[j].reshape(B_KV_HEADS, B_GROUP, B_HD, D_MODEL), 0, 1).reshape(B_Q, D_MODEL)
            bo = row(b_swa_out[j])
        else:
            proj_p, proj_s = _proj_rg(x_p, x_s, g_mix, w_rg_in[j].astype(bf16))
            rows8 = lambda v: jnp.broadcast_to(v.astype(f32)[..., None, :], v.shape[:-1] + (SUBLANES, C_WIDTH))
            weights = (rows8(w_rg_conv[j]), rows8(b_rg_conv[j]), w_rg_a[j].astype(bf16), rows8(b_rg_a[j]),
                       w_rg_x[j].astype(bf16), rows8(b_rg_x[j]), rows8(rg_lambda[j]))
            a_p, h_p = _rg_prompt(proj_p, weights)
            conv_pad = jnp.pad(state_rglru_conv[j].astype(f32),
                               ((0, 0), (SUBLANES - (CONV_W - 1), 0), (0, 0))).reshape(N_S, C_WIDTH)
            a_s, h_s = _rg_sample(proj_s, conv_pad, state_rglru_h[j].astype(f32), weights)
            outs["h_p"].append(h_p.reshape(BATCH, C_WIDTH))
            outs["cv_p"].append(_last_rows(proj_p, BATCH, SEQ, CONV_W - 1, 0, C_WIDTH))
            outs["h_s"].append(h_s)
            outs["cv_s"].append(proj_s[:, :C_WIDTH].reshape(DEC_BATCH, DEC_SEQ, C_WIDTH)[:, DEC_SEQ - (CONV_W - 1):])
            wo, bo = w_rg_out[j], zero_bias
        x_p, x_s = _out_mlp(x_p, x_s, a_p, a_s, wo.astype(bf16), bo, row(norm_mlp[i]), w_up_all, w_down_all, i,
                            row(norm_final), final=(i == DEPTH - 1))
    st = {k: jnp.stack(v) for k, v in outs.items()}
    y_p = x_p.reshape(BATCH, SEQ, D_MODEL)
    y_s = x_s.reshape(DEC_BATCH, DEC_SEQ, D_MODEL)
    return (y_p, y_s, st["c_p"], st["n_p"], st["m_p"], c_s_all, st["n_s"], st["m_s"],
            st["k_p"], st["v_p"], st["k_s"], st["v_s"], st["h_p"], st["cv_p"], st["h_s"], st["cv_s"])
```

```python
import functools
import math

import jax
import jax.numpy as jnp
import numpy as np
from jax import lax
from jax.experimental import pallas as pl
from jax.experimental.pallas import tpu as pltpu

f32 = jnp.float32
bf16 = jnp.bfloat16

D_MODEL = 1024
BATCH = 2
SEQ = 8192
DEPTH = 4
DEC_BATCH = 128
DEC_SEQ = 8
PAST_LEN = 8192
N_MIXERS = 3
NORM_EPS = 1e-6

A_HEADS = 4
A_DK = 128
A_DV = 256
A_QK = A_HEADS * A_DK
A_MAIN = 2 * A_QK + 2 * A_HEADS * A_DV
N_A = (DEPTH + 2) // 3

B_HEADS = 16
B_KV_HEADS = 4
B_HD = 64
B_GROUP = 4
B_Q = B_HEADS * B_HD
B_KV = B_KV_HEADS * B_HD
B_IN = B_Q + 2 * B_KV
WINDOW = 128
ROPE_THETA = 500000.0
ROPE_DIM = 16

C_WIDTH = 1024
C_BLOCKS = 4
C_BW = 256
CONV_W = 4
LRU_C = 8.0
D_FF = 4096

N_P = BATCH * SEQ
N_S = DEC_BATCH * DEC_SEQ

LANES = 128
SUBLANES = 8
VMEM_LIMIT = 56 * 1024 * 1024

TM = 512
NP_TILES = N_P // TM
NS_TILES = N_S // TM
MLSTM_CHUNK = 256
MLSTM_SB = 16
MLSTM_ST = MLSTM_SB * DEC_SEQ
MLSTM_UNROLL = 8
SWA_SB = 8
SWA_TILE = 512
RG_T = 512
FF_CHUNK = 512


def _dot(a, b):
    return jnp.dot(a, b, preferred_element_type=f32)


def _dot_nt(a, b):
    return lax.dot_general(a, b, (((1,), (1,)), ((), ())), preferred_element_type=f32)


def _split3(x):
    hi = x.astype(bf16)
    r1 = x - hi.astype(f32)
    mid = r1.astype(bf16)
    lo = (r1 - mid.astype(f32)).astype(bf16)
    return hi, mid, lo


def _mask_dot(mask, x):
    m = jnp.where(mask, 1.0, 0.0).astype(bf16)
    return sum(_dot(m, part) for part in _split3(x))


def _dot_mask(x, mask):
    m = jnp.where(mask, 1.0, 0.0).astype(bf16)
    return sum(_dot(part, m) for part in _split3(x))


def _idiv(x, d):
    assert d & (d - 1) == 0
    return x >> (d.bit_length() - 1)


def _imod(x, d):
    assert d & (d - 1) == 0
    return x & (d - 1)


def _rms_bf16(x, g):
    y = x * lax.rsqrt(jnp.mean(x * x, axis=-1, keepdims=True) + NORM_EPS)
    return (y * g).astype(bf16)


def _sigmoid(x):
    return 0.5 * jnp.tanh(0.5 * x) + 0.5


def _sqrt_nonneg(v):
    return jnp.where(v > 0.0, v * lax.rsqrt(v), 0.0)


def _softplus(x):
    return jnp.maximum(x, 0.0) + jnp.log1p(jnp.exp(-jnp.abs(x)))


def _gelu_tanh(x):
    return 0.5 * x * (1.0 + jnp.tanh(math.sqrt(2.0 / math.pi) * (x + 0.044715 * (x * x * x))))


def _params(*sem):
    return pltpu.CompilerParams(dimension_semantics=sem, vmem_limit_bytes=VMEM_LIMIT)


def _const_spec(shape):
    nd = len(shape)
    return pl.BlockSpec(shape, lambda *_: (0,) * nd)


def _p_spec(width):
    return pl.BlockSpec((TM, width), lambda i: (jnp.minimum(i, NP_TILES - 1), 0))


def _s_spec(width):
    return pl.BlockSpec((TM, width), lambda i: (jnp.maximum(i - NP_TILES, 0), 0))


def _for_each_group(body, p_refs, s_refs):
    i = pl.program_id(0)

    @pl.when(i < NP_TILES)
    def _():
        body(*p_refs)

    @pl.when(i >= NP_TILES)
    def _():
        body(*s_refs)


def _proj_mlstm_kernel(xp_ref, xs_ref, g_ref, w_ref, wg_ref, bg_ref, mp_ref, ms_ref, gp_ref, gs_ref, gtp_ref, gts_ref):
    def body(x_ref, main_ref, gate_ref, gate_t_ref):
        xn = _rms_bf16(x_ref[...], g_ref[...])
        for c in range(0, A_MAIN, 512):
            main_ref[:, c:c + 512] = _dot(xn, w_ref[:, c:c + 512])
        gp = _dot(xn, wg_ref[...]) + bg_ref[...]
        lane = lax.broadcasted_iota(jnp.int32, gp.shape, 1)
        gates = jnp.where(lane >= A_HEADS, -_softplus(-gp), gp)
        gate_ref[...] = gates
        gate_t_ref[...] = gates.T[0:SUBLANES, :]

    _for_each_group(body, (xp_ref, mp_ref, gp_ref, gtp_ref), (xs_ref, ms_ref, gs_ref, gts_ref))


def _proj_mlstm(x_p, x_s, g, w_main_all, w_gate_all, b_gate, layer):
    lspec = lambda shape: pl.BlockSpec((None,) + shape, lambda i: (layer, 0, 0))
    return pl.pallas_call(
        _proj_mlstm_kernel,
        grid=(NP_TILES + NS_TILES,),
        in_specs=[_p_spec(D_MODEL), _s_spec(D_MODEL),
                  _const_spec((1, D_MODEL)),
                  lspec((D_MODEL, A_MAIN)),
                  lspec((D_MODEL, LANES)),
                  _const_spec((1, LANES))],
        out_specs=[_p_spec(A_MAIN), _s_spec(A_MAIN), _p_spec(LANES), _s_spec(LANES),
                   pl.BlockSpec((SUBLANES, TM), lambda i: (0, jnp.minimum(i, NP_TILES - 1))),
                   pl.BlockSpec((SUBLANES, TM), lambda i: (0, jnp.maximum(i - NP_TILES, 0)))],
        out_shape=[jax.ShapeDtypeStruct((N_P, A_MAIN), f32), jax.ShapeDtypeStruct((N_S, A_MAIN), f32),
                   jax.ShapeDtypeStruct((N_P, LANES), f32), jax.ShapeDtypeStruct((N_S, LANES), f32),
                   jax.ShapeDtypeStruct((SUBLANES, N_P), f32), jax.ShapeDtypeStruct((SUBLANES, N_S), f32)],
        compiler_params=_params("arbitrary"),
        name="proj_mlstm",
    )(x_p, x_s, g, w_main_all, w_gate_all, b_gate)


def _proj_swa_kernel(xp_ref, xs_ref, g_ref, w_ref, b_ref, tab_ref, rc_ref, op_ref, os_ref):
    half = ROPE_DIM // 2

    def body(x_ref, o_ref):
        xn = _rms_bf16(x_ref[...], g_ref[...])
        cos = tab_ref[:, 0:LANES]
        sin = tab_ref[:, LANES:2 * LANES]
        sin_lo = sin * rc_ref[0:1, :]
        sin_hi = sin * rc_ref[1:2, :]
        for c in range(0, B_IN, 512):
            p = _dot(xn, w_ref[:, c:c + 512]) + b_ref[:, c:c + 512]
            for l in range(0, 512, LANES):
                pl_ = p[:, l:l + LANES]
                if c + l < B_Q + B_KV:
                    pl_ = (pl_ * cos + pltpu.roll(pl_, LANES - half, 1) * sin_lo
                           + pltpu.roll(pl_, half, 1) * sin_hi)
                o_ref[:, c + l:c + l + LANES] = pl_

    _for_each_group(body, (xp_ref, op_ref), (xs_ref, os_ref))


def _proj_swa(x_p, x_s, g, w, b, rope_tab, rope_consts):
    seq_tiles = SEQ // TM
    tab_idx = lambda i: (jnp.where(i < NP_TILES, i % seq_tiles, seq_tiles), 0)
    return pl.pallas_call(
        _proj_swa_kernel,
        grid=(NP_TILES + NS_TILES,),
        in_specs=[_p_spec(D_MODEL), _s_spec(D_MODEL),
                  _const_spec((1, D_MODEL)),
                  _const_spec((D_MODEL, B_IN)),
                  _const_spec((1, B_IN)),
                  pl.BlockSpec((TM, 2 * LANES), tab_idx),
                  _const_spec((SUBLANES, LANES))],
        out_specs=[_p_spec(B_IN), _s_spec(B_IN)],
        out_shape=[jax.ShapeDtypeStruct((N_P, B_IN), f32), jax.ShapeDtypeStruct((N_S, B_IN), f32)],
        compiler_params=_params("arbitrary"),
        name="proj_swa",
    )(x_p, x_s, g, w, b, rope_tab, rope_consts)


def _proj_rg_kernel(xp_ref, xs_ref, g_ref, w_ref, op_ref, os_ref):
    def body(x_ref, o_ref):
        xn = _rms_bf16(x_ref[...], g_ref[...])
        for c in range(0, 2 * C_WIDTH, 512):
            o_ref[:, c:c + 512] = _dot(xn, w_ref[:, c:c + 512])

    _for_each_group(body, (xp_ref, op_ref), (xs_ref, os_ref))


def _proj_rg(x_p, x_s, g, w):
    return pl.pallas_call(
        _proj_rg_kernel,
        grid=(NP_TILES + NS_TILES,),
        in_specs=[_p_spec(D_MODEL), _s_spec(D_MODEL),
                  _const_spec((1, D_MODEL)),
                  _const_spec((D_MODEL, 2 * C_WIDTH))],
        out_specs=[_p_spec(2 * C_WIDTH), _s_spec(2 * C_WIDTH)],
        out_shape=[jax.ShapeDtypeStruct((N_P, 2 * C_WIDTH), f32), jax.ShapeDtypeStruct((N_S, 2 * C_WIDTH), f32)],
        compiler_params=_params("arbitrary"),
        name="proj_rg",
    )(x_p, x_s, g, w)


def _mlstm_masks(t, seg):
    r = lax.broadcasted_iota(jnp.int32, (t, t), 0)
    c = lax.broadcasted_iota(jnp.int32, (t, t), 1)
    if seg == t:
        same = None
        lower, upper = r >= c, r <= c
    else:
        same = _idiv(r, seg) == _idiv(c, seg)
        lower, upper = same & (r >= c), same & (r <= c)
    return r, c, same, lower, upper


def _wide(col, n):
    return jnp.concatenate([col] * n, axis=1)


def _row_sum(x):
    acc = x[:, 0:LANES]
    for l in range(LANES, x.shape[1], LANES):
        acc = acc + x[:, l:l + LANES]
    return jnp.broadcast_to(jnp.sum(acc, axis=1, keepdims=True), acc.shape)


def _mlstm_cols(main_ref, h):
    q = main_ref[:, h * A_DK:(h + 1) * A_DK]
    k = main_ref[:, A_QK + h * A_DK:A_QK + (h + 1) * A_DK] * (A_DK ** -0.5)
    v = main_ref[:, 2 * A_QK + h * A_DV:2 * A_QK + (h + 1) * A_DV]
    o_pre = main_ref[:, 2 * A_QK + (A_HEADS + h) * A_DV:2 * A_QK + (A_HEADS + h + 1) * A_DV]
    return q, k, v, o_pre


def _mlstm_prompt_chunk(main_ref, gc_ref, gr_ref, gh_ref, hs_ref, c_ref, n_ref, mt_ref, c_s, n_s, m_s):
    t = MLSTM_CHUNK
    j = pl.program_id(0)

    @pl.when(j == 0)
    def _():
        c_s[...] = jnp.zeros_like(c_s)
        n_s[...] = jnp.zeros_like(n_s)
        m_s[...] = jnp.zeros_like(m_s)

    _, _, _, lower, upper = _mlstm_masks(t, t)
    gc = gc_ref[...]
    gr = gr_ref[...]
    btr_all = _dot_mask(gr, upper)
    lane = lax.broadcasted_iota(jnp.int32, (t, LANES), 1)
    mt_all = jnp.zeros((t, LANES), f32)

    def rep(col):
        return jnp.broadcast_to(col, (t, LANES))

    wide, row_sum = _wide, _row_sum

    heads = range(A_HEADS)
    stack = lambda xs: jnp.concatenate(list(xs), axis=0)
    head = lambda x, h: x[h * t:(h + 1) * t]
    cols = [_mlstm_cols(main_ref, h) for h in heads]
    qs, ks, vs = [c[0] for c in cols], [c[1] for c in cols], [c[2] for c in cols]
    qbs, kbs, vbs = [q.astype(bf16) for q in qs], [k.astype(bf16) for k in ks], [v.astype(bf16) for v in vs]
    cts = [c_s[h] for h in heads]
    ns = [n_s[h:h + 1, :] for h in heads]
    btrs = [btr_all[A_HEADS + h:A_HEADS + h + 1, :] for h in heads]
    itrs = [gr[h:h + 1, :] for h in heads]

    lf_rep = jnp.concatenate([rep(gc[:, A_HEADS + h:A_HEADS + h + 1]) for h in heads], axis=1)
    btc_all = _mask_dot(lower, lf_rep)
    btc = stack(btc_all[:, h * LANES:(h + 1) * LANES] for h in heads)
    itc = stack(rep(gc[:, h:h + 1]) for h in heads)
    m_in = stack(rep(m_s[0:1, h:h + 1]) for h in heads)
    dmat = stack(jnp.where(lower, wide(head(btc, h), t // LANES) + (itrs[h] - btrs[h]), -jnp.inf) for h in heads)
    inter = btc + m_in
    m_t = jnp.maximum(jnp.broadcast_to(jnp.max(dmat, axis=1, keepdims=True), inter.shape), inter)
    w = jnp.exp(dmat - wide(m_t, t // LANES))
    w_inter = jnp.exp(inter - m_t)
    s = stack(_dot_nt(qbs[h], kbs[h]) for h in heads) * w
    sb = s.astype(bf16)
    num = (stack(_dot(head(sb, h), vbs[h]) for h in heads)
           + wide(w_inter, A_DV // LANES) * stack(_dot(qbs[h], cts[h].astype(bf16)) for h in heads))
    den = row_sum(s) + w_inter * row_sum(stack(qs[h] * ns[h] for h in heads))
    inv = 1.0 / jnp.maximum(jnp.abs(den), jnp.exp(-m_t))
    norm = inv * lax.rsqrt(inv * inv * (row_sum(num * num) * (1.0 / A_DV)) + NORM_EPS)
    out = _sigmoid(stack(c[3] for c in cols)) * (num * wide(norm, A_DV // LANES))
    for h in heads:
        hs_ref[:, h * A_DV:(h + 1) * A_DV] = (head(out, h) * gh_ref[h:h + 1, :]).astype(bf16)

    m_news = [head(m_t, h)[t - 1:t, 0:1] for h in heads]
    last = lambda x: stack(rep(head(x, h)[t - 1:t, 0:1]) for h in heads)
    decay_c = jnp.exp(last(btc) - btc + itc - last(m_t))
    for h in heads:
        scale = jnp.exp(head(inter, h)[t - 1:t, 0:1] - m_news[h])
        decay_r = jnp.exp(btrs[h][:, t - 1:t] - btrs[h] + itrs[h] - m_news[h])
        c_s[h] = scale * cts[h] + _dot((ks[h].T * decay_r).astype(bf16), vbs[h])
        n_s[h:h + 1, :] = scale * ns[h] + jnp.sum(head(decay_c, h) * ks[h], axis=0, keepdims=True)
        mt_all = jnp.where(lane == h, head(m_t, h), mt_all)
    mt_ref[...] = mt_all
    m_s[0:1, :] = mt_all[t - 1:t, :]

    @pl.when(j == pl.num_programs(0) - 1)
    def _():
        for h in range(A_HEADS):
            c_ref[h] = c_s[h].T
        n_ref[...] = n_s[...]


def _mlstm_prompt_kernel(main_ref, gc_ref, *rest):
    gr_refs, (gh_ref, hs_ref, c_ref, n_ref, mt_ref, c_s, n_s, m_s) = rest[:BATCH], rest[BATCH:]
    for b in range(BATCH):
        _mlstm_prompt_chunk(main_ref.at[b], gc_ref.at[b], gr_refs[b], gh_ref, hs_ref.at[b], c_ref.at[b],
                            n_ref.at[b], mt_ref.at[b], c_s.at[b], n_s.at[b], m_s.at[b])


def _mlstm_prompt(main, gates, gates_t, g_head):
    t = MLSTM_CHUNK
    nc = SEQ // t
    return pl.pallas_call(
        _mlstm_prompt_kernel,
        grid=(nc,),
        in_specs=[pl.BlockSpec((BATCH, t, A_MAIN), lambda j: (0, j, 0)),
                  pl.BlockSpec((BATCH, t, LANES), lambda j: (0, j, 0))]
                 + [pl.BlockSpec((SUBLANES, t), functools.partial(lambda j, b: (0, b * nc + j), b=b))
                    for b in range(BATCH)]
                 + [_const_spec((A_HEADS, A_DV))],
        out_specs=[pl.BlockSpec((BATCH, t, A_HEADS * A_DV), lambda j: (0, j, 0)),
                   _const_spec((BATCH, A_HEADS, A_DV, A_DK)),
                   _const_spec((BATCH, A_HEADS, A_DK)),
                   pl.BlockSpec((BATCH, t, LANES), lambda j: (0, j, 0))],
        out_shape=[jax.ShapeDtypeStruct((BATCH, SEQ, A_HEADS * A_DV), bf16),
                   jax.ShapeDtypeStruct((BATCH, A_HEADS, A_DV, A_DK), f32),
                   jax.ShapeDtypeStruct((BATCH, A_HEADS, A_DK), f32),
                   jax.ShapeDtypeStruct((BATCH, SEQ, LANES), f32)],
        scratch_shapes=[pltpu.VMEM((BATCH, A_HEADS, A_DK, A_DV), f32),
                        pltpu.VMEM((BATCH, A_HEADS, A_DK), f32),
                        pltpu.VMEM((BATCH, SUBLANES, LANES), f32)],
        compiler_params=_params("arbitrary"),
        name="mlstm_prompt",
    )(main, gates, *([gates_t] * BATCH), g_head)


def _mlstm_sample_compute(main_ref, gc_ref, gr_ref, m0_ref, c0_ref, n0_ref, gh_ref,
                          hs_ref, n_ref, mt_ref, vdt_ref, kb_ref, sc_ref, acc_s, nt_s):
    t = MLSTM_ST
    seg = DEC_SEQ
    _, c, same, lower, upper = _mlstm_masks(t, seg)
    seg_last = same & (_imod(c, seg) == seg - 1)
    gc = gc_ref[...]
    gr = gr_ref[...]
    btr_all = _dot_mask(gr, upper)
    lane = lax.broadcasted_iota(jnp.int32, (t, LANES), 1)
    tok_seg = _idiv(lax.broadcasted_iota(jnp.int32, (A_DV, t), 1), seg)

    heads = range(A_HEADS)
    rep = lambda col: jnp.broadcast_to(col, (t, LANES))
    stack = lambda xs: jnp.concatenate(list(xs), axis=0)
    lanes = lambda xs: jnp.concatenate(list(xs), axis=1)
    head = lambda x, h: x[h * t:(h + 1) * t]
    cols = [_mlstm_cols(main_ref, h) for h in heads]
    qs, ks, vs = [c_[0] for c_ in cols], [c_[1] for c_ in cols], [c_[2] for c_ in cols]
    qbs, kbs, vbs = [q.astype(bf16) for q in qs], [k.astype(bf16) for k in ks], [v.astype(bf16) for v in vs]

    btc_all = _mask_dot(lower, lanes(rep(gc[:, A_HEADS + h:A_HEADS + h + 1]) for h in heads))
    btc = stack(btc_all[:, h * LANES:(h + 1) * LANES] for h in heads)
    itc = stack(rep(gc[:, h:h + 1]) for h in heads)
    m_in = stack(rep(m0_ref[:, h:h + 1]) for h in heads)
    dmat = stack(jnp.where(lower, head(btc, h) + (gr[h:h + 1, :] - btr_all[A_HEADS + h:A_HEADS + h + 1, :]), -jnp.inf)
                 for h in heads)
    inter = btc + m_in
    m_t = jnp.maximum(jnp.broadcast_to(jnp.max(dmat, axis=1, keepdims=True), inter.shape), inter)
    w = jnp.exp(dmat - m_t)
    w_inter = jnp.exp(inter - m_t)
    s = stack(_dot_nt(qbs[h], kbs[h]) for h in heads) * w
    sb = s.astype(bf16)
    num = stack(_dot(head(sb, h), vbs[h]) for h in heads)

    for h in heads:
        def inter_body(b, carry, h=h):
            rows = pl.ds(pl.multiple_of(h * t + b * seg, seg), seg)
            r = _dot_nt(c0_ref[b, h].astype(bf16), qbs[h])
            acc_s[h] = jnp.where(tok_seg == b, r, acc_s[h])
            nt_s[rows, :] = jnp.broadcast_to(n0_ref[b, h:h + 1, :], (seg, A_DK))
            return carry

        acc_s[h] = jnp.zeros((A_DV, t), f32)
        lax.fori_loop(0, MLSTM_SB, inter_body, 0, unroll=MLSTM_UNROLL)
    n_tok = nt_s[...]
    num = num + _wide(w_inter, A_DV // LANES) * stack(acc_s[h].T for h in heads)
    den = _row_sum(s) + w_inter * _row_sum(stack(qs) * n_tok)
    inv = 1.0 / jnp.maximum(jnp.abs(den), jnp.exp(-m_t))
    norm = inv * lax.rsqrt(inv * inv * (_row_sum(num * num) * (1.0 / A_DV)) + NORM_EPS)
    out = _sigmoid(stack(c_[3] for c_ in cols)) * (num * _wide(norm, A_DV // LANES))
    for h in heads:
        hs_ref[:, h * A_DV:(h + 1) * A_DV] = (head(out, h) * gh_ref[h:h + 1, :]).astype(bf16)

    last = _mask_dot(seg_last, lanes([head(x, h) for x in (m_t, btc, inter) for h in heads]))
    pick = lambda i: stack(last[:, (i * A_HEADS + h) * LANES:(i * A_HEADS + h + 1) * LANES] for h in heads)
    m_new, bt_last, inter_last = pick(0), pick(1), pick(2)
    decay = jnp.exp(bt_last - btc + itc - m_new)
    scale = jnp.exp(inter_last - m_new)
    seg_sum = _mask_dot(same, lanes(head(decay, h) * ks[h] for h in heads))
    n_new = scale * n_tok + stack(seg_sum[:, h * A_DK:(h + 1) * A_DK] for h in heads)
    mt_all = jnp.zeros((t, LANES), f32)
    sc_all = jnp.zeros((t, LANES), f32)
    for h in heads:
        vdt_ref[h] = (_wide(head(decay, h), A_DV // LANES) * vs[h]).T.astype(bf16)
        kb_ref[:, h * A_DK:(h + 1) * A_DK] = kbs[h]
        sc_all = jnp.where(lane == h, head(scale, h), sc_all)
        mt_all = jnp.where(lane == h, head(m_t, h), mt_all)
    for b in range(MLSTM_SB):
        n_ref[b] = stack(head(n_new, h)[b * seg:b * seg + 1, :] for h in heads)
    mt_ref[...] = mt_all
    sc_ref[...] = sc_all


def _mlstm_sample_update(c_in_ref, c_out_ref, vdt_ref, kb_ref, sc_ref):
    seg = DEC_SEQ
    lane_seg = _idiv(lax.broadcasted_iota(jnp.int32, (1, MLSTM_ST), 1), seg)
    for h in range(A_HEADS):
        def upd_body(b, carry, h=h):
            rows = pl.ds(pl.multiple_of(b * seg, seg), seg)
            scale = sc_ref[rows, :][0:1, h:h + 1]
            onehot = jnp.where(lane_seg == b, 1.0, 0.0).astype(bf16)
            upd = _dot(vdt_ref[h] * onehot, kb_ref[:, h * A_DK:(h + 1) * A_DK])
            c_out_ref[b, h] = scale * c_in_ref[b, h] + upd
            return carry

        lax.fori_loop(0, MLSTM_SB, upd_body, 0, unroll=MLSTM_UNROLL)


def _mlstm_sample_first_kernel(main_ref, gc_ref, gr_ref, m0_ref, c0_ref, n0_ref, gh_ref,
                               hs_ref, n_ref, mt_ref, vdt_ref, kb_ref, sc_ref, acc_s, nt_s):
    _mlstm_sample_compute(main_ref, gc_ref, gr_ref, m0_ref, c0_ref, n0_ref, gh_ref,
                          hs_ref, n_ref, mt_ref, vdt_ref, kb_ref, sc_ref, acc_s, nt_s)


def _mlstm_sample_second_kernel(main_ref, gc_ref, gr_ref, m0_ref, c0_ref, n0_ref, gh_ref, vdt0_ref, kb0_ref, sc0_ref,
                                hs_ref, n_ref, mt_ref, c_ref, vdt_s, kb_s, sc_s, acc_s, nt_s):
    layer = pl.program_id(0)

    @pl.when(layer == 0)
    def _():
        _mlstm_sample_update(c0_ref, c_ref, vdt0_ref, kb0_ref, sc0_ref)

    @pl.when(layer == 1)
    def _():
        _mlstm_sample_compute(main_ref, gc_ref, gr_ref, m0_ref, c0_ref, n0_ref, gh_ref,
                              hs_ref, n_ref, mt_ref, vdt_s, kb_s, sc_s, acc_s, nt_s)
        _mlstm_sample_update(c0_ref, c_ref, vdt_s, kb_s, sc_s)


def _mlstm_sample_scratch():
    t = MLSTM_ST
    return [pltpu.VMEM((A_HEADS, A_DV, t), f32), pltpu.VMEM((A_HEADS * t, A_DK), f32)]


def _mlstm_sample_first(main, gates, gates_t, m0_tok, c_all, n0, g_head):
    t = MLSTM_ST
    return pl.pallas_call(
        _mlstm_sample_first_kernel,
        grid=(DEC_BATCH // MLSTM_SB,),
        in_specs=[pl.BlockSpec((t, A_MAIN), lambda i: (i, 0)),
                  pl.BlockSpec((t, LANES), lambda i: (i, 0)),
                  pl.BlockSpec((SUBLANES, t), lambda i: (0, i)),
                  pl.BlockSpec((t, LANES), lambda i: (i, 0)),
                  pl.BlockSpec((None, MLSTM_SB, A_HEADS, A_DV, A_DK), lambda i: (0, i, 0, 0, 0)),
                  pl.BlockSpec((MLSTM_SB, A_HEADS, A_DK), lambda i: (i, 0, 0)),
                  _const_spec((A_HEADS, A_DV))],
        out_specs=[pl.BlockSpec((t, A_HEADS * A_DV), lambda i: (i, 0)),
                   pl.BlockSpec((MLSTM_SB, A_HEADS, A_DK), lambda i: (i, 0, 0)),
                   pl.BlockSpec((t, LANES), lambda i: (i, 0)),
                   pl.BlockSpec((A_HEADS, A_DV, t), lambda i: (0, 0, i)),
                   pl.BlockSpec((t, A_QK), lambda i: (i, 0)),
                   pl.BlockSpec((t, LANES), lambda i: (i, 0))],
        out_shape=[jax.ShapeDtypeStruct((N_S, A_HEADS * A_DV), bf16),
                   jax.ShapeDtypeStruct((DEC_BATCH, A_HEADS, A_DK), f32),
                   jax.ShapeDtypeStruct((N_S, LANES), f32),
                   jax.ShapeDtypeStruct((A_HEADS, A_DV, N_S), bf16),
                   jax.ShapeDtypeStruct((N_S, A_QK), bf16),
                   jax.ShapeDtypeStruct((N_S, LANES), f32)],
        scratch_shapes=_mlstm_sample_scratch(),
        compiler_params=_params("parallel"),
        name="mlstm_sample_first",
    )(main, gates, gates_t, m0_tok, c_all, n0, g_head)


def _mlstm_sample_second(main, gates, gates_t, m0_tok, c_all, n0, g_head, vdt0, kb0, sc0):
    t = MLSTM_ST
    own = lambda l, i: i * l
    first = lambda l, i: i * (1 - l)
    return pl.pallas_call(
        _mlstm_sample_second_kernel,
        grid=(N_A, DEC_BATCH // MLSTM_SB),
        in_specs=[pl.BlockSpec((t, A_MAIN), lambda l, i: (own(l, i), 0)),
                  pl.BlockSpec((t, LANES), lambda l, i: (own(l, i), 0)),
                  pl.BlockSpec((SUBLANES, t), lambda l, i: (0, own(l, i))),
                  pl.BlockSpec((t, LANES), lambda l, i: (own(l, i), 0)),
                  pl.BlockSpec((None, MLSTM_SB, A_HEADS, A_DV, A_DK), lambda l, i: (l, i, 0, 0, 0)),
                  pl.BlockSpec((MLSTM_SB, A_HEADS, A_DK), lambda l, i: (own(l, i), 0, 0)),
                  _const_spec((A_HEADS, A_DV)),
                  pl.BlockSpec((A_HEADS, A_DV, t), lambda l, i: (0, 0, first(l, i))),
                  pl.BlockSpec((t, A_QK), lambda l, i: (first(l, i), 0)),
                  pl.BlockSpec((t, LANES), lambda l, i: (first(l, i), 0))],
        out_specs=[pl.BlockSpec((t, A_HEADS * A_DV), lambda l, i: (own(l, i), 0)),
                   pl.BlockSpec((MLSTM_SB, A_HEADS, A_DK), lambda l, i: (own(l, i), 0, 0)),
                   pl.BlockSpec((t, LANES), lambda l, i: (own(l, i), 0)),
                   pl.BlockSpec((None, MLSTM_SB, A_HEADS, A_DV, A_DK), lambda l, i: (l, i, 0, 0, 0))],
        out_shape=[jax.ShapeDtypeStruct((N_S, A_HEADS * A_DV), bf16),
                   jax.ShapeDtypeStruct((DEC_BATCH, A_HEADS, A_DK), f32),
                   jax.ShapeDtypeStruct((N_S, LANES), f32),
                   jax.ShapeDtypeStruct((N_A, DEC_BATCH, A_HEADS, A_DV, A_DK), f32)],
        scratch_shapes=[pltpu.VMEM((A_HEADS, A_DV, t), bf16),
                        pltpu.VMEM((t, A_QK), bf16),
                        pltpu.VMEM((t, LANES), f32)] + _mlstm_sample_scratch(),
        compiler_params=_params("arbitrary", "arbitrary"),
        name="mlstm_sample_second",
    )(main, gates, gates_t, m0_tok, c_all, n0, g_head, vdt0, kb0, sc0)


def _swa_softmax_pv(s, sink, vb):
    n = s.shape[1] // LANES
    m = jnp.maximum(jnp.broadcast_to(jnp.max(s, axis=1, keepdims=True), sink.shape), sink)
    p = jnp.exp(s - _wide(m, n))
    den = _row_sum(p) + jnp.exp(sink - m)
    return _dot((p * _wide(1.0 / den, n)).astype(bf16), vb)


def _swa_window_bias(has_prev):
    w = WINDOW
    t = lax.broadcasted_iota(jnp.int32, (w, 2 * w), 0)
    jj = lax.broadcasted_iota(jnp.int32, (w, 2 * w), 1)
    bias = jnp.where((jj > t) & (jj <= t + w) & (has_prev | (jj >= w)), 0.0, -jnp.inf)
    return jnp.concatenate([bias] * B_GROUP, axis=0)


def _swa_window_block(q_groups, kb, vb, sink_ref, bias):
    w = WINDOW
    lane_head = _idiv(lax.broadcasted_iota(jnp.int32, (1, B_KV), 1), B_HD)
    outs = [jnp.zeros((w, B_KV), f32) for _ in range(B_GROUP)]
    for h in range(B_KV_HEADS):
        hm = lane_head == h
        qscale = jnp.where(hm, B_HD ** -0.5, 0.0)
        qh = jnp.concatenate([q * qscale for q in q_groups], axis=0).astype(bf16)
        sink = jnp.concatenate([jnp.full((w, LANES), sink_ref[h * B_GROUP + g], f32) for g in range(B_GROUP)], axis=0)
        r = _swa_softmax_pv(_dot_nt(qh, kb) + bias, sink, vb)
        for g in range(B_GROUP):
            outs[g] = jnp.where(hm, r[g * w:(g + 1) * w, :], outs[g])
    return outs


def _swa_prompt_kernel(sink_ref, q_ref, ko_ref, vo_ref, kp_ref, vp_ref, a_ref):
    w = WINDOW
    bias_first = _swa_window_bias(pl.program_id(1) > 0)
    bias_rest = _swa_window_bias(True)
    for nb in range(SWA_TILE // w):
        r0 = nb * w
        if nb == 0:
            k_prev, v_prev, bias = kp_ref[...], vp_ref[...], bias_first
        else:
            k_prev, v_prev, bias = ko_ref[r0 - w:r0, :], vo_ref[r0 - w:r0, :], bias_rest
        kb = jnp.concatenate([k_prev, ko_ref[r0:r0 + w, :]], axis=0).astype(bf16)
        vb = jnp.concatenate([v_prev, vo_ref[r0:r0 + w, :]], axis=0).astype(bf16)
        outs = _swa_window_block([q_ref[r0:r0 + w, g * B_KV:(g + 1) * B_KV] for g in range(B_GROUP)],
                                 kb, vb, sink_ref, bias)
        for g in range(B_GROUP):
            a_ref[r0:r0 + w, g * B_KV:(g + 1) * B_KV] = outs[g].astype(bf16)


def _swa_prompt(proj, sinks):
    nt = SEQ // SWA_TILE
    w = WINDOW
    per = SWA_TILE // w
    kcol = B_Q // B_KV
    vcol = kcol + 1
    tile = lambda b, n: b * nt + n
    prev = lambda b, n: jnp.maximum(tile(b, n) * per - 1, 0)
    return pl.pallas_call(
        _swa_prompt_kernel,
        grid=(BATCH, nt),
        in_specs=[pl.BlockSpec(memory_space=pltpu.SMEM),
                  pl.BlockSpec((SWA_TILE, B_Q), lambda b, n: (tile(b, n), 0)),
                  pl.BlockSpec((SWA_TILE, B_KV), lambda b, n: (tile(b, n), kcol)),
                  pl.BlockSpec((SWA_TILE, B_KV), lambda b, n: (tile(b, n), vcol)),
                  pl.BlockSpec((w, B_KV), lambda b, n: (prev(b, n), kcol)),
                  pl.BlockSpec((w, B_KV), lambda b, n: (prev(b, n), vcol))],
        out_specs=pl.BlockSpec((SWA_TILE, B_Q), lambda b, n: (tile(b, n), 0)),
        out_shape=jax.ShapeDtypeStruct((N_P, B_Q), bf16),
        compiler_params=_params("parallel", "arbitrary"),
        name="swa_prompt",
    )(sinks, proj, proj, proj, proj, proj)


def _swa_sample_kernel(sink_ref, q_ref, kn_ref, vn_ref, kc_ref, vc_ref, o_ref, ko_ref, vo_ref):
    s_len = DEC_SEQ
    buf = WINDOW
    rows = B_HEADS * s_len
    keys = 2 * buf
    ri = lax.broadcasted_iota(jnp.int32, (rows, keys), 0)
    jj = lax.broadcasted_iota(jnp.int32, (rows, keys), 1)
    t = _imod(ri, s_len)
    bias = jnp.where(((jj < buf) & (jj > t)) | ((jj >= buf) & (jj - buf <= t)), 0.0, -jnp.inf)
    lane_head = _idiv(lax.broadcasted_iota(jnp.int32, (1, B_KV), 1), B_HD)
    qscale = [jnp.where(lane_head == h, B_HD ** -0.5, 0.0) for h in range(B_KV_HEADS)]
    sink = jnp.concatenate([jnp.full((s_len, LANES), sink_ref[i], f32) for i in range(B_HEADS)], axis=0)
    pad = jnp.zeros((keys - buf - s_len, B_KV), f32)

    def body(e, carry):
        rs = pl.ds(pl.multiple_of(e * s_len, s_len), s_len)
        kn = kn_ref[rs, :]
        vn = vn_ref[rs, :]
        kc = kc_ref[e]
        vc = vc_ref[e]
        kpad = jnp.concatenate([kc, kn, pad], axis=0).astype(bf16)
        vpad = jnp.concatenate([vc, vn, pad], axis=0).astype(bf16)
        ko_ref[e, 0:buf - s_len, :] = kc[s_len:, :]
        ko_ref[e, buf - s_len:, :] = kn
        vo_ref[e, 0:buf - s_len, :] = vc[s_len:, :]
        vo_ref[e, buf - s_len:, :] = vn
        qe = q_ref[rs, :]
        qbig = jnp.concatenate([qe[:, g * B_KV:(g + 1) * B_KV] * qscale[h]
                                for h in range(B_KV_HEADS) for g in range(B_GROUP)], axis=0).astype(bf16)
        r = _swa_softmax_pv(_dot_nt(qbig, kpad) + bias, sink, vpad)
        for g in range(B_GROUP):
            og = jnp.zeros((s_len, B_KV), f32)
            for h in range(B_KV_HEADS):
                blk = (h * B_GROUP + g) * s_len
                og = jnp.where(lane_head == h, r[blk:blk + s_len, :], og)
            o_ref[rs, g * B_KV:(g + 1) * B_KV] = og.astype(bf16)
        return carry

    lax.fori_loop(0, SWA_SB, body, 0, unroll=True)


def _swa_sample(proj, sinks, k_cache, v_cache):
    t = SWA_SB * DEC_SEQ
    kcol = B_Q // B_KV
    return pl.pallas_call(
        _swa_sample_kernel,
        grid=(DEC_BATCH // SWA_SB,),
        in_specs=[pl.BlockSpec(memory_space=pltpu.SMEM),
                  pl.BlockSpec((t, B_Q), lambda i: (i, 0)),
                  pl.BlockSpec((t, B_KV), lambda i: (i, kcol)),
                  pl.BlockSpec((t, B_KV), lambda i: (i, kcol + 1)),
                  pl.BlockSpec((SWA_SB, WINDOW, B_KV), lambda i: (i, 0, 0)),
                  pl.BlockSpec((SWA_SB, WINDOW, B_KV), lambda i: (i, 0, 0))],
        out_specs=[pl.BlockSpec((t, B_Q), lambda i: (i, 0)),
                   pl.BlockSpec((SWA_SB, WINDOW, B_KV), lambda i: (i, 0, 0)),
                   pl.BlockSpec((SWA_SB, WINDOW, B_KV), lambda i: (i, 0, 0))],
        out_shape=[jax.ShapeDtypeStruct((N_S, B_Q), bf16),
                   jax.ShapeDtypeStruct((DEC_BATCH, WINDOW, B_KV), f32),
                   jax.ShapeDtypeStruct((DEC_BATCH, WINDOW, B_KV), f32)],
        compiler_params=_params("parallel"),
        name="swa_sample",
    )(sinks, proj, proj, proj, k_cache, v_cache)


def _rg_conv_group(x8, p8, wc_ref, bc_ref):
    row = lax.broadcasted_iota(jnp.int32, x8.shape, 0)
    u = bc_ref[...] + wc_ref[CONV_W - 1] * x8
    for d in range(1, CONV_W):
        sh = pltpu.roll(jnp.where(row >= SUBLANES - d, p8, x8), d, 0)
        u = u + wc_ref[CONV_W - 1 - d] * sh
    return u


def _rg_scan_group(a8, b8, carry):
    row = lax.broadcasted_iota(jnp.int32, a8.shape, 0)
    for d in (1, 2, 4):
        keep = row >= d
        b8 = jnp.where(keep, a8 * pltpu.roll(b8, d, 0) + b8, b8)
        a8 = jnp.where(keep, a8 * pltpu.roll(a8, d, 0), a8)
    return a8 * carry + b8


def _rg_gate_rows(ra, rx, neg_rate, u):
    z = _sigmoid(ra) * neg_rate
    a = jnp.exp(-z)
    return a, _sqrt_nonneg(jnp.tanh(z) * (a * a + 1.0)) * _sigmoid(rx) * u


def _rg_gates(u, gate, wa_ref, ba_ref, wx_ref, bx_ref, lam_ref):
    ub = u.astype(bf16)
    ra = jnp.concatenate([_dot(ub[:, n * C_BW:(n + 1) * C_BW], wa_ref[n]) for n in range(C_BLOCKS)], axis=1)
    rx = jnp.concatenate([_dot(ub[:, n * C_BW:(n + 1) * C_BW], wx_ref[n]) for n in range(C_BLOCKS)], axis=1)
    a, bterm = _rg_gate_rows(ra + ba_ref[0:1, :], rx + bx_ref[0:1, :], LRU_C * _softplus(-lam_ref[0:1, :]), u)
    return a, bterm, _gelu_tanh(gate)


def _rg_prompt_kernel(p_ref, wc_ref, bc_ref, wa_ref, ba_ref, wx_ref, bx_ref, lam_ref,
                      y_ref, h_ref, u_s, a_s, b_s, xc_s, hc_s):
    j = pl.program_id(1)
    ng = RG_T // SUBLANES
    out_rows = 2 * SUBLANES

    @pl.when(j == 0)
    def _():
        xc_s[...] = jnp.zeros_like(xc_s)
        hc_s[...] = jnp.zeros_like(hc_s)

    def group(gidx, n=SUBLANES):
        return pl.ds(pl.multiple_of(gidx * n, n), n)

    def conv_body(gidx, p8):
        x8 = p_ref[group(gidx), 0:C_WIDTH]
        u_s[group(gidx), :] = _rg_conv_group(x8, p8, wc_ref, bc_ref)
        return x8

    xc_s[...] = lax.fori_loop(0, ng, conv_body, xc_s[...], unroll=2)
    ub = u_s[...].astype(bf16)
    for n in range(C_BLOCKS):
        cols = slice(n * C_BW, (n + 1) * C_BW)
        a_s[:, cols] = _dot(ub[:, cols], wa_ref[n])
        b_s[:, cols] = _dot(ub[:, cols], wx_ref[n])
    neg_rate = LRU_C * _softplus(-lam_ref[...])

    def gate_body(gidx, carry):
        rows = group(gidx)
        a, bterm = _rg_gate_rows(a_s[rows, :] + ba_ref[...], b_s[rows, :] + bx_ref[...], neg_rate, u_s[rows, :])
        a_s[rows, :] = a
        b_s[rows, :] = bterm
        return carry

    lax.fori_loop(0, ng, gate_body, 0, unroll=4)

    def scan_body(gidx, carry):
        rows = group(gidx, out_rows)
        lo = pl.ds(pl.multiple_of(gidx * out_rows, out_rows), SUBLANES)
        hi = pl.ds(pl.multiple_of(gidx * out_rows + SUBLANES, SUBLANES), SUBLANES)
        h_lo = _rg_scan_group(a_s[lo, :], b_s[lo, :], carry)
        h_hi = _rg_scan_group(a_s[hi, :], b_s[hi, :], h_lo[SUBLANES - 1:SUBLANES, :])
        h16 = jnp.concatenate([h_lo, h_hi], axis=0)
        y_ref[rows, :] = (h16 * _gelu_tanh(p_ref[rows, C_WIDTH:])).astype(bf16)
        return h_hi[SUBLANES - 1:SUBLANES, :]

    h_last = lax.fori_loop(0, RG_T // out_rows, scan_body, hc_s[...])
    hc_s[...] = h_last

    @pl.when(j == pl.num_programs(1) - 1)
    def _():
        h_ref[0] = h_last


def _rg_weight_specs():
    rows = (SUBLANES, C_WIDTH)
    return [_const_spec((CONV_W,) + rows), _const_spec(rows),
            _const_spec((C_BLOCKS, C_BW, C_BW)), _const_spec(rows),
            _const_spec((C_BLOCKS, C_BW, C_BW)), _const_spec(rows),
            _const_spec(rows)]


def _rg_prompt(proj, weights):
    nt = SEQ // RG_T
    return pl.pallas_call(
        _rg_prompt_kernel,
        grid=(BATCH, nt),
        in_specs=[pl.BlockSpec((RG_T, 2 * C_WIDTH), lambda b, j: (b * nt + j, 0))] + _rg_weight_specs(),
        out_specs=[pl.BlockSpec((RG_T, C_WIDTH), lambda b, j: (b * nt + j, 0)),
                   pl.BlockSpec((1, 1, C_WIDTH), lambda b, j: (b, 0, 0))],
        out_shape=[jax.ShapeDtypeStruct((N_P, C_WIDTH), bf16),
                   jax.ShapeDtypeStruct((BATCH, 1, C_WIDTH), f32)],
        scratch_shapes=[pltpu.VMEM((RG_T, C_WIDTH), f32),
                        pltpu.VMEM((RG_T, C_WIDTH), f32),
                        pltpu.VMEM((RG_T, C_WIDTH), f32),
                        pltpu.VMEM((SUBLANES, C_WIDTH), f32),
                        pltpu.VMEM((1, C_WIDTH), f32)],
        compiler_params=_params("parallel", "arbitrary"),
        name="rg_prompt",
    )(proj, *weights)


def _rg_sample_kernel(p_ref, cv_ref, h0_ref, wc_ref, bc_ref, wa_ref, ba_ref, wx_ref, bx_ref, lam_ref,
                      y_ref, h_ref, u_s, a_s, b_s):
    def conv_body(gidx, carry):
        rows = pl.ds(pl.multiple_of(gidx * SUBLANES, SUBLANES), SUBLANES)
        u_s[rows, :] = _rg_conv_group(p_ref[rows, 0:C_WIDTH], cv_ref[rows, :], wc_ref, bc_ref)
        return carry

    lax.fori_loop(0, DEC_BATCH, conv_body, 0)
    a, bterm, gg = _rg_gates(u_s[...], p_ref[:, C_WIDTH:], wa_ref, ba_ref, wx_ref, bx_ref, lam_ref)
    a_s[...] = a
    b_s[...] = bterm

    def scan_body(gidx, carry):
        rows = pl.ds(pl.multiple_of(gidx * SUBLANES, SUBLANES), SUBLANES)
        h8 = _rg_scan_group(a_s[rows, :], b_s[rows, :], h0_ref[pl.ds(gidx, 1), :])
        u_s[rows, :] = h8
        h_ref[pl.ds(gidx, 1), :] = h8[SUBLANES - 1:SUBLANES, :]
        return carry

    lax.fori_loop(0, DEC_BATCH, scan_body, 0)
    y_ref[...] = (u_s[...] * gg).astype(bf16)


def _rg_sample(proj, conv_pad, h0, weights):
    return pl.pallas_call(
        _rg_sample_kernel,
        grid=(1,),
        in_specs=[_const_spec((N_S, 2 * C_WIDTH)),
                  _const_spec((N_S, C_WIDTH)),
                  _const_spec((DEC_BATCH, C_WIDTH))] + _rg_weight_specs(),
        out_specs=[_const_spec((N_S, C_WIDTH)), _const_spec((DEC_BATCH, C_WIDTH))],
        out_shape=[jax.ShapeDtypeStruct((N_S, C_WIDTH), bf16),
                   jax.ShapeDtypeStruct((DEC_BATCH, C_WIDTH), f32)],
        scratch_shapes=[pltpu.VMEM((N_S, C_WIDTH), f32),
                        pltpu.VMEM((N_S, C_WIDTH), f32),
                        pltpu.VMEM((N_S, C_WIDTH), f32)],
        compiler_params=_params("arbitrary"),
        name="rg_sample",
    )(proj, conv_pad, h0, *weights)


def _mlp_tile(x_ref, a_ref, o_ref, wo_ref, bo_ref, g_ref, wup_ref, wdn_ref, gf_ref, x1_s, xn_s, final):
    x1_s[...] = x_ref[...] + _dot(a_ref[...], wo_ref[...]) + bo_ref[...]
    xn_s[...] = _rms_bf16(x1_s[...], g_ref[...])
    for c in range(0, D_FF, FF_CHUNK):
        hmid = jnp.maximum(_dot(xn_s[...], wup_ref[:, c:c + FF_CHUNK]), 0.0)
        x1_s[...] += _dot((hmid * hmid).astype(bf16), wdn_ref[c:c + FF_CHUNK, :])
    if final:
        x1 = x1_s[...]
        y = x1 * lax.rsqrt(jnp.mean(x1 * x1, axis=-1, keepdims=True) + NORM_EPS)
        o_ref[...] = y * gf_ref[...]
    else:
        o_ref[...] = x1_s[...]


def _out_mlp_kernel(xp_ref, xs_ref, ap_ref, as_ref, wo_ref, bo_ref, g_ref, wup_ref, wdn_ref, gf_ref,
                    op_ref, os_ref, x1_s, xn_s, *, final):
    def body(x_ref, a_ref, o_ref):
        _mlp_tile(x_ref, a_ref, o_ref, wo_ref, bo_ref, g_ref, wup_ref, wdn_ref, gf_ref, x1_s, xn_s, final)

    _for_each_group(body, (xp_ref, ap_ref, op_ref), (xs_ref, as_ref, os_ref))


def _out_mlp(x_p, x_s, a_p, a_s, wo, bo, g, w_up_all, w_down_all, layer, g_final, final):
    single = pl.Buffered(1)
    wspec = lambda shape: pl.BlockSpec(shape, lambda i: (0, 0), pipeline_mode=single)
    lspec = lambda shape: pl.BlockSpec((None,) + shape, lambda i: (layer, 0, 0), pipeline_mode=single)
    return pl.pallas_call(
        functools.partial(_out_mlp_kernel, final=final),
        grid=(NP_TILES + NS_TILES,),
        in_specs=[_p_spec(D_MODEL), _s_spec(D_MODEL), _p_spec(D_MODEL), _s_spec(D_MODEL),
                  wspec((D_MODEL, D_MODEL)),
                  wspec((1, D_MODEL)),
                  wspec((1, D_MODEL)),
                  lspec((D_MODEL, D_FF)),
                  lspec((D_FF, D_MODEL)),
                  wspec((1, D_MODEL))],
        out_specs=[_p_spec(D_MODEL), _s_spec(D_MODEL)],
        out_shape=[jax.ShapeDtypeStruct((N_P, D_MODEL), f32), jax.ShapeDtypeStruct((N_S, D_MODEL), f32)],
        scratch_shapes=[pltpu.VMEM((TM, D_MODEL), f32),
                        pltpu.VMEM((TM, D_MODEL), bf16)],
        compiler_params=_params("arbitrary"),
        name="out_mlp",
    )(x_p, x_s, a_p, a_s, wo, bo, g, w_up_all, w_down_all, g_final)


def _rope_tables():
    half = ROPE_DIM // 2
    inv = np.float32(ROPE_THETA) ** (-np.arange(0, ROPE_DIM, 2, dtype=np.float32) / np.float32(ROPE_DIM))
    lane = np.arange(LANES) % B_HD
    inv_lane = np.where(lane < ROPE_DIM, inv[lane % half], np.float32(0.0)).astype(np.float32)
    pos = np.concatenate([np.arange(SEQ), PAST_LEN + np.arange(TM) % DEC_SEQ]).astype(np.float32)
    ang = pos[:, None] * inv_lane[None, :]
    tab = np.concatenate([np.cos(ang), np.sin(ang)], axis=1).astype(np.float32)
    consts = np.zeros((SUBLANES, LANES), np.float32)
    consts[0] = np.where(lane < half, -1.0, 0.0)
    consts[1] = np.where((lane >= half) & (lane < ROPE_DIM), 1.0, 0.0)
    return jnp.asarray(tab), jnp.asarray(consts)


def _q_cols_to_group_major(w):
    lead = w.shape[:-1]
    return jnp.swapaxes(w.reshape(lead + (B_KV_HEADS, B_GROUP, B_HD)), -3, -2).reshape(lead + (B_Q,))


def _last_rows(arr, n_seq, seq_len, n_rows, col0, col1):
    return jnp.stack([arr[(s + 1) * seq_len - n_rows:(s + 1) * seq_len, col0:col1] for s in range(n_seq)])


def kernel(x_prompt, x_sample, state_mlstm_c, state_mlstm_n, state_mlstm_m, cache_swa_k, cache_swa_v,
           state_rglru_h, state_rglru_conv, norm_mix, norm_mlp, norm_final, w_mlp_up, w_mlp_down,
           w_mlstm_in, b_mlstm_i, b_mlstm_f, g_mlstm_head, w_mlstm_out, w_swa_qkv, b_swa_qkv, swa_sinks,
           w_swa_out, b_swa_out, w_rg_in, w_rg_conv, b_rg_conv, w_rg_a, b_rg_a, w_rg_x, b_rg_x, rg_lambda,
           w_rg_out):
    assert N_A == 2
    x_p = x_prompt.reshape(N_P, D_MODEL)
    x_s = x_sample.reshape(N_S, D_MODEL)
    zero_bias = jnp.zeros((1, D_MODEL), f32)
    row = lambda v: v.reshape(1, -1).astype(f32)
    w_up_all = w_mlp_up.astype(bf16)
    w_down_all = w_mlp_down.astype(bf16)
    state_c = state_mlstm_c.astype(f32)
    n_gate = 2 * A_HEADS
    assert n_gate == SUBLANES
    w_mlstm_main = w_mlstm_in[:, :, :A_MAIN].astype(bf16)
    w_gate_cols = w_mlstm_in[:, :, A_MAIN:]
    w_mlstm_gate = jnp.pad(w_gate_cols, ((0, 0), (0, 0), (0, LANES - n_gate))).astype(bf16)
    outs = {k: [] for k in ("c_p", "n_p", "m_p", "n_s", "m_s", "k_p", "v_p", "k_s", "v_s",
                            "h_p", "cv_p", "h_s", "cv_s")}
    deferred = None
    c_s_all = None
    for i in range(DEPTH):
        kind, j = i % N_MIXERS, i // N_MIXERS
        g_mix = row(norm_mix[i])
        if kind == 0:
            b_gate = jnp.concatenate([b_mlstm_i[j], b_mlstm_f[j]]).astype(f32)
            main_p, main_s, gates_p, gates_s, gates_p_t, gates_s_t = _proj_mlstm(
                x_p, x_s, g_mix, w_mlstm_main, w_mlstm_gate, jnp.pad(b_gate, (0, LANES - n_gate)).reshape(1, LANES), j)
            g_head = g_mlstm_head[j].astype(f32)
            a_p, c_p, n_p, mt_p = _mlstm_prompt(main_p.reshape(BATCH, SEQ, A_MAIN),
                                                gates_p.reshape(BATCH, SEQ, LANES), gates_p_t, g_head)
            a_p = a_p.reshape(N_P, A_HEADS * A_DV)
            mt_p = mt_p.reshape(N_P, LANES)
            m0_tok = jnp.pad(jnp.repeat(state_mlstm_m[j].astype(f32), DEC_SEQ, axis=0),
                             ((0, 0), (0, LANES - A_HEADS)))
            n0 = state_mlstm_n[j].astype(f32)
            if j == 0:
                a_s, n_s, mt_s, vdt0, kb0, sc0 = _mlstm_sample_first(main_s, gates_s, gates_s_t, m0_tok, state_c, n0,
                                                                     g_head)
                deferred = (vdt0, kb0, sc0)
            else:
                a_s, n_s, mt_s, c_s_all = _mlstm_sample_second(main_s, gates_s, gates_s_t, m0_tok, state_c, n0,
                                                               g_head, *deferred)
            outs["c_p"].append(c_p); outs["n_p"].append(n_p)
            outs["m_p"].append(_last_rows(mt_p, BATCH, SEQ, 1, 0, A_HEADS).reshape(BATCH, A_HEADS))
            outs["n_s"].append(n_s)
            outs["m_s"].append(mt_s[DEC_SEQ - 1::DEC_SEQ, :A_HEADS])
            wo, bo = w_mlstm_out[j], zero_bias
        elif kind == 1:
            rope_tab, rope_consts = _rope_tables()
            w_qkv, b_qkv = w_swa_qkv[j], b_swa_qkv[j]
            w_qkv = jnp.concatenate([_q_cols_to_group_major(w_qkv[:, :B_Q]), w_qkv[:, B_Q:]], axis=1)
            b_qkv = jnp.concatenate([_q_cols_to_group_major(b_qkv[:B_Q]), b_qkv[B_Q:]])
            proj_p, proj_s = _proj_swa(x_p, x_s, g_mix, w_qkv.astype(bf16), row(b_qkv), rope_tab, rope_consts)
            sinks = swa_sinks[j].astype(f32)
            a_p = _swa_prompt(proj_p, sinks)
            buf = cache_swa_k.shape[2]
            a_s, k_s, v_s = _swa_sample(proj_s, sinks,
                                        cache_swa_k[j].astype(f32).reshape(DEC_BATCH, buf, B_KV),
                                        cache_swa_v[j].astype(f32).reshape(DEC_BATCH, buf, B_KV))
            outs["k_p"].append(_last_rows(proj_p, BATCH, SEQ, WINDOW, B_Q, B_Q + B_KV)
                               .reshape(BATCH, WINDOW, B_KV_HEADS, B_HD))
            outs["v_p"].append(_last_rows(proj_p, BATCH, SEQ, WINDOW, B_Q + B_KV, B_IN)
                               .reshape(BATCH, WINDOW, B_KV_HEADS, B_HD))
            outs["k_s"].append(k_s.reshape(DEC_BATCH, buf, B_KV_HEADS, B_HD))
            outs["v_s"].append(v_s.reshape(DEC_BATCH, buf, B_KV_HEADS, B_HD))
            wo = jnp.swapaxes(w_swa_out[j].reshape(B_KV_HEADS, B_GROUP, B_HD, D_MODEL), 0, 1).reshape(B_Q, D_MODEL)
            bo = row(b_swa_out[j])
        else:
            proj_p, proj_s = _proj_rg(x_p, x_s, g_mix, w_rg_in[j].astype(bf16))
            rows8 = lambda v: jnp.broadcast_to(v.astype(f32)[..., None, :], v.shape[:-1] + (SUBLANES, C_WIDTH))
            weights = (rows8(w_rg_conv[j]), rows8(b_rg_conv[j]), w_rg_a[j].astype(bf16), rows8(b_rg_a[j]),
                       w_rg_x[j].astype(bf16), rows8(b_rg_x[j]), rows8(rg_lambda[j]))
            a_p, h_p = _rg_prompt(proj_p, weights)
            conv_pad = jnp.pad(state_rglru_conv[j].astype(f32),
                               ((0, 0), (SUBLANES - (CONV_W - 1), 0), (0, 0))).reshape(N_S, C_WIDTH)
            a_s, h_s = _rg_sample(proj_s, conv_pad, state_rglru_h[j].astype(f32), weights)
            outs["h_p"].append(h_p.reshape(BATCH, C_WIDTH))
            outs["cv_p"].append(_last_rows(proj_p, BATCH, SEQ, CONV_W - 1, 0, C_WIDTH))
            outs["h_s"].append(h_s)
            outs["cv_s"].append(proj_s[:, :C_WIDTH].reshape(DEC_BATCH, DEC_SEQ, C_WIDTH)[:, DEC_SEQ - (CONV_W - 1):])
            wo, bo = w_rg_out[j], zero_bias
        x_p, x_s = _out_mlp(x_p, x_s, a_p, a_s, wo.astype(bf16), bo, row(norm_mlp[i]), w_up_all, w_down_all, i,
                            row(norm_final), final=(i == DEPTH - 1))
    st = {k: jnp.stack(v) for k, v in outs.items()}
    y_p = x_p.reshape(BATCH, SEQ, D_MODEL)
    y_s = x_s.reshape(DEC_BATCH, DEC_SEQ, D_MODEL)
    return (y_p, y_s, st["c_p"], st["n_p"], st["m_p"], c_s_all, st["n_s"], st["m_s"],
            st["k_p"], st["v_p"], st["k_s"], st["v_s"], st["h_p"], st["cv_p"], st["h_s"], st["cv_s"])
```

```python
import functools
import math

import jax
import jax.numpy as jnp
import numpy as np
from jax import lax
from jax.experimental import pallas as pl
from jax.experimental.pallas import tpu as pltpu

f32 = jnp.float32
bf16 = jnp.bfloat16

D_MODEL = 1024
BATCH = 2
SEQ = 8192
DEPTH = 4
DEC_BATCH = 128
DEC_SEQ = 8
PAST_LEN = 8192
N_MIXERS = 3
NORM_EPS = 1e-6

A_HEADS = 4
A_DK = 128
A_DV = 256
A_QK = A_HEADS * A_DK
A_MAIN = 2 * A_QK + 2 * A_HEADS * A_DV
N_A = (DEPTH + 2) // 3

B_HEADS = 16
B_KV_HEADS = 4
B_HD = 64
B_GROUP = 4
B_Q = B_HEADS * B_HD
B_KV = B_KV_HEADS * B_HD
B_IN = B_Q + 2 * B_KV
WINDOW = 128
ROPE_THETA = 500000.0
ROPE_DIM = 16

C_WIDTH = 1024
C_BLOCKS = 4
C_BW = 256
CONV_W = 4
LRU_C = 8.0
D_FF = 4096

N_P = BATCH * SEQ
N_S = DEC_BATCH * DEC_SEQ

LANES = 128
SUBLANES = 8
VMEM_LIMIT = 56 * 1024 * 1024

TM = 512
NP_TILES = N_P // TM
NS_TILES = N_S // TM
MLSTM_CHUNK = 256
MLSTM_SB = 16
MLSTM_ST = MLSTM_SB * DEC_SEQ
MLSTM_UNROLL = 8
SWA_SB = 8
SWA_TILE = 512
RG_T = 512
FF_CHUNK = 512


def _dot(a, b):
    return jnp.dot(a, b, preferred_element_type=f32)


def _dot_nt(a, b):
    return lax.dot_general(a, b, (((1,), (1,)), ((), ())), preferred_element_type=f32)


def _split3(x):
    hi = x.astype(bf16)
    r1 = x - hi.astype(f32)
    mid = r1.astype(bf16)
    lo = (r1 - mid.astype(f32)).astype(bf16)
    return hi, mid, lo


def _mask_dot(mask, x):
    m = jnp.where(mask, 1.0, 0.0).astype(bf16)
    return sum(_dot(m, part) for part in _split3(x))


def _dot_mask(x, mask):
    m = jnp.where(mask, 1.0, 0.0).astype(bf16)
    return sum(_dot(part, m) for part in _split3(x))


def _idiv(x, d):
    assert d & (d - 1) == 0
    return x >> (d.bit_length() - 1)


def _imod(x, d):
    assert d & (d - 1) == 0
    return x & (d - 1)


def _rms_bf16(x, g):
    y = x * lax.rsqrt(jnp.mean(x * x, axis=-1, keepdims=True) + NORM_EPS)
    return (y * g).astype(bf16)


def _sigmoid(x):
    return 0.5 * jnp.tanh(0.5 * x) + 0.5


def _sqrt_nonneg(v):
    return jnp.where(v > 0.0, v * lax.rsqrt(v), 0.0)


def _softplus(x):
    return jnp.maximum(x, 0.0) + jnp.log1p(jnp.exp(-jnp.abs(x)))


def _gelu_tanh(x):
    return 0.5 * x * (1.0 + jnp.tanh(math.sqrt(2.0 / math.pi) * (x + 0.044715 * (x * x * x))))


def _params(*sem):
    return pltpu.CompilerParams(dimension_semantics=sem, vmem_limit_bytes=VMEM_LIMIT)


def _const_spec(shape):
    nd = len(shape)
    return pl.BlockSpec(shape, lambda *_: (0,) * nd)


def _p_spec(width):
    return pl.BlockSpec((TM, width), lambda i: (jnp.minimum(i, NP_TILES - 1), 0))


def _s_spec(width):
    return pl.BlockSpec((TM, width), lambda i: (jnp.maximum(i - NP_TILES, 0), 0))


def _for_each_group(body, p_refs, s_refs):
    i = pl.program_id(0)

    @pl.when(i < NP_TILES)
    def _():
        body(*p_refs)

    @pl.when(i >= NP_TILES)
    def _():
        body(*s_refs)


def _proj_mlstm_kernel(xp_ref, xs_ref, g_ref, w_ref, wg_ref, bg_ref, mp_ref, ms_ref, gp_ref, gs_ref, gtp_ref, gts_ref):
    def body(x_ref, main_ref, gate_ref, gate_t_ref):
        xn = _rms_bf16(x_ref[...], g_ref[...])
        for c in range(0, A_MAIN, 512):
            main_ref[:, c:c + 512] = _dot(xn, w_ref[:, c:c + 512])
        gp = _dot(xn, wg_ref[...]) + bg_ref[...]
        lane = lax.broadcasted_iota(jnp.int32, gp.shape, 1)
        gates = jnp.where(lane >= A_HEADS, -_softplus(-gp), gp)
        gate_ref[...] = gates
        gate_t_ref[...] = gates.T[0:SUBLANES, :]

    _for_each_group(body, (xp_ref, mp_ref, gp_ref, gtp_ref), (xs_ref, ms_ref, gs_ref, gts_ref))


def _proj_mlstm(x_p, x_s, g, w_main_all, w_gate_all, b_gate, layer):
    lspec = lambda shape: pl.BlockSpec((None,) + shape, lambda i: (layer, 0, 0))
    return pl.pallas_call(
        _proj_mlstm_kernel,
        grid=(NP_TILES + NS_TILES,),
        in_specs=[_p_spec(D_MODEL), _s_spec(D_MODEL),
                  _const_spec((1, D_MODEL)),
                  lspec((D_MODEL, A_MAIN)),
                  lspec((D_MODEL, LANES)),
                  _const_spec((1, LANES))],
        out_specs=[_p_spec(A_MAIN), _s_spec(A_MAIN), _p_spec(LANES), _s_spec(LANES),
                   pl.BlockSpec((SUBLANES, TM), lambda i: (0, jnp.minimum(i, NP_TILES - 1))),
                   pl.BlockSpec((SUBLANES, TM), lambda i: (0, jnp.maximum(i - NP_TILES, 0)))],
        out_shape=[jax.ShapeDtypeStruct((N_P, A_MAIN), f32), jax.ShapeDtypeStruct((N_S, A_MAIN), f32),
                   jax.ShapeDtypeStruct((N_P, LANES), f32), jax.ShapeDtypeStruct((N_S, LANES), f32),
                   jax.ShapeDtypeStruct((SUBLANES, N_P), f32), jax.ShapeDtypeStruct((SUBLANES, N_S), f32)],
        compiler_params=_params("arbitrary"),
        name="proj_mlstm",
    )(x_p, x_s, g, w_main_all, w_gate_all, b_gate)


def _proj_swa_kernel(xp_ref, xs_ref, g_ref, w_ref, b_ref, tab_ref, rc_ref, op_ref, os_ref):
    half = ROPE_DIM // 2

    def body(x_ref, o_ref):
        xn = _rms_bf16(x_ref[...], g_ref[...])
        cos = tab_ref[:, 0:LANES]
        sin = tab_ref[:, LANES:2 * LANES]
        sin_lo = sin * rc_ref[0:1, :]
        sin_hi = sin * rc_ref[1:2, :]
        for c in range(0, B_IN, 512):
            p = _dot(xn, w_ref[:, c:c + 512]) + b_ref[:, c:c + 512]
            for l in range(0, 512, LANES):
                pl_ = p[:, l:l + LANES]
                if c + l < B_Q + B_KV:
                    pl_ = (pl_ * cos + pltpu.roll(pl_, LANES - half, 1) * sin_lo
                           + pltpu.roll(pl_, half, 1) * sin_hi)
                o_ref[:, c + l:c + l + LANES] = pl_

    _for_each_group(body, (xp_ref, op_ref), (xs_ref, os_ref))


def _proj_swa(x_p, x_s, g, w, b, rope_tab, rope_consts):
    seq_tiles = SEQ // TM
    tab_idx = lambda i: (jnp.where(i < NP_TILES, i % seq_tiles, seq_tiles), 0)
    return pl.pallas_call(
        _proj_swa_kernel,
        grid=(NP_TILES + NS_TILES,),
        in_specs=[_p_spec(D_MODEL), _s_spec(D_MODEL),
                  _const_spec((1, D_MODEL)),
                  _const_spec((D_MODEL, B_IN)),
                  _const_spec((1, B_IN)),
                  pl.BlockSpec((TM, 2 * LANES), tab_idx),
                  _const_spec((SUBLANES, LANES))],
        out_specs=[_p_spec(B_IN), _s_spec(B_IN)],
        out_shape=[jax.ShapeDtypeStruct((N_P, B_IN), f32), jax.ShapeDtypeStruct((N_S, B_IN), f32)],
        compiler_params=_params("arbitrary"),
        name="proj_swa",
    )(x_p, x_s, g, w, b, rope_tab, rope_consts)


def _proj_rg_kernel(xp_ref, xs_ref, g_ref, w_ref, op_ref, os_ref):
    def body(x_ref, o_ref):
        xn = _rms_bf16(x_ref[...], g_ref[...])
        for c in range(0, 2 * C_WIDTH, 512):
            o_ref[:, c:c + 512] = _dot(xn, w_ref[:, c:c + 512])

    _for_each_group(body, (xp_ref, op_ref), (xs_ref, os_ref))


def _proj_rg(x_p, x_s, g, w):
    return pl.pallas_call(
        _proj_rg_kernel,
        grid=(NP_TILES + NS_TILES,),
        in_specs=[_p_spec(D_MODEL), _s_spec(D_MODEL),
                  _const_spec((1, D_MODEL)),
                  _const_spec((D_MODEL, 2 * C_WIDTH))],
        out_specs=[_p_spec(2 * C_WIDTH), _s_spec(2 * C_WIDTH)],
        out_shape=[jax.ShapeDtypeStruct((N_P, 2 * C_WIDTH), f32), jax.ShapeDtypeStruct((N_S, 2 * C_WIDTH), f32)],
        compiler_params=_params("arbitrary"),
        name="proj_rg",
    )(x_p, x_s, g, w)


def _mlstm_masks(t, seg):
    r = lax.broadcasted_iota(jnp.int32, (t, t), 0)
    c = lax.broadcasted_iota(jnp.int32, (t, t), 1)
    if seg == t:
        same = None
        lower, upper = r >= c, r <= c
    else:
        same = _idiv(r, seg) == _idiv(c, seg)
        lower, upper = same & (r >= c), same & (r <= c)
    return r, c, same, lower, upper


def _wide(col, n):
    return jnp.concatenate([col] * n, axis=1)


def _row_sum(x):
    acc = x[:, 0:LANES]
    for l in range(LANES, x.shape[1], LANES):
        acc = acc + x[:, l:l + LANES]
    return jnp.broadcast_to(jnp.sum(acc, axis=1, keepdims=True), acc.shape)


def _mlstm_cols(main_ref, h):
    q = main_ref[:, h * A_DK:(h + 1) * A_DK]
    k = main_ref[:, A_QK + h * A_DK:A_QK + (h + 1) * A_DK] * (A_DK ** -0.5)
    v = main_ref[:, 2 * A_QK + h * A_DV:2 * A_QK + (h + 1) * A_DV]
    o_pre = main_ref[:, 2 * A_QK + (A_HEADS + h) * A_DV:2 * A_QK + (A_HEADS + h + 1) * A_DV]
    return q, k, v, o_pre


def _mlstm_prompt_chunk(main_ref, gc_ref, gr_ref, gh_ref, hs_ref, c_ref, n_ref, mt_ref, c_s, n_s, m_s):
    t = MLSTM_CHUNK
    j = pl.program_id(0)

    @pl.when(j == 0)
    def _():
        c_s[...] = jnp.zeros_like(c_s)
        n_s[...] = jnp.zeros_like(n_s)
        m_s[...] = jnp.zeros_like(m_s)

    _, _, _, lower, upper = _mlstm_masks(t, t)
    gc = gc_ref[...]
    gr = gr_ref[...]
    btr_all = _dot_mask(gr, upper)
    lane = lax.broadcasted_iota(jnp.int32, (t, LANES), 1)
    mt_all = jnp.zeros((t, LANES), f32)

    def rep(col):
        return jnp.broadcast_to(col, (t, LANES))

    wide, row_sum = _wide, _row_sum

    heads = range(A_HEADS)
    stack = lambda xs: jnp.concatenate(list(xs), axis=0)
    head = lambda x, h: x[h * t:(h + 1) * t]
    cols = [_mlstm_cols(main_ref, h) for h in heads]
    qs, ks, vs = [c[0] for c in cols], [c[1] for c in cols], [c[2] for c in cols]
    qbs, kbs, vbs = [q.astype(bf16) for q in qs], [k.astype(bf16) for k in ks], [v.astype(bf16) for v in vs]
    cts = [c_s[h] for h in heads]
    ns = [n_s[h:h + 1, :] for h in heads]
    btrs = [btr_all[A_HEADS + h:A_HEADS + h + 1, :] for h in heads]
    itrs = [gr[h:h + 1, :] for h in heads]

    lf_rep = jnp.concatenate([rep(gc[:, A_HEADS + h:A_HEADS + h + 1]) for h in heads], axis=1)
    btc_all = _mask_dot(lower, lf_rep)
    btc = stack(btc_all[:, h * LANES:(h + 1) * LANES] for h in heads)
    itc = stack(rep(gc[:, h:h + 1]) for h in heads)
    m_in = stack(rep(m_s[0:1, h:h + 1]) for h in heads)
    dmat = stack(jnp.where(lower, wide(head(btc, h), t // LANES) + (itrs[h] - btrs[h]), -jnp.inf) for h in heads)
    inter = btc + m_in
    m_t = jnp.maximum(jnp.broadcast_to(jnp.max(dmat, axis=1, keepdims=True), inter.shape), inter)
    w = jnp.exp(dmat - wide(m_t, t // LANES))
    w_inter = jnp.exp(inter - m_t)
    s = stack(_dot_nt(qbs[h], kbs[h]) for h in heads) * w
    sb = s.astype(bf16)
    num = (stack(_dot(head(sb, h), vbs[h]) for h in heads)
           + wide(w_inter, A_DV // LANES) * stack(_dot(qbs[h], cts[h].astype(bf16)) for h in heads))
    den = row_sum(s) + w_inter * row_sum(stack(qs[h] * ns[h] for h in heads))
    inv = 1.0 / jnp.maximum(jnp.abs(den), jnp.exp(-m_t))
    norm = inv * lax.rsqrt(inv * inv * (row_sum(num * num) * (1.0 / A_DV)) + NORM_EPS)
    out = _sigmoid(stack(c[3] for c in cols)) * (num * wide(norm, A_DV // LANES))
    for h in heads:
        hs_ref[:, h * A_DV:(h + 1) * A_DV] = (head(out, h) * gh_ref[h:h + 1, :]).astype(bf16)

    m_news = [head(m_t, h)[t - 1:t, 0:1] for h in heads]
    last = lambda x: stack(rep(head(x, h)[t - 1:t, 0:1]) for h in heads)
    decay_c = jnp.exp(last(btc) - btc + itc - last(m_t))
    for h in heads:
        scale = jnp.exp(head(inter, h)[t - 1:t, 0:1] - m_news[h])
        decay_r = jnp.exp(btrs[h][:, t - 1:t] - btrs[h] + itrs[h] - m_news[h])
        c_s[h] = scale * cts[h] + _dot((ks[h].T * decay_r).astype(bf16), vbs[h])
        n_s[h:h + 1, :] = scale * ns[h] + jnp.sum(head(decay_c, h) * ks[h], axis=0, keepdims=True)
        mt_all = jnp.where(lane == h, head(m_t, h), mt_all)
    mt_ref[...] = mt_all
    m_s[0:1, :] = mt_all[t - 1:t, :]

    @pl.when(j == pl.num_programs(0) - 1)
    def _():
        for h in range(A_HEADS):
            c_ref[h] = c_s[h].T
        n_ref[...] = n_s[...]


def _mlstm_prompt_kernel(main_ref, gc_ref, *rest):
    gr_refs, (gh_ref, hs_ref, c_ref, n_ref, mt_ref, c_s, n_s, m_s) = rest[:BATCH], rest[BATCH:]
    for b in range(BATCH):
        _mlstm_prompt_chunk(main_ref.at[b], gc_ref.at[b], gr_refs[b], gh_ref, hs_ref.at[b], c_ref.at[b],
                            n_ref.at[b], mt_ref.at[b], c_s.at[b], n_s.at[b], m_s.at[b])


def _mlstm_prompt(main, gates, gates_t, g_head):
    t = MLSTM_CHUNK
    nc = SEQ // t
    return pl.pallas_call(
        _mlstm_prompt_kernel,
        grid=(nc,),
        in_specs=[pl.BlockSpec((BATCH, t, A_MAIN), lambda j: (0, j, 0)),
                  pl.BlockSpec((BATCH, t, LANES), lambda j: (0, j, 0))]
                 + [pl.BlockSpec((SUBLANES, t), functools.partial(lambda j, b: (0, b * nc + j), b=b))
                    for b in range(BATCH)]
                 + [_const_spec((A_HEADS, A_DV))],
        out_specs=[pl.BlockSpec((BATCH, t, A_HEADS * A_DV), lambda j: (0, j, 0)),
                   _const_spec((BATCH, A_HEADS, A_DV, A_DK)),
                   _const_spec((BATCH, A_HEADS, A_DK)),
                   pl.BlockSpec((BATCH, t, LANES), lambda j: (0, j, 0))],
        out_shape=[jax.ShapeDtypeStruct((BATCH, SEQ, A_HEADS * A_DV), bf16),
                   jax.ShapeDtypeStruct((BATCH, A_HEADS, A_DV, A_DK), f32),
                   jax.ShapeDtypeStruct((BATCH, A_HEADS, A_DK), f32),
                   jax.ShapeDtypeStruct((BATCH, SEQ, LANES), f32)],
        scratch_shapes=[pltpu.VMEM((BATCH, A_HEADS, A_DK, A_DV), f32),
                        pltpu.VMEM((BATCH, A_HEADS, A_DK), f32),
                        pltpu.VMEM((BATCH, SUBLANES, LANES), f32)],
        compiler_params=_params("arbitrary"),
        name="mlstm_prompt",
    )(main, gates, *([gates_t] * BATCH), g_head)


def _mlstm_sample_compute(main_ref, gc_ref, gr_ref, m0_ref, c0_ref, n0_ref, gh_ref,
                          hs_ref, n_ref, mt_ref, vdt_ref, kb_ref, sc_ref, acc_s, nt_s):
    t = MLSTM_ST
    seg = DEC_SEQ
    _, c, same, lower, upper = _mlstm_masks(t, seg)
    seg_last = same & (_imod(c, seg) == seg - 1)
    gc = gc_ref[...]
    gr = gr_ref[...]
    btr_all = _dot_mask(gr, upper)
    lane = lax.broadcasted_iota(jnp.int32, (t, LANES), 1)
    tok_seg = _idiv(lax.broadcasted_iota(jnp.int32, (A_DV, t), 1), seg)

    heads = range(A_HEADS)
    rep = lambda col: jnp.broadcast_to(col, (t, LANES))
    stack = lambda xs: jnp.concatenate(list(xs), axis=0)
    lanes = lambda xs: jnp.concatenate(list(xs), axis=1)
    head = lambda x, h: x[h * t:(h + 1) * t]
    cols = [_mlstm_cols(main_ref, h) for h in heads]
    qs, ks, vs = [c_[0] for c_ in cols], [c_[1] for c_ in cols], [c_[2] for c_ in cols]
    qbs, kbs, vbs = [q.astype(bf16) for q in qs], [k.astype(bf16) for k in ks], [v.astype(bf16) for v in vs]

    btc_all = _mask_dot(lower, lanes(rep(gc[:, A_HEADS + h:A_HEADS + h + 1]) for h in heads))
    btc = stack(btc_all[:, h * LANES:(h + 1) * LANES] for h in heads)
    itc = stack(rep(gc[:, h:h + 1]) for h in heads)
    m_in = stack(rep(m0_ref[:, h:h + 1]) for h in heads)
    dmat = stack(jnp.where(lower, head(btc, h) + (gr[h:h + 1, :] - btr_all[A_HEADS + h:A_HEADS + h + 1, :]), -jnp.inf)
                 for h in heads)
    inter = btc + m_in
    m_t = jnp.maximum(jnp.broadcast_to(jnp.max(dmat, axis=1, keepdims=True), inter.shape), inter)
    w = jnp.exp(dmat - m_t)
    w_inter = jnp.exp(inter - m_t)
    s = stack(_dot_nt(qbs[h], kbs[h]) for h in heads) * w
    sb = s.astype(bf16)
    num = stack(_dot(head(sb, h), vbs[h]) for h in heads)

    for h in heads:
        def inter_body(b, carry, h=h):
            rows = pl.ds(pl.multiple_of(h * t + b * seg, seg), seg)
            r = _dot_nt(c0_ref[b, h].astype(bf16), qbs[h])
            acc_s[h] = jnp.where(tok_seg == b, r, acc_s[h])
            nt_s[rows, :] = jnp.broadcast_to(n0_ref[b, h:h + 1, :], (seg, A_DK))
            return carry

        acc_s[h] = jnp.zeros((A_DV, t), f32)
        lax.fori_loop(0, MLSTM_SB, inter_body, 0, unroll=MLSTM_UNROLL)
    n_tok = nt_s[...]
    num = num + _wide(w_inter, A_DV // LANES) * stack(acc_s[h].T for h in heads)
    den = _row_sum(s) + w_inter * _row_sum(stack(qs) * n_tok)
    inv = 1.0 / jnp.maximum(jnp.abs(den), jnp.exp(-m_t))
    norm = inv * lax.rsqrt(inv * inv * (_row_sum(num * num) * (1.0 / A_DV)) + NORM_EPS)
    out = _sigmoid(stack(c_[3] for c_ in cols)) * (num * _wide(norm, A_DV // LANES))
    for h in heads:
        hs_ref[:, h * A_DV:(h + 1) * A_DV] = (head(out, h) * gh_ref[h:h + 1, :]).astype(bf16)

    last = _mask_dot(seg_last, lanes([head(x, h) for x in (m_t, btc, inter) for h in heads]))
    pick = lambda i: stack(last[:, (i * A_HEADS + h) * LANES:(i * A_HEADS + h + 1) * LANES] for h in heads)
    m_new, bt_last, inter_last = pick(0), pick(1), pick(2)
    decay = jnp.exp(bt_last - btc + itc - m_new)
    scale = jnp.exp(inter_last - m_new)
    seg_sum = _mask_dot(same, lanes(head(decay, h) * ks[h] for h in heads))
    n_new = scale * n_tok + stack(seg_sum[:, h * A_DK:(h + 1) * A_DK] for h in heads)
    mt_all = jnp.zeros((t, LANES), f32)
    sc_all = jnp.zeros((t, LANES), f32)
    for h in heads:
        vdt_ref[h] = (_wide(head(decay, h), A_DV // LANES) * vs[h]).T.astype(bf16)
        kb_ref[:, h * A_DK:(h + 1) * A_DK] = kbs[h]
        sc_all = jnp.where(lane == h, head(scale, h), sc_all)
        mt_all = jnp.where(lane == h, head(m_t, h), mt_all)
    for b in range(MLSTM_SB):
        n_ref[b] = stack(head(n_new, h)[b * seg:b * seg + 1, :] for h in heads)
    mt_ref[...] = mt_all
    sc_ref[...] = sc_all


def _mlstm_sample_update(c_in_ref, c_out_ref, vdt_ref, kb_ref, sc_ref):
    seg = DEC_SEQ
    lane_seg = _idiv(lax.broadcasted_iota(jnp.int32, (1, MLSTM_ST), 1), seg)
    for h in range(A_HEADS):
        def upd_body(b, carry, h=h):
            rows = pl.ds(pl.multiple_of(b * seg, seg), seg)
            scale = sc_ref[rows, :][0:1, h:h + 1]
            onehot = jnp.where(lane_seg == b, 1.0, 0.0).astype(bf16)
            upd = _dot(vdt_ref[h] * onehot, kb_ref[:, h * A_DK:(h + 1) * A_DK])
            c_out_ref[b, h] = scale * c_in_ref[b, h] + upd
            return carry

        lax.fori_loop(0, MLSTM_SB, upd_body, 0, unroll=MLSTM_UNROLL)


def _mlstm_sample_first_kernel(main_ref, gc_ref, gr_ref, m0_ref, c0_ref, n0_ref, gh_ref,
                               hs_ref, n_ref, mt_ref, vdt_ref, kb_ref, sc_ref, acc_s, nt_s):
    _mlstm_sample_compute(main_ref, gc_ref, gr_ref, m0_ref, c0_ref, n0_ref, gh_ref,
                          hs_ref, n_ref, mt_ref, vdt_ref, kb_ref, sc_ref, acc_s, nt_s)


def _mlstm_sample_second_kernel(main_ref, gc_ref, gr_ref, m0_ref, c0_ref, n0_ref, gh_ref, vdt0_ref, kb0_ref, sc0_ref,
                                hs_ref, n_ref, mt_ref, c_ref, vdt_s, kb_s, sc_s, acc_s, nt_s):
    layer = pl.program_id(0)

    @pl.when(layer == 0)
    def _():
        _mlstm_sample_update(c0_ref, c_ref, vdt0_ref, kb0_ref, sc0_ref)

    @pl.when(layer == 1)
    def _():
        _mlstm_sample_compute(main_ref, gc_ref, gr_ref, m0_ref, c0_ref, n0_ref, gh_ref,
                              hs_ref, n_ref, mt_ref, vdt_s, kb_s, sc_s, acc_s, nt_s)
        _mlstm_sample_update(c0_ref, c_ref, vdt_s, kb_s, sc_s)


def _mlstm_sample_scratch():
    t = MLSTM_ST
    return [pltpu.VMEM((A_HEADS, A_DV, t), f32), pltpu.VMEM((A_HEADS * t, A_DK), f32)]


def _mlstm_sample_first(main, gates, gates_t, m0_tok, c_all, n0, g_head):
    t = MLSTM_ST
    return pl.pallas_call(
        _mlstm_sample_first_kernel,
        grid=(DEC_BATCH // MLSTM_SB,),
        in_specs=[pl.BlockSpec((t, A_MAIN), lambda i: (i, 0)),
                  pl.BlockSpec((t, LANES), lambda i: (i, 0)),
                  pl.BlockSpec((SUBLANES, t), lambda i: (0, i)),
                  pl.BlockSpec((t, LANES), lambda i: (i, 0)),
                  pl.BlockSpec((None, MLSTM_SB, A_HEADS, A_DV, A_DK), lambda i: (0, i, 0, 0, 0)),
                  pl.BlockSpec((MLSTM_SB, A_HEADS, A_DK), lambda i: (i, 0, 0)),
                  _const_spec((A_HEADS, A_DV))],
        out_specs=[pl.BlockSpec((t, A_HEADS * A_DV), lambda i: (i, 0)),
                   pl.BlockSpec((MLSTM_SB, A_HEADS, A_DK), lambda i: (i, 0, 0)),
                   pl.BlockSpec((t, LANES), lambda i: (i, 0)),
                   pl.BlockSpec((A_HEADS, A_DV, t), lambda i: (0, 0, i)),
                   pl.BlockSpec((t, A_QK), lambda i: (i, 0)),
                   pl.BlockSpec((t, LANES), lambda i: (i, 0))],
        out_shape=[jax.ShapeDtypeStruct((N_S, A_HEADS * A_DV), bf16),
                   jax.ShapeDtypeStruct((DEC_BATCH, A_HEADS, A_DK), f32),
                   jax.ShapeDtypeStruct((N_S, LANES), f32),
                   jax.ShapeDtypeStruct((A_HEADS, A_DV, N_S), bf16),
                   jax.ShapeDtypeStruct((N_S, A_QK), bf16),
                   jax.ShapeDtypeStruct((N_S, LANES), f32)],
        scratch_shapes=_mlstm_sample_scratch(),
        compiler_params=_params("parallel"),
        name="mlstm_sample_first",
    )(main, gates, gates_t, m0_tok, c_all, n0, g_head)


def _mlstm_sample_second(main, gates, gates_t, m0_tok, c_all, n0, g_head, vdt0, kb0, sc0):
    t = MLSTM_ST
    own = lambda l, i: i * l
    first = lambda l, i: i * (1 - l)
    return pl.pallas_call(
        _mlstm_sample_second_kernel,
        grid=(N_A, DEC_BATCH // MLSTM_SB),
        in_specs=[pl.BlockSpec((t, A_MAIN), lambda l, i: (own(l, i), 0)),
                  pl.BlockSpec((t, LANES), lambda l, i: (own(l, i), 0)),
                  pl.BlockSpec((SUBLANES, t), lambda l, i: (0, own(l, i))),
                  pl.BlockSpec((t, LANES), lambda l, i: (own(l, i), 0)),
                  pl.BlockSpec((None, MLSTM_SB, A_HEADS, A_DV, A_DK), lambda l, i: (l, i, 0, 0, 0)),
                  pl.BlockSpec((MLSTM_SB, A_HEADS, A_DK), lambda l, i: (own(l, i), 0, 0)),
                  _const_spec((A_HEADS, A_DV)),
                  pl.BlockSpec((A_HEADS, A_DV, t), lambda l, i: (0, 0, first(l, i))),
                  pl.BlockSpec((t, A_QK), lambda l, i: (first(l, i), 0)),
                  pl.BlockSpec((t, LANES), lambda l, i: (first(l, i), 0))],
        out_specs=[pl.BlockSpec((t, A_HEADS * A_DV), lambda l, i: (own(l, i), 0)),
                   pl.BlockSpec((MLSTM_SB, A_HEADS, A_DK), lambda l, i: (own(l, i), 0, 0)),
                   pl.BlockSpec((t, LANES), lambda l, i: (own(l, i), 0)),
                   pl.BlockSpec((None, MLSTM_SB, A_HEADS, A_DV, A_DK), lambda l, i: (l, i, 0, 0, 0))],
        out_shape=[jax.ShapeDtypeStruct((N_S, A_HEADS * A_DV), bf16),
                   jax.ShapeDtypeStruct((DEC_BATCH, A_HEADS, A_DK), f32),
                   jax.ShapeDtypeStruct((N_S, LANES), f32),
                   jax.ShapeDtypeStruct((N_A, DEC_BATCH, A_HEADS, A_DV, A_DK), f32)],
        scratch_shapes=[pltpu.VMEM((A_HEADS, A_DV, t), bf16),
                        pltpu.VMEM((t, A_QK), bf16),
                        pltpu.VMEM((t, LANES), f32)] + _mlstm_sample_scratch(),
        compiler_params=_params("arbitrary", "arbitrary"),
        name="mlstm_sample_second",
    )(main, gates, gates_t, m0_tok, c_all, n0, g_head, vdt0, kb0, sc0)


def _swa_softmax_pv(s, sink, vb):
    n = s.shape[1] // LANES
    m = jnp.maximum(jnp.broadcast_to(jnp.max(s, axis=1, keepdims=True), sink.shape), sink)
    p = jnp.exp(s - _wide(m, n))
    den = _row_sum(p) + jnp.exp(sink - m)
    return _dot((p * _wide(1.0 / den, n)).astype(bf16), vb)


def _swa_window_bias(has_prev):
    w = WINDOW
    t = lax.broadcasted_iota(jnp.int32, (w, 2 * w), 0)
    jj = lax.broadcasted_iota(jnp.int32, (w, 2 * w), 1)
    bias = jnp.where((jj > t) & (jj <= t + w) & (has_prev | (jj >= w)), 0.0, -jnp.inf)
    return jnp.concatenate([bias] * B_GROUP, axis=0)


def _swa_window_block(q_groups, kb, vb, sink_ref, bias):
    w = WINDOW
    lane_head = _idiv(lax.broadcasted_iota(jnp.int32, (1, B_KV), 1), B_HD)
    outs = [jnp.zeros((w, B_KV), f32) for _ in range(B_GROUP)]
    for h in range(B_KV_HEADS):
        hm = lane_head == h
        qscale = jnp.where(hm, B_HD ** -0.5, 0.0)
        qh = jnp.concatenate([q * qscale for q in q_groups], axis=0).astype(bf16)
        sink = jnp.concatenate([jnp.full((w, LANES), sink_ref[h * B_GROUP + g], f32) for g in range(B_GROUP)], axis=0)
        r = _swa_softmax_pv(_dot_nt(qh, kb) + bias, sink, vb)
        for g in range(B_GROUP):
            outs[g] = jnp.where(hm, r[g * w:(g + 1) * w, :], outs[g])
    return outs


def _swa_prompt_kernel(sink_ref, q_ref, ko_ref, vo_ref, kp_ref, vp_ref, a_ref):
    w = WINDOW
    bias_first = _swa_window_bias(pl.program_id(1) > 0)
    bias_rest = _swa_window_bias(True)
    for nb in range(SWA_TILE // w):
        r0 = nb * w
        if nb == 0:
            k_prev, v_prev, bias = kp_ref[...], vp_ref[...], bias_first
        else:
            k_prev, v_prev, bias = ko_ref[r0 - w:r0, :], vo_ref[r0 - w:r0, :], bias_rest
        kb = jnp.concatenate([k_prev, ko_ref[r0:r0 + w, :]], axis=0).astype(bf16)
        vb = jnp.concatenate([v_prev, vo_ref[r0:r0 + w, :]], axis=0).astype(bf16)
        outs = _swa_window_block([q_ref[r0:r0 + w, g * B_KV:(g + 1) * B_KV] for g in range(B_GROUP)],
                                 kb, vb, sink_ref, bias)
        for g in range(B_GROUP):
            a_ref[r0:r0 + w, g * B_KV:(g + 1) * B_KV] = outs[g].astype(bf16)


def _swa_prompt(proj, sinks):
    nt = SEQ // SWA_TILE
    w = WINDOW
    per = SWA_TILE // w
    kcol = B_Q // B_KV
    vcol = kcol + 1
    tile = lambda b, n: b * nt + n
    prev = lambda b, n: jnp.maximum(tile(b, n) * per - 1, 0)
    return pl.pallas_call(
        _swa_prompt_kernel,
        grid=(BATCH, nt),
        in_specs=[pl.BlockSpec(memory_space=pltpu.SMEM),
                  pl.BlockSpec((SWA_TILE, B_Q), lambda b, n: (tile(b, n), 0)),
                  pl.BlockSpec((SWA_TILE, B_KV), lambda b, n: (tile(b, n), kcol)),
                  pl.BlockSpec((SWA_TILE, B_KV), lambda b, n: (tile(b, n), vcol)),
                  pl.BlockSpec((w, B_KV), lambda b, n: (prev(b, n), kcol)),
                  pl.BlockSpec((w, B_KV), lambda b, n: (prev(b, n), vcol))],
        out_specs=pl.BlockSpec((SWA_TILE, B_Q), lambda b, n: (tile(b, n), 0)),
        out_shape=jax.ShapeDtypeStruct((N_P, B_Q), bf16),
        compiler_params=_params("parallel", "arbitrary"),
        name="swa_prompt",
    )(sinks, proj, proj, proj, proj, proj)


def _swa_sample_kernel(sink_ref, q_ref, kn_ref, vn_ref, kc_ref, vc_ref, o_ref, ko_ref, vo_ref):
    s_len = DEC_SEQ
    buf = WINDOW
    rows = B_HEADS * s_len
    keys = 2 * buf
    ri = lax.broadcasted_iota(jnp.int32, (rows, keys), 0)
    jj = lax.broadcasted_iota(jnp.int32, (rows, keys), 1)
    t = _imod(ri, s_len)
    bias = jnp.where(((jj < buf) & (jj > t)) | ((jj >= buf) & (jj - buf <= t)), 0.0, -jnp.inf)
    lane_head = _idiv(lax.broadcasted_iota(jnp.int32, (1, B_KV), 1), B_HD)
    qscale = [jnp.where(lane_head == h, B_HD ** -0.5, 0.0) for h in range(B_KV_HEADS)]
    sink = jnp.concatenate([jnp.full((s_len, LANES), sink_ref[i], f32) for i in range(B_HEADS)], axis=0)
    pad = jnp.zeros((keys - buf - s_len, B_KV), f32)

    def body(e, carry):
        rs = pl.ds(pl.multiple_of(e * s_len, s_len), s_len)
        kn = kn_ref[rs, :]
        vn = vn_ref[rs, :]
        kc = kc_ref[e]
        vc = vc_ref[e]
        kpad = jnp.concatenate([kc, kn, pad], axis=0).astype(bf16)
        vpad = jnp.concatenate([vc, vn, pad], axis=0).astype(bf16)
        ko_ref[e, 0:buf - s_len, :] = kc[s_len:, :]
        ko_ref[e, buf - s_len:, :] = kn
        vo_ref[e, 0:buf - s_len, :] = vc[s_len:, :]
        vo_ref[e, buf - s_len:, :] = vn
        qe = q_ref[rs, :]
        qbig = jnp.concatenate([qe[:, g * B_KV:(g + 1) * B_KV] * qscale[h]
                                for h in range(B_KV_HEADS) for g in range(B_GROUP)], axis=0).astype(bf16)
        r = _swa_softmax_pv(_dot_nt(qbig, kpad) + bias, sink, vpad)
        for g in range(B_GROUP):
            og = jnp.zeros((s_len, B_KV), f32)
            for h in range(B_KV_HEADS):
                blk = (h * B_GROUP + g) * s_len
                og = jnp.where(lane_head == h, r[blk:blk + s_len, :], og)
            o_ref[rs, g * B_KV:(g + 1) * B_KV] = og.astype(bf16)
        return carry

    lax.fori_loop(0, SWA_SB, body, 0, unroll=True)


def _swa_sample(proj, sinks, k_cache, v_cache):
    t = SWA_SB * DEC_SEQ
    kcol = B_Q // B_KV
    return pl.pallas_call(
        _swa_sample_kernel,
        grid=(DEC_BATCH // SWA_SB,),
        in_specs=[pl.BlockSpec(memory_space=pltpu.SMEM),
                  pl.BlockSpec((t, B_Q), lambda i: (i, 0)),
                  pl.BlockSpec((t, B_KV), lambda i: (i, kcol)),
                  pl.BlockSpec((t, B_KV), lambda i: (i, kcol + 1)),
                  pl.BlockSpec((SWA_SB, WINDOW, B_KV), lambda i: (i, 0, 0)),
                  pl.BlockSpec((SWA_SB, WINDOW, B_KV), lambda i: (i, 0, 0))],
        out_specs=[pl.BlockSpec((t, B_Q), lambda i: (i, 0)),
                   pl.BlockSpec((SWA_SB, WINDOW, B_KV), lambda i: (i, 0, 0)),
                   pl.BlockSpec((SWA_SB, WINDOW, B_KV), lambda i: (i, 0, 0))],
        out_shape=[jax.ShapeDtypeStruct((N_S, B_Q), bf16),
                   jax.ShapeDtypeStruct((DEC_BATCH, WINDOW, B_KV), f32),
                   jax.ShapeDtypeStruct((DEC_BATCH, WINDOW, B_KV), f32)],
        compiler_params=_params("parallel"),
        name="swa_sample",
    )(sinks, proj, proj, proj, k_cache, v_cache)


def _rg_conv_group(x8, p8, wc_ref, bc_ref):
    row = lax.broadcasted_iota(jnp.int32, x8.shape, 0)
    u = bc_ref[...] + wc_ref[CONV_W - 1] * x8
    for d in range(1, CONV_W):
        sh = pltpu.roll(jnp.where(row >= SUBLANES - d, p8, x8), d, 0)
        u = u + wc_ref[CONV_W - 1 - d] * sh
    return u


def _rg_scan_group(a8, b8, carry):
    row = lax.broadcasted_iota(jnp.int32, a8.shape, 0)
    for d in (1, 2, 4):
        keep = row >= d
        b8 = jnp.where(keep, a8 * pltpu.roll(b8, d, 0) + b8, b8)
        a8 = jnp.where(keep, a8 * pltpu.roll(a8, d, 0), a8)
    return a8 * carry + b8


def _rg_gate_rows(ra, rx, neg_rate, u):
    z = _sigmoid(ra) * neg_rate
    a = jnp.exp(-z)
    return a, _sqrt_nonneg(jnp.tanh(z) * (a * a + 1.0)) * _sigmoid(rx) * u


def _rg_gates(u, gate, wa_ref, ba_ref, wx_ref, bx_ref, lam_ref):
    ub = u.astype(bf16)
    ra = jnp.concatenate([_dot(ub[:, n * C_BW:(n + 1) * C_BW], wa_ref[n]) for n in range(C_BLOCKS)], axis=1)
    rx = jnp.concatenate([_dot(ub[:, n * C_BW:(n + 1) * C_BW], wx_ref[n]) for n in range(C_BLOCKS)], axis=1)
    a, bterm = _rg_gate_rows(ra + ba_ref[0:1, :], rx + bx_ref[0:1, :], LRU_C * _softplus(-lam_ref[0:1, :]), u)
    return a, bterm, _gelu_tanh(gate)


def _rg_prompt_kernel(p_ref, wc_ref, bc_ref, wa_ref, ba_ref, wx_ref, bx_ref, lam_ref,
                      y_ref, h_ref, u_s, a_s, b_s, xc_s, hc_s):
    j = pl.program_id(1)
    ng = RG_T // SUBLANES
    out_rows = 2 * SUBLANES

    @pl.when(j == 0)
    def _():
        xc_s[...] = jnp.zeros_like(xc_s)
        hc_s[...] = jnp.zeros_like(hc_s)

    def group(gidx, n=SUBLANES):
        return pl.ds(pl.multiple_of(gidx * n, n), n)

    def conv_body(gidx, p8):
        x8 = p_ref[group(gidx), 0:C_WIDTH]
        u_s[group(gidx), :] = _rg_conv_group(x8, p8, wc_ref, bc_ref)
        return x8

    xc_s[...] = lax.fori_loop(0, ng, conv_body, xc_s[...], unroll=2)
    ub = u_s[...].astype(bf16)
    for n in range(C_BLOCKS):
        cols = slice(n * C_BW, (n + 1) * C_BW)
        a_s[:, cols] = _dot(ub[:, cols], wa_ref[n])
        b_s[:, cols] = _dot(ub[:, cols], wx_ref[n])
    neg_rate = LRU_C * _softplus(-lam_ref[...])

    def gate_body(gidx, carry):
        rows = group(gidx)
        a, bterm = _rg_gate_rows(a_s[rows, :] + ba_ref[...], b_s[rows, :] + bx_ref[...], neg_rate, u_s[rows, :])
        a_s[rows, :] = a
        b_s[rows, :] = bterm
        return carry

    lax.fori_loop(0, ng, gate_body, 0, unroll=4)

    def scan_body(gidx, carry):
        rows = group(gidx, out_rows)
        lo = pl.ds(pl.multiple_of(gidx * out_rows, out_rows), SUBLANES)
        hi = pl.ds(pl.multiple_of(gidx * out_rows + SUBLANES, SUBLANES), SUBLANES)
        h_lo = _rg_scan_group(a_s[lo, :], b_s[lo, :], carry)
        h_hi = _rg_scan_group(a_s[hi, :], b_s[hi, :], h_lo[SUBLANES - 1:SUBLANES, :])
        h16 = jnp.concatenate([h_lo, h_hi], axis=0)
        y_ref[rows, :] = (h16 * _gelu_tanh(p_ref[rows, C_WIDTH:])).astype(bf16)
        return h_hi[SUBLANES - 1:SUBLANES, :]

    h_last = lax.fori_loop(0, RG_T // out_rows, scan_body, hc_s[...])
    hc_s[...] = h_last

    @pl.when(j == pl.num_programs(1) - 1)
    def _():
        h_ref[0] = h_last


def _rg_weight_specs():
    rows = (SUBLANES, C_WIDTH)
    return [_const_spec((CONV_W,) + rows), _const_spec(rows),
            _const_spec((C_BLOCKS, C_BW, C_BW)), _const_spec(rows),
            _const_spec((C_BLOCKS, C_BW, C_BW)), _const_spec(rows),
            _const_spec(rows)]


def _rg_prompt(proj, weights):
    nt = SEQ // RG_T
    return pl.pallas_call(
        _rg_prompt_kernel,
        grid=(BATCH, nt),
        in_specs=[pl.BlockSpec((RG_T, 2 * C_WIDTH), lambda b, j: (b * nt + j, 0))] + _rg_weight_specs(),
        out_specs=[pl.BlockSpec((RG_T, C_WIDTH), lambda b, j: (b * nt + j, 0)),
                   pl.BlockSpec((1, 1, C_WIDTH), lambda b, j: (b, 0, 0))],
        out_shape=[jax.ShapeDtypeStruct((N_P, C_WIDTH), bf16),
                   jax.ShapeDtypeStruct((BATCH, 1, C_WIDTH), f32)],
        scratch_shapes=[pltpu.VMEM((RG_T, C_WIDTH), f32),
                        pltpu.VMEM((RG_T, C_WIDTH), f32),
                        pltpu.VMEM((RG_T, C_WIDTH), f32),
                        pltpu.VMEM((SUBLANES, C_WIDTH), f32),
                        pltpu.VMEM((1, C_WIDTH), f32)],
        compiler_params=_params("parallel", "arbitrary"),
        name="rg_prompt",
    )(proj, *weights)


def _rg_sample_kernel(p_ref, cv_ref, h0_ref, wc_ref, bc_ref, wa_ref, ba_ref, wx_ref, bx_ref, lam_ref,
                      y_ref, h_ref, u_s, a_s, b_s):
    def conv_body(gidx, carry):
        rows = pl.ds(pl.multiple_of(gidx * SUBLANES, SUBLANES), SUBLANES)
        u_s[rows, :] = _rg_conv_group(p_ref[rows, 0:C_WIDTH], cv_ref[rows, :], wc_ref, bc_ref)
        return carry

    lax.fori_loop(0, DEC_BATCH, conv_body, 0)
    a, bterm, gg = _rg_gates(u_s[...], p_ref[:, C_WIDTH:], wa_ref, ba_ref, wx_ref, bx_ref, lam_ref)
    a_s[...] = a
    b_s[...] = bterm

    def scan_body(gidx, carry):
        rows = pl.ds(pl.multiple_of(gidx * SUBLANES, SUBLANES), SUBLANES)
        h8 = _rg_scan_group(a_s[rows, :], b_s[rows, :], h0_ref[pl.ds(gidx, 1), :])
        u_s[rows, :] = h8
        h_ref[pl.ds(gidx, 1), :] = h8[SUBLANES - 1:SUBLANES, :]
        return carry

    lax.fori_loop(0, DEC_BATCH, scan_body, 0)
    y_ref[...] = (u_s[...] * gg).astype(bf16)


def _rg_sample(proj, conv_pad, h0, weights):
    return pl.pallas_call(
        _rg_sample_kernel,
        grid=(1,),
        in_specs=[_const_spec((N_S, 2 * C_WIDTH)),
                  _const_spec((N_S, C_WIDTH)),
                  _const_spec((DEC_BATCH, C_WIDTH))] + _rg_weight_specs(),
        out_specs=[_const_spec((N_S, C_WIDTH)), _const_spec((DEC_BATCH, C_WIDTH))],
        out_shape=[jax.ShapeDtypeStruct((N_S, C_WIDTH), bf16),
                   jax.ShapeDtypeStruct((DEC_BATCH, C_WIDTH), f32)],
        scratch_shapes=[pltpu.VMEM((N_S, C_WIDTH), f32),
                        pltpu.VMEM((N_S, C_WIDTH), f32),
                        pltpu.VMEM((N_S, C_WIDTH), f32)],
        compiler_params=_params("arbitrary"),
        name="rg_sample",
    )(proj, conv_pad, h0, *weights)


def _mlp_tile(x_ref, a_ref, o_ref, wo_ref, bo_ref, g_ref, wup_ref, wdn_ref, gf_ref, x1_s, xn_s, final):
    x1_s[...] = x_ref[...] + _dot(a_ref[...], wo_ref[...]) + bo_ref[...]
    xn_s[...] = _rms_bf16(x1_s[...], g_ref[...])
    for c in range(0, D_FF, FF_CHUNK):
        hmid = jnp.maximum(_dot(xn_s[...], wup_ref[:, c:c + FF_CHUNK]), 0.0)
        x1_s[...] += _dot((hmid * hmid).astype(bf16), wdn_ref[c:c + FF_CHUNK, :])
    if final:
        x1 = x1_s[...]
        y = x1 * lax.rsqrt(jnp.mean(x1 * x1, axis=-1, keepdims=True) + NORM_EPS)
        o_ref[...] = y * gf_ref[...]
    else:
        o_ref[...] = x1_s[...]


def _out_mlp_kernel(xp_ref, xs_ref, ap_ref, as_ref, wo_ref, bo_ref, g_ref, wup_ref, wdn_ref, gf_ref, *rest,
                    final, cast_next):
    if cast_next:
        nup_ref, ndn_ref, op_ref, os_ref, nup_out, ndn_out, x1_s, xn_s = rest
    else:
        op_ref, os_ref, x1_s, xn_s = rest

    def body(x_ref, a_ref, o_ref):
        _mlp_tile(x_ref, a_ref, o_ref, wo_ref, bo_ref, g_ref, wup_ref, wdn_ref, gf_ref, x1_s, xn_s, final)

    i = pl.program_id(0)

    @pl.when(i < NP_TILES)
    def _():
        body(xp_ref, ap_ref, op_ref)
        if cast_next:
            nup_out[...] = nup_ref[...].astype(bf16)
            ndn_out[...] = ndn_ref[...].astype(bf16)

    @pl.when(i >= NP_TILES)
    def _():
        body(xs_ref, as_ref, os_ref)


def _out_mlp(x_p, x_s, a_p, a_s, wo, bo, g, w_up, w_down, g_final, final, next_weights=None):
    single = pl.Buffered(1)
    wspec = lambda shape: pl.BlockSpec(shape, lambda i: (0, 0), pipeline_mode=single)
    in_specs = [_p_spec(D_MODEL), _s_spec(D_MODEL), _p_spec(D_MODEL), _s_spec(D_MODEL),
                wspec((D_MODEL, D_MODEL)), wspec((1, D_MODEL)), wspec((1, D_MODEL)),
                wspec((D_MODEL, D_FF)), wspec((D_FF, D_MODEL)), wspec((1, D_MODEL))]
    out_specs = [_p_spec(D_MODEL), _s_spec(D_MODEL)]
    out_shape = [jax.ShapeDtypeStruct((N_P, D_MODEL), f32), jax.ShapeDtypeStruct((N_S, D_MODEL), f32)]
    args = [x_p, x_s, a_p, a_s, wo, bo, g, w_up, w_down, g_final]
    if next_weights is not None:
        up_all, down_all, layer = next_weights
        step = lambda i: jnp.minimum(i, NP_TILES - 1)
        for w_all, rows, width in ((up_all, D_MODEL // NP_TILES, D_FF), (down_all, D_FF // NP_TILES, D_MODEL)):
            in_specs.append(pl.BlockSpec((None, rows, width), lambda i: (layer, step(i), 0)))
            out_specs.append(pl.BlockSpec((rows, width), lambda i: (step(i), 0)))
            out_shape.append(jax.ShapeDtypeStruct((rows * NP_TILES, width), bf16))
            args.append(w_all)
    return pl.pallas_call(
        functools.partial(_out_mlp_kernel, final=final, cast_next=next_weights is not None),
        grid=(NP_TILES + NS_TILES,),
        in_specs=in_specs,
        out_specs=out_specs,
        out_shape=out_shape,
        scratch_shapes=[pltpu.VMEM((TM, D_MODEL), f32),
                        pltpu.VMEM((TM, D_MODEL), bf16)],
        compiler_params=_params("arbitrary"),
        name="out_mlp",
    )(*args)


def _rope_tables():
    half = ROPE_DIM // 2
    inv = np.float32(ROPE_THETA) ** (-np.arange(0, ROPE_DIM, 2, dtype=np.float32) / np.float32(ROPE_DIM))
    lane = np.arange(LANES) % B_HD
    inv_lane = np.where(lane < ROPE_DIM, inv[lane % half], np.float32(0.0)).astype(np.float32)
    pos = np.concatenate([np.arange(SEQ), PAST_LEN + np.arange(TM) % DEC_SEQ]).astype(np.float32)
    ang = pos[:, None] * inv_lane[None, :]
    tab = np.concatenate([np.cos(ang), np.sin(ang)], axis=1).astype(np.float32)
    consts = np.zeros((SUBLANES, LANES), np.float32)
    consts[0] = np.where(lane < half, -1.0, 0.0)
    consts[1] = np.where((lane >= half) & (lane < ROPE_DIM), 1.0, 0.0)
    return jnp.asarray(tab), jnp.asarray(consts)


def _q_cols_to_group_major(w):
    lead = w.shape[:-1]
    return jnp.swapaxes(w.reshape(lead + (B_KV_HEADS, B_GROUP, B_HD)), -3, -2).reshape(lead + (B_Q,))


def _last_rows(arr, n_seq, seq_len, n_rows, col0, col1):
    return jnp.stack([arr[(s + 1) * seq_len - n_rows:(s + 1) * seq_len, col0:col1] for s in range(n_seq)])


def kernel(x_prompt, x_sample, state_mlstm_c, state_mlstm_n, state_mlstm_m, cache_swa_k, cache_swa_v,
           state_rglru_h, state_rglru_conv, norm_mix, norm_mlp, norm_final, w_mlp_up, w_mlp_down,
           w_mlstm_in, b_mlstm_i, b_mlstm_f, g_mlstm_head, w_mlstm_out, w_swa_qkv, b_swa_qkv, swa_sinks,
           w_swa_out, b_swa_out, w_rg_in, w_rg_conv, b_rg_conv, w_rg_a, b_rg_a, w_rg_x, b_rg_x, rg_lambda,
           w_rg_out):
    assert N_A == 2
    x_p = x_prompt.reshape(N_P, D_MODEL)
    x_s = x_sample.reshape(N_S, D_MODEL)
    zero_bias = jnp.zeros((1, D_MODEL), f32)
    row = lambda v: v.reshape(1, -1).astype(f32)
    w_up = w_mlp_up[0].astype(bf16)
    w_down = w_mlp_down[0].astype(bf16)
    state_c = state_mlstm_c.astype(f32)
    n_gate = 2 * A_HEADS
    assert n_gate == SUBLANES
    w_mlstm_main = w_mlstm_in[:, :, :A_MAIN].astype(bf16)
    w_gate_cols = w_mlstm_in[:, :, A_MAIN:]
    w_mlstm_gate = jnp.pad(w_gate_cols, ((0, 0), (0, 0), (0, LANES - n_gate))).astype(bf16)
    outs = {k: [] for k in ("c_p", "n_p", "m_p", "n_s", "m_s", "k_p", "v_p", "k_s", "v_s",
                            "h_p", "cv_p", "h_s", "cv_s")}
    deferred = None
    c_s_all = None
    for i in range(DEPTH):
        kind, j = i % N_MIXERS, i // N_MIXERS
        g_mix = row(norm_mix[i])
        if kind == 0:
            b_gate = jnp.concatenate([b_mlstm_i[j], b_mlstm_f[j]]).astype(f32)
            main_p, main_s, gates_p, gates_s, gates_p_t, gates_s_t = _proj_mlstm(
                x_p, x_s, g_mix, w_mlstm_main, w_mlstm_gate, jnp.pad(b_gate, (0, LANES - n_gate)).reshape(1, LANES), j)
            g_head = g_mlstm_head[j].astype(f32)
            a_p, c_p, n_p, mt_p = _mlstm_prompt(main_p.reshape(BATCH, SEQ, A_MAIN),
                                                gates_p.reshape(BATCH, SEQ, LANES), gates_p_t, g_head)
            a_p = a_p.reshape(N_P, A_HEADS * A_DV)
            mt_p = mt_p.reshape(N_P, LANES)
            m0_tok = jnp.pad(jnp.repeat(state_mlstm_m[j].astype(f32), DEC_SEQ, axis=0),
                             ((0, 0), (0, LANES - A_HEADS)))
            n0 = state_mlstm_n[j].astype(f32)
            if j == 0:
                a_s, n_s, mt_s, vdt0, kb0, sc0 = _mlstm_sample_first(main_s, gates_s, gates_s_t, m0_tok, state_c, n0,
                                                                     g_head)
                deferred = (vdt0, kb0, sc0)
            else:
                a_s, n_s, mt_s, c_s_all = _mlstm_sample_second(main_s, gates_s, gates_s_t, m0_tok, state_c, n0,
                                                               g_head, *deferred)
            outs["c_p"].append(c_p); outs["n_p"].append(n_p)
            outs["m_p"].append(_last_rows(mt_p, BATCH, SEQ, 1, 0, A_HEADS).reshape(BATCH, A_HEADS))
            outs["n_s"].append(n_s)
            outs["m_s"].append(mt_s[DEC_SEQ - 1::DEC_SEQ, :A_HEADS])
            wo, bo = w_mlstm_out[j], zero_bias
        elif kind == 1:
            rope_tab, rope_consts = _rope_tables()
            w_qkv, b_qkv = w_swa_qkv[j], b_swa_qkv[j]
            w_qkv = jnp.concatenate([_q_cols_to_group_major(w_qkv[:, :B_Q]), w_qkv[:, B_Q:]], axis=1)
            b_qkv = jnp.concatenate([_q_cols_to_group_major(b_qkv[:B_Q]), b_qkv[B_Q:]])
            proj_p, proj_s = _proj_swa(x_p, x_s, g_mix, w_qkv.astype(bf16), row(b_qkv), rope_tab, rope_consts)
            sinks = swa_sinks[j].astype(f32)
            a_p = _swa_prompt(proj_p, sinks)
            buf = cache_swa_k.shape[2]
            a_s, k_s, v_s = _swa_sample(proj_s, sinks,
                                        cache_swa_k[j].astype(f32).reshape(DEC_BATCH, buf, B_KV),
                                        cache_swa_v[j].astype(f32).reshape(DEC_BATCH, buf, B_KV))
            outs["k_p"].append(_last_rows(proj_p, BATCH, SEQ, WINDOW, B_Q, B_Q + B_KV)
                               .reshape(BATCH, WINDOW, B_KV_HEADS, B_HD))
            outs["v_p"].append(_last_rows(proj_p, BATCH, SEQ, WINDOW, B_Q + B_KV, B_IN)
                               .reshape(BATCH, WINDOW, B_KV_HEADS, B_HD))
            outs["k_s"].append(k_s.reshape(DEC_BATCH, buf, B_KV_HEADS, B_HD))
            outs["v_s"].append(v_s.reshape(DEC_BATCH, buf, B_KV_HEADS, B_HD))
            wo = jnp.swapaxes(w_swa_out[j].reshape(B_KV_HEADS, B_GROUP, B_HD, D_MODEL), 0, 1).reshape(B_Q, D_MODEL)
            bo = row(b_swa_out[j])
        else:
            proj_p, proj_s = _proj_rg(x_p, x_s, g_mix, w_rg_in[j].astype(bf16))
            rows8 = lambda v: jnp.broadcast_to(v.astype(f32)[..., None, :], v.shape[:-1] + (SUBLANES, C_WIDTH))
            weights = (rows8(w_rg_conv[j]), rows8(b_rg_conv[j]), w_rg_a[j].astype(bf16), rows8(b_rg_a[j]),
                       w_rg_x[j].astype(bf16), rows8(b_rg_x[j]), rows8(rg_lambda[j]))
            a_p, h_p = _rg_prompt(proj_p, weights)
            conv_pad = jnp.pad(state_rglru_conv[j].astype(f32),
                               ((0, 0), (SUBLANES - (CONV_W - 1), 0), (0, 0))).reshape(N_S, C_WIDTH)
            a_s, h_s = _rg_sample(proj_s, conv_pad, state_rglru_h[j].astype(f32), weights)
            outs["h_p"].append(h_p.reshape(BATCH, C_WIDTH))
            outs["cv_p"].append(_last_rows(proj_p, BATCH, SEQ, CONV_W - 1, 0, C_WIDTH))
            outs["h_s"].append(h_s)
            outs["cv_s"].append(proj_s[:, :C_WIDTH].reshape(DEC_BATCH, DEC_SEQ, C_WIDTH)[:, DEC_SEQ - (CONV_W - 1):])
            wo, bo = w_rg_out[j], zero_bias
        last = i == DEPTH - 1
        res = _out_mlp(x_p, x_s, a_p, a_s, wo.astype(bf16), bo, row(norm_mlp[i]), w_up, w_down, row(norm_final),
                       final=last, next_weights=None if last else (w_mlp_up, w_mlp_down, i + 1))
        x_p, x_s = res[0], res[1]
        if not last:
            w_up, w_down = res[2], res[3]
    st = {k: jnp.stack(v) for k, v in outs.items()}
    y_p = x_p.reshape(BATCH, SEQ, D_MODEL)
    y_s = x_s.reshape(DEC_BATCH, DEC_SEQ, D_MODEL)
    return (y_p, y_s, st["c_p"], st["n_p"], st["m_p"], c_s_all, st["n_s"], st["m_s"],
            st["k_p"], st["v_p"], st["k_s"], st["v_s"], st["h_p"], st["cv_p"], st["h_s"], st["cv_s"])
```

```python
import functools
import math

import jax
import jax.numpy as jnp
import numpy as np
from jax import lax
from jax.experimental import pallas as pl
from jax.experimental.pallas import tpu as pltpu

f32 = jnp.float32
bf16 = jnp.bfloat16

D_MODEL = 1024
BATCH = 2
SEQ = 8192
DEPTH = 4
DEC_BATCH = 128
DEC_SEQ = 8
PAST_LEN = 8192
N_MIXERS = 3
NORM_EPS = 1e-6

A_HEADS = 4
A_DK = 128
A_DV = 256
A_QK = A_HEADS * A_DK
A_MAIN = 2 * A_QK + 2 * A_HEADS * A_DV
N_A = (DEPTH + 2) // 3

B_HEADS = 16
B_KV_HEADS = 4
B_HD = 64
B_GROUP = 4
B_Q = B_HEADS * B_HD
B_KV = B_KV_HEADS * B_HD
B_IN = B_Q + 2 * B_KV
WINDOW = 128
ROPE_THETA = 500000.0
ROPE_DIM = 16

C_WIDTH = 1024
C_BLOCKS = 4
C_BW = 256
CONV_W = 4
LRU_C = 8.0
D_FF = 4096

N_P = BATCH * SEQ
N_S = DEC_BATCH * DEC_SEQ

LANES = 128
SUBLANES = 8
VMEM_LIMIT = 56 * 1024 * 1024

TM = 512
NP_TILES = N_P // TM
NS_TILES = N_S // TM
MLSTM_CHUNK = 256
MLSTM_SB = 16
MLSTM_ST = MLSTM_SB * DEC_SEQ
MLSTM_UNROLL = 8
SWA_SB = 8
SWA_TILE = 512
RG_T = 512
FF_CHUNK = 512


def _dot(a, b):
    return jnp.dot(a, b, preferred_element_type=f32)


def _dot_nt(a, b):
    return lax.dot_general(a, b, (((1,), (1,)), ((), ())), preferred_element_type=f32)


def _split3(x):
    hi = x.astype(bf16)
    r1 = x - hi.astype(f32)
    mid = r1.astype(bf16)
    lo = (r1 - mid.astype(f32)).astype(bf16)
    return hi, mid, lo


def _mask_dot(mask, x):
    m = jnp.where(mask, 1.0, 0.0).astype(bf16)
    return sum(_dot(m, part) for part in _split3(x))


def _dot_mask(x, mask):
    m = jnp.where(mask, 1.0, 0.0).astype(bf16)
    return sum(_dot(part, m) for part in _split3(x))


def _idiv(x, d):
    assert d & (d - 1) == 0
    return x >> (d.bit_length() - 1)


def _imod(x, d):
    assert d & (d - 1) == 0
    return x & (d - 1)


def _rms_bf16(x, g):
    y = x * lax.rsqrt(jnp.mean(x * x, axis=-1, keepdims=True) + NORM_EPS)
    return (y * g).astype(bf16)


def _sigmoid(x):
    return 0.5 * jnp.tanh(0.5 * x) + 0.5


def _sqrt_nonneg(v):
    return jnp.where(v > 0.0, v * lax.rsqrt(v), 0.0)


def _softplus(x):
    return jnp.maximum(x, 0.0) + jnp.log1p(jnp.exp(-jnp.abs(x)))


def _gelu_tanh(x):
    return 0.5 * x * (1.0 + jnp.tanh(math.sqrt(2.0 / math.pi) * (x + 0.044715 * (x * x * x))))


def _params(*sem):
    return pltpu.CompilerParams(dimension_semantics=sem, vmem_limit_bytes=VMEM_LIMIT)


def _const_spec(shape):
    nd = len(shape)
    return pl.BlockSpec(shape, lambda *_: (0,) * nd)


def _p_spec(width):
    return pl.BlockSpec((TM, width), lambda i: (jnp.minimum(i, NP_TILES - 1), 0))


def _s_spec(width):
    return pl.BlockSpec((TM, width), lambda i: (jnp.maximum(i - NP_TILES, 0), 0))


def _for_each_group(body, p_refs, s_refs):
    i = pl.program_id(0)

    @pl.when(i < NP_TILES)
    def _():
        body(*p_refs)

    @pl.when(i >= NP_TILES)
    def _():
        body(*s_refs)


def _mlp_weight_cast_specs(next_weights):
    up_all, down_all, layer = next_weights
    step = lambda i: jnp.minimum(i, NP_TILES - 1)
    in_specs, out_specs, out_shape = [], [], []
    for rows, width in ((D_MODEL // NP_TILES, D_FF), (D_FF // NP_TILES, D_MODEL)):
        in_specs.append(pl.BlockSpec((None, rows, width), lambda i: (layer, step(i), 0)))
        out_specs.append(pl.BlockSpec((rows, width), lambda i: (step(i), 0)))
        out_shape.append(jax.ShapeDtypeStruct((rows * NP_TILES, width), bf16))
    return in_specs, out_specs, out_shape, [up_all, down_all]


def _proj_mlstm_kernel(xp_ref, xs_ref, g_ref, w_ref, wg_ref, bg_ref, *rest, cast_next):
    if cast_next:
        nup_ref, ndn_ref, mp_ref, ms_ref, gp_ref, gs_ref, gtp_ref, gts_ref, nup_out, ndn_out = rest
    else:
        mp_ref, ms_ref, gp_ref, gs_ref, gtp_ref, gts_ref = rest

    def body(x_ref, main_ref, gate_ref, gate_t_ref):
        xn = _rms_bf16(x_ref[...], g_ref[...])
        for c in range(0, A_MAIN, 512):
            main_ref[:, c:c + 512] = _dot(xn, w_ref[:, c:c + 512])
        gp = _dot(xn, wg_ref[...]) + bg_ref[...]
        lane = lax.broadcasted_iota(jnp.int32, gp.shape, 1)
        gates = jnp.where(lane >= A_HEADS, -_softplus(-gp), gp)
        gate_ref[...] = gates
        gate_t_ref[...] = gates.T[0:SUBLANES, :]

    i = pl.program_id(0)

    @pl.when(i < NP_TILES)
    def _():
        body(xp_ref, mp_ref, gp_ref, gtp_ref)
        if cast_next:
            nup_out[...] = nup_ref[...].astype(bf16)
            ndn_out[...] = ndn_ref[...].astype(bf16)

    @pl.when(i >= NP_TILES)
    def _():
        body(xs_ref, ms_ref, gs_ref, gts_ref)


def _proj_mlstm(x_p, x_s, g, w_main_all, w_gate_all, b_gate, layer, next_weights=None):
    lspec = lambda shape: pl.BlockSpec((None,) + shape, lambda i: (layer, 0, 0))
    in_specs = [_p_spec(D_MODEL), _s_spec(D_MODEL),
                _const_spec((1, D_MODEL)),
                lspec((D_MODEL, A_MAIN)),
                lspec((D_MODEL, LANES)),
                _const_spec((1, LANES))]
    out_specs = [_p_spec(A_MAIN), _s_spec(A_MAIN), _p_spec(LANES), _s_spec(LANES),
                 pl.BlockSpec((SUBLANES, TM), lambda i: (0, jnp.minimum(i, NP_TILES - 1))),
                 pl.BlockSpec((SUBLANES, TM), lambda i: (0, jnp.maximum(i - NP_TILES, 0)))]
    out_shape = [jax.ShapeDtypeStruct((N_P, A_MAIN), f32), jax.ShapeDtypeStruct((N_S, A_MAIN), f32),
                 jax.ShapeDtypeStruct((N_P, LANES), f32), jax.ShapeDtypeStruct((N_S, LANES), f32),
                 jax.ShapeDtypeStruct((SUBLANES, N_P), f32), jax.ShapeDtypeStruct((SUBLANES, N_S), f32)]
    args = [x_p, x_s, g, w_main_all, w_gate_all, b_gate]
    if next_weights is not None:
        extra = _mlp_weight_cast_specs(next_weights)
        in_specs, out_specs, out_shape, args = (in_specs + extra[0], out_specs + extra[1], out_shape + extra[2],
                                                args + extra[3])
    return pl.pallas_call(
        functools.partial(_proj_mlstm_kernel, cast_next=next_weights is not None),
        grid=(NP_TILES + NS_TILES,),
        in_specs=in_specs,
        out_specs=out_specs,
        out_shape=out_shape,
        compiler_params=_params("arbitrary"),
        name="proj_mlstm",
    )(*args)


def _proj_swa_kernel(xp_ref, xs_ref, g_ref, w_ref, b_ref, tab_ref, rc_ref, op_ref, os_ref):
    half = ROPE_DIM // 2

    def body(x_ref, o_ref):
        xn = _rms_bf16(x_ref[...], g_ref[...])
        cos = tab_ref[:, 0:LANES]
        sin = tab_ref[:, LANES:2 * LANES]
        sin_lo = sin * rc_ref[0:1, :]
        sin_hi = sin * rc_ref[1:2, :]
        for c in range(0, B_IN, 512):
            p = _dot(xn, w_ref[:, c:c + 512]) + b_ref[:, c:c + 512]
            for l in range(0, 512, LANES):
                pl_ = p[:, l:l + LANES]
                if c + l < B_Q + B_KV:
                    pl_ = (pl_ * cos + pltpu.roll(pl_, LANES - half, 1) * sin_lo
                           + pltpu.roll(pl_, half, 1) * sin_hi)
                o_ref[:, c + l:c + l + LANES] = pl_

    _for_each_group(body, (xp_ref, op_ref), (xs_ref, os_ref))


def _proj_swa(x_p, x_s, g, w, b, rope_tab, rope_consts):
    seq_tiles = SEQ // TM
    tab_idx = lambda i: (jnp.where(i < NP_TILES, i % seq_tiles, seq_tiles), 0)
    return pl.pallas_call(
        _proj_swa_kernel,
        grid=(NP_TILES + NS_TILES,),
        in_specs=[_p_spec(D_MODEL), _s_spec(D_MODEL),
                  _const_spec((1, D_MODEL)),
                  _const_spec((D_MODEL, B_IN)),
                  _const_spec((1, B_IN)),
                  pl.BlockSpec((TM, 2 * LANES), tab_idx),
                  _const_spec((SUBLANES, LANES))],
        out_specs=[_p_spec(B_IN), _s_spec(B_IN)],
        out_shape=[jax.ShapeDtypeStruct((N_P, B_IN), f32), jax.ShapeDtypeStruct((N_S, B_IN), f32)],
        compiler_params=_params("arbitrary"),
        name="proj_swa",
    )(x_p, x_s, g, w, b, rope_tab, rope_consts)


def _proj_rg_kernel(xp_ref, xs_ref, g_ref, w_ref, op_ref, os_ref):
    def body(x_ref, o_ref):
        xn = _rms_bf16(x_ref[...], g_ref[...])
        for c in range(0, 2 * C_WIDTH, 512):
            o_ref[:, c:c + 512] = _dot(xn, w_ref[:, c:c + 512])

    _for_each_group(body, (xp_ref, op_ref), (xs_ref, os_ref))


def _proj_rg(x_p, x_s, g, w):
    return pl.pallas_call(
        _proj_rg_kernel,
        grid=(NP_TILES + NS_TILES,),
        in_specs=[_p_spec(D_MODEL), _s_spec(D_MODEL),
                  _const_spec((1, D_MODEL)),
                  _const_spec((D_MODEL, 2 * C_WIDTH))],
        out_specs=[_p_spec(2 * C_WIDTH), _s_spec(2 * C_WIDTH)],
        out_shape=[jax.ShapeDtypeStruct((N_P, 2 * C_WIDTH), f32), jax.ShapeDtypeStruct((N_S, 2 * C_WIDTH), f32)],
        compiler_params=_params("arbitrary"),
        name="proj_rg",
    )(x_p, x_s, g, w)


def _mlstm_masks(t, seg):
    r = lax.broadcasted_iota(jnp.int32, (t, t), 0)
    c = lax.broadcasted_iota(jnp.int32, (t, t), 1)
    if seg == t:
        same = None
        lower, upper = r >= c, r <= c
    else:
        same = _idiv(r, seg) == _idiv(c, seg)
        lower, upper = same & (r >= c), same & (r <= c)
    return r, c, same, lower, upper


def _wide(col, n):
    return jnp.concatenate([col] * n, axis=1)


def _row_sum(x):
    acc = x[:, 0:LANES]
    for l in range(LANES, x.shape[1], LANES):
        acc = acc + x[:, l:l + LANES]
    return jnp.broadcast_to(jnp.sum(acc, axis=1, keepdims=True), acc.shape)


def _mlstm_cols(main_ref, h):
    q = main_ref[:, h * A_DK:(h + 1) * A_DK]
    k = main_ref[:, A_QK + h * A_DK:A_QK + (h + 1) * A_DK] * (A_DK ** -0.5)
    v = main_ref[:, 2 * A_QK + h * A_DV:2 * A_QK + (h + 1) * A_DV]
    o_pre = main_ref[:, 2 * A_QK + (A_HEADS + h) * A_DV:2 * A_QK + (A_HEADS + h + 1) * A_DV]
    return q, k, v, o_pre


def _mlstm_prompt_chunk(main_ref, gc_ref, gr_ref, gh_ref, hs_ref, c_ref, n_ref, mt_ref, c_s, n_s, m_s):
    t = MLSTM_CHUNK
    j = pl.program_id(0)

    @pl.when(j == 0)
    def _():
        c_s[...] = jnp.zeros_like(c_s)
        n_s[...] = jnp.zeros_like(n_s)
        m_s[...] = jnp.zeros_like(m_s)

    _, _, _, lower, upper = _mlstm_masks(t, t)
    gc = gc_ref[...]
    gr = gr_ref[...]
    btr_all = _dot_mask(gr, upper)
    lane = lax.broadcasted_iota(jnp.int32, (t, LANES), 1)
    mt_all = jnp.zeros((t, LANES), f32)

    def rep(col):
        return jnp.broadcast_to(col, (t, LANES))

    wide, row_sum = _wide, _row_sum

    heads = range(A_HEADS)
    stack = lambda xs: jnp.concatenate(list(xs), axis=0)
    head = lambda x, h: x[h * t:(h + 1) * t]
    cols = [_mlstm_cols(main_ref, h) for h in heads]
    qs, ks, vs = [c[0] for c in cols], [c[1] for c in cols], [c[2] for c in cols]
    qbs, kbs, vbs = [q.astype(bf16) for q in qs], [k.astype(bf16) for k in ks], [v.astype(bf16) for v in vs]
    cts = [c_s[h] for h in heads]
    ns = [n_s[h:h + 1, :] for h in heads]
    btrs = [btr_all[A_HEADS + h:A_HEADS + h + 1, :] for h in heads]
    itrs = [gr[h:h + 1, :] for h in heads]

    lf_rep = jnp.concatenate([rep(gc[:, A_HEADS + h:A_HEADS + h + 1]) for h in heads], axis=1)
    btc_all = _mask_dot(lower, lf_rep)
    btc = stack(btc_all[:, h * LANES:(h + 1) * LANES] for h in heads)
    itc = stack(rep(gc[:, h:h + 1]) for h in heads)
    m_in = stack(rep(m_s[0:1, h:h + 1]) for h in heads)
    dmat = stack(jnp.where(lower, wide(head(btc, h), t // LANES) + (itrs[h] - btrs[h]), -jnp.inf) for h in heads)
    inter = btc + m_in
    m_t = jnp.maximum(jnp.broadcast_to(jnp.max(dmat, axis=1, keepdims=True), inter.shape), inter)
    w = jnp.exp(dmat - wide(m_t, t // LANES))
    w_inter = jnp.exp(inter - m_t)
    s = stack(_dot_nt(qbs[h], kbs[h]) for h in heads) * w
    sb = s.astype(bf16)
    num = (stack(_dot(head(sb, h), vbs[h]) for h in heads)
           + wide(w_inter, A_DV // LANES) * stack(_dot(qbs[h], cts[h].astype(bf16)) for h in heads))
    den = row_sum(s) + w_inter * row_sum(stack(qs[h] * ns[h] for h in heads))
    inv = 1.0 / jnp.maximum(jnp.abs(den), jnp.exp(-m_t))
    norm = inv * lax.rsqrt(inv * inv * (row_sum(num * num) * (1.0 / A_DV)) + NORM_EPS)
    out = _sigmoid(stack(c[3] for c in cols)) * (num * wide(norm, A_DV // LANES))
    for h in heads:
        hs_ref[:, h * A_DV:(h + 1) * A_DV] = (head(out, h) * gh_ref[h:h + 1, :]).astype(bf16)

    m_news = [head(m_t, h)[t - 1:t, 0:1] for h in heads]
    last = lambda x: stack(rep(head(x, h)[t - 1:t, 0:1]) for h in heads)
    decay_c = jnp.exp(last(btc) - btc + itc - last(m_t))
    for h in heads:
        scale = jnp.exp(head(inter, h)[t - 1:t, 0:1] - m_news[h])
        decay_r = jnp.exp(btrs[h][:, t - 1:t] - btrs[h] + itrs[h] - m_news[h])
        c_s[h] = scale * cts[h] + _dot((ks[h].T * decay_r).astype(bf16), vbs[h])
        n_s[h:h + 1, :] = scale * ns[h] + jnp.sum(head(decay_c, h) * ks[h], axis=0, keepdims=True)
        mt_all = jnp.where(lane == h, head(m_t, h), mt_all)
    mt_ref[...] = mt_all
    m_s[0:1, :] = mt_all[t - 1:t, :]

    @pl.when(j == pl.num_programs(0) - 1)
    def _():
        for h in range(A_HEADS):
            c_ref[h] = c_s[h].T
        n_ref[...] = n_s[...]


def _mlstm_prompt_kernel(main_ref, gc_ref, *rest):
    gr_refs, (gh_ref, hs_ref, c_ref, n_ref, mt_ref, c_s, n_s, m_s) = rest[:BATCH], rest[BATCH:]
    for b in range(BATCH):
        _mlstm_prompt_chunk(main_ref.at[b], gc_ref.at[b], gr_refs[b], gh_ref, hs_ref.at[b], c_ref.at[b],
                            n_ref.at[b], mt_ref.at[b], c_s.at[b], n_s.at[b], m_s.at[b])


def _mlstm_prompt(main, gates, gates_t, g_head):
    t = MLSTM_CHUNK
    nc = SEQ // t
    return pl.pallas_call(
        _mlstm_prompt_kernel,
        grid=(nc,),
        in_specs=[pl.BlockSpec((BATCH, t, A_MAIN), lambda j: (0, j, 0)),
                  pl.BlockSpec((BATCH, t, LANES), lambda j: (0, j, 0))]
                 + [pl.BlockSpec((SUBLANES, t), functools.partial(lambda j, b: (0, b * nc + j), b=b))
                    for b in range(BATCH)]
                 + [_const_spec((A_HEADS, A_DV))],
        out_specs=[pl.BlockSpec((BATCH, t, A_HEADS * A_DV), lambda j: (0, j, 0)),
                   _const_spec((BATCH, A_HEADS, A_DV, A_DK)),
                   _const_spec((BATCH, A_HEADS, A_DK)),
                   pl.BlockSpec((BATCH, t, LANES), lambda j: (0, j, 0))],
        out_shape=[jax.ShapeDtypeStruct((BATCH, SEQ, A_HEADS * A_DV), bf16),
                   jax.ShapeDtypeStruct((BATCH, A_HEADS, A_DV, A_DK), f32),
                   jax.ShapeDtypeStruct((BATCH, A_HEADS, A_DK), f32),
                   jax.ShapeDtypeStruct((BATCH, SEQ, LANES), f32)],
        scratch_shapes=[pltpu.VMEM((BATCH, A_HEADS, A_DK, A_DV), f32),
                        pltpu.VMEM((BATCH, A_HEADS, A_DK), f32),
                        pltpu.VMEM((BATCH, SUBLANES, LANES), f32)],
        compiler_params=_params("arbitrary"),
        name="mlstm_prompt",
    )(main, gates, *([gates_t] * BATCH), g_head)


def _mlstm_sample_compute(main_ref, gc_ref, gr_ref, m0_ref, c0_ref, n0_ref, gh_ref,
                          hs_ref, n_ref, mt_ref, vdt_ref, kb_ref, sc_ref, acc_s, nt_s):
    t = MLSTM_ST
    seg = DEC_SEQ
    _, c, same, lower, upper = _mlstm_masks(t, seg)
    seg_last = same & (_imod(c, seg) == seg - 1)
    gc = gc_ref[...]
    gr = gr_ref[...]
    btr_all = _dot_mask(gr, upper)
    lane = lax.broadcasted_iota(jnp.int32, (t, LANES), 1)
    tok_seg = _idiv(lax.broadcasted_iota(jnp.int32, (A_DV, t), 1), seg)

    heads = range(A_HEADS)
    rep = lambda col: jnp.broadcast_to(col, (t, LANES))
    stack = lambda xs: jnp.concatenate(list(xs), axis=0)
    lanes = lambda xs: jnp.concatenate(list(xs), axis=1)
    head = lambda x, h: x[h * t:(h + 1) * t]
    cols = [_mlstm_cols(main_ref, h) for h in heads]
    qs, ks, vs = [c_[0] for c_ in cols], [c_[1] for c_ in cols], [c_[2] for c_ in cols]
    qbs, kbs, vbs = [q.astype(bf16) for q in qs], [k.astype(bf16) for k in ks], [v.astype(bf16) for v in vs]

    btc_all = _mask_dot(lower, lanes(rep(gc[:, A_HEADS + h:A_HEADS + h + 1]) for h in heads))
    btc = stack(btc_all[:, h * LANES:(h + 1) * LANES] for h in heads)
    itc = stack(rep(gc[:, h:h + 1]) for h in heads)
    m_in = stack(rep(m0_ref[:, h:h + 1]) for h in heads)
    dmat = stack(jnp.where(lower, head(btc, h) + (gr[h:h + 1, :] - btr_all[A_HEADS + h:A_HEADS + h + 1, :]), -jnp.inf)
                 for h in heads)
    inter = btc + m_in
    m_t = jnp.maximum(jnp.broadcast_to(jnp.max(dmat, axis=1, keepdims=True), inter.shape), inter)
    w = jnp.exp(dmat - m_t)
    w_inter = jnp.exp(inter - m_t)
    s = stack(_dot_nt(qbs[h], kbs[h]) for h in heads) * w
    sb = s.astype(bf16)
    num = stack(_dot(head(sb, h), vbs[h]) for h in heads)

    for h in heads:
        def inter_body(b, carry, h=h):
            rows = pl.ds(pl.multiple_of(h * t + b * seg, seg), seg)
            r = _dot_nt(c0_ref[b, h].astype(bf16), qbs[h])
            acc_s[h] = jnp.where(tok_seg == b, r, acc_s[h])
            nt_s[rows, :] = jnp.broadcast_to(n0_ref[b, h:h + 1, :], (seg, A_DK))
            return carry

        acc_s[h] = jnp.zeros((A_DV, t), f32)
        lax.fori_loop(0, MLSTM_SB, inter_body, 0, unroll=MLSTM_UNROLL)
    n_tok = nt_s[...]
    num = num + _wide(w_inter, A_DV // LANES) * stack(acc_s[h].T for h in heads)
    den = _row_sum(s) + w_inter * _row_sum(stack(qs) * n_tok)
    inv = 1.0 / jnp.maximum(jnp.abs(den), jnp.exp(-m_t))
    norm = inv * lax.rsqrt(inv * inv * (_row_sum(num * num) * (1.0 / A_DV)) + NORM_EPS)
    out = _sigmoid(stack(c_[3] for c_ in cols)) * (num * _wide(norm, A_DV // LANES))
    for h in heads:
        hs_ref[:, h * A_DV:(h + 1) * A_DV] = (head(out, h) * gh_ref[h:h + 1, :]).astype(bf16)

    last = _mask_dot(seg_last, lanes([head(x, h) for x in (m_t, btc, inter) for h in heads]))
    pick = lambda i: stack(last[:, (i * A_HEADS + h) * LANES:(i * A_HEADS + h + 1) * LANES] for h in heads)
    m_new, bt_last, inter_last = pick(0), pick(1), pick(2)
    decay = jnp.exp(bt_last - btc + itc - m_new)
    scale = jnp.exp(inter_last - m_new)
    seg_sum = _mask_dot(same, lanes(head(decay, h) * ks[h] for h in heads))
    n_new = scale * n_tok + stack(seg_sum[:, h * A_DK:(h + 1) * A_DK] for h in heads)
    mt_all = jnp.zeros((t, LANES), f32)
    sc_all = jnp.zeros((t, LANES), f32)
    for h in heads:
        vdt_ref[h] = (_wide(head(decay, h), A_DV // LANES) * vs[h]).T.astype(bf16)
        kb_ref[:, h * A_DK:(h + 1) * A_DK] = kbs[h]
        sc_all = jnp.where(lane == h, head(scale, h), sc_all)
        mt_all = jnp.where(lane == h, head(m_t, h), mt_all)
    for b in range(MLSTM_SB):
        n_ref[b] = stack(head(n_new, h)[b * seg:b * seg + 1, :] for h in heads)
    mt_ref[...] = mt_all
    sc_ref[...] = sc_all


def _mlstm_sample_update(c_in_ref, c_out_ref, vdt_ref, kb_ref, sc_ref):
    seg = DEC_SEQ
    lane_seg = _idiv(lax.broadcasted_iota(jnp.int32, (1, MLSTM_ST), 1), seg)
    for h in range(A_HEADS):
        def upd_body(b, carry, h=h):
            rows = pl.ds(pl.multiple_of(b * seg, seg), seg)
            scale = sc_ref[rows, :][0:1, h:h + 1]
            onehot = jnp.where(lane_seg == b, 1.0, 0.0).astype(bf16)
            upd = _dot(vdt_ref[h] * onehot, kb_ref[:, h * A_DK:(h + 1) * A_DK])
            c_out_ref[b, h] = scale * c_in_ref[b, h] + upd
            return carry

        lax.fori_loop(0, MLSTM_SB, upd_body, 0, unroll=MLSTM_UNROLL)


def _mlstm_sample_first_kernel(main_ref, gc_ref, gr_ref, m0_ref, c0_ref, n0_ref, gh_ref,
                               hs_ref, n_ref, mt_ref, vdt_ref, kb_ref, sc_ref, acc_s, nt_s):
    _mlstm_sample_compute(main_ref, gc_ref, gr_ref, m0_ref, c0_ref, n0_ref, gh_ref,
                          hs_ref, n_ref, mt_ref, vdt_ref, kb_ref, sc_ref, acc_s, nt_s)


def _mlstm_sample_second_kernel(main_ref, gc_ref, gr_ref, m0_ref, c0_ref, n0_ref, gh_ref, vdt0_ref, kb0_ref, sc0_ref,
                                hs_ref, n_ref, mt_ref, c_ref, vdt_s, kb_s, sc_s, acc_s, nt_s):
    layer = pl.program_id(0)

    @pl.when(layer == 0)
    def _():
        _mlstm_sample_update(c0_ref, c_ref, vdt0_ref, kb0_ref, sc0_ref)

    @pl.when(layer == 1)
    def _():
        _mlstm_sample_compute(main_ref, gc_ref, gr_ref, m0_ref, c0_ref, n0_ref, gh_ref,
                              hs_ref, n_ref, mt_ref, vdt_s, kb_s, sc_s, acc_s, nt_s)
        _mlstm_sample_update(c0_ref, c_ref, vdt_s, kb_s, sc_s)


def _mlstm_sample_scratch():
    t = MLSTM_ST
    return [pltpu.VMEM((A_HEADS, A_DV, t), f32), pltpu.VMEM((A_HEADS * t, A_DK), f32)]


def _mlstm_sample_first(main, gates, gates_t, m0_tok, c_all, n0, g_head):
    t = MLSTM_ST
    return pl.pallas_call(
        _mlstm_sample_first_kernel,
        grid=(DEC_BATCH // MLSTM_SB,),
        in_specs=[pl.BlockSpec((t, A_MAIN), lambda i: (i, 0)),
                  pl.BlockSpec((t, LANES), lambda i: (i, 0)),
                  pl.BlockSpec((SUBLANES, t), lambda i: (0, i)),
                  pl.BlockSpec((t, LANES), lambda i: (i, 0)),
                  pl.BlockSpec((None, MLSTM_SB, A_HEADS, A_DV, A_DK), lambda i: (0, i, 0, 0, 0)),
                  pl.BlockSpec((MLSTM_SB, A_HEADS, A_DK), lambda i: (i, 0, 0)),
                  _const_spec((A_HEADS, A_DV))],
        out_specs=[pl.BlockSpec((t, A_HEADS * A_DV), lambda i: (i, 0)),
                   pl.BlockSpec((MLSTM_SB, A_HEADS, A_DK), lambda i: (i, 0, 0)),
                   pl.BlockSpec((t, LANES), lambda i: (i, 0)),
                   pl.BlockSpec((A_HEADS, A_DV, t), lambda i: (0, 0, i)),
                   pl.BlockSpec((t, A_QK), lambda i: (i, 0)),
                   pl.BlockSpec((t, LANES), lambda i: (i, 0))],
        out_shape=[jax.ShapeDtypeStruct((N_S, A_HEADS * A_DV), bf16),
                   jax.ShapeDtypeStruct((DEC_BATCH, A_HEADS, A_DK), f32),
                   jax.ShapeDtypeStruct((N_S, LANES), f32),
                   jax.ShapeDtypeStruct((A_HEADS, A_DV, N_S), bf16),
                   jax.ShapeDtypeStruct((N_S, A_QK), bf16),
                   jax.ShapeDtypeStruct((N_S, LANES), f32)],
        scratch_shapes=_mlstm_sample_scratch(),
        compiler_params=_params("parallel"),
        name="mlstm_sample_first",
    )(main, gates, gates_t, m0_tok, c_all, n0, g_head)


def _mlstm_sample_second(main, gates, gates_t, m0_tok, c_all, n0, g_head, vdt0, kb0, sc0):
    t = MLSTM_ST
    own = lambda l, i: i * l
    first = lambda l, i: i * (1 - l)
    return pl.pallas_call(
        _mlstm_sample_second_kernel,
        grid=(N_A, DEC_BATCH // MLSTM_SB),
        in_specs=[pl.BlockSpec((t, A_MAIN), lambda l, i: (own(l, i), 0)),
                  pl.BlockSpec((t, LANES), lambda l, i: (own(l, i), 0)),
                  pl.BlockSpec((SUBLANES, t), lambda l, i: (0, own(l, i))),
                  pl.BlockSpec((t, LANES), lambda l, i: (own(l, i), 0)),
                  pl.BlockSpec((None, MLSTM_SB, A_HEADS, A_DV, A_DK), lambda l, i: (l, i, 0, 0, 0)),
                  pl.BlockSpec((MLSTM_SB, A_HEADS, A_DK), lambda l, i: (own(l, i), 0, 0)),
                  _const_spec((A_HEADS, A_DV)),
                  pl.BlockSpec((A_HEADS, A_DV, t), lambda l, i: (0, 0, first(l, i))),
                  pl.BlockSpec((t, A_QK), lambda l, i: (first(l, i), 0)),
                  pl.BlockSpec((t, LANES), lambda l, i: (first(l, i), 0))],
        out_specs=[pl.BlockSpec((t, A_HEADS * A_DV), lambda l, i: (own(l, i), 0)),
                   pl.BlockSpec((MLSTM_SB, A_HEADS, A_DK), lambda l, i: (own(l, i), 0, 0)),
                   pl.BlockSpec((t, LANES), lambda l, i: (own(l, i), 0)),
                   pl.BlockSpec((None, MLSTM_SB, A_HEADS, A_DV, A_DK), lambda l, i: (l, i, 0, 0, 0))],
        out_shape=[jax.ShapeDtypeStruct((N_S, A_HEADS * A_DV), bf16),
                   jax.ShapeDtypeStruct((DEC_BATCH, A_HEADS, A_DK), f32),
                   jax.ShapeDtypeStruct((N_S, LANES), f32),
                   jax.ShapeDtypeStruct((N_A, DEC_BATCH, A_HEADS, A_DV, A_DK), f32)],
        scratch_shapes=[pltpu.VMEM((A_HEADS, A_DV, t), bf16),
                        pltpu.VMEM((t, A_QK), bf16),
                        pltpu.VMEM((t, LANES), f32)] + _mlstm_sample_scratch(),
        compiler_params=_params("arbitrary", "arbitrary"),
        name="mlstm_sample_second",
    )(main, gates, gates_t, m0_tok, c_all, n0, g_head, vdt0, kb0, sc0)


def _swa_softmax_pv(s, sink, vb):
    n = s.shape[1] // LANES
    m = jnp.maximum(jnp.broadcast_to(jnp.max(s, axis=1, keepdims=True), sink.shape), sink)
    p = jnp.exp(s - _wide(m, n))
    den = _row_sum(p) + jnp.exp(sink - m)
    return _dot((p * _wide(1.0 / den, n)).astype(bf16), vb)


def _swa_window_bias(has_prev):
    w = WINDOW
    t = lax.broadcasted_iota(jnp.int32, (w, 2 * w), 0)
    jj = lax.broadcasted_iota(jnp.int32, (w, 2 * w), 1)
    bias = jnp.where((jj > t) & (jj <= t + w) & (has_prev | (jj >= w)), 0.0, -jnp.inf)
    return jnp.concatenate([bias] * B_GROUP, axis=0)


def _swa_window_block(q_groups, kb, vb, sink_ref, bias):
    w = WINDOW
    lane_head = _idiv(lax.broadcasted_iota(jnp.int32, (1, B_KV), 1), B_HD)
    outs = [jnp.zeros((w, B_KV), f32) for _ in range(B_GROUP)]
    for h in range(B_KV_HEADS):
        hm = lane_head == h
        qscale = jnp.where(hm, B_HD ** -0.5, 0.0)
        qh = jnp.concatenate([q * qscale for q in q_groups], axis=0).astype(bf16)
        sink = jnp.concatenate([jnp.full((w, LANES), sink_ref[h * B_GROUP + g], f32) for g in range(B_GROUP)], axis=0)
        r = _swa_softmax_pv(_dot_nt(qh, kb) + bias, sink, vb)
        for g in range(B_GROUP):
            outs[g] = jnp.where(hm, r[g * w:(g + 1) * w, :], outs[g])
    return outs


def _swa_prompt_kernel(sink_ref, q_ref, ko_ref, vo_ref, kp_ref, vp_ref, a_ref):
    w = WINDOW
    bias_first = _swa_window_bias(pl.program_id(1) > 0)
    bias_rest = _swa_window_bias(True)
    for nb in range(SWA_TILE // w):
        r0 = nb * w
        if nb == 0:
            k_prev, v_prev, bias = kp_ref[...], vp_ref[...], bias_first
        else:
            k_prev, v_prev, bias = ko_ref[r0 - w:r0, :], vo_ref[r0 - w:r0, :], bias_rest
        kb = jnp.concatenate([k_prev, ko_ref[r0:r0 + w, :]], axis=0).astype(bf16)
        vb = jnp.concatenate([v_prev, vo_ref[r0:r0 + w, :]], axis=0).astype(bf16)
        outs = _swa_window_block([q_ref[r0:r0 + w, g * B_KV:(g + 1) * B_KV] for g in range(B_GROUP)],
                                 kb, vb, sink_ref, bias)
        for g in range(B_GROUP):
            a_ref[r0:r0 + w, g * B_KV:(g + 1) * B_KV] = outs[g].astype(bf16)


def _swa_prompt(proj, sinks):
    nt = SEQ // SWA_TILE
    w = WINDOW
    per = SWA_TILE // w
    kcol = B_Q // B_KV
    vcol = kcol + 1
    tile = lambda b, n: b * nt + n
    prev = lambda b, n: jnp.maximum(tile(b, n) * per - 1, 0)
    return pl.pallas_call(
        _swa_prompt_kernel,
        grid=(BATCH, nt),
        in_specs=[pl.BlockSpec(memory_space=pltpu.SMEM),
                  pl.BlockSpec((SWA_TILE, B_Q), lambda b, n: (tile(b, n), 0)),
                  pl.BlockSpec((SWA_TILE, B_KV), lambda b, n: (tile(b, n), kcol)),
                  pl.BlockSpec((SWA_TILE, B_KV), lambda b, n: (tile(b, n), vcol)),
                  pl.BlockSpec((w, B_KV), lambda b, n: (prev(b, n), kcol)),
                  pl.BlockSpec((w, B_KV), lambda b, n: (prev(b, n), vcol))],
        out_specs=pl.BlockSpec((SWA_TILE, B_Q), lambda b, n: (tile(b, n), 0)),
        out_shape=jax.ShapeDtypeStruct((N_P, B_Q), bf16),
        compiler_params=_params("parallel", "arbitrary"),
        name="swa_prompt",
    )(sinks, proj, proj, proj, proj, proj)


def _swa_sample_kernel(sink_ref, q_ref, kn_ref, vn_ref, kc_ref, vc_ref, o_ref, ko_ref, vo_ref):
    s_len = DEC_SEQ
    buf = WINDOW
    rows = B_HEADS * s_len
    keys = 2 * buf
    ri = lax.broadcasted_iota(jnp.int32, (rows, keys), 0)
    jj = lax.broadcasted_iota(jnp.int32, (rows, keys), 1)
    t = _imod(ri, s_len)
    bias = jnp.where(((jj < buf) & (jj > t)) | ((jj >= buf) & (jj - buf <= t)), 0.0, -jnp.inf)
    lane_head = _idiv(lax.broadcasted_iota(jnp.int32, (1, B_KV), 1), B_HD)
    qscale = [jnp.where(lane_head == h, B_HD ** -0.5, 0.0) for h in range(B_KV_HEADS)]
    sink = jnp.concatenate([jnp.full((s_len, LANES), sink_ref[i], f32) for i in range(B_HEADS)], axis=0)
    pad = jnp.zeros((keys - buf - s_len, B_KV), f32)

    def body(e, carry):
        rs = pl.ds(pl.multiple_of(e * s_len, s_len), s_len)
        kn = kn_ref[rs, :]
        vn = vn_ref[rs, :]
        kc = kc_ref[e]
        vc = vc_ref[e]
        kpad = jnp.concatenate([kc, kn, pad], axis=0).astype(bf16)
        vpad = jnp.concatenate([vc, vn, pad], axis=0).astype(bf16)
        ko_ref[e, 0:buf - s_len, :] = kc[s_len:, :]
        ko_ref[e, buf - s_len:, :] = kn
        vo_ref[e, 0:buf - s_len, :] = vc[s_len:, :]
        vo_ref[e, buf - s_len:, :] = vn
        qe = q_ref[rs, :]
        qbig = jnp.concatenate([qe[:, g * B_KV:(g + 1) * B_KV] * qscale[h]
                                for h in range(B_KV_HEADS) for g in range(B_GROUP)], axis=0).astype(bf16)
        r = _swa_softmax_pv(_dot_nt(qbig, kpad) + bias, sink, vpad)
        for g in range(B_GROUP):
            og = jnp.zeros((s_len, B_KV), f32)
            for h in range(B_KV_HEADS):
                blk = (h * B_GROUP + g) * s_len
                og = jnp.where(lane_head == h, r[blk:blk + s_len, :], og)
            o_ref[rs, g * B_KV:(g + 1) * B_KV] = og.astype(bf16)
        return carry

    lax.fori_loop(0, SWA_SB, body, 0, unroll=True)


def _swa_sample(proj, sinks, k_cache, v_cache):
    t = SWA_SB * DEC_SEQ
    kcol = B_Q // B_KV
    return pl.pallas_call(
        _swa_sample_kernel,
        grid=(DEC_BATCH // SWA_SB,),
        in_specs=[pl.BlockSpec(memory_space=pltpu.SMEM),
                  pl.BlockSpec((t, B_Q), lambda i: (i, 0)),
                  pl.BlockSpec((t, B_KV), lambda i: (i, kcol)),
                  pl.BlockSpec((t, B_KV), lambda i: (i, kcol + 1)),
                  pl.BlockSpec((SWA_SB, WINDOW, B_KV), lambda i: (i, 0, 0)),
                  pl.BlockSpec((SWA_SB, WINDOW, B_KV), lambda i: (i, 0, 0))],
        out_specs=[pl.BlockSpec((t, B_Q), lambda i: (i, 0)),
                   pl.BlockSpec((SWA_SB, WINDOW, B_KV), lambda i: (i, 0, 0)),
                   pl.BlockSpec((SWA_SB, WINDOW, B_KV), lambda i: (i, 0, 0))],
        out_shape=[jax.ShapeDtypeStruct((N_S, B_Q), bf16),
                   jax.ShapeDtypeStruct((DEC_BATCH, WINDOW, B_KV), f32),
                   jax.ShapeDtypeStruct((DEC_BATCH, WINDOW, B_KV), f32)],
        compiler_params=_params("parallel"),
        name="swa_sample",
    )(sinks, proj, proj, proj, k_cache, v_cache)


def _rg_conv_group(x8, p8, wc_ref, bc_ref):
    row = lax.broadcasted_iota(jnp.int32, x8.shape, 0)
    u = bc_ref[...] + wc_ref[CONV_W - 1] * x8
    for d in range(1, CONV_W):
        sh = pltpu.roll(jnp.where(row >= SUBLANES - d, p8, x8), d, 0)
        u = u + wc_ref[CONV_W - 1 - d] * sh
    return u


def _rg_scan_group(a8, b8, carry):
    row = lax.broadcasted_iota(jnp.int32, a8.shape, 0)
    for d in (1, 2, 4):
        keep = row >= d
        b8 = jnp.where(keep, a8 * pltpu.roll(b8, d, 0) + b8, b8)
        a8 = jnp.where(keep, a8 * pltpu.roll(a8, d, 0), a8)
    return a8 * carry + b8


def _rg_gate_rows(ra, rx, neg_rate, u):
    z = _sigmoid(ra) * neg_rate
    a = jnp.exp(-z)
    return a, _sqrt_nonneg(jnp.tanh(z) * (a * a + 1.0)) * _sigmoid(rx) * u


def _rg_gates(u, gate, wa_ref, ba_ref, wx_ref, bx_ref, lam_ref):
    ub = u.astype(bf16)
    ra = jnp.concatenate([_dot(ub[:, n * C_BW:(n + 1) * C_BW], wa_ref[n]) for n in range(C_BLOCKS)], axis=1)
    rx = jnp.concatenate([_dot(ub[:, n * C_BW:(n + 1) * C_BW], wx_ref[n]) for n in range(C_BLOCKS)], axis=1)
    a, bterm = _rg_gate_rows(ra + ba_ref[0:1, :], rx + bx_ref[0:1, :], LRU_C * _softplus(-lam_ref[0:1, :]), u)
    return a, bterm, _gelu_tanh(gate)


def _rg_prompt_kernel(p_ref, wc_ref, bc_ref, wa_ref, ba_ref, wx_ref, bx_ref, lam_ref,
                      y_ref, h_ref, u_s, a_s, b_s, xc_s, hc_s):
    j = pl.program_id(1)
    ng = RG_T // SUBLANES
    out_rows = 2 * SUBLANES

    @pl.when(j == 0)
    def _():
        xc_s[...] = jnp.zeros_like(xc_s)
        hc_s[...] = jnp.zeros_like(hc_s)

    def group(gidx, n=SUBLANES):
        return pl.ds(pl.multiple_of(gidx * n, n), n)

    def conv_body(gidx, p8):
        x8 = p_ref[group(gidx), 0:C_WIDTH]
        u_s[group(gidx), :] = _rg_conv_group(x8, p8, wc_ref, bc_ref)
        return x8

    xc_s[...] = lax.fori_loop(0, ng, conv_body, xc_s[...], unroll=2)
    ub = u_s[...].astype(bf16)
    for n in range(C_BLOCKS):
        cols = slice(n * C_BW, (n + 1) * C_BW)
        a_s[:, cols] = _dot(ub[:, cols], wa_ref[n])
        b_s[:, cols] = _dot(ub[:, cols], wx_ref[n])
    neg_rate = LRU_C * _softplus(-lam_ref[...])

    def gate_body(gidx, carry):
        rows = group(gidx)
        a, bterm = _rg_gate_rows(a_s[rows, :] + ba_ref[...], b_s[rows, :] + bx_ref[...], neg_rate, u_s[rows, :])
        a_s[rows, :] = a
        b_s[rows, :] = bterm
        return carry

    lax.fori_loop(0, ng, gate_body, 0, unroll=4)

    def scan_body(gidx, carry):
        rows = group(gidx, out_rows)
        lo = pl.ds(pl.multiple_of(gidx * out_rows, out_rows), SUBLANES)
        hi = pl.ds(pl.multiple_of(gidx * out_rows + SUBLANES, SUBLANES), SUBLANES)
        h_lo = _rg_scan_group(a_s[lo, :], b_s[lo, :], carry)
        h_hi = _rg_scan_group(a_s[hi, :], b_s[hi, :], h_lo[SUBLANES - 1:SUBLANES, :])
        h16 = jnp.concatenate([h_lo, h_hi], axis=0)
        y_ref[rows, :] = (h16 * _gelu_tanh(p_ref[rows, C_WIDTH:])).astype(bf16)
        return h_hi[SUBLANES - 1:SUBLANES, :]

    h_last = lax.fori_loop(0, RG_T // out_rows, scan_body, hc_s[...])
    hc_s[...] = h_last

    @pl.when(j == pl.num_programs(1) - 1)
    def _():
        h_ref[0] = h_last


def _rg_weight_specs():
    rows = (SUBLANES, C_WIDTH)
    return [_const_spec((CONV_W,) + rows), _const_spec(rows),
            _const_spec((C_BLOCKS, C_BW, C_BW)), _const_spec(rows),
            _const_spec((C_BLOCKS, C_BW, C_BW)), _const_spec(rows),
            _const_spec(rows)]


def _rg_prompt(proj, weights):
    nt = SEQ // RG_T
    return pl.pallas_call(
        _rg_prompt_kernel,
        grid=(BATCH, nt),
        in_specs=[pl.BlockSpec((RG_T, 2 * C_WIDTH), lambda b, j: (b * nt + j, 0))] + _rg_weight_specs(),
        out_specs=[pl.BlockSpec((RG_T, C_WIDTH), lambda b, j: (b * nt + j, 0)),
                   pl.BlockSpec((1, 1, C_WIDTH), lambda b, j: (b, 0, 0))],
        out_shape=[jax.ShapeDtypeStruct((N_P, C_WIDTH), bf16),
                   jax.ShapeDtypeStruct((BATCH, 1, C_WIDTH), f32)],
        scratch_shapes=[pltpu.VMEM((RG_T, C_WIDTH), f32),
                        pltpu.VMEM((RG_T, C_WIDTH), f32),
                        pltpu.VMEM((RG_T, C_WIDTH), f32),
                        pltpu.VMEM((SUBLANES, C_WIDTH), f32),
                        pltpu.VMEM((1, C_WIDTH), f32)],
        compiler_params=_params("parallel", "arbitrary"),
        name="rg_prompt",
    )(proj, *weights)


def _rg_sample_kernel(p_ref, cv_ref, h0_ref, wc_ref, bc_ref, wa_ref, ba_ref, wx_ref, bx_ref, lam_ref,
                      y_ref, h_ref, u_s, a_s, b_s):
    def conv_body(gidx, carry):
        rows = pl.ds(pl.multiple_of(gidx * SUBLANES, SUBLANES), SUBLANES)
        u_s[rows, :] = _rg_conv_group(p_ref[rows, 0:C_WIDTH], cv_ref[rows, :], wc_ref, bc_ref)
        return carry

    lax.fori_loop(0, DEC_BATCH, conv_body, 0)
    a, bterm, gg = _rg_gates(u_s[...], p_ref[:, C_WIDTH:], wa_ref, ba_ref, wx_ref, bx_ref, lam_ref)
    a_s[...] = a
    b_s[...] = bterm

    def scan_body(gidx, carry):
        rows = pl.ds(pl.multiple_of(gidx * SUBLANES, SUBLANES), SUBLANES)
        h8 = _rg_scan_group(a_s[rows, :], b_s[rows, :], h0_ref[pl.ds(gidx, 1), :])
        u_s[rows, :] = h8
        h_ref[pl.ds(gidx, 1), :] = h8[SUBLANES - 1:SUBLANES, :]
        return carry

    lax.fori_loop(0, DEC_BATCH, scan_body, 0)
    y_ref[...] = (u_s[...] * gg).astype(bf16)


def _rg_sample(proj, conv_pad, h0, weights):
    return pl.pallas_call(
        _rg_sample_kernel,
        grid=(1,),
        in_specs=[_const_spec((N_S, 2 * C_WIDTH)),
                  _const_spec((N_S, C_WIDTH)),
                  _const_spec((DEC_BATCH, C_WIDTH))] + _rg_weight_specs(),
        out_specs=[_const_spec((N_S, C_WIDTH)), _const_spec((DEC_BATCH, C_WIDTH))],
        out_shape=[jax.ShapeDtypeStruct((N_S, C_WIDTH), bf16),
                   jax.ShapeDtypeStruct((DEC_BATCH, C_WIDTH), f32)],
        scratch_shapes=[pltpu.VMEM((N_S, C_WIDTH), f32),
                        pltpu.VMEM((N_S, C_WIDTH), f32),
                        pltpu.VMEM((N_S, C_WIDTH), f32)],
        compiler_params=_params("arbitrary"),
        name="rg_sample",
    )(proj, conv_pad, h0, *weights)


def _mlp_tile(x_ref, a_ref, o_ref, wo_ref, bo_ref, g_ref, wup_ref, wdn_ref, gf_ref, x1_s, xn_s, final):
    x1_s[...] = x_ref[...] + _dot(a_ref[...], wo_ref[...]) + bo_ref[...]
    xn_s[...] = _rms_bf16(x1_s[...], g_ref[...])
    for c in range(0, D_FF, FF_CHUNK):
        hmid = jnp.maximum(_dot(xn_s[...], wup_ref[:, c:c + FF_CHUNK]), 0.0)
        x1_s[...] += _dot((hmid * hmid).astype(bf16), wdn_ref[c:c + FF_CHUNK, :])
    if final:
        x1 = x1_s[...]
        y = x1 * lax.rsqrt(jnp.mean(x1 * x1, axis=-1, keepdims=True) + NORM_EPS)
        o_ref[...] = y * gf_ref[...]
    else:
        o_ref[...] = x1_s[...]


def _out_mlp_kernel(xp_ref, xs_ref, ap_ref, as_ref, wo_ref, bo_ref, g_ref, wup_ref, wdn_ref, gf_ref, *rest,
                    final, cast_next):
    if cast_next:
        nup_ref, ndn_ref, op_ref, os_ref, nup_out, ndn_out, x1_s, xn_s = rest
    else:
        op_ref, os_ref, x1_s, xn_s = rest

    def body(x_ref, a_ref, o_ref):
        _mlp_tile(x_ref, a_ref, o_ref, wo_ref, bo_ref, g_ref, wup_ref, wdn_ref, gf_ref, x1_s, xn_s, final)

    i = pl.program_id(0)

    @pl.when(i < NP_TILES)
    def _():
        body(xp_ref, ap_ref, op_ref)
        if cast_next:
            nup_out[...] = nup_ref[...].astype(bf16)
            ndn_out[...] = ndn_ref[...].astype(bf16)

    @pl.when(i >= NP_TILES)
    def _():
        body(xs_ref, as_ref, os_ref)


def _out_mlp(x_p, x_s, a_p, a_s, wo, bo, g, w_up, w_down, g_final, final, next_weights=None):
    single = pl.Buffered(1)
    wspec = lambda shape: pl.BlockSpec(shape, lambda i: (0, 0), pipeline_mode=single)
    in_specs = [_p_spec(D_MODEL), _s_spec(D_MODEL), _p_spec(D_MODEL), _s_spec(D_MODEL),
                wspec((D_MODEL, D_MODEL)), wspec((1, D_MODEL)), wspec((1, D_MODEL)),
                wspec((D_MODEL, D_FF)), wspec((D_FF, D_MODEL)), wspec((1, D_MODEL))]
    out_specs = [_p_spec(D_MODEL), _s_spec(D_MODEL)]
    out_shape = [jax.ShapeDtypeStruct((N_P, D_MODEL), f32), jax.ShapeDtypeStruct((N_S, D_MODEL), f32)]
    args = [x_p, x_s, a_p, a_s, wo, bo, g, w_up, w_down, g_final]
    if next_weights is not None:
        extra = _mlp_weight_cast_specs(next_weights)
        in_specs, out_specs, out_shape, args = (in_specs + extra[0], out_specs + extra[1], out_shape + extra[2],
                                                args + extra[3])
    return pl.pallas_call(
        functools.partial(_out_mlp_kernel, final=final, cast_next=next_weights is not None),
        grid=(NP_TILES + NS_TILES,),
        in_specs=in_specs,
        out_specs=out_specs,
        out_shape=out_shape,
        scratch_shapes=[pltpu.VMEM((TM, D_MODEL), f32),
                        pltpu.VMEM((TM, D_MODEL), bf16)],
        compiler_params=_params("arbitrary"),
        name="out_mlp",
    )(*args)


def _rope_tables():
    half = ROPE_DIM // 2
    inv = np.float32(ROPE_THETA) ** (-np.arange(0, ROPE_DIM, 2, dtype=np.float32) / np.float32(ROPE_DIM))
    lane = np.arange(LANES) % B_HD
    inv_lane = np.where(lane < ROPE_DIM, inv[lane % half], np.float32(0.0)).astype(np.float32)
    pos = np.concatenate([np.arange(SEQ), PAST_LEN + np.arange(TM) % DEC_SEQ]).astype(np.float32)
    ang = pos[:, None] * inv_lane[None, :]
    tab = np.concatenate([np.cos(ang), np.sin(ang)], axis=1).astype(np.float32)
    consts = np.zeros((SUBLANES, LANES), np.float32)
    consts[0] = np.where(lane < half, -1.0, 0.0)
    consts[1] = np.where((lane >= half) & (lane < ROPE_DIM), 1.0, 0.0)
    return jnp.asarray(tab), jnp.asarray(consts)


def _q_cols_to_group_major(w):
    lead = w.shape[:-1]
    return jnp.swapaxes(w.reshape(lead + (B_KV_HEADS, B_GROUP, B_HD)), -3, -2).reshape(lead + (B_Q,))


def _last_rows(arr, n_seq, seq_len, n_rows, col0, col1):
    return jnp.stack([arr[(s + 1) * seq_len - n_rows:(s + 1) * seq_len, col0:col1] for s in range(n_seq)])


def kernel(x_prompt, x_sample, state_mlstm_c, state_mlstm_n, state_mlstm_m, cache_swa_k, cache_swa_v,
           state_rglru_h, state_rglru_conv, norm_mix, norm_mlp, norm_final, w_mlp_up, w_mlp_down,
           w_mlstm_in, b_mlstm_i, b_mlstm_f, g_mlstm_head, w_mlstm_out, w_swa_qkv, b_swa_qkv, swa_sinks,
           w_swa_out, b_swa_out, w_rg_in, w_rg_conv, b_rg_conv, w_rg_a, b_rg_a, w_rg_x, b_rg_x, rg_lambda,
           w_rg_out):
    assert N_A == 2
    x_p = x_prompt.reshape(N_P, D_MODEL)
    x_s = x_sample.reshape(N_S, D_MODEL)
    zero_bias = jnp.zeros((1, D_MODEL), f32)
    row = lambda v: v.reshape(1, -1).astype(f32)
    w_up = w_down = None
    state_c = state_mlstm_c.astype(f32)
    n_gate = 2 * A_HEADS
    assert n_gate == SUBLANES
    w_mlstm_main = w_mlstm_in[:, :, :A_MAIN].astype(bf16)
    w_gate_cols = w_mlstm_in[:, :, A_MAIN:]
    w_mlstm_gate = jnp.pad(w_gate_cols, ((0, 0), (0, 0), (0, LANES - n_gate))).astype(bf16)
    outs = {k: [] for k in ("c_p", "n_p", "m_p", "n_s", "m_s", "k_p", "v_p", "k_s", "v_s",
                            "h_p", "cv_p", "h_s", "cv_s")}
    deferred = None
    c_s_all = None
    for i in range(DEPTH):
        kind, j = i % N_MIXERS, i // N_MIXERS
        g_mix = row(norm_mix[i])
        if kind == 0:
            b_gate = jnp.concatenate([b_mlstm_i[j], b_mlstm_f[j]]).astype(f32)
            res = _proj_mlstm(x_p, x_s, g_mix, w_mlstm_main, w_mlstm_gate,
                              jnp.pad(b_gate, (0, LANES - n_gate)).reshape(1, LANES), j,
                              next_weights=(w_mlp_up, w_mlp_down, 0) if i == 0 else None)
            main_p, main_s, gates_p, gates_s, gates_p_t, gates_s_t = res[:6]
            if i == 0:
                w_up, w_down = res[6], res[7]
            g_head = g_mlstm_head[j].astype(f32)
            a_p, c_p, n_p, mt_p = _mlstm_prompt(main_p.reshape(BATCH, SEQ, A_MAIN),
                                                gates_p.reshape(BATCH, SEQ, LANES), gates_p_t, g_head)
            a_p = a_p.reshape(N_P, A_HEADS * A_DV)
            mt_p = mt_p.reshape(N_P, LANES)
            m0_tok = jnp.pad(jnp.repeat(state_mlstm_m[j].astype(f32), DEC_SEQ, axis=0),
                             ((0, 0), (0, LANES - A_HEADS)))
            n0 = state_mlstm_n[j].astype(f32)
            if j == 0:
                a_s, n_s, mt_s, vdt0, kb0, sc0 = _mlstm_sample_first(main_s, gates_s, gates_s_t, m0_tok, state_c, n0,
                                                                     g_head)
                deferred = (vdt0, kb0, sc0)
            else:
                a_s, n_s, mt_s, c_s_all = _mlstm_sample_second(main_s, gates_s, gates_s_t, m0_tok, state_c, n0,
                                                               g_head, *deferred)
            outs["c_p"].append(c_p); outs["n_p"].append(n_p)
            outs["m_p"].append(_last_rows(mt_p, BATCH, SEQ, 1, 0, A_HEADS).reshape(BATCH, A_HEADS))
            outs["n_s"].append(n_s)
            outs["m_s"].append(mt_s[DEC_SEQ - 1::DEC_SEQ, :A_HEADS])
            wo, bo = w_mlstm_out[j], zero_bias
        elif kind == 1:
            rope_tab, rope_consts = _rope_tables()
            w_qkv, b_qkv = w_swa_qkv[j], b_swa_qkv[j]
            w_qkv = jnp.concatenate([_q_cols_to_group_major(w_qkv[:, :B_Q]), w_qkv[:, B_Q:]], axis=1)
            b_qkv = jnp.concatenate([_q_cols_to_group_major(b_qkv[:B_Q]), b_qkv[B_Q:]])
            proj_p, proj_s = _proj_swa(x_p, x_s, g_mix, w_qkv.astype(bf16), row(b_qkv), rope_tab, rope_consts)
            sinks = swa_sinks[j].astype(f32)
            a_p = _swa_prompt(proj_p, sinks)
            buf = cache_swa_k.shape[2]
            a_s, k_s, v_s = _swa_sample(proj_s, sinks,
                                        cache_swa_k[j].astype(f32).reshape(DEC_BATCH, buf, B_KV),
                                        cache_swa_v[j].astype(f32).reshape(DEC_BATCH, buf, B_KV))
            outs["k_p"].append(_last_rows(proj_p, BATCH, SEQ, WINDOW, B_Q, B_Q + B_KV)
                               .reshape(BATCH, WINDOW, B_KV_HEADS, B_HD))
            outs["v_p"].append(_last_rows(proj_p, BATCH, SEQ, WINDOW, B_Q + B_KV, B_IN)
                               .reshape(BATCH, WINDOW, B_KV_HEADS, B_HD))
            outs["k_s"].append(k_s.reshape(DEC_BATCH, buf, B_KV_HEADS, B_HD))
            outs["v_s"].append(v_s.reshape(DEC_BATCH, buf, B_KV_HEADS, B_HD))
            wo = jnp.swapaxes(w_swa_out[j].reshape(B_KV_HEADS, B_GROUP, B_HD, D_MODEL), 0, 1).reshape(B_Q, D_MODEL)
            bo = row(b_swa_out[j])
        else:
            proj_p, proj_s = _proj_rg(x_p, x_s, g_mix, w_rg_in[j].astype(bf16))
            rows8 = lambda v: jnp.broadcast_to(v.astype(f32)[..., None, :], v.shape[:-1] + (SUBLANES, C_WIDTH))
            weights = (rows8(w_rg_conv[j]), rows8(b_rg_conv[j]), w_rg_a[j].astype(bf16), rows8(b_rg_a[j]),
                       w_rg_x[j].astype(bf16), rows8(b_rg_x[j]), rows8(rg_lambda[j]))
            a_p, h_p = _rg_prompt(proj_p, weights)
            conv_pad = jnp.pad(state_rglru_conv[j].astype(f32),
                               ((0, 0), (SUBLANES - (CONV_W - 1), 0), (0, 0))).reshape(N_S, C_WIDTH)
            a_s, h_s = _rg_sample(proj_s, conv_pad, state_rglru_h[j].astype(f32), weights)
            outs["h_p"].append(h_p.reshape(BATCH, C_WIDTH))
            outs["cv_p"].append(_last_rows(proj_p, BATCH, SEQ, CONV_W - 1, 0, C_WIDTH))
            outs["h_s"].append(h_s)
            outs["cv_s"].append(proj_s[:, :C_WIDTH].reshape(DEC_BATCH, DEC_SEQ, C_WIDTH)[:, DEC_SEQ - (CONV_W - 1):])
            wo, bo = w_rg_out[j], zero_bias
        last = i == DEPTH - 1
        res = _out_mlp(x_p, x_s, a_p, a_s, wo.astype(bf16), bo, row(norm_mlp[i]), w_up, w_down, row(norm_final),
                       final=last, next_weights=None if last else (w_mlp_up, w_mlp_down, i + 1))
        x_p, x_s = res[0], res[1]
        if not last:
            w_up, w_down = res[2], res[3]
    st = {k: jnp.stack(v) for k, v in outs.items()}
    y_p = x_p.reshape(BATCH, SEQ, D_MODEL)
    y_s = x_s.reshape(DEC_BATCH, DEC_SEQ, D_MODEL)
    return (y_p, y_s, st["c_p"], st["n_p"], st["m_p"], c_s_all, st["n_s"], st["m_s"],
            st["k_p"], st["v_p"], st["k_s"], st["v_s"], st["h_p"], st["cv_p"], st["h_s"], st["cv_s"])
```

```python
import functools
import math

import jax
import jax.numpy as jnp
import numpy as np
from jax import lax
from jax.experimental import pallas as pl
from jax.experimental.pallas import tpu as pltpu

f32 = jnp.float32
bf16 = jnp.bfloat16

D_MODEL = 1024
BATCH = 2
SEQ = 8192
DEPTH = 4
DEC_BATCH = 128
DEC_SEQ = 8
PAST_LEN = 8192
N_MIXERS = 3
NORM_EPS = 1e-6

A_HEADS = 4
A_DK = 128
A_DV = 256
A_QK = A_HEADS * A_DK
A_MAIN = 2 * A_QK + 2 * A_HEADS * A_DV
N_A = (DEPTH + 2) // 3

B_HEADS = 16
B_KV_HEADS = 4
B_HD = 64
B_GROUP = 4
B_Q = B_HEADS * B_HD
B_KV = B_KV_HEADS * B_HD
B_IN = B_Q + 2 * B_KV
WINDOW = 128
ROPE_THETA = 500000.0
ROPE_DIM = 16

C_WIDTH = 1024
C_BLOCKS = 4
C_BW = 256
CONV_W = 4
LRU_C = 8.0
D_FF = 4096

N_P = BATCH * SEQ
N_S = DEC_BATCH * DEC_SEQ

LANES = 128
SUBLANES = 8
VMEM_LIMIT = 56 * 1024 * 1024

TM = 512
NP_TILES = N_P // TM
NS_TILES = N_S // TM
MLSTM_CHUNK = 256
MLSTM_SB = 16
MLSTM_ST = MLSTM_SB * DEC_SEQ
MLSTM_UNROLL = 8
SWA_SB = 8
SWA_TILE = 512
RG_T = 512
FF_CHUNK = 512


def _dot(a, b):
    return jnp.dot(a, b, preferred_element_type=f32)


def _dot_nt(a, b):
    return lax.dot_general(a, b, (((1,), (1,)), ((), ())), preferred_element_type=f32)


def _split3(x):
    hi = x.astype(bf16)
    r1 = x - hi.astype(f32)
    mid = r1.astype(bf16)
    lo = (r1 - mid.astype(f32)).astype(bf16)
    return hi, mid, lo


def _mask_dot(mask, x):
    m = jnp.where(mask, 1.0, 0.0).astype(bf16)
    return sum(_dot(m, part) for part in _split3(x))


def _dot_mask(x, mask):
    m = jnp.where(mask, 1.0, 0.0).astype(bf16)
    return sum(_dot(part, m) for part in _split3(x))


def _idiv(x, d):
    assert d & (d - 1) == 0
    return x >> (d.bit_length() - 1)


def _imod(x, d):
    assert d & (d - 1) == 0
    return x & (d - 1)


def _rms_bf16(x, g):
    y = x * lax.rsqrt(jnp.mean(x * x, axis=-1, keepdims=True) + NORM_EPS)
    return (y * g).astype(bf16)


def _sigmoid(x):
    return 0.5 * jnp.tanh(0.5 * x) + 0.5


def _sqrt_nonneg(v):
    return jnp.where(v > 0.0, v * lax.rsqrt(v), 0.0)


def _softplus(x):
    return jnp.maximum(x, 0.0) + jnp.log1p(jnp.exp(-jnp.abs(x)))


def _gelu_tanh(x):
    return 0.5 * x * (1.0 + jnp.tanh(math.sqrt(2.0 / math.pi) * (x + 0.044715 * (x * x * x))))


def _params(*sem):
    return pltpu.CompilerParams(dimension_semantics=sem, vmem_limit_bytes=VMEM_LIMIT)


def _const_spec(shape):
    nd = len(shape)
    return pl.BlockSpec(shape, lambda *_: (0,) * nd)


def _p_spec(width):
    return pl.BlockSpec((TM, width), lambda i: (jnp.minimum(i, NP_TILES - 1), 0))


def _s_spec(width):
    return pl.BlockSpec((TM, width), lambda i: (jnp.maximum(i - NP_TILES, 0), 0))


def _for_each_group(body, p_refs, s_refs):
    i = pl.program_id(0)

    @pl.when(i < NP_TILES)
    def _():
        body(*p_refs)

    @pl.when(i >= NP_TILES)
    def _():
        body(*s_refs)


def _mlp_weight_cast_specs(next_weights):
    up_all, down_all, layer = next_weights
    step = lambda i: jnp.minimum(i, NP_TILES - 1)
    in_specs, out_specs, out_shape = [], [], []
    for rows, width in ((D_MODEL // NP_TILES, D_FF), (D_FF // NP_TILES, D_MODEL)):
        in_specs.append(pl.BlockSpec((None, rows, width), lambda i: (layer, step(i), 0)))
        out_specs.append(pl.BlockSpec((rows, width), lambda i: (step(i), 0)))
        out_shape.append(jax.ShapeDtypeStruct((rows * NP_TILES, width), bf16))
    return in_specs, out_specs, out_shape, [up_all, down_all]


def _proj_mlstm_kernel(xp_ref, xs_ref, g_ref, w_ref, wg_ref, bg_ref, *rest, cast_next):
    if cast_next:
        nup_ref, ndn_ref, mp_ref, ms_ref, gp_ref, gs_ref, gtp_ref, gts_ref, nup_out, ndn_out = rest
    else:
        mp_ref, ms_ref, gp_ref, gs_ref, gtp_ref, gts_ref = rest

    def body(x_ref, main_ref, gate_ref, gate_t_ref):
        xn = _rms_bf16(x_ref[...], g_ref[...])
        for c in range(0, A_MAIN, 512):
            main_ref[:, c:c + 512] = _dot(xn, w_ref[:, c:c + 512])
        gp = _dot(xn, wg_ref[...]) + bg_ref[...]
        lane = lax.broadcasted_iota(jnp.int32, gp.shape, 1)
        gates = jnp.where(lane >= A_HEADS, -_softplus(-gp), gp)
        gate_ref[...] = gates
        gate_t_ref[...] = gates.T[0:SUBLANES, :]

    i = pl.program_id(0)

    @pl.when(i < NP_TILES)
    def _():
        body(xp_ref, mp_ref, gp_ref, gtp_ref)
        if cast_next:
            nup_out[...] = nup_ref[...].astype(bf16)
            ndn_out[...] = ndn_ref[...].astype(bf16)

    @pl.when(i >= NP_TILES)
    def _():
        body(xs_ref, ms_ref, gs_ref, gts_ref)


def _proj_mlstm(x_p, x_s, g, w_main_all, w_gate_all, b_gate, layer, next_weights=None):
    lspec = lambda shape: pl.BlockSpec((None,) + shape, lambda i: (layer, 0, 0))
    in_specs = [_p_spec(D_MODEL), _s_spec(D_MODEL),
                _const_spec((1, D_MODEL)),
                lspec((D_MODEL, A_MAIN)),
                lspec((D_MODEL, LANES)),
                _const_spec((1, LANES))]
    out_specs = [_p_spec(A_MAIN), _s_spec(A_MAIN), _p_spec(LANES), _s_spec(LANES),
                 pl.BlockSpec((SUBLANES, TM), lambda i: (0, jnp.minimum(i, NP_TILES - 1))),
                 pl.BlockSpec((SUBLANES, TM), lambda i: (0, jnp.maximum(i - NP_TILES, 0)))]
    out_shape = [jax.ShapeDtypeStruct((N_P, A_MAIN), f32), jax.ShapeDtypeStruct((N_S, A_MAIN), f32),
                 jax.ShapeDtypeStruct((N_P, LANES), f32), jax.ShapeDtypeStruct((N_S, LANES), f32),
                 jax.ShapeDtypeStruct((SUBLANES, N_P), f32), jax.ShapeDtypeStruct((SUBLANES, N_S), f32)]
    args = [x_p, x_s, g, w_main_all, w_gate_all, b_gate]
    if next_weights is not None:
        extra = _mlp_weight_cast_specs(next_weights)
        in_specs, out_specs, out_shape, args = (in_specs + extra[0], out_specs + extra[1], out_shape + extra[2],
                                                args + extra[3])
    return pl.pallas_call(
        functools.partial(_proj_mlstm_kernel, cast_next=next_weights is not None),
        grid=(NP_TILES + NS_TILES,),
        in_specs=in_specs,
        out_specs=out_specs,
        out_shape=out_shape,
        compiler_params=_params("arbitrary"),
        name="proj_mlstm",
    )(*args)


def _proj_swa_kernel(xp_ref, xs_ref, g_ref, w_ref, b_ref, tab_ref, rc_ref, op_ref, os_ref):
    half = ROPE_DIM // 2

    def body(x_ref, o_ref):
        xn = _rms_bf16(x_ref[...], g_ref[...])
        cos = tab_ref[:, 0:LANES]
        sin = tab_ref[:, LANES:2 * LANES]
        sin_lo = sin * rc_ref[0:1, :]
        sin_hi = sin * rc_ref[1:2, :]
        for c in range(0, B_IN, 512):
            p = _dot(xn, w_ref[:, c:c + 512]) + b_ref[:, c:c + 512]
            for l in range(0, 512, LANES):
                pl_ = p[:, l:l + LANES]
                if c + l < B_Q + B_KV:
                    pl_ = (pl_ * cos + pltpu.roll(pl_, LANES - half, 1) * sin_lo
                           + pltpu.roll(pl_, half, 1) * sin_hi)
                o_ref[:, c + l:c + l + LANES] = pl_

    _for_each_group(body, (xp_ref, op_ref), (xs_ref, os_ref))


def _proj_swa(x_p, x_s, g, w, b, rope_tab, rope_consts):
    seq_tiles = SEQ // TM
    tab_idx = lambda i: (jnp.where(i < NP_TILES, i % seq_tiles, seq_tiles), 0)
    return pl.pallas_call(
        _proj_swa_kernel,
        grid=(NP_TILES + NS_TILES,),
        in_specs=[_p_spec(D_MODEL), _s_spec(D_MODEL),
                  _const_spec((1, D_MODEL)),
                  _const_spec((D_MODEL, B_IN)),
                  _const_spec((1, B_IN)),
                  pl.BlockSpec((TM, 2 * LANES), tab_idx),
                  _const_spec((SUBLANES, LANES))],
        out_specs=[_p_spec(B_IN), _s_spec(B_IN)],
        out_shape=[jax.ShapeDtypeStruct((N_P, B_IN), f32), jax.ShapeDtypeStruct((N_S, B_IN), f32)],
        compiler_params=_params("arbitrary"),
        name="proj_swa",
    )(x_p, x_s, g, w, b, rope_tab, rope_consts)


def _proj_rg_kernel(xp_ref, xs_ref, g_ref, w_ref, op_ref, os_ref):
    def body(x_ref, o_ref):
        xn = _rms_bf16(x_ref[...], g_ref[...])
        for c in range(0, 2 * C_WIDTH, 512):
            o_ref[:, c:c + 512] = _dot(xn, w_ref[:, c:c + 512])

    _for_each_group(body, (xp_ref, op_ref), (xs_ref, os_ref))


def _proj_rg(x_p, x_s, g, w):
    return pl.pallas_call(
        _proj_rg_kernel,
        grid=(NP_TILES + NS_TILES,),
        in_specs=[_p_spec(D_MODEL), _s_spec(D_MODEL),
                  _const_spec((1, D_MODEL)),
                  _const_spec((D_MODEL, 2 * C_WIDTH))],
        out_specs=[_p_spec(2 * C_WIDTH), _s_spec(2 * C_WIDTH)],
        out_shape=[jax.ShapeDtypeStruct((N_P, 2 * C_WIDTH), f32), jax.ShapeDtypeStruct((N_S, 2 * C_WIDTH), f32)],
        compiler_params=_params("arbitrary"),
        name="proj_rg",
    )(x_p, x_s, g, w)


def _mlstm_masks(t, seg):
    r = lax.broadcasted_iota(jnp.int32, (t, t), 0)
    c = lax.broadcasted_iota(jnp.int32, (t, t), 1)
    if seg == t:
        same = None
        lower, upper = r >= c, r <= c
    else:
        same = _idiv(r, seg) == _idiv(c, seg)
        lower, upper = same & (r >= c), same & (r <= c)
    return r, c, same, lower, upper


def _wide(col, n):
    return jnp.concatenate([col] * n, axis=1)


def _row_sum(x):
    acc = x[:, 0:LANES]
    for l in range(LANES, x.shape[1], LANES):
        acc = acc + x[:, l:l + LANES]
    return jnp.broadcast_to(jnp.sum(acc, axis=1, keepdims=True), acc.shape)


def _mlstm_cols(main_ref, h):
    q = main_ref[:, h * A_DK:(h + 1) * A_DK]
    k = main_ref[:, A_QK + h * A_DK:A_QK + (h + 1) * A_DK] * (A_DK ** -0.5)
    v = main_ref[:, 2 * A_QK + h * A_DV:2 * A_QK + (h + 1) * A_DV]
    o_pre = main_ref[:, 2 * A_QK + (A_HEADS + h) * A_DV:2 * A_QK + (A_HEADS + h + 1) * A_DV]
    return q, k, v, o_pre


def _mlstm_prompt_chunk(main_ref, gc_ref, gr_ref, gh_ref, hs_ref, c_ref, n_ref, mt_ref, c_s, n_s, m_s):
    t = MLSTM_CHUNK
    j = pl.program_id(0)

    @pl.when(j == 0)
    def _():
        c_s[...] = jnp.zeros_like(c_s)
        n_s[...] = jnp.zeros_like(n_s)
        m_s[...] = jnp.zeros_like(m_s)

    _, _, _, lower, upper = _mlstm_masks(t, t)
    gc = gc_ref[...]
    gr = gr_ref[...]
    btr_all = _dot_mask(gr, upper)
    lane = lax.broadcasted_iota(jnp.int32, (t, LANES), 1)
    mt_all = jnp.zeros((t, LANES), f32)

    def rep(col):
        return jnp.broadcast_to(col, (t, LANES))

    wide, row_sum = _wide, _row_sum

    heads = range(A_HEADS)
    stack = lambda xs: jnp.concatenate(list(xs), axis=0)
    head = lambda x, h: x[h * t:(h + 1) * t]
    cols = [_mlstm_cols(main_ref, h) for h in heads]
    qs, ks, vs = [c[0] for c in cols], [c[1] for c in cols], [c[2] for c in cols]
    qbs, kbs, vbs = [q.astype(bf16) for q in qs], [k.astype(bf16) for k in ks], [v.astype(bf16) for v in vs]
    cts = [c_s[h] for h in heads]
    ns = [n_s[h:h + 1, :] for h in heads]
    btrs = [btr_all[A_HEADS + h:A_HEADS + h + 1, :] for h in heads]
    itrs = [gr[h:h + 1, :] for h in heads]

    lf_rep = jnp.concatenate([rep(gc[:, A_HEADS + h:A_HEADS + h + 1]) for h in heads], axis=1)
    btc_all = _mask_dot(lower, lf_rep)
    btc = stack(btc_all[:, h * LANES:(h + 1) * LANES] for h in heads)
    itc = stack(rep(gc[:, h:h + 1]) for h in heads)
    m_in = stack(rep(m_s[0:1, h:h + 1]) for h in heads)
    dmat = stack(jnp.where(lower, wide(head(btc, h), t // LANES) + (itrs[h] - btrs[h]), -jnp.inf) for h in heads)
    inter = btc + m_in
    m_t = jnp.maximum(jnp.broadcast_to(jnp.max(dmat, axis=1, keepdims=True), inter.shape), inter)
    w = jnp.exp(dmat - wide(m_t, t // LANES))
    w_inter = jnp.exp(inter - m_t)
    s = stack(_dot_nt(qbs[h], kbs[h]) for h in heads) * w
    sb = s.astype(bf16)
    num = (stack(_dot(head(sb, h), vbs[h]) for h in heads)
           + wide(w_inter, A_DV // LANES) * stack(_dot(qbs[h], cts[h].astype(bf16)) for h in heads))
    den = row_sum(s) + w_inter * row_sum(stack(qs[h] * ns[h] for h in heads))
    inv = 1.0 / jnp.maximum(jnp.abs(den), jnp.exp(-m_t))
    norm = inv * lax.rsqrt(inv * inv * (row_sum(num * num) * (1.0 / A_DV)) + NORM_EPS)
    out = _sigmoid(stack(c[3] for c in cols)) * (num * wide(norm, A_DV // LANES))
    for h in heads:
        hs_ref[:, h * A_DV:(h + 1) * A_DV] = (head(out, h) * gh_ref[h:h + 1, :]).astype(bf16)

    m_news = [head(m_t, h)[t - 1:t, 0:1] for h in heads]
    last = lambda x: stack(rep(head(x, h)[t - 1:t, 0:1]) for h in heads)
    decay_c = jnp.exp(last(btc) - btc + itc - last(m_t))
    for h in heads:
        scale = jnp.exp(head(inter, h)[t - 1:t, 0:1] - m_news[h])
        decay_r = jnp.exp(btrs[h][:, t - 1:t] - btrs[h] + itrs[h] - m_news[h])
        c_s[h] = scale * cts[h] + _dot((ks[h].T * decay_r).astype(bf16), vbs[h])
        n_s[h:h + 1, :] = scale * ns[h] + jnp.sum(head(decay_c, h) * ks[h], axis=0, keepdims=True)
        mt_all = jnp.where(lane == h, head(m_t, h), mt_all)
    mt_ref[...] = mt_all
    m_s[0:1, :] = mt_all[t - 1:t, :]

    @pl.when(j == pl.num_programs(0) - 1)
    def _():
        for h in range(A_HEADS):
            c_ref[h] = c_s[h].T
        n_ref[...] = n_s[...]


def _mlstm_prompt_kernel(main_ref, gc_ref, *rest):
    gr_refs, (gh_ref, hs_ref, c_ref, n_ref, mt_ref, c_s, n_s, m_s) = rest[:BATCH], rest[BATCH:]
    for b in range(BATCH):
        _mlstm_prompt_chunk(main_ref.at[b], gc_ref.at[b], gr_refs[b], gh_ref, hs_ref.at[b], c_ref.at[b],
                            n_ref.at[b], mt_ref.at[b], c_s.at[b], n_s.at[b], m_s.at[b])


def _mlstm_prompt(main, gates, gates_t, g_head):
    t = MLSTM_CHUNK
    nc = SEQ // t
    return pl.pallas_call(
        _mlstm_prompt_kernel,
        grid=(nc,),
        in_specs=[pl.BlockSpec((BATCH, t, A_MAIN), lambda j: (0, j, 0)),
                  pl.BlockSpec((BATCH, t, LANES), lambda j: (0, j, 0))]
                 + [pl.BlockSpec((SUBLANES, t), functools.partial(lambda j, b: (0, b * nc + j), b=b))
                    for b in range(BATCH)]
                 + [_const_spec((A_HEADS, A_DV))],
        out_specs=[pl.BlockSpec((BATCH, t, A_HEADS * A_DV), lambda j: (0, j, 0)),
                   _const_spec((BATCH, A_HEADS, A_DV, A_DK)),
                   _const_spec((BATCH, A_HEADS, A_DK)),
                   pl.BlockSpec((BATCH, t, LANES), lambda j: (0, j, 0))],
        out_shape=[jax.ShapeDtypeStruct((BATCH, SEQ, A_HEADS * A_DV), bf16),
                   jax.ShapeDtypeStruct((BATCH, A_HEADS, A_DV, A_DK), f32),
                   jax.ShapeDtypeStruct((BATCH, A_HEADS, A_DK), f32),
                   jax.ShapeDtypeStruct((BATCH, SEQ, LANES), f32)],
        scratch_shapes=[pltpu.VMEM((BATCH, A_HEADS, A_DK, A_DV), f32),
                        pltpu.VMEM((BATCH, A_HEADS, A_DK), f32),
                        pltpu.VMEM((BATCH, SUBLANES, LANES), f32)],
        compiler_params=_params("arbitrary"),
        name="mlstm_prompt",
    )(main, gates, *([gates_t] * BATCH), g_head)


def _mlstm_sample_compute(main_ref, gc_ref, gr_ref, m0_ref, c0_ref, n0_ref, gh_ref,
                          hs_ref, n_ref, mt_ref, vdt_ref, kb_ref, sc_ref, acc_s, nt_s):
    t = MLSTM_ST
    seg = DEC_SEQ
    _, c, same, lower, upper = _mlstm_masks(t, seg)
    seg_last = same & (_imod(c, seg) == seg - 1)
    gc = gc_ref[...]
    gr = gr_ref[...]
    btr_all = _dot_mask(gr, upper)
    lane = lax.broadcasted_iota(jnp.int32, (t, LANES), 1)
    tok_seg = _idiv(lax.broadcasted_iota(jnp.int32, (A_DV, t), 1), seg)

    heads = range(A_HEADS)
    rep = lambda col: jnp.broadcast_to(col, (t, LANES))
    stack = lambda xs: jnp.concatenate(list(xs), axis=0)
    lanes = lambda xs: jnp.concatenate(list(xs), axis=1)
    head = lambda x, h: x[h * t:(h + 1) * t]
    cols = [_mlstm_cols(main_ref, h) for h in heads]
    qs, ks, vs = [c_[0] for c_ in cols], [c_[1] for c_ in cols], [c_[2] for c_ in cols]
    qbs, kbs, vbs = [q.astype(bf16) for q in qs], [k.astype(bf16) for k in ks], [v.astype(bf16) for v in vs]

    btc_all = _mask_dot(lower, lanes(rep(gc[:, A_HEADS + h:A_HEADS + h + 1]) for h in heads))
    btc = stack(btc_all[:, h * LANES:(h + 1) * LANES] for h in heads)
    itc = stack(rep(gc[:, h:h + 1]) for h in heads)
    m_in = stack(rep(m0_ref[:, h:h + 1]) for h in heads)
    dmat = stack(jnp.where(lower, head(btc, h) + (gr[h:h + 1, :] - btr_all[A_HEADS + h:A_HEADS + h + 1, :]), -jnp.inf)
                 for h in heads)
    inter = btc + m_in
    m_t = jnp.maximum(jnp.broadcast_to(jnp.max(dmat, axis=1, keepdims=True), inter.shape), inter)
    w = jnp.exp(dmat - m_t)
    w_inter = jnp.exp(inter - m_t)
    s = stack(_dot_nt(qbs[h], kbs[h]) for h in heads) * w
    sb = s.astype(bf16)
    num = stack(_dot(head(sb, h), vbs[h]) for h in heads)

    for h in heads:
        def inter_body(b, carry, h=h):
            rows = pl.ds(pl.multiple_of(h * t + b * seg, seg), seg)
            r = _dot_nt(c0_ref[b, h].astype(bf16), qbs[h])
            acc_s[h] = jnp.where(tok_seg == b, r, acc_s[h])
            nt_s[rows, :] = jnp.broadcast_to(n0_ref[b, h:h + 1, :], (seg, A_DK))
            return carry

        acc_s[h] = jnp.zeros((A_DV, t), f32)
        lax.fori_loop(0, MLSTM_SB, inter_body, 0, unroll=MLSTM_UNROLL)
    n_tok = nt_s[...]
    num = num + _wide(w_inter, A_DV // LANES) * stack(acc_s[h].T for h in heads)
    den = _row_sum(s) + w_inter * _row_sum(stack(qs) * n_tok)
    inv = 1.0 / jnp.maximum(jnp.abs(den), jnp.exp(-m_t))
    norm = inv * lax.rsqrt(inv * inv * (_row_sum(num * num) * (1.0 / A_DV)) + NORM_EPS)
    out = _sigmoid(stack(c_[3] for c_ in cols)) * (num * _wide(norm, A_DV // LANES))
    for h in heads:
        hs_ref[:, h * A_DV:(h + 1) * A_DV] = (head(out, h) * gh_ref[h:h + 1, :]).astype(bf16)

    last = _mask_dot(seg_last, lanes([head(x, h) for x in (m_t, btc, inter) for h in heads]))
    pick = lambda i: stack(last[:, (i * A_HEADS + h) * LANES:(i * A_HEADS + h + 1) * LANES] for h in heads)
    m_new, bt_last, inter_last = pick(0), pick(1), pick(2)
    decay = jnp.exp(bt_last - btc + itc - m_new)
    scale = jnp.exp(inter_last - m_new)
    seg_sum = _mask_dot(same, lanes(head(decay, h) * ks[h] for h in heads))
    n_new = scale * n_tok + stack(seg_sum[:, h * A_DK:(h + 1) * A_DK] for h in heads)
    mt_all = jnp.zeros((t, LANES), f32)
    sc_all = jnp.zeros((t, LANES), f32)
    for h in heads:
        vdt_ref[h] = (_wide(head(decay, h), A_DV // LANES) * vs[h]).T.astype(bf16)
        kb_ref[:, h * A_DK:(h + 1) * A_DK] = kbs[h]
        sc_all = jnp.where(lane == h, head(scale, h), sc_all)
        mt_all = jnp.where(lane == h, head(m_t, h), mt_all)
    for b in range(MLSTM_SB):
        n_ref[b] = stack(head(n_new, h)[b * seg:b * seg + 1, :] for h in heads)
    mt_ref[...] = mt_all
    sc_ref[...] = sc_all


def _mlstm_sample_update(c_in_ref, c_out_ref, vdt_ref, kb_ref, sc_ref):
    seg = DEC_SEQ
    lane_seg = _idiv(lax.broadcasted_iota(jnp.int32, (1, MLSTM_ST), 1), seg)
    for h in range(A_HEADS):
        def upd_body(b, carry, h=h):
            rows = pl.ds(pl.multiple_of(b * seg, seg), seg)
            scale = sc_ref[rows, :][0:1, h:h + 1]
            onehot = jnp.where(lane_seg == b, 1.0, 0.0).astype(bf16)
            upd = _dot(vdt_ref[h] * onehot, kb_ref[:, h * A_DK:(h + 1) * A_DK])
            c_out_ref[b, h] = scale * c_in_ref[b, h] + upd
            return carry

        lax.fori_loop(0, MLSTM_SB, upd_body, 0, unroll=MLSTM_UNROLL)


def _mlstm_sample_first_kernel(main_ref, gc_ref, gr_ref, m0_ref, c0_ref, n0_ref, gh_ref,
                               hs_ref, n_ref, mt_ref, vdt_ref, kb_ref, sc_ref, acc_s, nt_s):
    _mlstm_sample_compute(main_ref, gc_ref, gr_ref, m0_ref, c0_ref, n0_ref, gh_ref,
                          hs_ref, n_ref, mt_ref, vdt_ref, kb_ref, sc_ref, acc_s, nt_s)


def _mlstm_sample_second_kernel(main_ref, gc_ref, gr_ref, m0_ref, c0_ref, n0_ref, gh_ref, vdt0_ref, kb0_ref, sc0_ref,
                                hs_ref, n_ref, mt_ref, c_ref, vdt_s, kb_s, sc_s, acc_s, nt_s):
    layer = pl.program_id(0)

    @pl.when(layer == 0)
    def _():
        _mlstm_sample_update(c0_ref, c_ref, vdt0_ref, kb0_ref, sc0_ref)

    @pl.when(layer == 1)
    def _():
        _mlstm_sample_compute(main_ref, gc_ref, gr_ref, m0_ref, c0_ref, n0_ref, gh_ref,
                              hs_ref, n_ref, mt_ref, vdt_s, kb_s, sc_s, acc_s, nt_s)
        _mlstm_sample_update(c0_ref, c_ref, vdt_s, kb_s, sc_s)


def _mlstm_sample_scratch():
    t = MLSTM_ST
    return [pltpu.VMEM((A_HEADS, A_DV, t), f32), pltpu.VMEM((A_HEADS * t, A_DK), f32)]


def _mlstm_sample_first(main, gates, gates_t, m0_tok, c_all, n0, g_head):
    t = MLSTM_ST
    return pl.pallas_call(
        _mlstm_sample_first_kernel,
        grid=(DEC_BATCH // MLSTM_SB,),
        in_specs=[pl.BlockSpec((t, A_MAIN), lambda i: (i, 0)),
                  pl.BlockSpec((t, LANES), lambda i: (i, 0)),
                  pl.BlockSpec((SUBLANES, t), lambda i: (0, i)),
                  pl.BlockSpec((t, LANES), lambda i: (i, 0)),
                  pl.BlockSpec((None, MLSTM_SB, A_HEADS, A_DV, A_DK), lambda i: (0, i, 0, 0, 0)),
                  pl.BlockSpec((MLSTM_SB, A_HEADS, A_DK), lambda i: (i, 0, 0)),
                  _const_spec((A_HEADS, A_DV))],
        out_specs=[pl.BlockSpec((t, A_HEADS * A_DV), lambda i: (i, 0)),
                   pl.BlockSpec((MLSTM_SB, A_HEADS, A_DK), lambda i: (i, 0, 0)),
                   pl.BlockSpec((t, LANES), lambda i: (i, 0)),
                   pl.BlockSpec((A_HEADS, A_DV, t), lambda i: (0, 0, i)),
                   pl.BlockSpec((t, A_QK), lambda i: (i, 0)),
                   pl.BlockSpec((t, LANES), lambda i: (i, 0))],
        out_shape=[jax.ShapeDtypeStruct((N_S, A_HEADS * A_DV), bf16),
                   jax.ShapeDtypeStruct((DEC_BATCH, A_HEADS, A_DK), f32),
                   jax.ShapeDtypeStruct((N_S, LANES), f32),
                   jax.ShapeDtypeStruct((A_HEADS, A_DV, N_S), bf16),
                   jax.ShapeDtypeStruct((N_S, A_QK), bf16),
                   jax.ShapeDtypeStruct((N_S, LANES), f32)],
        scratch_shapes=_mlstm_sample_scratch(),
        compiler_params=_params("parallel"),
        name="mlstm_sample_first",
    )(main, gates, gates_t, m0_tok, c_all, n0, g_head)


def _mlstm_sample_second(main, gates, gates_t, m0_tok, c_all, n0, g_head, vdt0, kb0, sc0):
    t = MLSTM_ST
    own = lambda l, i: i * l
    first = lambda l, i: i * (1 - l)
    return pl.pallas_call(
        _mlstm_sample_second_kernel,
        grid=(N_A, DEC_BATCH // MLSTM_SB),
        in_specs=[pl.BlockSpec((t, A_MAIN), lambda l, i: (own(l, i), 0)),
                  pl.BlockSpec((t, LANES), lambda l, i: (own(l, i), 0)),
                  pl.BlockSpec((SUBLANES, t), lambda l, i: (0, own(l, i))),
                  pl.BlockSpec((t, LANES), lambda l, i: (own(l, i), 0)),
                  pl.BlockSpec((None, MLSTM_SB, A_HEADS, A_DV, A_DK), lambda l, i: (l, i, 0, 0, 0)),
                  pl.BlockSpec((MLSTM_SB, A_HEADS, A_DK), lambda l, i: (own(l, i), 0, 0)),
                  _const_spec((A_HEADS, A_DV)),
                  pl.BlockSpec((A_HEADS, A_DV, t), lambda l, i: (0, 0, first(l, i))),
                  pl.BlockSpec((t, A_QK), lambda l, i: (first(l, i), 0)),
                  pl.BlockSpec((t, LANES), lambda l, i: (first(l, i), 0))],
        out_specs=[pl.BlockSpec((t, A_HEADS * A_DV), lambda l, i: (own(l, i), 0)),
                   pl.BlockSpec((MLSTM_SB, A_HEADS, A_DK), lambda l, i: (own(l, i), 0, 0)),
                   pl.BlockSpec((t, LANES), lambda l, i: (own(l, i), 0)),
                   pl.BlockSpec((None, MLSTM_SB, A_HEADS, A_DV, A_DK), lambda l, i: (l, i, 0, 0, 0))],
        out_shape=[jax.ShapeDtypeStruct((N_S, A_HEADS * A_DV), bf16),
                   jax.ShapeDtypeStruct((DEC_BATCH, A_HEADS, A_DK), f32),
                   jax.ShapeDtypeStruct((N_S, LANES), f32),
                   jax.ShapeDtypeStruct((N_A, DEC_BATCH, A_HEADS, A_DV, A_DK), f32)],
        scratch_shapes=[pltpu.VMEM((A_HEADS, A_DV, t), bf16),
                        pltpu.VMEM((t, A_QK), bf16),
                        pltpu.VMEM((t, LANES), f32)] + _mlstm_sample_scratch(),
        compiler_params=_params("arbitrary", "arbitrary"),
        name="mlstm_sample_second",
    )(main, gates, gates_t, m0_tok, c_all, n0, g_head, vdt0, kb0, sc0)


def _swa_softmax_pv(s, sink, vb):
    n = s.shape[1] // LANES
    m = jnp.maximum(jnp.broadcast_to(jnp.max(s, axis=1, keepdims=True), sink.shape), sink)
    p = jnp.exp(s - _wide(m, n))
    den = _row_sum(p) + jnp.exp(sink - m)
    return _dot((p * _wide(1.0 / den, n)).astype(bf16), vb)


def _swa_window_bias(has_prev):
    w = WINDOW
    t = lax.broadcasted_iota(jnp.int32, (w, 2 * w), 0)
    jj = lax.broadcasted_iota(jnp.int32, (w, 2 * w), 1)
    bias = jnp.where((jj > t) & (jj <= t + w) & (has_prev | (jj >= w)), 0.0, -jnp.inf)
    return jnp.concatenate([bias] * B_GROUP, axis=0)


def _swa_window_block(q_groups, kb, vb, sink_ref, bias):
    w = WINDOW
    lane_head = _idiv(lax.broadcasted_iota(jnp.int32, (1, B_KV), 1), B_HD)
    outs = [jnp.zeros((w, B_KV), f32) for _ in range(B_GROUP)]
    for h in range(B_KV_HEADS):
        hm = lane_head == h
        qscale = jnp.where(hm, B_HD ** -0.5, 0.0)
        qh = jnp.concatenate([q * qscale for q in q_groups], axis=0).astype(bf16)
        sink = jnp.concatenate([jnp.full((w, LANES), sink_ref[h * B_GROUP + g], f32) for g in range(B_GROUP)], axis=0)
        r = _swa_softmax_pv(_dot_nt(qh, kb) + bias, sink, vb)
        for g in range(B_GROUP):
            outs[g] = jnp.where(hm, r[g * w:(g + 1) * w, :], outs[g])
    return outs


def _swa_prompt_kernel(sink_ref, q_ref, ko_ref, vo_ref, kp_ref, vp_ref, a_ref):
    w = WINDOW
    bias_first = _swa_window_bias(pl.program_id(1) > 0)
    bias_rest = _swa_window_bias(True)
    for nb in range(SWA_TILE // w):
        r0 = nb * w
        if nb == 0:
            k_prev, v_prev, bias = kp_ref[...], vp_ref[...], bias_first
        else:
            k_prev, v_prev, bias = ko_ref[r0 - w:r0, :], vo_ref[r0 - w:r0, :], bias_rest
        kb = jnp.concatenate([k_prev, ko_ref[r0:r0 + w, :]], axis=0).astype(bf16)
        vb = jnp.concatenate([v_prev, vo_ref[r0:r0 + w, :]], axis=0).astype(bf16)
        outs = _swa_window_block([q_ref[r0:r0 + w, g * B_KV:(g + 1) * B_KV] for g in range(B_GROUP)],
                                 kb, vb, sink_ref, bias)
        for g in range(B_GROUP):
            a_ref[r0:r0 + w, g * B_KV:(g + 1) * B_KV] = outs[g].astype(bf16)


def _swa_prompt(proj, sinks):
    nt = SEQ // SWA_TILE
    w = WINDOW
    per = SWA_TILE // w
    kcol = B_Q // B_KV
    vcol = kcol + 1
    tile = lambda b, n: b * nt + n
    prev = lambda b, n: jnp.maximum(tile(b, n) * per - 1, 0)
    return pl.pallas_call(
        _swa_prompt_kernel,
        grid=(BATCH, nt),
        in_specs=[pl.BlockSpec(memory_space=pltpu.SMEM),
                  pl.BlockSpec((SWA_TILE, B_Q), lambda b, n: (tile(b, n), 0)),
                  pl.BlockSpec((SWA_TILE, B_KV), lambda b, n: (tile(b, n), kcol)),
                  pl.BlockSpec((SWA_TILE, B_KV), lambda b, n: (tile(b, n), vcol)),
                  pl.BlockSpec((w, B_KV), lambda b, n: (prev(b, n), kcol)),
                  pl.BlockSpec((w, B_KV), lambda b, n: (prev(b, n), vcol))],
        out_specs=pl.BlockSpec((SWA_TILE, B_Q), lambda b, n: (tile(b, n), 0)),
        out_shape=jax.ShapeDtypeStruct((N_P, B_Q), bf16),
        compiler_params=_params("parallel", "arbitrary"),
        name="swa_prompt",
    )(sinks, proj, proj, proj, proj, proj)


def _swa_sample_kernel(sink_ref, q_ref, kn_ref, vn_ref, kc_ref, vc_ref, o_ref, ko_ref, vo_ref):
    s_len = DEC_SEQ
    buf = WINDOW
    rows = B_HEADS * s_len
    keys = 2 * buf
    ri = lax.broadcasted_iota(jnp.int32, (rows, keys), 0)
    jj = lax.broadcasted_iota(jnp.int32, (rows, keys), 1)
    t = _imod(ri, s_len)
    bias = jnp.where(((jj < buf) & (jj > t)) | ((jj >= buf) & (jj - buf <= t)), 0.0, -jnp.inf)
    lane_head = _idiv(lax.broadcasted_iota(jnp.int32, (1, B_KV), 1), B_HD)
    qscale = [jnp.where(lane_head == h, B_HD ** -0.5, 0.0) for h in range(B_KV_HEADS)]
    sink = jnp.concatenate([jnp.full((s_len, LANES), sink_ref[i], f32) for i in range(B_HEADS)], axis=0)
    pad = jnp.zeros((keys - buf - s_len, B_KV), f32)

    def body(e, carry):
        rs = pl.ds(pl.multiple_of(e * s_len, s_len), s_len)
        kn = kn_ref[rs, :]
        vn = vn_ref[rs, :]
        kc = kc_ref[e]
        vc = vc_ref[e]
        kpad = jnp.concatenate([kc, kn, pad], axis=0).astype(bf16)
        vpad = jnp.concatenate([vc, vn, pad], axis=0).astype(bf16)
        ko_ref[e, 0:buf - s_len, :] = kc[s_len:, :]
        ko_ref[e, buf - s_len:, :] = kn
        vo_ref[e, 0:buf - s_len, :] = vc[s_len:, :]
        vo_ref[e, buf - s_len:, :] = vn
        qe = q_ref[rs, :]
        qbig = jnp.concatenate([qe[:, g * B_KV:(g + 1) * B_KV] * qscale[h]
                                for h in range(B_KV_HEADS) for g in range(B_GROUP)], axis=0).astype(bf16)
        r = _swa_softmax_pv(_dot_nt(qbig, kpad) + bias, sink, vpad)
        for g in range(B_GROUP):
            og = jnp.zeros((s_len, B_KV), f32)
            for h in range(B_KV_HEADS):
                blk = (h * B_GROUP + g) * s_len
                og = jnp.where(lane_head == h, r[blk:blk + s_len, :], og)
            o_ref[rs, g * B_KV:(g + 1) * B_KV] = og.astype(bf16)
        return carry

    lax.fori_loop(0, SWA_SB, body, 0, unroll=True)


def _swa_sample(proj, sinks, k_cache, v_cache):
    t = SWA_SB * DEC_SEQ
    kcol = B_Q // B_KV
    return pl.pallas_call(
        _swa_sample_kernel,
        grid=(DEC_BATCH // SWA_SB,),
        in_specs=[pl.BlockSpec(memory_space=pltpu.SMEM),
                  pl.BlockSpec((t, B_Q), lambda i: (i, 0)),
                  pl.BlockSpec((t, B_KV), lambda i: (i, kcol)),
                  pl.BlockSpec((t, B_KV), lambda i: (i, kcol + 1)),
                  pl.BlockSpec((SWA_SB, WINDOW, B_KV), lambda i: (i, 0, 0)),
                  pl.BlockSpec((SWA_SB, WINDOW, B_KV), lambda i: (i, 0, 0))],
        out_specs=[pl.BlockSpec((t, B_Q), lambda i: (i, 0)),
                   pl.BlockSpec((SWA_SB, WINDOW, B_KV), lambda i: (i, 0, 0)),
                   pl.BlockSpec((SWA_SB, WINDOW, B_KV), lambda i: (i, 0, 0))],
        out_shape=[jax.ShapeDtypeStruct((N_S, B_Q), bf16),
                   jax.ShapeDtypeStruct((DEC_BATCH, WINDOW, B_KV), f32),
                   jax.ShapeDtypeStruct((DEC_BATCH, WINDOW, B_KV), f32)],
        compiler_params=_params("parallel"),
        name="swa_sample",
    )(sinks, proj, proj, proj, k_cache, v_cache)


def _rg_conv_group(x8, p8, wc_ref, bc_ref):
    row = lax.broadcasted_iota(jnp.int32, x8.shape, 0)
    u = bc_ref[...] + wc_ref[CONV_W - 1] * x8
    for d in range(1, CONV_W):
        sh = pltpu.roll(jnp.where(row >= SUBLANES - d, p8, x8), d, 0)
        u = u + wc_ref[CONV_W - 1 - d] * sh
    return u


def _rg_scan_group(a8, b8, h_prev8):
    first = lax.broadcasted_iota(jnp.int32, a8.shape, 0) == 0
    b = b8 + jnp.where(first, a8 * pltpu.roll(h_prev8, 1, 0), 0.0)
    a = jnp.where(first, 0.0, a8)
    for d in (1, 2, 4):
        b = a * pltpu.roll(b, d, 0) + b
        if d < SUBLANES // 2:
            a = a * pltpu.roll(a, d, 0)
    return b


def _rg_gate_rows(ra, rx, neg_rate, u):
    z = _sigmoid(ra) * neg_rate
    a = jnp.exp(-z)
    return a, _sqrt_nonneg(jnp.tanh(z) * (a * a + 1.0)) * _sigmoid(rx) * u


def _rg_gates(u, gate, wa_ref, ba_ref, wx_ref, bx_ref, lam_ref):
    ub = u.astype(bf16)
    ra = jnp.concatenate([_dot(ub[:, n * C_BW:(n + 1) * C_BW], wa_ref[n]) for n in range(C_BLOCKS)], axis=1)
    rx = jnp.concatenate([_dot(ub[:, n * C_BW:(n + 1) * C_BW], wx_ref[n]) for n in range(C_BLOCKS)], axis=1)
    a, bterm = _rg_gate_rows(ra + ba_ref[0:1, :], rx + bx_ref[0:1, :], LRU_C * _softplus(-lam_ref[0:1, :]), u)
    return a, bterm, _gelu_tanh(gate)


def _rg_prompt_kernel(p_ref, wc_ref, bc_ref, wa_ref, ba_ref, wx_ref, bx_ref, lam_ref,
                      y_ref, h_ref, u_s, a_s, b_s, xc_s, hc_s):
    j = pl.program_id(1)
    ng = RG_T // SUBLANES
    out_rows = 2 * SUBLANES

    @pl.when(j == 0)
    def _():
        xc_s[...] = jnp.zeros_like(xc_s)
        hc_s[...] = jnp.zeros_like(hc_s)

    def group(gidx, n=SUBLANES):
        return pl.ds(pl.multiple_of(gidx * n, n), n)

    def conv_body(gidx, p8):
        x8 = p_ref[group(gidx), 0:C_WIDTH]
        u_s[group(gidx), :] = _rg_conv_group(x8, p8, wc_ref, bc_ref)
        return x8

    xc_s[...] = lax.fori_loop(0, ng, conv_body, xc_s[...], unroll=2)
    ub = u_s[...].astype(bf16)
    for n in range(C_BLOCKS):
        cols = slice(n * C_BW, (n + 1) * C_BW)
        a_s[:, cols] = _dot(ub[:, cols], wa_ref[n])
        b_s[:, cols] = _dot(ub[:, cols], wx_ref[n])
    neg_rate = LRU_C * _softplus(-lam_ref[...])

    def gate_body(gidx, carry):
        rows = group(gidx)
        a, bterm = _rg_gate_rows(a_s[rows, :] + ba_ref[...], b_s[rows, :] + bx_ref[...], neg_rate, u_s[rows, :])
        a_s[rows, :] = a
        b_s[rows, :] = bterm
        return carry

    lax.fori_loop(0, ng, gate_body, 0, unroll=4)

    def scan_body(gidx, carry):
        rows = group(gidx, out_rows)
        lo = pl.ds(pl.multiple_of(gidx * out_rows, out_rows), SUBLANES)
        hi = pl.ds(pl.multiple_of(gidx * out_rows + SUBLANES, SUBLANES), SUBLANES)
        h_lo = _rg_scan_group(a_s[lo, :], b_s[lo, :], carry)
        h_hi = _rg_scan_group(a_s[hi, :], b_s[hi, :], h_lo)
        h16 = jnp.concatenate([h_lo, h_hi], axis=0)
        y_ref[rows, :] = (h16 * _gelu_tanh(p_ref[rows, C_WIDTH:])).astype(bf16)
        return h_hi

    h_last = lax.fori_loop(0, RG_T // out_rows, scan_body, hc_s[...])
    hc_s[...] = h_last

    @pl.when(j == pl.num_programs(1) - 1)
    def _():
        h_ref[0] = h_last[SUBLANES - 1:SUBLANES, :]


def _rg_weight_specs():
    rows = (SUBLANES, C_WIDTH)
    return [_const_spec((CONV_W,) + rows), _const_spec(rows),
            _const_spec((C_BLOCKS, C_BW, C_BW)), _const_spec(rows),
            _const_spec((C_BLOCKS, C_BW, C_BW)), _const_spec(rows),
            _const_spec(rows)]


def _rg_prompt(proj, weights):
    nt = SEQ // RG_T
    return pl.pallas_call(
        _rg_prompt_kernel,
        grid=(BATCH, nt),
        in_specs=[pl.BlockSpec((RG_T, 2 * C_WIDTH), lambda b, j: (b * nt + j, 0))] + _rg_weight_specs(),
        out_specs=[pl.BlockSpec((RG_T, C_WIDTH), lambda b, j: (b * nt + j, 0)),
                   pl.BlockSpec((1, 1, C_WIDTH), lambda b, j: (b, 0, 0))],
        out_shape=[jax.ShapeDtypeStruct((N_P, C_WIDTH), bf16),
                   jax.ShapeDtypeStruct((BATCH, 1, C_WIDTH), f32)],
        scratch_shapes=[pltpu.VMEM((RG_T, C_WIDTH), f32),
                        pltpu.VMEM((RG_T, C_WIDTH), f32),
                        pltpu.VMEM((RG_T, C_WIDTH), f32),
                        pltpu.VMEM((SUBLANES, C_WIDTH), f32),
                        pltpu.VMEM((SUBLANES, C_WIDTH), f32)],
        compiler_params=_params("parallel", "arbitrary"),
        name="rg_prompt",
    )(proj, *weights)


def _rg_sample_kernel(p_ref, cv_ref, h0_ref, wc_ref, bc_ref, wa_ref, ba_ref, wx_ref, bx_ref, lam_ref,
                      y_ref, h_ref, u_s, a_s, b_s):
    def conv_body(gidx, carry):
        rows = pl.ds(pl.multiple_of(gidx * SUBLANES, SUBLANES), SUBLANES)
        u_s[rows, :] = _rg_conv_group(p_ref[rows, 0:C_WIDTH], cv_ref[rows, :], wc_ref, bc_ref)
        return carry

    lax.fori_loop(0, DEC_BATCH, conv_body, 0)
    a, bterm, gg = _rg_gates(u_s[...], p_ref[:, C_WIDTH:], wa_ref, ba_ref, wx_ref, bx_ref, lam_ref)
    a_s[...] = a
    b_s[...] = bterm

    def scan_body(gidx, carry):
        rows = pl.ds(pl.multiple_of(gidx * SUBLANES, SUBLANES), SUBLANES)
        h0 = jnp.broadcast_to(h0_ref[pl.ds(gidx, 1), :], (SUBLANES, C_WIDTH))
        h8 = _rg_scan_group(a_s[rows, :], b_s[rows, :], h0)
        u_s[rows, :] = h8
        h_ref[pl.ds(gidx, 1), :] = h8[SUBLANES - 1:SUBLANES, :]
        return carry

    lax.fori_loop(0, DEC_BATCH, scan_body, 0)
    y_ref[...] = (u_s[...] * gg).astype(bf16)


def _rg_sample(proj, conv_pad, h0, weights):
    return pl.pallas_call(
        _rg_sample_kernel,
        grid=(1,),
        in_specs=[_const_spec((N_S, 2 * C_WIDTH)),
                  _const_spec((N_S, C_WIDTH)),
                  _const_spec((DEC_BATCH, C_WIDTH))] + _rg_weight_specs(),
        out_specs=[_const_spec((N_S, C_WIDTH)), _const_spec((DEC_BATCH, C_WIDTH))],
        out_shape=[jax.ShapeDtypeStruct((N_S, C_WIDTH), bf16),
                   jax.ShapeDtypeStruct((DEC_BATCH, C_WIDTH), f32)],
        scratch_shapes=[pltpu.VMEM((N_S, C_WIDTH), f32),
                        pltpu.VMEM((N_S, C_WIDTH), f32),
                        pltpu.VMEM((N_S, C_WIDTH), f32)],
        compiler_params=_params("arbitrary"),
        name="rg_sample",
    )(proj, conv_pad, h0, *weights)


def _mlp_tile(x_ref, a_ref, o_ref, wo_ref, bo_ref, g_ref, wup_ref, wdn_ref, gf_ref, x1_s, xn_s, final):
    x1_s[...] = x_ref[...] + _dot(a_ref[...], wo_ref[...]) + bo_ref[...]
    xn_s[...] = _rms_bf16(x1_s[...], g_ref[...])
    for c in range(0, D_FF, FF_CHUNK):
        hmid = jnp.maximum(_dot(xn_s[...], wup_ref[:, c:c + FF_CHUNK]), 0.0)
        x1_s[...] += _dot((hmid * hmid).astype(bf16), wdn_ref[c:c + FF_CHUNK, :])
    if final:
        x1 = x1_s[...]
        y = x1 * lax.rsqrt(jnp.mean(x1 * x1, axis=-1, keepdims=True) + NORM_EPS)
        o_ref[...] = y * gf_ref[...]
    else:
        o_ref[...] = x1_s[...]


def _out_mlp_kernel(xp_ref, xs_ref, ap_ref, as_ref, wo_ref, bo_ref, g_ref, wup_ref, wdn_ref, gf_ref, *rest,
                    final, cast_next):
    if cast_next:
        nup_ref, ndn_ref, op_ref, os_ref, nup_out, ndn_out, x1_s, xn_s = rest
    else:
        op_ref, os_ref, x1_s, xn_s = rest

    def body(x_ref, a_ref, o_ref):
        _mlp_tile(x_ref, a_ref, o_ref, wo_ref, bo_ref, g_ref, wup_ref, wdn_ref, gf_ref, x1_s, xn_s, final)

    i = pl.program_id(0)

    @pl.when(i < NP_TILES)
    def _():
        body(xp_ref, ap_ref, op_ref)
        if cast_next:
            nup_out[...] = nup_ref[...].astype(bf16)
            ndn_out[...] = ndn_ref[...].astype(bf16)

    @pl.when(i >= NP_TILES)
    def _():
        body(xs_ref, as_ref, os_ref)


def _out_mlp(x_p, x_s, a_p, a_s, wo, bo, g, w_up, w_down, g_final, final, next_weights=None):
    single = pl.Buffered(1)
    wspec = lambda shape: pl.BlockSpec(shape, lambda i: (0, 0), pipeline_mode=single)
    in_specs = [_p_spec(D_MODEL), _s_spec(D_MODEL), _p_spec(D_MODEL), _s_spec(D_MODEL),
                wspec((D_MODEL, D_MODEL)), wspec((1, D_MODEL)), wspec((1, D_MODEL)),
                wspec((D_MODEL, D_FF)), wspec((D_FF, D_MODEL)), wspec((1, D_MODEL))]
    out_specs = [_p_spec(D_MODEL), _s_spec(D_MODEL)]
    out_shape = [jax.ShapeDtypeStruct((N_P, D_MODEL), f32), jax.ShapeDtypeStruct((N_S, D_MODEL), f32)]
    args = [x_p, x_s, a_p, a_s, wo, bo, g, w_up, w_down, g_final]
    if next_weights is not None:
        extra = _mlp_weight_cast_specs(next_weights)
        in_specs, out_specs, out_shape, args = (in_specs + extra[0], out_specs + extra[1], out_shape + extra[2],
                                                args + extra[3])
    return pl.pallas_call(
        functools.partial(_out_mlp_kernel, final=final, cast_next=next_weights is not None),
        grid=(NP_TILES + NS_TILES,),
        in_specs=in_specs,
        out_specs=out_specs,
        out_shape=out_shape,
        scratch_shapes=[pltpu.VMEM((TM, D_MODEL), f32),
                        pltpu.VMEM((TM, D_MODEL), bf16)],
        compiler_params=_params("arbitrary"),
        name="out_mlp",
    )(*args)


def _rope_tables():
    half = ROPE_DIM // 2
    inv = np.float32(ROPE_THETA) ** (-np.arange(0, ROPE_DIM, 2, dtype=np.float32) / np.float32(ROPE_DIM))
    lane = np.arange(LANES) % B_HD
    inv_lane = np.where(lane < ROPE_DIM, inv[lane % half], np.float32(0.0)).astype(np.float32)
    pos = np.concatenate([np.arange(SEQ), PAST_LEN + np.arange(TM) % DEC_SEQ]).astype(np.float32)
    ang = pos[:, None] * inv_lane[None, :]
    tab = np.concatenate([np.cos(ang), np.sin(ang)], axis=1).astype(np.float32)
    consts = np.zeros((SUBLANES, LANES), np.float32)
    consts[0] = np.where(lane < half, -1.0, 0.0)
    consts[1] = np.where((lane >= half) & (lane < ROPE_DIM), 1.0, 0.0)
    return jnp.asarray(tab), jnp.asarray(consts)


def _q_cols_to_group_major(w):
    lead = w.shape[:-1]
    return jnp.swapaxes(w.reshape(lead + (B_KV_HEADS, B_GROUP, B_HD)), -3, -2).reshape(lead + (B_Q,))


def _last_rows(arr, n_seq, seq_len, n_rows, col0, col1):
    return jnp.stack([arr[(s + 1) * seq_len - n_rows:(s + 1) * seq_len, col0:col1] for s in range(n_seq)])


def kernel(x_prompt, x_sample, state_mlstm_c, state_mlstm_n, state_mlstm_m, cache_swa_k, cache_swa_v,
           state_rglru_h, state_rglru_conv, norm_mix, norm_mlp, norm_final, w_mlp_up, w_mlp_down,
           w_mlstm_in, b_mlstm_i, b_mlstm_f, g_mlstm_head, w_mlstm_out, w_swa_qkv, b_swa_qkv, swa_sinks,
           w_swa_out, b_swa_out, w_rg_in, w_rg_conv, b_rg_conv, w_rg_a, b_rg_a, w_rg_x, b_rg_x, rg_lambda,
           w_rg_out):
    assert N_A == 2
    x_p = x_prompt.reshape(N_P, D_MODEL)
    x_s = x_sample.reshape(N_S, D_MODEL)
    zero_bias = jnp.zeros((1, D_MODEL), f32)
    row = lambda v: v.reshape(1, -1).astype(f32)
    w_up = w_down = None
    state_c = state_mlstm_c.astype(f32)
    n_gate = 2 * A_HEADS
    assert n_gate == SUBLANES
    w_mlstm_main = w_mlstm_in[:, :, :A_MAIN].astype(bf16)
    w_gate_cols = w_mlstm_in[:, :, A_MAIN:]
    w_mlstm_gate = jnp.pad(w_gate_cols, ((0, 0), (0, 0), (0, LANES - n_gate))).astype(bf16)
    outs = {k: [] for k in ("c_p", "n_p", "m_p", "n_s", "m_s", "k_p", "v_p", "k_s", "v_s",
                            "h_p", "cv_p", "h_s", "cv_s")}
    deferred = None
    c_s_all = None
    for i in range(DEPTH):
        kind, j = i % N_MIXERS, i // N_MIXERS
        g_mix = row(norm_mix[i])
        if kind == 0:
            b_gate = jnp.concatenate([b_mlstm_i[j], b_mlstm_f[j]]).astype(f32)
            res = _proj_mlstm(x_p, x_s, g_mix, w_mlstm_main, w_mlstm_gate,
                              jnp.pad(b_gate, (0, LANES - n_gate)).reshape(1, LANES), j,
                              next_weights=(w_mlp_up, w_mlp_down, 0) if i == 0 else None)
            main_p, main_s, gates_p, gates_s, gates_p_t, gates_s_t = res[:6]
            if i == 0:
                w_up, w_down = res[6], res[7]
            g_head = g_mlstm_head[j].astype(f32)
            a_p, c_p, n_p, mt_p = _mlstm_prompt(main_p.reshape(BATCH, SEQ, A_MAIN),
                                                gates_p.reshape(BATCH, SEQ, LANES), gates_p_t, g_head)
            a_p = a_p.reshape(N_P, A_HEADS * A_DV)
            mt_p = mt_p.reshape(N_P, LANES)
            m0_tok = jnp.pad(jnp.repeat(state_mlstm_m[j].astype(f32), DEC_SEQ, axis=0),
                             ((0, 0), (0, LANES - A_HEADS)))
            n0 = state_mlstm_n[j].astype(f32)
            if j == 0:
                a_s, n_s, mt_s, vdt0, kb0, sc0 = _mlstm_sample_first(main_s, gates_s, gates_s_t, m0_tok, state_c, n0,
                                                                     g_head)
                deferred = (vdt0, kb0, sc0)
            else:
                a_s, n_s, mt_s, c_s_all = _mlstm_sample_second(main_s, gates_s, gates_s_t, m0_tok, state_c, n0,
                                                               g_head, *deferred)
            outs["c_p"].append(c_p); outs["n_p"].append(n_p)
            outs["m_p"].append(_last_rows(mt_p, BATCH, SEQ, 1, 0, A_HEADS).reshape(BATCH, A_HEADS))
            outs["n_s"].append(n_s)
            outs["m_s"].append(mt_s[DEC_SEQ - 1::DEC_SEQ, :A_HEADS])
            wo, bo = w_mlstm_out[j], zero_bias
        elif kind == 1:
            rope_tab, rope_consts = _rope_tables()
            w_qkv, b_qkv = w_swa_qkv[j], b_swa_qkv[j]
            w_qkv = jnp.concatenate([_q_cols_to_group_major(w_qkv[:, :B_Q]), w_qkv[:, B_Q:]], axis=1)
            b_qkv = jnp.concatenate([_q_cols_to_group_major(b_qkv[:B_Q]), b_qkv[B_Q:]])
            proj_p, proj_s = _proj_swa(x_p, x_s, g_mix, w_qkv.astype(bf16), row(b_qkv), rope_tab, rope_consts)
            sinks = swa_sinks[j].astype(f32)
            a_p = _swa_prompt(proj_p, sinks)
            buf = cache_swa_k.shape[2]
            a_s, k_s, v_s = _swa_sample(proj_s, sinks,
                                        cache_swa_k[j].astype(f32).reshape(DEC_BATCH, buf, B_KV),
                                        cache_swa_v[j].astype(f32).reshape(DEC_BATCH, buf, B_KV))
            outs["k_p"].append(_last_rows(proj_p, BATCH, SEQ, WINDOW, B_Q, B_Q + B_KV)
                               .reshape(BATCH, WINDOW, B_KV_HEADS, B_HD))
            outs["v_p"].append(_last_rows(proj_p, BATCH, SEQ, WINDOW, B_Q + B_KV, B_IN)
                               .reshape(BATCH, WINDOW, B_KV_HEADS, B_HD))
            outs["k_s"].append(k_s.reshape(DEC_BATCH, buf, B_KV_HEADS, B_HD))
            outs["v_s"].append(v_s.reshape(DEC_BATCH, buf, B_KV_HEADS, B_HD))
            wo = jnp.swapaxes(w_swa_out[j].reshape(B_KV_HEADS, B_GROUP, B_HD, D_MODEL), 0, 1).reshape(B_Q, D_MODEL)
            bo = row(b_swa_out[j])
        else:
            proj_p, proj_s = _proj_rg(x_p, x_s, g_mix, w_rg_in[j].astype(bf16))
            rows8 = lambda v: jnp.broadcast_to(v.astype(f32)[..., None, :], v.shape[:-1] + (SUBLANES, C_WIDTH))
            weights = (rows8(w_rg_conv[j]), rows8(b_rg_conv[j]), w_rg_a[j].astype(bf16), rows8(b_rg_a[j]),
                       w_rg_x[j].astype(bf16), rows8(b_rg_x[j]), rows8(rg_lambda[j]))
            a_p, h_p = _rg_prompt(proj_p, weights)
            conv_pad = jnp.pad(state_rglru_conv[j].astype(f32),
                               ((0, 0), (SUBLANES - (CONV_W - 1), 0), (0, 0))).reshape(N_S, C_WIDTH)
            a_s, h_s = _rg_sample(proj_s, conv_pad, state_rglru_h[j].astype(f32), weights)
            outs["h_p"].append(h_p.reshape(BATCH, C_WIDTH))
            outs["cv_p"].append(_last_rows(proj_p, BATCH, SEQ, CONV_W - 1, 0, C_WIDTH))
            outs["h_s"].append(h_s)
            outs["cv_s"].append(proj_s[:, :C_WIDTH].reshape(DEC_BATCH, DEC_SEQ, C_WIDTH)[:, DEC_SEQ - (CONV_W - 1):])
            wo, bo = w_rg_out[j], zero_bias
        last = i == DEPTH - 1
        res = _out_mlp(x_p, x_s, a_p, a_s, wo.astype(bf16), bo, row(norm_mlp[i]), w_up, w_down, row(norm_final),
                       final=last, next_weights=None if last else (w_mlp_up, w_mlp_down, i + 1))
        x_p, x_s = res[0], res[1]
        if not last:
            w_up, w_down = res[2], res[3]
    st = {k: jnp.stack(v) for k, v in outs.items()}
    y_p = x_p.reshape(BATCH, SEQ, D_MODEL)
    y_s = x_s.reshape(DEC_BATCH, DEC_SEQ, D_MODEL)
    return (y_p, y_s, st["c_p"], st["n_p"], st["m_p"], c_s_all, st["n_s"], st["m_s"],
            st["k_p"], st["v_p"], st["k_s"], st["v_s"], st["h_p"], st["cv_p"], st["h_s"], st["cv_s"])
```

```python
import functools
import math

import jax
import jax.numpy as jnp
import numpy as np
from jax import lax
from jax.experimental import pallas as pl
from jax.experimental.pallas import tpu as pltpu

f32 = jnp.float32
bf16 = jnp.bfloat16

D_MODEL = 1024
BATCH = 2
SEQ = 8192
DEPTH = 4
DEC_BATCH = 128
DEC_SEQ = 8
PAST_LEN = 8192
N_MIXERS = 3
NORM_EPS = 1e-6

A_HEADS = 4
A_DK = 128
A_DV = 256
A_QK = A_HEADS * A_DK
A_MAIN = 2 * A_QK + 2 * A_HEADS * A_DV
N_A = (DEPTH + 2) // 3

B_HEADS = 16
B_KV_HEADS = 4
B_HD = 64
B_GROUP = 4
B_Q = B_HEADS * B_HD
B_KV = B_KV_HEADS * B_HD
B_IN = B_Q + 2 * B_KV
WINDOW = 128
ROPE_THETA = 500000.0
ROPE_DIM = 16

C_WIDTH = 1024
C_BLOCKS = 4
C_BW = 256
CONV_W = 4
LRU_C = 8.0
D_FF = 4096

N_P = BATCH * SEQ
N_S = DEC_BATCH * DEC_SEQ

LANES = 128
SUBLANES = 8
VMEM_LIMIT = 56 * 1024 * 1024

TM = 512
NP_TILES = N_P // TM
NS_TILES = N_S // TM
MLSTM_CHUNK = 256
MLSTM_SB = 16
MLSTM_ST = MLSTM_SB * DEC_SEQ
MLSTM_UNROLL = 8
SWA_SB = 8
SWA_TILE = 512
RG_T = 512
FF_CHUNK = 512


def _dot(a, b):
    return jnp.dot(a, b, preferred_element_type=f32)


def _dot_nt(a, b):
    return lax.dot_general(a, b, (((1,), (1,)), ((), ())), preferred_element_type=f32)


def _split3(x):
    hi = x.astype(bf16)
    r1 = x - hi.astype(f32)
    mid = r1.astype(bf16)
    lo = (r1 - mid.astype(f32)).astype(bf16)
    return hi, mid, lo


def _mask_dot(mask, x):
    m = jnp.where(mask, 1.0, 0.0).astype(bf16)
    return sum(_dot(m, part) for part in _split3(x))


def _dot_mask(x, mask):
    m = jnp.where(mask, 1.0, 0.0).astype(bf16)
    return sum(_dot(part, m) for part in _split3(x))


def _idiv(x, d):
    assert d & (d - 1) == 0
    return x >> (d.bit_length() - 1)


def _imod(x, d):
    assert d & (d - 1) == 0
    return x & (d - 1)


def _rms_bf16(x, g):
    y = x * lax.rsqrt(jnp.mean(x * x, axis=-1, keepdims=True) + NORM_EPS)
    return (y * g).astype(bf16)


def _sigmoid(x):
    return 0.5 * jnp.tanh(0.5 * x) + 0.5


def _sqrt_nonneg(v):
    return jnp.where(v > 0.0, v * lax.rsqrt(v), 0.0)


def _softplus(x):
    return jnp.maximum(x, 0.0) + jnp.log1p(jnp.exp(-jnp.abs(x)))


def _gelu_tanh(x):
    return 0.5 * x * (1.0 + jnp.tanh(math.sqrt(2.0 / math.pi) * (x + 0.044715 * (x * x * x))))


def _params(*sem):
    return pltpu.CompilerParams(dimension_semantics=sem, vmem_limit_bytes=VMEM_LIMIT)


def _const_spec(shape):
    nd = len(shape)
    return pl.BlockSpec(shape, lambda *_: (0,) * nd)


def _p_spec(width):
    return pl.BlockSpec((TM, width), lambda i: (jnp.minimum(i, NP_TILES - 1), 0))


def _s_spec(width):
    return pl.BlockSpec((TM, width), lambda i: (jnp.maximum(i - NP_TILES, 0), 0))


def _for_each_group(body, p_refs, s_refs):
    i = pl.program_id(0)

    @pl.when(i < NP_TILES)
    def _():
        body(*p_refs)

    @pl.when(i >= NP_TILES)
    def _():
        body(*s_refs)


def _mlp_weight_cast_specs(next_weights):
    up_all, down_all, layer = next_weights
    step = lambda i: jnp.minimum(i, NP_TILES - 1)
    in_specs, out_specs, out_shape = [], [], []
    for rows, width in ((D_MODEL // NP_TILES, D_FF), (D_FF // NP_TILES, D_MODEL)):
        in_specs.append(pl.BlockSpec((None, rows, width), lambda i: (layer, step(i), 0)))
        out_specs.append(pl.BlockSpec((rows, width), lambda i: (step(i), 0)))
        out_shape.append(jax.ShapeDtypeStruct((rows * NP_TILES, width), bf16))
    return in_specs, out_specs, out_shape, [up_all, down_all]


def _proj_mlstm_kernel(xp_ref, xs_ref, g_ref, w_ref, wg_ref, bg_ref, *rest, cast_next):
    if cast_next:
        nup_ref, ndn_ref, mp_ref, ms_ref, gp_ref, gs_ref, gtp_ref, gts_ref, nup_out, ndn_out = rest
    else:
        mp_ref, ms_ref, gp_ref, gs_ref, gtp_ref, gts_ref = rest

    def body(x_ref, main_ref, gate_ref, gate_t_ref):
        xn = _rms_bf16(x_ref[...], g_ref[...])
        for c in range(0, A_MAIN, 512):
            main_ref[:, c:c + 512] = _dot(xn, w_ref[:, c:c + 512])
        gp = _dot(xn, wg_ref[...]) + bg_ref[...]
        lane = lax.broadcasted_iota(jnp.int32, gp.shape, 1)
        gates = jnp.where(lane >= A_HEADS, -_softplus(-gp), gp)
        gate_ref[...] = gates
        gate_t_ref[...] = gates.T[0:SUBLANES, :]

    i = pl.program_id(0)

    @pl.when(i < NP_TILES)
    def _():
        body(xp_ref, mp_ref, gp_ref, gtp_ref)
        if cast_next:
            nup_out[...] = nup_ref[...].astype(bf16)
            ndn_out[...] = ndn_ref[...].astype(bf16)

    @pl.when(i >= NP_TILES)
    def _():
        body(xs_ref, ms_ref, gs_ref, gts_ref)


def _proj_mlstm(x_p, x_s, g, w_main_all, w_gate_all, b_gate, layer, next_weights=None):
    lspec = lambda shape: pl.BlockSpec((None,) + shape, lambda i: (layer, 0, 0))
    in_specs = [_p_spec(D_MODEL), _s_spec(D_MODEL),
                _const_spec((1, D_MODEL)),
                lspec((D_MODEL, A_MAIN)),
                lspec((D_MODEL, LANES)),
                _const_spec((1, LANES))]
    out_specs = [_p_spec(A_MAIN), _s_spec(A_MAIN), _p_spec(LANES), _s_spec(LANES),
                 pl.BlockSpec((SUBLANES, TM), lambda i: (0, jnp.minimum(i, NP_TILES - 1))),
                 pl.BlockSpec((SUBLANES, TM), lambda i: (0, jnp.maximum(i - NP_TILES, 0)))]
    out_shape = [jax.ShapeDtypeStruct((N_P, A_MAIN), f32), jax.ShapeDtypeStruct((N_S, A_MAIN), f32),
                 jax.ShapeDtypeStruct((N_P, LANES), f32), jax.ShapeDtypeStruct((N_S, LANES), f32),
                 jax.ShapeDtypeStruct((SUBLANES, N_P), f32), jax.ShapeDtypeStruct((SUBLANES, N_S), f32)]
    args = [x_p, x_s, g, w_main_all, w_gate_all, b_gate]
    if next_weights is not None:
        extra = _mlp_weight_cast_specs(next_weights)
        in_specs, out_specs, out_shape, args = (in_specs + extra[0], out_specs + extra[1], out_shape + extra[2],
                                                args + extra[3])
    return pl.pallas_call(
        functools.partial(_proj_mlstm_kernel, cast_next=next_weights is not None),
        grid=(NP_TILES + NS_TILES,),
        in_specs=in_specs,
        out_specs=out_specs,
        out_shape=out_shape,
        compiler_params=_params("arbitrary"),
        name="proj_mlstm",
    )(*args)


def _proj_swa_kernel(xp_ref, xs_ref, g_ref, w_ref, b_ref, tab_ref, rc_ref, op_ref, os_ref):
    half = ROPE_DIM // 2

    def body(x_ref, o_ref):
        xn = _rms_bf16(x_ref[...], g_ref[...])
        cos = tab_ref[:, 0:LANES]
        sin = tab_ref[:, LANES:2 * LANES]
        sin_lo = sin * rc_ref[0:1, :]
        sin_hi = sin * rc_ref[1:2, :]
        for c in range(0, B_IN, 512):
            p = _dot(xn, w_ref[:, c:c + 512]) + b_ref[:, c:c + 512]
            for l in range(0, 512, LANES):
                pl_ = p[:, l:l + LANES]
                if c + l < B_Q + B_KV:
                    pl_ = (pl_ * cos + pltpu.roll(pl_, LANES - half, 1) * sin_lo
                           + pltpu.roll(pl_, half, 1) * sin_hi)
                o_ref[:, c + l:c + l + LANES] = pl_

    _for_each_group(body, (xp_ref, op_ref), (xs_ref, os_ref))


def _proj_swa(x_p, x_s, g, w, b, rope_tab, rope_consts):
    seq_tiles = SEQ // TM
    tab_idx = lambda i: (jnp.where(i < NP_TILES, i % seq_tiles, seq_tiles), 0)
    return pl.pallas_call(
        _proj_swa_kernel,
        grid=(NP_TILES + NS_TILES,),
        in_specs=[_p_spec(D_MODEL), _s_spec(D_MODEL),
                  _const_spec((1, D_MODEL)),
                  _const_spec((D_MODEL, B_IN)),
                  _const_spec((1, B_IN)),
                  pl.BlockSpec((TM, 2 * LANES), tab_idx),
                  _const_spec((SUBLANES, LANES))],
        out_specs=[_p_spec(B_IN), _s_spec(B_IN)],
        out_shape=[jax.ShapeDtypeStruct((N_P, B_IN), f32), jax.ShapeDtypeStruct((N_S, B_IN), f32)],
        compiler_params=_params("arbitrary"),
        name="proj_swa",
    )(x_p, x_s, g, w, b, rope_tab, rope_consts)


def _proj_rg_kernel(xp_ref, xs_ref, g_ref, w_ref, op_ref, os_ref):
    def body(x_ref, o_ref):
        xn = _rms_bf16(x_ref[...], g_ref[...])
        for c in range(0, 2 * C_WIDTH, 512):
            o_ref[:, c:c + 512] = _dot(xn, w_ref[:, c:c + 512])

    _for_each_group(body, (xp_ref, op_ref), (xs_ref, os_ref))


def _proj_rg(x_p, x_s, g, w):
    return pl.pallas_call(
        _proj_rg_kernel,
        grid=(NP_TILES + NS_TILES,),
        in_specs=[_p_spec(D_MODEL), _s_spec(D_MODEL),
                  _const_spec((1, D_MODEL)),
                  _const_spec((D_MODEL, 2 * C_WIDTH))],
        out_specs=[_p_spec(2 * C_WIDTH), _s_spec(2 * C_WIDTH)],
        out_shape=[jax.ShapeDtypeStruct((N_P, 2 * C_WIDTH), f32), jax.ShapeDtypeStruct((N_S, 2 * C_WIDTH), f32)],
        compiler_params=_params("arbitrary"),
        name="proj_rg",
    )(x_p, x_s, g, w)


def _mlstm_masks(t, seg):
    r = lax.broadcasted_iota(jnp.int32, (t, t), 0)
    c = lax.broadcasted_iota(jnp.int32, (t, t), 1)
    if seg == t:
        same = None
        lower, upper = r >= c, r <= c
    else:
        same = _idiv(r, seg) == _idiv(c, seg)
        lower, upper = same & (r >= c), same & (r <= c)
    return r, c, same, lower, upper


def _wide(col, n):
    return jnp.concatenate([col] * n, axis=1)


def _row_sum(x):
    acc = x[:, 0:LANES]
    for l in range(LANES, x.shape[1], LANES):
        acc = acc + x[:, l:l + LANES]
    return jnp.broadcast_to(jnp.sum(acc, axis=1, keepdims=True), acc.shape)


def _mlstm_cols(main_ref, h):
    q = main_ref[:, h * A_DK:(h + 1) * A_DK]
    k = main_ref[:, A_QK + h * A_DK:A_QK + (h + 1) * A_DK] * (A_DK ** -0.5)
    v = main_ref[:, 2 * A_QK + h * A_DV:2 * A_QK + (h + 1) * A_DV]
    o_pre = main_ref[:, 2 * A_QK + (A_HEADS + h) * A_DV:2 * A_QK + (A_HEADS + h + 1) * A_DV]
    return q, k, v, o_pre


def _mlstm_prompt_chunk(main_ref, gc_ref, gr_ref, gh_ref, hs_ref, c_ref, n_ref, mt_ref, c_s, n_s, m_s):
    t = MLSTM_CHUNK
    j = pl.program_id(0)

    @pl.when(j == 0)
    def _():
        c_s[...] = jnp.zeros_like(c_s)
        n_s[...] = jnp.zeros_like(n_s)
        m_s[...] = jnp.zeros_like(m_s)

    _, _, _, lower, upper = _mlstm_masks(t, t)
    gc = gc_ref[...]
    gr = gr_ref[...]
    btr_all = _dot_mask(gr, upper)
    lane = lax.broadcasted_iota(jnp.int32, (t, LANES), 1)
    mt_all = jnp.zeros((t, LANES), f32)

    def rep(col):
        return jnp.broadcast_to(col, (t, LANES))

    wide, row_sum = _wide, _row_sum

    heads = range(A_HEADS)
    stack = lambda xs: jnp.concatenate(list(xs), axis=0)
    head = lambda x, h: x[h * t:(h + 1) * t]
    cols = [_mlstm_cols(main_ref, h) for h in heads]
    qs, ks, vs = [c[0] for c in cols], [c[1] for c in cols], [c[2] for c in cols]
    qbs, kbs, vbs = [q.astype(bf16) for q in qs], [k.astype(bf16) for k in ks], [v.astype(bf16) for v in vs]
    cts = [c_s[h] for h in heads]
    ns = [n_s[h:h + 1, :] for h in heads]
    btrs = [btr_all[A_HEADS + h:A_HEADS + h + 1, :] for h in heads]
    itrs = [gr[h:h + 1, :] for h in heads]

    lf_rep = jnp.concatenate([rep(gc[:, A_HEADS + h:A_HEADS + h + 1]) for h in heads], axis=1)
    btc_all = _mask_dot(lower, lf_rep)
    btc = stack(btc_all[:, h * LANES:(h + 1) * LANES] for h in heads)
    itc = stack(rep(gc[:, h:h + 1]) for h in heads)
    m_in = stack(rep(m_s[0:1, h:h + 1]) for h in heads)
    dmat = stack(jnp.where(lower, wide(head(btc, h), t // LANES) + (itrs[h] - btrs[h]), -jnp.inf) for h in heads)
    inter = btc + m_in
    m_t = jnp.maximum(jnp.broadcast_to(jnp.max(dmat, axis=1, keepdims=True), inter.shape), inter)
    w = jnp.exp(dmat - wide(m_t, t // LANES))
    w_inter = jnp.exp(inter - m_t)
    s = stack(_dot_nt(qbs[h], kbs[h]) for h in heads) * w
    sb = s.astype(bf16)
    num = (stack(_dot(head(sb, h), vbs[h]) for h in heads)
           + wide(w_inter, A_DV // LANES) * stack(_dot(qbs[h], cts[h].astype(bf16)) for h in heads))
    den = row_sum(s) + w_inter * row_sum(stack(qs[h] * ns[h] for h in heads))
    inv = 1.0 / jnp.maximum(jnp.abs(den), jnp.exp(-m_t))
    norm = inv * lax.rsqrt(inv * inv * (row_sum(num * num) * (1.0 / A_DV)) + NORM_EPS)
    out = _sigmoid(stack(c[3] for c in cols)) * (num * wide(norm, A_DV // LANES))
    for h in heads:
        hs_ref[:, h * A_DV:(h + 1) * A_DV] = (head(out, h) * gh_ref[h:h + 1, :]).astype(bf16)

    m_news = [head(m_t, h)[t - 1:t, 0:1] for h in heads]
    last = lambda x: stack(rep(head(x, h)[t - 1:t, 0:1]) for h in heads)
    decay_c = jnp.exp(last(btc) - btc + itc - last(m_t))
    for h in heads:
        scale = jnp.exp(head(inter, h)[t - 1:t, 0:1] - m_news[h])
        decay_r = jnp.exp(btrs[h][:, t - 1:t] - btrs[h] + itrs[h] - m_news[h])
        c_s[h] = scale * cts[h] + _dot((ks[h].T * decay_r).astype(bf16), vbs[h])
        n_s[h:h + 1, :] = scale * ns[h] + jnp.sum(head(decay_c, h) * ks[h], axis=0, keepdims=True)
        mt_all = jnp.where(lane == h, head(m_t, h), mt_all)
    mt_ref[...] = mt_all
    m_s[0:1, :] = mt_all[t - 1:t, :]

    @pl.when(j == pl.num_programs(0) - 1)
    def _():
        for h in range(A_HEADS):
            c_ref[h] = c_s[h].T
        n_ref[...] = n_s[...]


def _mlstm_prompt_kernel(main_ref, gc_ref, *rest):
    gr_refs, (gh_ref, hs_ref, c_ref, n_ref, mt_ref, c_s, n_s, m_s) = rest[:BATCH], rest[BATCH:]
    for b in range(BATCH):
        _mlstm_prompt_chunk(main_ref.at[b], gc_ref.at[b], gr_refs[b], gh_ref, hs_ref.at[b], c_ref.at[b],
                            n_ref.at[b], mt_ref.at[b], c_s.at[b], n_s.at[b], m_s.at[b])


def _mlstm_prompt(main, gates, gates_t, g_head):
    t = MLSTM_CHUNK
    nc = SEQ // t
    return pl.pallas_call(
        _mlstm_prompt_kernel,
        grid=(nc,),
        in_specs=[pl.BlockSpec((BATCH, t, A_MAIN), lambda j: (0, j, 0)),
                  pl.BlockSpec((BATCH, t, LANES), lambda j: (0, j, 0))]
                 + [pl.BlockSpec((SUBLANES, t), functools.partial(lambda j, b: (0, b * nc + j), b=b))
                    for b in range(BATCH)]
                 + [_const_spec((A_HEADS, A_DV))],
        out_specs=[pl.BlockSpec((BATCH, t, A_HEADS * A_DV), lambda j: (0, j, 0)),
                   _const_spec((BATCH, A_HEADS, A_DV, A_DK)),
                   _const_spec((BATCH, A_HEADS, A_DK)),
                   pl.BlockSpec((BATCH, t, LANES), lambda j: (0, j, 0))],
        out_shape=[jax.ShapeDtypeStruct((BATCH, SEQ, A_HEADS * A_DV), bf16),
                   jax.ShapeDtypeStruct((BATCH, A_HEADS, A_DV, A_DK), f32),
                   jax.ShapeDtypeStruct((BATCH, A_HEADS, A_DK), f32),
                   jax.ShapeDtypeStruct((BATCH, SEQ, LANES), f32)],
        scratch_shapes=[pltpu.VMEM((BATCH, A_HEADS, A_DK, A_DV), f32),
                        pltpu.VMEM((BATCH, A_HEADS, A_DK), f32),
                        pltpu.VMEM((BATCH, SUBLANES, LANES), f32)],
        compiler_params=_params("arbitrary"),
        name="mlstm_prompt",
    )(main, gates, *([gates_t] * BATCH), g_head)


def _mlstm_sample_compute(main_ref, gc_ref, gr_ref, m0_ref, c0_ref, n0_ref, gh_ref,
                          hs_ref, n_ref, mt_ref, vdt_ref, kb_ref, sc_ref, acc_s, nt_s):
    t = MLSTM_ST
    seg = DEC_SEQ
    _, c, same, lower, upper = _mlstm_masks(t, seg)
    seg_last = same & (_imod(c, seg) == seg - 1)
    gc = gc_ref[...]
    gr = gr_ref[...]
    btr_all = _dot_mask(gr, upper)
    lane = lax.broadcasted_iota(jnp.int32, (t, LANES), 1)
    tok_seg = _idiv(lax.broadcasted_iota(jnp.int32, (A_DV, t), 1), seg)

    heads = range(A_HEADS)
    rep = lambda col: jnp.broadcast_to(col, (t, LANES))
    stack = lambda xs: jnp.concatenate(list(xs), axis=0)
    lanes = lambda xs: jnp.concatenate(list(xs), axis=1)
    head = lambda x, h: x[h * t:(h + 1) * t]
    cols = [_mlstm_cols(main_ref, h) for h in heads]
    qs, ks, vs = [c_[0] for c_ in cols], [c_[1] for c_ in cols], [c_[2] for c_ in cols]
    qbs, kbs, vbs = [q.astype(bf16) for q in qs], [k.astype(bf16) for k in ks], [v.astype(bf16) for v in vs]

    btc_all = _mask_dot(lower, lanes(rep(gc[:, A_HEADS + h:A_HEADS + h + 1]) for h in heads))
    btc = stack(btc_all[:, h * LANES:(h + 1) * LANES] for h in heads)
    itc = stack(rep(gc[:, h:h + 1]) for h in heads)
    m_in = stack(rep(m0_ref[:, h:h + 1]) for h in heads)
    dmat = stack(jnp.where(lower, head(btc, h) + (gr[h:h + 1, :] - btr_all[A_HEADS + h:A_HEADS + h + 1, :]), -jnp.inf)
                 for h in heads)
    inter = btc + m_in
    m_t = jnp.maximum(jnp.broadcast_to(jnp.max(dmat, axis=1, keepdims=True), inter.shape), inter)
    w = jnp.exp(dmat - m_t)
    w_inter = jnp.exp(inter - m_t)
    s = stack(_dot_nt(qbs[h], kbs[h]) for h in heads) * w
    sb = s.astype(bf16)
    num = stack(_dot(head(sb, h), vbs[h]) for h in heads)

    for h in heads:
        def inter_body(b, carry, h=h):
            rows = pl.ds(pl.multiple_of(h * t + b * seg, seg), seg)
            r = _dot_nt(c0_ref[b, h].astype(bf16), qbs[h])
            acc_s[h] = jnp.where(tok_seg == b, r, acc_s[h])
            nt_s[rows, :] = jnp.broadcast_to(n0_ref[b, h:h + 1, :], (seg, A_DK))
            return carry

        acc_s[h] = jnp.zeros((A_DV, t), f32)
        lax.fori_loop(0, MLSTM_SB, inter_body, 0, unroll=MLSTM_UNROLL)
    n_tok = nt_s[...]
    num = num + _wide(w_inter, A_DV // LANES) * stack(acc_s[h].T for h in heads)
    den = _row_sum(s) + w_inter * _row_sum(stack(qs) * n_tok)
    inv = 1.0 / jnp.maximum(jnp.abs(den), jnp.exp(-m_t))
    norm = inv * lax.rsqrt(inv * inv * (_row_sum(num * num) * (1.0 / A_DV)) + NORM_EPS)
    out = _sigmoid(stack(c_[3] for c_ in cols)) * (num * _wide(norm, A_DV // LANES))
    for h in heads:
        hs_ref[:, h * A_DV:(h + 1) * A_DV] = (head(out, h) * gh_ref[h:h + 1, :]).astype(bf16)

    last = _mask_dot(seg_last, lanes([head(x, h) for x in (m_t, btc, inter) for h in heads]))
    pick = lambda i: stack(last[:, (i * A_HEADS + h) * LANES:(i * A_HEADS + h + 1) * LANES] for h in heads)
    m_new, bt_last, inter_last = pick(0), pick(1), pick(2)
    decay = jnp.exp(bt_last - btc + itc - m_new)
    scale = jnp.exp(inter_last - m_new)
    seg_sum = _mask_dot(same, lanes(head(decay, h) * ks[h] for h in heads))
    n_new = scale * n_tok + stack(seg_sum[:, h * A_DK:(h + 1) * A_DK] for h in heads)
    mt_all = jnp.zeros((t, LANES), f32)
    sc_all = jnp.zeros((t, LANES), f32)
    for h in heads:
        vdt_ref[h] = (_wide(head(decay, h), A_DV // LANES) * vs[h]).T.astype(bf16)
        kb_ref[:, h * A_DK:(h + 1) * A_DK] = kbs[h]
        sc_all = jnp.where(lane == h, head(scale, h), sc_all)
        mt_all = jnp.where(lane == h, head(m_t, h), mt_all)
    for b in range(MLSTM_SB):
        n_ref[b] = stack(head(n_new, h)[b * seg:b * seg + 1, :] for h in heads)
    mt_ref[...] = mt_all
    sc_ref[...] = sc_all


def _mlstm_sample_update(c_in_ref, c_out_ref, vdt_ref, kb_ref, sc_ref):
    seg = DEC_SEQ
    lane_seg = _idiv(lax.broadcasted_iota(jnp.int32, (1, MLSTM_ST), 1), seg)
    for h in range(A_HEADS):
        def upd_body(b, carry, h=h):
            rows = pl.ds(pl.multiple_of(b * seg, seg), seg)
            scale = sc_ref[rows, :][0:1, h:h + 1]
            onehot = jnp.where(lane_seg == b, 1.0, 0.0).astype(bf16)
            upd = _dot(vdt_ref[h] * onehot, kb_ref[:, h * A_DK:(h + 1) * A_DK])
            c_out_ref[b, h] = scale * c_in_ref[b, h] + upd
            return carry

        lax.fori_loop(0, MLSTM_SB, upd_body, 0, unroll=MLSTM_UNROLL)


def _mlstm_sample_first_kernel(main_ref, gc_ref, gr_ref, m0_ref, c0_ref, n0_ref, gh_ref,
                               hs_ref, n_ref, mt_ref, vdt_ref, kb_ref, sc_ref, acc_s, nt_s):
    _mlstm_sample_compute(main_ref, gc_ref, gr_ref, m0_ref, c0_ref, n0_ref, gh_ref,
                          hs_ref, n_ref, mt_ref, vdt_ref, kb_ref, sc_ref, acc_s, nt_s)


def _mlstm_sample_second_kernel(main_ref, gc_ref, gr_ref, m0_ref, c0_ref, n0_ref, gh_ref, vdt0_ref, kb0_ref, sc0_ref,
                                hs_ref, n_ref, mt_ref, c_ref, vdt_s, kb_s, sc_s, acc_s, nt_s):
    layer = pl.program_id(0)

    @pl.when(layer == 0)
    def _():
        _mlstm_sample_update(c0_ref, c_ref, vdt0_ref, kb0_ref, sc0_ref)

    @pl.when(layer == 1)
    def _():
        _mlstm_sample_compute(main_ref, gc_ref, gr_ref, m0_ref, c0_ref, n0_ref, gh_ref,
                              hs_ref, n_ref, mt_ref, vdt_s, kb_s, sc_s, acc_s, nt_s)
        _mlstm_sample_update(c0_ref, c_ref, vdt_s, kb_s, sc_s)


def _mlstm_sample_scratch():
    t = MLSTM_ST
    return [pltpu.VMEM((A_HEADS, A_DV, t), f32), pltpu.VMEM((A_HEADS * t, A_DK), f32)]


def _mlstm_sample_first(main, gates, gates_t, m0_tok, c_all, n0, g_head):
    t = MLSTM_ST
    return pl.pallas_call(
        _mlstm_sample_first_kernel,
        grid=(DEC_BATCH // MLSTM_SB,),
        in_specs=[pl.BlockSpec((t, A_MAIN), lambda i: (i, 0)),
                  pl.BlockSpec((t, LANES), lambda i: (i, 0)),
                  pl.BlockSpec((SUBLANES, t), lambda i: (0, i)),
                  pl.BlockSpec((t, LANES), lambda i: (i, 0)),
                  pl.BlockSpec((None, MLSTM_SB, A_HEADS, A_DV, A_DK), lambda i: (0, i, 0, 0, 0)),
                  pl.BlockSpec((MLSTM_SB, A_HEADS, A_DK), lambda i: (i, 0, 0)),
                  _const_spec((A_HEADS, A_DV))],
        out_specs=[pl.BlockSpec((t, A_HEADS * A_DV), lambda i: (i, 0)),
                   pl.BlockSpec((MLSTM_SB, A_HEADS, A_DK), lambda i: (i, 0, 0)),
                   pl.BlockSpec((t, LANES), lambda i: (i, 0)),
                   pl.BlockSpec((A_HEADS, A_DV, t), lambda i: (0, 0, i)),
                   pl.BlockSpec((t, A_QK), lambda i: (i, 0)),
                   pl.BlockSpec((t, LANES), lambda i: (i, 0))],
        out_shape=[jax.ShapeDtypeStruct((N_S, A_HEADS * A_DV), bf16),
                   jax.ShapeDtypeStruct((DEC_BATCH, A_HEADS, A_DK), f32),
                   jax.ShapeDtypeStruct((N_S, LANES), f32),
                   jax.ShapeDtypeStruct((A_HEADS, A_DV, N_S), bf16),
                   jax.ShapeDtypeStruct((N_S, A_QK), bf16),
                   jax.ShapeDtypeStruct((N_S, LANES), f32)],
        scratch_shapes=_mlstm_sample_scratch(),
        compiler_params=_params("parallel"),
        name="mlstm_sample_first",
    )(main, gates, gates_t, m0_tok, c_all, n0, g_head)


def _mlstm_sample_second(main, gates, gates_t, m0_tok, c_all, n0, g_head, vdt0, kb0, sc0):
    t = MLSTM_ST
    own = lambda l, i: i * l
    first = lambda l, i: i * (1 - l)
    return pl.pallas_call(
        _mlstm_sample_second_kernel,
        grid=(N_A, DEC_BATCH // MLSTM_SB),
        in_specs=[pl.BlockSpec((t, A_MAIN), lambda l, i: (own(l, i), 0)),
                  pl.BlockSpec((t, LANES), lambda l, i: (own(l, i), 0)),
                  pl.BlockSpec((SUBLANES, t), lambda l, i: (0, own(l, i))),
                  pl.BlockSpec((t, LANES), lambda l, i: (own(l, i), 0)),
                  pl.BlockSpec((None, MLSTM_SB, A_HEADS, A_DV, A_DK), lambda l, i: (l, i, 0, 0, 0)),
                  pl.BlockSpec((MLSTM_SB, A_HEADS, A_DK), lambda l, i: (own(l, i), 0, 0)),
                  _const_spec((A_HEADS, A_DV)),
                  pl.BlockSpec((A_HEADS, A_DV, t), lambda l, i: (0, 0, first(l, i))),
                  pl.BlockSpec((t, A_QK), lambda l, i: (first(l, i), 0)),
                  pl.BlockSpec((t, LANES), lambda l, i: (first(l, i), 0))],
        out_specs=[pl.BlockSpec((t, A_HEADS * A_DV), lambda l, i: (own(l, i), 0)),
                   pl.BlockSpec((MLSTM_SB, A_HEADS, A_DK), lambda l, i: (own(l, i), 0, 0)),
                   pl.BlockSpec((t, LANES), lambda l, i: (own(l, i), 0)),
                   pl.BlockSpec((None, MLSTM_SB, A_HEADS, A_DV, A_DK), lambda l, i: (l, i, 0, 0, 0))],
        out_shape=[jax.ShapeDtypeStruct((N_S, A_HEADS * A_DV), bf16),
                   jax.ShapeDtypeStruct((DEC_BATCH, A_HEADS, A_DK), f32),
                   jax.ShapeDtypeStruct((N_S, LANES), f32),
                   jax.ShapeDtypeStruct((N_A, DEC_BATCH, A_HEADS, A_DV, A_DK), f32)],
        scratch_shapes=[pltpu.VMEM((A_HEADS, A_DV, t), bf16),
                        pltpu.VMEM((t, A_QK), bf16),
                        pltpu.VMEM((t, LANES), f32)] + _mlstm_sample_scratch(),
        compiler_params=_params("arbitrary", "arbitrary"),
        name="mlstm_sample_second",
    )(main, gates, gates_t, m0_tok, c_all, n0, g_head, vdt0, kb0, sc0)


def _swa_softmax_pv(s, sink, vb):
    n = s.shape[1] // LANES
    m = jnp.maximum(jnp.broadcast_to(jnp.max(s, axis=1, keepdims=True), sink.shape), sink)
    p = jnp.exp(s - _wide(m, n))
    den = _row_sum(p) + jnp.exp(sink - m)
    return _dot((p * _wide(1.0 / den, n)).astype(bf16), vb)


def _swa_window_bias(has_prev):
    w = WINDOW
    t = lax.broadcasted_iota(jnp.int32, (w, 2 * w), 0)
    jj = lax.broadcasted_iota(jnp.int32, (w, 2 * w), 1)
    bias = jnp.where((jj > t) & (jj <= t + w) & (has_prev | (jj >= w)), 0.0, -jnp.inf)
    return jnp.concatenate([bias] * B_GROUP, axis=0)


def _swa_window_block(q_groups, kb, vb, sink_ref, bias):
    w = WINDOW
    lane_head = _idiv(lax.broadcasted_iota(jnp.int32, (1, B_KV), 1), B_HD)
    outs = [jnp.zeros((w, B_KV), f32) for _ in range(B_GROUP)]
    for h in range(B_KV_HEADS):
        hm = lane_head == h
        qscale = jnp.where(hm, B_HD ** -0.5, 0.0)
        qh = jnp.concatenate([q * qscale for q in q_groups], axis=0).astype(bf16)
        sink = jnp.concatenate([jnp.full((w, LANES), sink_ref[h * B_GROUP + g], f32) for g in range(B_GROUP)], axis=0)
        r = _swa_softmax_pv(_dot_nt(qh, kb) + bias, sink, vb)
        for g in range(B_GROUP):
            outs[g] = jnp.where(hm, r[g * w:(g + 1) * w, :], outs[g])
    return outs


def _swa_prompt_kernel(sink_ref, q_ref, ko_ref, vo_ref, kp_ref, vp_ref, a_ref):
    w = WINDOW
    bias_first = _swa_window_bias(pl.program_id(1) > 0)
    bias_rest = _swa_window_bias(True)
    for nb in range(SWA_TILE // w):
        r0 = nb * w
        if nb == 0:
            k_prev, v_prev, bias = kp_ref[...], vp_ref[...], bias_first
        else:
            k_prev, v_prev, bias = ko_ref[r0 - w:r0, :], vo_ref[r0 - w:r0, :], bias_rest
        kb = jnp.concatenate([k_prev, ko_ref[r0:r0 + w, :]], axis=0).astype(bf16)
        vb = jnp.concatenate([v_prev, vo_ref[r0:r0 + w, :]], axis=0).astype(bf16)
        outs = _swa_window_block([q_ref[r0:r0 + w, g * B_KV:(g + 1) * B_KV] for g in range(B_GROUP)],
                                 kb, vb, sink_ref, bias)
        for g in range(B_GROUP):
            a_ref[r0:r0 + w, g * B_KV:(g + 1) * B_KV] = outs[g].astype(bf16)


def _swa_prompt(proj, sinks):
    nt = SEQ // SWA_TILE
    w = WINDOW
    per = SWA_TILE // w
    kcol = B_Q // B_KV
    vcol = kcol + 1
    tile = lambda b, n: b * nt + n
    prev = lambda b, n: jnp.maximum(tile(b, n) * per - 1, 0)
    return pl.pallas_call(
        _swa_prompt_kernel,
        grid=(BATCH, nt),
        in_specs=[pl.BlockSpec(memory_space=pltpu.SMEM),
                  pl.BlockSpec((SWA_TILE, B_Q), lambda b, n: (tile(b, n), 0)),
                  pl.BlockSpec((SWA_TILE, B_KV), lambda b, n: (tile(b, n), kcol)),
                  pl.BlockSpec((SWA_TILE, B_KV), lambda b, n: (tile(b, n), vcol)),
                  pl.BlockSpec((w, B_KV), lambda b, n: (prev(b, n), kcol)),
                  pl.BlockSpec((w, B_KV), lambda b, n: (prev(b, n), vcol))],
        out_specs=pl.BlockSpec((SWA_TILE, B_Q), lambda b, n: (tile(b, n), 0)),
        out_shape=jax.ShapeDtypeStruct((N_P, B_Q), bf16),
        compiler_params=_params("parallel", "arbitrary"),
        name="swa_prompt",
    )(sinks, proj, proj, proj, proj, proj)


def _swa_sample_kernel(sink_ref, q_ref, kn_ref, vn_ref, kc_ref, vc_ref, o_ref, ko_ref, vo_ref):
    s_len = DEC_SEQ
    buf = WINDOW
    rows = B_HEADS * s_len
    keys = 2 * buf
    ri = lax.broadcasted_iota(jnp.int32, (rows, keys), 0)
    jj = lax.broadcasted_iota(jnp.int32, (rows, keys), 1)
    t = _imod(ri, s_len)
    bias = jnp.where(((jj < buf) & (jj > t)) | ((jj >= buf) & (jj - buf <= t)), 0.0, -jnp.inf)
    lane_head = _idiv(lax.broadcasted_iota(jnp.int32, (1, B_KV), 1), B_HD)
    qscale = [jnp.where(lane_head == h, B_HD ** -0.5, 0.0) for h in range(B_KV_HEADS)]
    sink = jnp.concatenate([jnp.full((s_len, LANES), sink_ref[i], f32) for i in range(B_HEADS)], axis=0)
    pad = jnp.zeros((keys - buf - s_len, B_KV), f32)

    def body(e, carry):
        rs = pl.ds(pl.multiple_of(e * s_len, s_len), s_len)
        kn = kn_ref[rs, :]
        vn = vn_ref[rs, :]
        kc = kc_ref[e]
        vc = vc_ref[e]
        kpad = jnp.concatenate([kc, kn, pad], axis=0).astype(bf16)
        vpad = jnp.concatenate([vc, vn, pad], axis=0).astype(bf16)
        ko_ref[e, 0:buf - s_len, :] = kc[s_len:, :]
        ko_ref[e, buf - s_len:, :] = kn
        vo_ref[e, 0:buf - s_len, :] = vc[s_len:, :]
        vo_ref[e, buf - s_len:, :] = vn
        qe = q_ref[rs, :]
        qbig = jnp.concatenate([qe[:, g * B_KV:(g + 1) * B_KV] * qscale[h]
                                for h in range(B_KV_HEADS) for g in range(B_GROUP)], axis=0).astype(bf16)
        r = _swa_softmax_pv(_dot_nt(qbig, kpad) + bias, sink, vpad)
        for g in range(B_GROUP):
            og = jnp.zeros((s_len, B_KV), f32)
            for h in range(B_KV_HEADS):
                blk = (h * B_GROUP + g) * s_len
                og = jnp.where(lane_head == h, r[blk:blk + s_len, :], og)
            o_ref[rs, g * B_KV:(g + 1) * B_KV] = og.astype(bf16)
        return carry

    lax.fori_loop(0, SWA_SB, body, 0, unroll=True)


def _swa_sample(proj, sinks, k_cache, v_cache):
    t = SWA_SB * DEC_SEQ
    kcol = B_Q // B_KV
    return pl.pallas_call(
        _swa_sample_kernel,
        grid=(DEC_BATCH // SWA_SB,),
        in_specs=[pl.BlockSpec(memory_space=pltpu.SMEM),
                  pl.BlockSpec((t, B_Q), lambda i: (i, 0)),
                  pl.BlockSpec((t, B_KV), lambda i: (i, kcol)),
                  pl.BlockSpec((t, B_KV), lambda i: (i, kcol + 1)),
                  pl.BlockSpec((SWA_SB, WINDOW, B_KV), lambda i: (i, 0, 0)),
                  pl.BlockSpec((SWA_SB, WINDOW, B_KV), lambda i: (i, 0, 0))],
        out_specs=[pl.BlockSpec((t, B_Q), lambda i: (i, 0)),
                   pl.BlockSpec((SWA_SB, WINDOW, B_KV), lambda i: (i, 0, 0)),
                   pl.BlockSpec((SWA_SB, WINDOW, B_KV), lambda i: (i, 0, 0))],
        out_shape=[jax.ShapeDtypeStruct((N_S, B_Q), bf16),
                   jax.ShapeDtypeStruct((DEC_BATCH, WINDOW, B_KV), f32),
                   jax.ShapeDtypeStruct((DEC_BATCH, WINDOW, B_KV), f32)],
        compiler_params=_params("parallel"),
        name="swa_sample",
    )(sinks, proj, proj, proj, k_cache, v_cache)


def _rg_conv_group(x8, p8, wc_ref, bc_ref):
    row = lax.broadcasted_iota(jnp.int32, x8.shape, 0)
    u = bc_ref[...] + wc_ref[CONV_W - 1] * x8
    for d in range(1, CONV_W):
        sh = pltpu.roll(jnp.where(row >= SUBLANES - d, p8, x8), d, 0)
        u = u + wc_ref[CONV_W - 1 - d] * sh
    return u


def _rg_scan_group(a8, b8, h_prev8):
    first = lax.broadcasted_iota(jnp.int32, a8.shape, 0) == 0
    b = b8 + jnp.where(first, a8 * pltpu.roll(h_prev8, 1, 0), 0.0)
    a = jnp.where(first, 0.0, a8)
    for d in (1, 2, 4):
        b = a * pltpu.roll(b, d, 0) + b
        if d < SUBLANES // 2:
            a = a * pltpu.roll(a, d, 0)
    return b


def _rg_gate_rows(ra, rx, neg_rate, u):
    z = _sigmoid(ra) * neg_rate
    a = jnp.exp(-z)
    return a, _sqrt_nonneg(jnp.tanh(z) * (a * a + 1.0)) * _sigmoid(rx) * u


def _rg_gates(u, gate, wa_ref, ba_ref, wx_ref, bx_ref, lam_ref):
    ub = u.astype(bf16)
    ra = jnp.concatenate([_dot(ub[:, n * C_BW:(n + 1) * C_BW], wa_ref[n]) for n in range(C_BLOCKS)], axis=1)
    rx = jnp.concatenate([_dot(ub[:, n * C_BW:(n + 1) * C_BW], wx_ref[n]) for n in range(C_BLOCKS)], axis=1)
    a, bterm = _rg_gate_rows(ra + ba_ref[0:1, :], rx + bx_ref[0:1, :], LRU_C * _softplus(-lam_ref[0:1, :]), u)
    return a, bterm, _gelu_tanh(gate)


def _rg_prompt_kernel(p_ref, wc_ref, bc_ref, wa_ref, ba_ref, wx_ref, bx_ref, lam_ref,
                      y_ref, h_ref, u_s, a_s, b_s, xc_s, hc_s):
    j = pl.program_id(1)
    ng = RG_T // SUBLANES
    out_rows = 2 * SUBLANES

    @pl.when(j == 0)
    def _():
        xc_s[...] = jnp.zeros_like(xc_s)
        hc_s[...] = jnp.zeros_like(hc_s)

    def group(gidx, n=SUBLANES):
        return pl.ds(pl.multiple_of(gidx * n, n), n)

    def conv_body(gidx, p8):
        x8 = p_ref[group(gidx), 0:C_WIDTH]
        u_s[group(gidx), :] = _rg_conv_group(x8, p8, wc_ref, bc_ref)
        return x8

    xc_s[...] = lax.fori_loop(0, ng, conv_body, xc_s[...], unroll=2)
    ub = u_s[...].astype(bf16)
    for n in range(C_BLOCKS):
        cols = slice(n * C_BW, (n + 1) * C_BW)
        a_s[:, cols] = _dot(ub[:, cols], wa_ref[n])
        b_s[:, cols] = _dot(ub[:, cols], wx_ref[n])
    neg_rate = LRU_C * _softplus(-lam_ref[...])

    def gate_body(gidx, carry):
        rows = group(gidx)
        a, bterm = _rg_gate_rows(a_s[rows, :] + ba_ref[...], b_s[rows, :] + bx_ref[...], neg_rate, u_s[rows, :])
        a_s[rows, :] = a
        b_s[rows, :] = bterm
        return carry

    lax.fori_loop(0, ng, gate_body, 0, unroll=4)

    def scan_body(gidx, carry):
        rows = group(gidx, out_rows)
        lo = pl.ds(pl.multiple_of(gidx * out_rows, out_rows), SUBLANES)
        hi = pl.ds(pl.multiple_of(gidx * out_rows + SUBLANES, SUBLANES), SUBLANES)
        h_lo = _rg_scan_group(a_s[lo, :], b_s[lo, :], carry)
        h_hi = _rg_scan_group(a_s[hi, :], b_s[hi, :], h_lo)
        h16 = jnp.concatenate([h_lo, h_hi], axis=0)
        y_ref[rows, :] = (h16 * _gelu_tanh(p_ref[rows, C_WIDTH:])).astype(bf16)
        return h_hi

    h_last = lax.fori_loop(0, RG_T // out_rows, scan_body, hc_s[...], unroll=2)
    hc_s[...] = h_last

    @pl.when(j == pl.num_programs(1) - 1)
    def _():
        h_ref[0] = h_last[SUBLANES - 1:SUBLANES, :]


def _rg_weight_specs():
    rows = (SUBLANES, C_WIDTH)
    return [_const_spec((CONV_W,) + rows), _const_spec(rows),
            _const_spec((C_BLOCKS, C_BW, C_BW)), _const_spec(rows),
            _const_spec((C_BLOCKS, C_BW, C_BW)), _const_spec(rows),
            _const_spec(rows)]


def _rg_prompt(proj, weights):
    nt = SEQ // RG_T
    return pl.pallas_call(
        _rg_prompt_kernel,
        grid=(BATCH, nt),
        in_specs=[pl.BlockSpec((RG_T, 2 * C_WIDTH), lambda b, j: (b * nt + j, 0))] + _rg_weight_specs(),
        out_specs=[pl.BlockSpec((RG_T, C_WIDTH), lambda b, j: (b * nt + j, 0)),
                   pl.BlockSpec((1, 1, C_WIDTH), lambda b, j: (b, 0, 0))],
        out_shape=[jax.ShapeDtypeStruct((N_P, C_WIDTH), bf16),
                   jax.ShapeDtypeStruct((BATCH, 1, C_WIDTH), f32)],
        scratch_shapes=[pltpu.VMEM((RG_T, C_WIDTH), f32),
                        pltpu.VMEM((RG_T, C_WIDTH), f32),
                        pltpu.VMEM((RG_T, C_WIDTH), f32),
                        pltpu.VMEM((SUBLANES, C_WIDTH), f32),
                        pltpu.VMEM((SUBLANES, C_WIDTH), f32)],
        compiler_params=_params("parallel", "arbitrary"),
        name="rg_prompt",
    )(proj, *weights)


def _rg_sample_kernel(p_ref, cv_ref, h0_ref, wc_ref, bc_ref, wa_ref, ba_ref, wx_ref, bx_ref, lam_ref,
                      y_ref, h_ref, u_s, a_s, b_s):
    def conv_body(gidx, carry):
        rows = pl.ds(pl.multiple_of(gidx * SUBLANES, SUBLANES), SUBLANES)
        u_s[rows, :] = _rg_conv_group(p_ref[rows, 0:C_WIDTH], cv_ref[rows, :], wc_ref, bc_ref)
        return carry

    lax.fori_loop(0, DEC_BATCH, conv_body, 0)
    a, bterm, gg = _rg_gates(u_s[...], p_ref[:, C_WIDTH:], wa_ref, ba_ref, wx_ref, bx_ref, lam_ref)
    a_s[...] = a
    b_s[...] = bterm

    def scan_body(gidx, carry):
        rows = pl.ds(pl.multiple_of(gidx * SUBLANES, SUBLANES), SUBLANES)
        h0 = jnp.broadcast_to(h0_ref[pl.ds(gidx, 1), :], (SUBLANES, C_WIDTH))
        h8 = _rg_scan_group(a_s[rows, :], b_s[rows, :], h0)
        u_s[rows, :] = h8
        h_ref[pl.ds(gidx, 1), :] = h8[SUBLANES - 1:SUBLANES, :]
        return carry

    lax.fori_loop(0, DEC_BATCH, scan_body, 0)
    y_ref[...] = (u_s[...] * gg).astype(bf16)


def _rg_sample(proj, conv_pad, h0, weights):
    return pl.pallas_call(
        _rg_sample_kernel,
        grid=(1,),
        in_specs=[_const_spec((N_S, 2 * C_WIDTH)),
                  _const_spec((N_S, C_WIDTH)),
                  _const_spec((DEC_BATCH, C_WIDTH))] + _rg_weight_specs(),
        out_specs=[_const_spec((N_S, C_WIDTH)), _const_spec((DEC_BATCH, C_WIDTH))],
        out_shape=[jax.ShapeDtypeStruct((N_S, C_WIDTH), bf16),
                   jax.ShapeDtypeStruct((DEC_BATCH, C_WIDTH), f32)],
        scratch_shapes=[pltpu.VMEM((N_S, C_WIDTH), f32),
                        pltpu.VMEM((N_S, C_WIDTH), f32),
                        pltpu.VMEM((N_S, C_WIDTH), f32)],
        compiler_params=_params("arbitrary"),
        name="rg_sample",
    )(proj, conv_pad, h0, *weights)


def _mlp_tile(x_ref, a_ref, o_ref, wo_ref, bo_ref, g_ref, wup_ref, wdn_ref, gf_ref, x1_s, xn_s, final):
    x1_s[...] = x_ref[...] + _dot(a_ref[...], wo_ref[...]) + bo_ref[...]
    xn_s[...] = _rms_bf16(x1_s[...], g_ref[...])
    for c in range(0, D_FF, FF_CHUNK):
        hmid = jnp.maximum(_dot(xn_s[...], wup_ref[:, c:c + FF_CHUNK]), 0.0)
        x1_s[...] += _dot((hmid * hmid).astype(bf16), wdn_ref[c:c + FF_CHUNK, :])
    if final:
        x1 = x1_s[...]
        y = x1 * lax.rsqrt(jnp.mean(x1 * x1, axis=-1, keepdims=True) + NORM_EPS)
        o_ref[...] = y * gf_ref[...]
    else:
        o_ref[...] = x1_s[...]


def _out_mlp_kernel(xp_ref, xs_ref, ap_ref, as_ref, wo_ref, bo_ref, g_ref, wup_ref, wdn_ref, gf_ref, *rest,
                    final, cast_next):
    if cast_next:
        nup_ref, ndn_ref, op_ref, os_ref, nup_out, ndn_out, x1_s, xn_s = rest
    else:
        op_ref, os_ref, x1_s, xn_s = rest

    def body(x_ref, a_ref, o_ref):
        _mlp_tile(x_ref, a_ref, o_ref, wo_ref, bo_ref, g_ref, wup_ref, wdn_ref, gf_ref, x1_s, xn_s, final)

    i = pl.program_id(0)

    @pl.when(i < NP_TILES)
    def _():
        body(xp_ref, ap_ref, op_ref)
        if cast_next:
            nup_out[...] = nup_ref[...].astype(bf16)
            ndn_out[...] = ndn_ref[...].astype(bf16)

    @pl.when(i >= NP_TILES)
    def _():
        body(xs_ref, as_ref, os_ref)


def _out_mlp(x_p, x_s, a_p, a_s, wo, bo, g, w_up, w_down, g_final, final, next_weights=None):
    single = pl.Buffered(1)
    wspec = lambda shape: pl.BlockSpec(shape, lambda i: (0, 0), pipeline_mode=single)
    in_specs = [_p_spec(D_MODEL), _s_spec(D_MODEL), _p_spec(D_MODEL), _s_spec(D_MODEL),
                wspec((D_MODEL, D_MODEL)), wspec((1, D_MODEL)), wspec((1, D_MODEL)),
                wspec((D_MODEL, D_FF)), wspec((D_FF, D_MODEL)), wspec((1, D_MODEL))]
    out_specs = [_p_spec(D_MODEL), _s_spec(D_MODEL)]
    out_shape = [jax.ShapeDtypeStruct((N_P, D_MODEL), f32), jax.ShapeDtypeStruct((N_S, D_MODEL), f32)]
    args = [x_p, x_s, a_p, a_s, wo, bo, g, w_up, w_down, g_final]
    if next_weights is not None:
        extra = _mlp_weight_cast_specs(next_weights)
        in_specs, out_specs, out_shape, args = (in_specs + extra[0], out_specs + extra[1], out_shape + extra[2],
                                                args + extra[3])
    return pl.pallas_call(
        functools.partial(_out_mlp_kernel, final=final, cast_next=next_weights is not None),
        grid=(NP_TILES + NS_TILES,),
        in_specs=in_specs,
        out_specs=out_specs,
        out_shape=out_shape,
        scratch_shapes=[pltpu.VMEM((TM, D_MODEL), f32),
                        pltpu.VMEM((TM, D_MODEL), bf16)],
        compiler_params=_params("arbitrary"),
        name="out_mlp",
    )(*args)


def _rope_tables():
    half = ROPE_DIM // 2
    inv = np.float32(ROPE_THETA) ** (-np.arange(0, ROPE_DIM, 2, dtype=np.float32) / np.float32(ROPE_DIM))
    lane = np.arange(LANES) % B_HD
    inv_lane = np.where(lane < ROPE_DIM, inv[lane % half], np.float32(0.0)).astype(np.float32)
    pos = np.concatenate([np.arange(SEQ), PAST_LEN + np.arange(TM) % DEC_SEQ]).astype(np.float32)
    ang = pos[:, None] * inv_lane[None, :]
    tab = np.concatenate([np.cos(ang), np.sin(ang)], axis=1).astype(np.float32)
    consts = np.zeros((SUBLANES, LANES), np.float32)
    consts[0] = np.where(lane < half, -1.0, 0.0)
    consts[1] = np.where((lane >= half) & (lane < ROPE_DIM), 1.0, 0.0)
    return jnp.asarray(tab), jnp.asarray(consts)


def _q_cols_to_group_major(w):
    lead = w.shape[:-1]
    return jnp.swapaxes(w.reshape(lead + (B_KV_HEADS, B_GROUP, B_HD)), -3, -2).reshape(lead + (B_Q,))


def _last_rows(arr, n_seq, seq_len, n_rows, col0, col1):
    return jnp.stack([arr[(s + 1) * seq_len - n_rows:(s + 1) * seq_len, col0:col1] for s in range(n_seq)])


def kernel(x_prompt, x_sample, state_mlstm_c, state_mlstm_n, state_mlstm_m, cache_swa_k, cache_swa_v,
           state_rglru_h, state_rglru_conv, norm_mix, norm_mlp, norm_final, w_mlp_up, w_mlp_down,
           w_mlstm_in, b_mlstm_i, b_mlstm_f, g_mlstm_head, w_mlstm_out, w_swa_qkv, b_swa_qkv, swa_sinks,
           w_swa_out, b_swa_out, w_rg_in, w_rg_conv, b_rg_conv, w_rg_a, b_rg_a, w_rg_x, b_rg_x, rg_lambda,
           w_rg_out):
    assert N_A == 2
    x_p = x_prompt.reshape(N_P, D_MODEL)
    x_s = x_sample.reshape(N_S, D_MODEL)
    zero_bias = jnp.zeros((1, D_MODEL), f32)
    row = lambda v: v.reshape(1, -1).astype(f32)
    w_up = w_down = None
    state_c = state_mlstm_c.astype(f32)
    n_gate = 2 * A_HEADS
    assert n_gate == SUBLANES
    w_mlstm_main = w_mlstm_in[:, :, :A_MAIN].astype(bf16)
    w_gate_cols = w_mlstm_in[:, :, A_MAIN:]
    w_mlstm_gate = jnp.pad(w_gate_cols, ((0, 0), (0, 0), (0, LANES - n_gate))).astype(bf16)
    outs = {k: [] for k in ("c_p", "n_p", "m_p", "n_s", "m_s", "k_p", "v_p", "k_s", "v_s",
                            "h_p", "cv_p", "h_s", "cv_s")}
    deferred = None
    c_s_all = None
    for i in range(DEPTH):
        kind, j = i % N_MIXERS, i // N_MIXERS
        g_mix = row(norm_mix[i])
        if kind == 0:
            b_gate = jnp.concatenate([b_mlstm_i[j], b_mlstm_f[j]]).astype(f32)
            res = _proj_mlstm(x_p, x_s, g_mix, w_mlstm_main, w_mlstm_gate,
                              jnp.pad(b_gate, (0, LANES - n_gate)).reshape(1, LANES), j,
                              next_weights=(w_mlp_up, w_mlp_down, 0) if i == 0 else None)
            main_p, main_s, gates_p, gates_s, gates_p_t, gates_s_t = res[:6]
            if i == 0:
                w_up, w_down = res[6], res[7]
            g_head = g_mlstm_head[j].astype(f32)
            a_p, c_p, n_p, mt_p = _mlstm_prompt(main_p.reshape(BATCH, SEQ, A_MAIN),
                                                gates_p.reshape(BATCH, SEQ, LANES), gates_p_t, g_head)
            a_p = a_p.reshape(N_P, A_HEADS * A_DV)
            mt_p = mt_p.reshape(N_P, LANES)
            m0_tok = jnp.pad(jnp.repeat(state_mlstm_m[j].astype(f32), DEC_SEQ, axis=0),
                             ((0, 0), (0, LANES - A_HEADS)))
            n0 = state_mlstm_n[j].astype(f32)
            if j == 0:
                a_s, n_s, mt_s, vdt0, kb0, sc0 = _mlstm_sample_first(main_s, gates_s, gates_s_t, m0_tok, state_c, n0,
                                                                     g_head)
                deferred = (vdt0, kb0, sc0)
            else:
                a_s, n_s, mt_s, c_s_all = _mlstm_sample_second(main_s, gates_s, gates_s_t, m0_tok, state_c, n0,
                                                               g_head, *deferred)
            outs["c_p"].append(c_p); outs["n_p"].append(n_p)
            outs["m_p"].append(_last_rows(mt_p, BATCH, SEQ, 1, 0, A_HEADS).reshape(BATCH, A_HEADS))
            outs["n_s"].append(n_s)
            outs["m_s"].append(mt_s[DEC_SEQ - 1::DEC_SEQ, :A_HEADS])
            wo, bo = w_mlstm_out[j], zero_bias
        elif kind == 1:
            rope_tab, rope_consts = _rope_tables()
            w_qkv, b_qkv = w_swa_qkv[j], b_swa_qkv[j]
            w_qkv = jnp.concatenate([_q_cols_to_group_major(w_qkv[:, :B_Q]), w_qkv[:, B_Q:]], axis=1)
            b_qkv = jnp.concatenate([_q_cols_to_group_major(b_qkv[:B_Q]), b_qkv[B_Q:]])
            proj_p, proj_s = _proj_swa(x_p, x_s, g_mix, w_qkv.astype(bf16), row(b_qkv), rope_tab, rope_consts)
            sinks = swa_sinks[j].astype(f32)
            a_p = _swa_prompt(proj_p, sinks)
            buf = cache_swa_k.shape[2]
            a_s, k_s, v_s = _swa_sample(proj_s, sinks,
                                        cache_swa_k[j].astype(f32).reshape(DEC_BATCH, buf, B_KV),
                                        cache_swa_v[j].astype(f32).reshape(DEC_BATCH, buf, B_KV))
            outs["k_p"].append(_last_rows(proj_p, BATCH, SEQ, WINDOW, B_Q, B_Q + B_KV)
                               .reshape(BATCH, WINDOW, B_KV_HEADS, B_HD))
            outs["v_p"].append(_last_rows(proj_p, BATCH, SEQ, WINDOW, B_Q + B_KV, B_IN)
                               .reshape(BATCH, WINDOW, B_KV_HEADS, B_HD))
            outs["k_s"].append(k_s.reshape(DEC_BATCH, buf, B_KV_HEADS, B_HD))
            outs["v_s"].append(v_s.reshape(DEC_BATCH, buf, B_KV_HEADS, B_HD))
            wo = jnp.swapaxes(w_swa_out[j].reshape(B_KV_HEADS, B_GROUP, B_HD, D_MODEL), 0, 1).reshape(B_Q, D_MODEL)
            bo = row(b_swa_out[j])
        else:
            proj_p, proj_s = _proj_rg(x_p, x_s, g_mix, w_rg_in[j].astype(bf16))
            rows8 = lambda v: jnp.broadcast_to(v.astype(f32)[..., None, :], v.shape[:-1] + (SUBLANES, C_WIDTH))
            weights = (rows8(w_rg_conv[j]), rows8(b_rg_conv[j]), w_rg_a[j].astype(bf16), rows8(b_rg_a[j]),
                       w_rg_x[j].astype(bf16), rows8(b_rg_x[j]), rows8(rg_lambda[j]))
            a_p, h_p = _rg_prompt(proj_p, weights)
            conv_pad = jnp.pad(state_rglru_conv[j].astype(f32),
                               ((0, 0), (SUBLANES - (CONV_W - 1), 0), (0, 0))).reshape(N_S, C_WIDTH)
            a_s, h_s = _rg_sample(proj_s, conv_pad, state_rglru_h[j].astype(f32), weights)
            outs["h_p"].append(h_p.reshape(BATCH, C_WIDTH))
            outs["cv_p"].append(_last_rows(proj_p, BATCH, SEQ, CONV_W - 1, 0, C_WIDTH))
            outs["h_s"].append(h_s)
            outs["cv_s"].append(proj_s[:, :C_WIDTH].reshape(DEC_BATCH, DEC_SEQ, C_WIDTH)[:, DEC_SEQ - (CONV_W - 1):])
            wo, bo = w_rg_out[j], zero_bias
        last = i == DEPTH - 1
        res = _out_mlp(x_p, x_s, a_p, a_s, wo.astype(bf16), bo, row(norm_mlp[i]), w_up, w_down, row(norm_final),
                       final=last, next_weights=None if last else (w_mlp_up, w_mlp_down, i + 1))
        x_p, x_s = res[0], res[1]
        if not last:
            w_up, w_down = res[2], res[3]
    st = {k: jnp.stack(v) for k, v in outs.items()}
    y_p = x_p.reshape(BATCH, SEQ, D_MODEL)
    y_s = x_s.reshape(DEC_BATCH, DEC_SEQ, D_MODEL)
    return (y_p, y_s, st["c_p"], st["n_p"], st["m_p"], c_s_all, st["n_s"], st["m_s"],
            st["k_p"], st["v_p"], st["k_s"], st["v_s"], st["h_p"], st["cv_p"], st["h_s"], st["cv_s"])
```

```python
import functools
import math

import jax
import jax.numpy as jnp
import numpy as np
from jax import lax
from jax.experimental import pallas as pl
from jax.experimental.pallas import tpu as pltpu

f32 = jnp.float32
bf16 = jnp.bfloat16

D_MODEL = 1024
BATCH = 2
SEQ = 8192
DEPTH = 4
DEC_BATCH = 128
DEC_SEQ = 8
PAST_LEN = 8192
N_MIXERS = 3
NORM_EPS = 1e-6

A_HEADS = 4
A_DK = 128
A_DV = 256
A_QK = A_HEADS * A_DK
A_MAIN = 2 * A_QK + 2 * A_HEADS * A_DV
N_A = (DEPTH + 2) // 3

B_HEADS = 16
B_KV_HEADS = 4
B_HD = 64
B_GROUP = 4
B_Q = B_HEADS * B_HD
B_KV = B_KV_HEADS * B_HD
B_IN = B_Q + 2 * B_KV
WINDOW = 128
ROPE_THETA = 500000.0
ROPE_DIM = 16

C_WIDTH = 1024
C_BLOCKS = 4
C_BW = 256
CONV_W = 4
LRU_C = 8.0
D_FF = 4096

N_P = BATCH * SEQ
N_S = DEC_BATCH * DEC_SEQ

LANES = 128
SUBLANES = 8
VMEM_LIMIT = 56 * 1024 * 1024

TM = 512
NP_TILES = N_P // TM
NS_TILES = N_S // TM
MLSTM_CHUNK = 256
MLSTM_SB = 16
MLSTM_ST = MLSTM_SB * DEC_SEQ
MLSTM_UNROLL = 8
SWA_SB = 8
SWA_TILE = 512
RG_T = 512
FF_CHUNK = 512


def _dot(a, b):
    return jnp.dot(a, b, preferred_element_type=f32)


def _dot_nt(a, b):
    return lax.dot_general(a, b, (((1,), (1,)), ((), ())), preferred_element_type=f32)


def _split3(x):
    hi = x.astype(bf16)
    r1 = x - hi.astype(f32)
    mid = r1.astype(bf16)
    lo = (r1 - mid.astype(f32)).astype(bf16)
    return hi, mid, lo


def _mask_dot(mask, x):
    m = jnp.where(mask, 1.0, 0.0).astype(bf16)
    return sum(_dot(m, part) for part in _split3(x))


def _dot_mask(x, mask):
    m = jnp.where(mask, 1.0, 0.0).astype(bf16)
    return sum(_dot(part, m) for part in _split3(x))


def _idiv(x, d):
    assert d & (d - 1) == 0
    return x >> (d.bit_length() - 1)


def _imod(x, d):
    assert d & (d - 1) == 0
    return x & (d - 1)


def _rms_bf16(x, g):
    y = x * lax.rsqrt(jnp.mean(x * x, axis=-1, keepdims=True) + NORM_EPS)
    return (y * g).astype(bf16)


def _sigmoid(x):
    return 0.5 * jnp.tanh(0.5 * x) + 0.5


def _sqrt_nonneg(v):
    return jnp.where(v > 0.0, v * lax.rsqrt(v), 0.0)


def _softplus(x):
    return jnp.maximum(x, 0.0) + jnp.log1p(jnp.exp(-jnp.abs(x)))


def _gelu_tanh(x):
    return 0.5 * x * (1.0 + jnp.tanh(math.sqrt(2.0 / math.pi) * (x + 0.044715 * (x * x * x))))


def _params(*sem, fuse_inputs=None):
    fusion = None if fuse_inputs is None else [k in fuse_inputs[1] for k in range(fuse_inputs[0])]
    return pltpu.CompilerParams(dimension_semantics=sem, vmem_limit_bytes=VMEM_LIMIT, allow_input_fusion=fusion)


def _const_spec(shape):
    nd = len(shape)
    return pl.BlockSpec(shape, lambda *_: (0,) * nd)


def _p_spec(width):
    return pl.BlockSpec((TM, width), lambda i: (jnp.minimum(i, NP_TILES - 1), 0))


def _s_spec(width):
    return pl.BlockSpec((TM, width), lambda i: (jnp.maximum(i - NP_TILES, 0), 0))


def _for_each_group(body, p_refs, s_refs):
    i = pl.program_id(0)

    @pl.when(i < NP_TILES)
    def _():
        body(*p_refs)

    @pl.when(i >= NP_TILES)
    def _():
        body(*s_refs)


def _mlp_weight_cast_specs(next_weights):
    up_all, down_all, layer = next_weights
    step = lambda i: jnp.minimum(i, NP_TILES - 1)
    in_specs, out_specs, out_shape = [], [], []
    for rows, width in ((D_MODEL // NP_TILES, D_FF), (D_FF // NP_TILES, D_MODEL)):
        in_specs.append(pl.BlockSpec((None, rows, width), lambda i: (layer, step(i), 0)))
        out_specs.append(pl.BlockSpec((rows, width), lambda i: (step(i), 0)))
        out_shape.append(jax.ShapeDtypeStruct((rows * NP_TILES, width), bf16))
    return in_specs, out_specs, out_shape, [up_all, down_all]


def _proj_mlstm_kernel(xp_ref, xs_ref, g_ref, w_ref, wg_ref, bg_ref, *rest, cast_next):
    if cast_next:
        nup_ref, ndn_ref, mp_ref, ms_ref, gp_ref, gs_ref, gtp_ref, gts_ref, nup_out, ndn_out = rest
    else:
        mp_ref, ms_ref, gp_ref, gs_ref, gtp_ref, gts_ref = rest

    def body(x_ref, main_ref, gate_ref, gate_t_ref):
        xn = _rms_bf16(x_ref[...], g_ref[...])
        for c in range(0, A_MAIN, 512):
            main_ref[:, c:c + 512] = _dot(xn, w_ref[:, c:c + 512])
        gp = _dot(xn, wg_ref[...]) + bg_ref[...]
        lane = lax.broadcasted_iota(jnp.int32, gp.shape, 1)
        gates = jnp.where(lane >= A_HEADS, -_softplus(-gp), gp)
        gate_ref[...] = gates
        gate_t_ref[...] = gates.T[0:SUBLANES, :]

    i = pl.program_id(0)

    @pl.when(i < NP_TILES)
    def _():
        body(xp_ref, mp_ref, gp_ref, gtp_ref)
        if cast_next:
            nup_out[...] = nup_ref[...].astype(bf16)
            ndn_out[...] = ndn_ref[...].astype(bf16)

    @pl.when(i >= NP_TILES)
    def _():
        body(xs_ref, ms_ref, gs_ref, gts_ref)


def _proj_mlstm(x_p, x_s, g, w_main_all, w_gate_all, b_gate, layer, next_weights=None):
    lspec = lambda shape: pl.BlockSpec((None,) + shape, lambda i: (layer, 0, 0))
    in_specs = [_p_spec(D_MODEL), _s_spec(D_MODEL),
                _const_spec((1, D_MODEL)),
                lspec((D_MODEL, A_MAIN)),
                lspec((D_MODEL, LANES)),
                _const_spec((1, LANES))]
    out_specs = [_p_spec(A_MAIN), _s_spec(A_MAIN), _p_spec(LANES), _s_spec(LANES),
                 pl.BlockSpec((SUBLANES, TM), lambda i: (0, jnp.minimum(i, NP_TILES - 1))),
                 pl.BlockSpec((SUBLANES, TM), lambda i: (0, jnp.maximum(i - NP_TILES, 0)))]
    out_shape = [jax.ShapeDtypeStruct((N_P, A_MAIN), f32), jax.ShapeDtypeStruct((N_S, A_MAIN), f32),
                 jax.ShapeDtypeStruct((N_P, LANES), f32), jax.ShapeDtypeStruct((N_S, LANES), f32),
                 jax.ShapeDtypeStruct((SUBLANES, N_P), f32), jax.ShapeDtypeStruct((SUBLANES, N_S), f32)]
    args = [x_p, x_s, g, w_main_all, w_gate_all, b_gate]
    if next_weights is not None:
        extra = _mlp_weight_cast_specs(next_weights)
        in_specs, out_specs, out_shape, args = (in_specs + extra[0], out_specs + extra[1], out_shape + extra[2],
                                                args + extra[3])
    return pl.pallas_call(
        functools.partial(_proj_mlstm_kernel, cast_next=next_weights is not None),
        grid=(NP_TILES + NS_TILES,),
        in_specs=in_specs,
        out_specs=out_specs,
        out_shape=out_shape,
        compiler_params=_params("arbitrary"),
        name="proj_mlstm",
    )(*args)


def _proj_swa_kernel(xp_ref, xs_ref, g_ref, w_ref, b_ref, tab_ref, rc_ref, op_ref, os_ref):
    half = ROPE_DIM // 2

    def body(x_ref, o_ref):
        xn = _rms_bf16(x_ref[...], g_ref[...])
        cos = tab_ref[:, 0:LANES]
        sin = tab_ref[:, LANES:2 * LANES]
        sin_lo = sin * rc_ref[0:1, :]
        sin_hi = sin * rc_ref[1:2, :]
        for c in range(0, B_IN, 512):
            p = _dot(xn, w_ref[:, c:c + 512]) + b_ref[:, c:c + 512]
            for l in range(0, 512, LANES):
                pl_ = p[:, l:l + LANES]
                if c + l < B_Q + B_KV:
                    pl_ = (pl_ * cos + pltpu.roll(pl_, LANES - half, 1) * sin_lo
                           + pltpu.roll(pl_, half, 1) * sin_hi)
                o_ref[:, c + l:c + l + LANES] = pl_

    _for_each_group(body, (xp_ref, op_ref), (xs_ref, os_ref))


def _proj_swa(x_p, x_s, g, w, b, rope_tab, rope_consts):
    seq_tiles = SEQ // TM
    tab_idx = lambda i: (jnp.where(i < NP_TILES, i % seq_tiles, seq_tiles), 0)
    return pl.pallas_call(
        _proj_swa_kernel,
        grid=(NP_TILES + NS_TILES,),
        in_specs=[_p_spec(D_MODEL), _s_spec(D_MODEL),
                  _const_spec((1, D_MODEL)),
                  _const_spec((D_MODEL, B_IN)),
                  _const_spec((1, B_IN)),
                  pl.BlockSpec((TM, 2 * LANES), tab_idx),
                  _const_spec((SUBLANES, LANES))],
        out_specs=[_p_spec(B_IN), _s_spec(B_IN)],
        out_shape=[jax.ShapeDtypeStruct((N_P, B_IN), f32), jax.ShapeDtypeStruct((N_S, B_IN), f32)],
        compiler_params=_params("arbitrary", fuse_inputs=(7, (3,))),
        name="proj_swa",
    )(x_p, x_s, g, w, b, rope_tab, rope_consts)


def _proj_rg_kernel(xp_ref, xs_ref, g_ref, w_ref, op_ref, os_ref):
    def body(x_ref, o_ref):
        xn = _rms_bf16(x_ref[...], g_ref[...])
        for c in range(0, 2 * C_WIDTH, 512):
            o_ref[:, c:c + 512] = _dot(xn, w_ref[:, c:c + 512])

    _for_each_group(body, (xp_ref, op_ref), (xs_ref, os_ref))


def _proj_rg(x_p, x_s, g, w):
    return pl.pallas_call(
        _proj_rg_kernel,
        grid=(NP_TILES + NS_TILES,),
        in_specs=[_p_spec(D_MODEL), _s_spec(D_MODEL),
                  _const_spec((1, D_MODEL)),
                  _const_spec((D_MODEL, 2 * C_WIDTH))],
        out_specs=[_p_spec(2 * C_WIDTH), _s_spec(2 * C_WIDTH)],
        out_shape=[jax.ShapeDtypeStruct((N_P, 2 * C_WIDTH), f32), jax.ShapeDtypeStruct((N_S, 2 * C_WIDTH), f32)],
        compiler_params=_params("arbitrary", fuse_inputs=(4, (3,))),
        name="proj_rg",
    )(x_p, x_s, g, w)


def _mlstm_masks(t, seg):
    r = lax.broadcasted_iota(jnp.int32, (t, t), 0)
    c = lax.broadcasted_iota(jnp.int32, (t, t), 1)
    if seg == t:
        same = None
        lower, upper = r >= c, r <= c
    else:
        same = _idiv(r, seg) == _idiv(c, seg)
        lower, upper = same & (r >= c), same & (r <= c)
    return r, c, same, lower, upper


def _wide(col, n):
    return jnp.concatenate([col] * n, axis=1)


def _row_sum(x):
    acc = x[:, 0:LANES]
    for l in range(LANES, x.shape[1], LANES):
        acc = acc + x[:, l:l + LANES]
    return jnp.broadcast_to(jnp.sum(acc, axis=1, keepdims=True), acc.shape)


def _mlstm_cols(main_ref, h):
    q = main_ref[:, h * A_DK:(h + 1) * A_DK]
    k = main_ref[:, A_QK + h * A_DK:A_QK + (h + 1) * A_DK] * (A_DK ** -0.5)
    v = main_ref[:, 2 * A_QK + h * A_DV:2 * A_QK + (h + 1) * A_DV]
    o_pre = main_ref[:, 2 * A_QK + (A_HEADS + h) * A_DV:2 * A_QK + (A_HEADS + h + 1) * A_DV]
    return q, k, v, o_pre


def _mlstm_prompt_chunk(main_ref, gc_ref, gr_ref, gh_ref, hs_ref, c_ref, n_ref, mt_ref, c_s, n_s, m_s):
    t = MLSTM_CHUNK
    j = pl.program_id(0)

    @pl.when(j == 0)
    def _():
        c_s[...] = jnp.zeros_like(c_s)
        n_s[...] = jnp.zeros_like(n_s)
        m_s[...] = jnp.zeros_like(m_s)

    _, _, _, lower, upper = _mlstm_masks(t, t)
    gc = gc_ref[...]
    gr = gr_ref[...]
    btr_all = _dot_mask(gr, upper)
    lane = lax.broadcasted_iota(jnp.int32, (t, LANES), 1)
    mt_all = jnp.zeros((t, LANES), f32)

    def rep(col):
        return jnp.broadcast_to(col, (t, LANES))

    wide, row_sum = _wide, _row_sum

    heads = range(A_HEADS)
    stack = lambda xs: jnp.concatenate(list(xs), axis=0)
    head = lambda x, h: x[h * t:(h + 1) * t]
    cols = [_mlstm_cols(main_ref, h) for h in heads]
    qs, ks, vs = [c[0] for c in cols], [c[1] for c in cols], [c[2] for c in cols]
    qbs, kbs, vbs = [q.astype(bf16) for q in qs], [k.astype(bf16) for k in ks], [v.astype(bf16) for v in vs]
    cts = [c_s[h] for h in heads]
    ns = [n_s[h:h + 1, :] for h in heads]
    btrs = [btr_all[A_HEADS + h:A_HEADS + h + 1, :] for h in heads]
    itrs = [gr[h:h + 1, :] for h in heads]

    lf_rep = jnp.concatenate([rep(gc[:, A_HEADS + h:A_HEADS + h + 1]) for h in heads], axis=1)
    btc_all = _mask_dot(lower, lf_rep)
    btc = stack(btc_all[:, h * LANES:(h + 1) * LANES] for h in heads)
    itc = stack(rep(gc[:, h:h + 1]) for h in heads)
    m_in = stack(rep(m_s[0:1, h:h + 1]) for h in heads)
    dmat = stack(jnp.where(lower, wide(head(btc, h), t // LANES) + (itrs[h] - btrs[h]), -jnp.inf) for h in heads)
    inter = btc + m_in
    m_t = jnp.maximum(jnp.broadcast_to(jnp.max(dmat, axis=1, keepdims=True), inter.shape), inter)
    w = jnp.exp(dmat - wide(m_t, t // LANES))
    w_inter = jnp.exp(inter - m_t)
    s = stack(_dot_nt(qbs[h], kbs[h]) for h in heads) * w
    sb = s.astype(bf16)
    num = (stack(_dot(head(sb, h), vbs[h]) for h in heads)
           + wide(w_inter, A_DV // LANES) * stack(_dot(qbs[h], cts[h].astype(bf16)) for h in heads))
    den = row_sum(s) + w_inter * row_sum(stack(qs[h] * ns[h] for h in heads))
    inv = 1.0 / jnp.maximum(jnp.abs(den), jnp.exp(-m_t))
    norm = inv * lax.rsqrt(inv * inv * (row_sum(num * num) * (1.0 / A_DV)) + NORM_EPS)
    out = _sigmoid(stack(c[3] for c in cols)) * (num * wide(norm, A_DV // LANES))
    for h in heads:
        hs_ref[:, h * A_DV:(h + 1) * A_DV] = (head(out, h) * gh_ref[h:h + 1, :]).astype(bf16)

    m_news = [head(m_t, h)[t - 1:t, 0:1] for h in heads]
    last = lambda x: stack(rep(head(x, h)[t - 1:t, 0:1]) for h in heads)
    decay_c = jnp.exp(last(btc) - btc + itc - last(m_t))
    for h in heads:
        scale = jnp.exp(head(inter, h)[t - 1:t, 0:1] - m_news[h])
        decay_r = jnp.exp(btrs[h][:, t - 1:t] - btrs[h] + itrs[h] - m_news[h])
        c_s[h] = scale * cts[h] + _dot((ks[h].T * decay_r).astype(bf16), vbs[h])
        n_s[h:h + 1, :] = scale * ns[h] + jnp.sum(head(decay_c, h) * ks[h], axis=0, keepdims=True)
        mt_all = jnp.where(lane == h, head(m_t, h), mt_all)
    mt_ref[...] = mt_all
    m_s[0:1, :] = mt_all[t - 1:t, :]

    @pl.when(j == pl.num_programs(0) - 1)
    def _():
        for h in range(A_HEADS):
            c_ref[h] = c_s[h].T
        n_ref[...] = n_s[...]


def _mlstm_prompt_kernel(main_ref, gc_ref, *rest):
    gr_refs, (gh_ref, hs_ref, c_ref, n_ref, mt_ref, c_s, n_s, m_s) = rest[:BATCH], rest[BATCH:]
    for b in range(BATCH):
        _mlstm_prompt_chunk(main_ref.at[b], gc_ref.at[b], gr_refs[b], gh_ref, hs_ref.at[b], c_ref.at[b],
                            n_ref.at[b], mt_ref.at[b], c_s.at[b], n_s.at[b], m_s.at[b])


def _mlstm_prompt(main, gates, gates_t, g_head):
    t = MLSTM_CHUNK
    nc = SEQ // t
    return pl.pallas_call(
        _mlstm_prompt_kernel,
        grid=(nc,),
        in_specs=[pl.BlockSpec((BATCH, t, A_MAIN), lambda j: (0, j, 0)),
                  pl.BlockSpec((BATCH, t, LANES), lambda j: (0, j, 0))]
                 + [pl.BlockSpec((SUBLANES, t), functools.partial(lambda j, b: (0, b * nc + j), b=b))
                    for b in range(BATCH)]
                 + [_const_spec((A_HEADS, A_DV))],
        out_specs=[pl.BlockSpec((BATCH, t, A_HEADS * A_DV), lambda j: (0, j, 0)),
                   _const_spec((BATCH, A_HEADS, A_DV, A_DK)),
                   _const_spec((BATCH, A_HEADS, A_DK)),
                   pl.BlockSpec((BATCH, t, LANES), lambda j: (0, j, 0))],
        out_shape=[jax.ShapeDtypeStruct((BATCH, SEQ, A_HEADS * A_DV), bf16),
                   jax.ShapeDtypeStruct((BATCH, A_HEADS, A_DV, A_DK), f32),
                   jax.ShapeDtypeStruct((BATCH, A_HEADS, A_DK), f32),
                   jax.ShapeDtypeStruct((BATCH, SEQ, LANES), f32)],
        scratch_shapes=[pltpu.VMEM((BATCH, A_HEADS, A_DK, A_DV), f32),
                        pltpu.VMEM((BATCH, A_HEADS, A_DK), f32),
                        pltpu.VMEM((BATCH, SUBLANES, LANES), f32)],
        compiler_params=_params("arbitrary"),
        name="mlstm_prompt",
    )(main, gates, *([gates_t] * BATCH), g_head)


def _mlstm_sample_compute(main_ref, gc_ref, gr_ref, m0_ref, c0_ref, n0_ref, gh_ref,
                          hs_ref, n_ref, mt_ref, vdt_ref, kb_ref, sc_ref, acc_s, nt_s):
    t = MLSTM_ST
    seg = DEC_SEQ
    _, c, same, lower, upper = _mlstm_masks(t, seg)
    seg_last = same & (_imod(c, seg) == seg - 1)
    gc = gc_ref[...]
    gr = gr_ref[...]
    btr_all = _dot_mask(gr, upper)
    lane = lax.broadcasted_iota(jnp.int32, (t, LANES), 1)
    tok_seg = _idiv(lax.broadcasted_iota(jnp.int32, (A_DV, t), 1), seg)

    heads = range(A_HEADS)
    rep = lambda col: jnp.broadcast_to(col, (t, LANES))
    stack = lambda xs: jnp.concatenate(list(xs), axis=0)
    lanes = lambda xs: jnp.concatenate(list(xs), axis=1)
    head = lambda x, h: x[h * t:(h + 1) * t]
    cols = [_mlstm_cols(main_ref, h) for h in heads]
    qs, ks, vs = [c_[0] for c_ in cols], [c_[1] for c_ in cols], [c_[2] for c_ in cols]
    qbs, kbs, vbs = [q.astype(bf16) for q in qs], [k.astype(bf16) for k in ks], [v.astype(bf16) for v in vs]

    btc_all = _mask_dot(lower, lanes(rep(gc[:, A_HEADS + h:A_HEADS + h + 1]) for h in heads))
    btc = stack(btc_all[:, h * LANES:(h + 1) * LANES] for h in heads)
    itc = stack(rep(gc[:, h:h + 1]) for h in heads)
    m_in = stack(rep(m0_ref[:, h:h + 1]) for h in heads)
    dmat = stack(jnp.where(lower, head(btc, h) + (gr[h:h + 1, :] - btr_all[A_HEADS + h:A_HEADS + h + 1, :]), -jnp.inf)
                 for h in heads)
    inter = btc + m_in
    m_t = jnp.maximum(jnp.broadcast_to(jnp.max(dmat, axis=1, keepdims=True), inter.shape), inter)
    w = jnp.exp(dmat - m_t)
    w_inter = jnp.exp(inter - m_t)
    s = stack(_dot_nt(qbs[h], kbs[h]) for h in heads) * w
    sb = s.astype(bf16)
    num = stack(_dot(head(sb, h), vbs[h]) for h in heads)

    for h in heads:
        def inter_body(b, carry, h=h):
            rows = pl.ds(pl.multiple_of(h * t + b * seg, seg), seg)
            r = _dot_nt(c0_ref[b, h].astype(bf16), qbs[h])
            acc_s[h] = jnp.where(tok_seg == b, r, acc_s[h])
            nt_s[rows, :] = jnp.broadcast_to(n0_ref[b, h:h + 1, :], (seg, A_DK))
            return carry

        acc_s[h] = jnp.zeros((A_DV, t), f32)
        lax.fori_loop(0, MLSTM_SB, inter_body, 0, unroll=MLSTM_UNROLL)
    n_tok = nt_s[...]
    num = num + _wide(w_inter, A_DV // LANES) * stack(acc_s[h].T for h in heads)
    den = _row_sum(s) + w_inter * _row_sum(stack(qs) * n_tok)
    inv = 1.0 / jnp.maximum(jnp.abs(den), jnp.exp(-m_t))
    norm = inv * lax.rsqrt(inv * inv * (_row_sum(num * num) * (1.0 / A_DV)) + NORM_EPS)
    out = _sigmoid(stack(c_[3] for c_ in cols)) * (num * _wide(norm, A_DV // LANES))
    for h in heads:
        hs_ref[:, h * A_DV:(h + 1) * A_DV] = (head(out, h) * gh_ref[h:h + 1, :]).astype(bf16)

    last = _mask_dot(seg_last, lanes([head(x, h) for x in (m_t, btc, inter) for h in heads]))
    pick = lambda i: stack(last[:, (i * A_HEADS + h) * LANES:(i * A_HEADS + h + 1) * LANES] for h in heads)
    m_new, bt_last, inter_last = pick(0), pick(1), pick(2)
    decay = jnp.exp(bt_last - btc + itc - m_new)
    scale = jnp.exp(inter_last - m_new)
    seg_sum = _mask_dot(same, lanes(head(decay, h) * ks[h] for h in heads))
    n_new = scale * n_tok + stack(seg_sum[:, h * A_DK:(h + 1) * A_DK] for h in heads)
    mt_all = jnp.zeros((t, LANES), f32)
    sc_all = jnp.zeros((t, LANES), f32)
    for h in heads:
        vdt_ref[h] = (_wide(head(decay, h), A_DV // LANES) * vs[h]).T.astype(bf16)
        kb_ref[:, h * A_DK:(h + 1) * A_DK] = kbs[h]
        sc_all = jnp.where(lane == h, head(scale, h), sc_all)
        mt_all = jnp.where(lane == h, head(m_t, h), mt_all)
    for b in range(MLSTM_SB):
        n_ref[b] = stack(head(n_new, h)[b * seg:b * seg + 1, :] for h in heads)
    mt_ref[...] = mt_all
    sc_ref[...] = sc_all


def _mlstm_sample_update(c_in_ref, c_out_ref, vdt_ref, kb_ref, sc_ref):
    seg = DEC_SEQ
    lane_seg = _idiv(lax.broadcasted_iota(jnp.int32, (1, MLSTM_ST), 1), seg)
    for h in range(A_HEADS):
        def upd_body(b, carry, h=h):
            rows = pl.ds(pl.multiple_of(b * seg, seg), seg)
            scale = sc_ref[rows, :][0:1, h:h + 1]
            onehot = jnp.where(lane_seg == b, 1.0, 0.0).astype(bf16)
            upd = _dot(vdt_ref[h] * onehot, kb_ref[:, h * A_DK:(h + 1) * A_DK])
            c_out_ref[b, h] = scale * c_in_ref[b, h] + upd
            return carry

        lax.fori_loop(0, MLSTM_SB, upd_body, 0, unroll=MLSTM_UNROLL)


def _mlstm_sample_first_kernel(main_ref, gc_ref, gr_ref, m0_ref, c0_ref, n0_ref, gh_ref,
                               hs_ref, n_ref, mt_ref, vdt_ref, kb_ref, sc_ref, acc_s, nt_s):
    _mlstm_sample_compute(main_ref, gc_ref, gr_ref, m0_ref, c0_ref, n0_ref, gh_ref,
                          hs_ref, n_ref, mt_ref, vdt_ref, kb_ref, sc_ref, acc_s, nt_s)


def _mlstm_sample_second_kernel(main_ref, gc_ref, gr_ref, m0_ref, c0_ref, n0_ref, gh_ref, vdt0_ref, kb0_ref, sc0_ref,
                                hs_ref, n_ref, mt_ref, c_ref, vdt_s, kb_s, sc_s, acc_s, nt_s):
    layer = pl.program_id(0)

    @pl.when(layer == 0)
    def _():
        _mlstm_sample_update(c0_ref, c_ref, vdt0_ref, kb0_ref, sc0_ref)

    @pl.when(layer == 1)
    def _():
        _mlstm_sample_compute(main_ref, gc_ref, gr_ref, m0_ref, c0_ref, n0_ref, gh_ref,
                              hs_ref, n_ref, mt_ref, vdt_s, kb_s, sc_s, acc_s, nt_s)
        _mlstm_sample_update(c0_ref, c_ref, vdt_s, kb_s, sc_s)


def _mlstm_sample_scratch():
    t = MLSTM_ST
    return [pltpu.VMEM((A_HEADS, A_DV, t), f32), pltpu.VMEM((A_HEADS * t, A_DK), f32)]


def _mlstm_sample_first(main, gates, gates_t, m0_tok, c_all, n0, g_head):
    t = MLSTM_ST
    return pl.pallas_call(
        _mlstm_sample_first_kernel,
        grid=(DEC_BATCH // MLSTM_SB,),
        in_specs=[pl.BlockSpec((t, A_MAIN), lambda i: (i, 0)),
                  pl.BlockSpec((t, LANES), lambda i: (i, 0)),
                  pl.BlockSpec((SUBLANES, t), lambda i: (0, i)),
                  pl.BlockSpec((t, LANES), lambda i: (i, 0)),
                  pl.BlockSpec((None, MLSTM_SB, A_HEADS, A_DV, A_DK), lambda i: (0, i, 0, 0, 0)),
                  pl.BlockSpec((MLSTM_SB, A_HEADS, A_DK), lambda i: (i, 0, 0)),
                  _const_spec((A_HEADS, A_DV))],
        out_specs=[pl.BlockSpec((t, A_HEADS * A_DV), lambda i: (i, 0)),
                   pl.BlockSpec((MLSTM_SB, A_HEADS, A_DK), lambda i: (i, 0, 0)),
                   pl.BlockSpec((t, LANES), lambda i: (i, 0)),
                   pl.BlockSpec((A_HEADS, A_DV, t), lambda i: (0, 0, i)),
                   pl.BlockSpec((t, A_QK), lambda i: (i, 0)),
                   pl.BlockSpec((t, LANES), lambda i: (i, 0))],
        out_shape=[jax.ShapeDtypeStruct((N_S, A_HEADS * A_DV), bf16),
                   jax.ShapeDtypeStruct((DEC_BATCH, A_HEADS, A_DK), f32),
                   jax.ShapeDtypeStruct((N_S, LANES), f32),
                   jax.ShapeDtypeStruct((A_HEADS, A_DV, N_S), bf16),
                   jax.ShapeDtypeStruct((N_S, A_QK), bf16),
                   jax.ShapeDtypeStruct((N_S, LANES), f32)],
        scratch_shapes=_mlstm_sample_scratch(),
        compiler_params=_params("parallel"),
        name="mlstm_sample_first",
    )(main, gates, gates_t, m0_tok, c_all, n0, g_head)


def _mlstm_sample_second(main, gates, gates_t, m0_tok, c_all, n0, g_head, vdt0, kb0, sc0):
    t = MLSTM_ST
    own = lambda l, i: i * l
    first = lambda l, i: i * (1 - l)
    return pl.pallas_call(
        _mlstm_sample_second_kernel,
        grid=(N_A, DEC_BATCH // MLSTM_SB),
        in_specs=[pl.BlockSpec((t, A_MAIN), lambda l, i: (own(l, i), 0)),
                  pl.BlockSpec((t, LANES), lambda l, i: (own(l, i), 0)),
                  pl.BlockSpec((SUBLANES, t), lambda l, i: (0, own(l, i))),
                  pl.BlockSpec((t, LANES), lambda l, i: (own(l, i), 0)),
                  pl.BlockSpec((None, MLSTM_SB, A_HEADS, A_DV, A_DK), lambda l, i: (l, i, 0, 0, 0)),
                  pl.BlockSpec((MLSTM_SB, A_HEADS, A_DK), lambda l, i: (own(l, i), 0, 0)),
                  _const_spec((A_HEADS, A_DV)),
                  pl.BlockSpec((A_HEADS, A_DV, t), lambda l, i: (0, 0, first(l, i))),
                  pl.BlockSpec((t, A_QK), lambda l, i: (first(l, i), 0)),
                  pl.BlockSpec((t, LANES), lambda l, i: (first(l, i), 0))],
        out_specs=[pl.BlockSpec((t, A_HEADS * A_DV), lambda l, i: (own(l, i), 0)),
                   pl.BlockSpec((MLSTM_SB, A_HEADS, A_DK), lambda l, i: (own(l, i), 0, 0)),
                   pl.BlockSpec((t, LANES), lambda l, i: (own(l, i), 0)),
                   pl.BlockSpec((None, MLSTM_SB, A_HEADS, A_DV, A_DK), lambda l, i: (l, i, 0, 0, 0))],
        out_shape=[jax.ShapeDtypeStruct((N_S, A_HEADS * A_DV), bf16),
                   jax.ShapeDtypeStruct((DEC_BATCH, A_HEADS, A_DK), f32),
                   jax.ShapeDtypeStruct((N_S, LANES), f32),
                   jax.ShapeDtypeStruct((N_A, DEC_BATCH, A_HEADS, A_DV, A_DK), f32)],
        scratch_shapes=[pltpu.VMEM((A_HEADS, A_DV, t), bf16),
                        pltpu.VMEM((t, A_QK), bf16),
                        pltpu.VMEM((t, LANES), f32)] + _mlstm_sample_scratch(),
        compiler_params=_params("arbitrary", "arbitrary"),
        name="mlstm_sample_second",
    )(main, gates, gates_t, m0_tok, c_all, n0, g_head, vdt0, kb0, sc0)


def _swa_softmax_pv(s, sink, vb):
    n = s.shape[1] // LANES
    m = jnp.maximum(jnp.broadcast_to(jnp.max(s, axis=1, keepdims=True), sink.shape), sink)
    p = jnp.exp(s - _wide(m, n))
    den = _row_sum(p) + jnp.exp(sink - m)
    return _dot((p * _wide(1.0 / den, n)).astype(bf16), vb)


def _swa_window_bias(has_prev):
    w = WINDOW
    t = lax.broadcasted_iota(jnp.int32, (w, 2 * w), 0)
    jj = lax.broadcasted_iota(jnp.int32, (w, 2 * w), 1)
    bias = jnp.where((jj > t) & (jj <= t + w) & (has_prev | (jj >= w)), 0.0, -jnp.inf)
    return jnp.concatenate([bias] * B_GROUP, axis=0)


def _swa_window_block(q_groups, kb, vb, sink_ref, bias):
    w = WINDOW
    lane_head = _idiv(lax.broadcasted_iota(jnp.int32, (1, B_KV), 1), B_HD)
    outs = [jnp.zeros((w, B_KV), f32) for _ in range(B_GROUP)]
    for h in range(B_KV_HEADS):
        hm = lane_head == h
        qscale = jnp.where(hm, B_HD ** -0.5, 0.0)
        qh = jnp.concatenate([q * qscale for q in q_groups], axis=0).astype(bf16)
        sink = jnp.concatenate([jnp.full((w, LANES), sink_ref[h * B_GROUP + g], f32) for g in range(B_GROUP)], axis=0)
        r = _swa_softmax_pv(_dot_nt(qh, kb) + bias, sink, vb)
        for g in range(B_GROUP):
            outs[g] = jnp.where(hm, r[g * w:(g + 1) * w, :], outs[g])
    return outs


def _swa_prompt_kernel(sink_ref, q_ref, ko_ref, vo_ref, kp_ref, vp_ref, a_ref):
    w = WINDOW
    bias_first = _swa_window_bias(pl.program_id(1) > 0)
    bias_rest = _swa_window_bias(True)
    for nb in range(SWA_TILE // w):
        r0 = nb * w
        if nb == 0:
            k_prev, v_prev, bias = kp_ref[...], vp_ref[...], bias_first
        else:
            k_prev, v_prev, bias = ko_ref[r0 - w:r0, :], vo_ref[r0 - w:r0, :], bias_rest
        kb = jnp.concatenate([k_prev, ko_ref[r0:r0 + w, :]], axis=0).astype(bf16)
        vb = jnp.concatenate([v_prev, vo_ref[r0:r0 + w, :]], axis=0).astype(bf16)
        outs = _swa_window_block([q_ref[r0:r0 + w, g * B_KV:(g + 1) * B_KV] for g in range(B_GROUP)],
                                 kb, vb, sink_ref, bias)
        for g in range(B_GROUP):
            a_ref[r0:r0 + w, g * B_KV:(g + 1) * B_KV] = outs[g].astype(bf16)


def _swa_prompt(proj, sinks):
    nt = SEQ // SWA_TILE
    w = WINDOW
    per = SWA_TILE // w
    kcol = B_Q // B_KV
    vcol = kcol + 1
    tile = lambda b, n: b * nt + n
    prev = lambda b, n: jnp.maximum(tile(b, n) * per - 1, 0)
    return pl.pallas_call(
        _swa_prompt_kernel,
        grid=(BATCH, nt),
        in_specs=[pl.BlockSpec(memory_space=pltpu.SMEM),
                  pl.BlockSpec((SWA_TILE, B_Q), lambda b, n: (tile(b, n), 0)),
                  pl.BlockSpec((SWA_TILE, B_KV), lambda b, n: (tile(b, n), kcol)),
                  pl.BlockSpec((SWA_TILE, B_KV), lambda b, n: (tile(b, n), vcol)),
                  pl.BlockSpec((w, B_KV), lambda b, n: (prev(b, n), kcol)),
                  pl.BlockSpec((w, B_KV), lambda b, n: (prev(b, n), vcol))],
        out_specs=pl.BlockSpec((SWA_TILE, B_Q), lambda b, n: (tile(b, n), 0)),
        out_shape=jax.ShapeDtypeStruct((N_P, B_Q), bf16),
        compiler_params=_params("parallel", "arbitrary"),
        name="swa_prompt",
    )(sinks, proj, proj, proj, proj, proj)


def _swa_sample_kernel(sink_ref, q_ref, kn_ref, vn_ref, kc_ref, vc_ref, o_ref, ko_ref, vo_ref):
    s_len = DEC_SEQ
    buf = WINDOW
    rows = B_HEADS * s_len
    keys = 2 * buf
    ri = lax.broadcasted_iota(jnp.int32, (rows, keys), 0)
    jj = lax.broadcasted_iota(jnp.int32, (rows, keys), 1)
    t = _imod(ri, s_len)
    bias = jnp.where(((jj < buf) & (jj > t)) | ((jj >= buf) & (jj - buf <= t)), 0.0, -jnp.inf)
    lane_head = _idiv(lax.broadcasted_iota(jnp.int32, (1, B_KV), 1), B_HD)
    qscale = [jnp.where(lane_head == h, B_HD ** -0.5, 0.0) for h in range(B_KV_HEADS)]
    sink = jnp.concatenate([jnp.full((s_len, LANES), sink_ref[i], f32) for i in range(B_HEADS)], axis=0)
    pad = jnp.zeros((keys - buf - s_len, B_KV), f32)

    def body(e, carry):
        rs = pl.ds(pl.multiple_of(e * s_len, s_len), s_len)
        kn = kn_ref[rs, :]
        vn = vn_ref[rs, :]
        kc = kc_ref[e]
        vc = vc_ref[e]
        kpad = jnp.concatenate([kc, kn, pad], axis=0).astype(bf16)
        vpad = jnp.concatenate([vc, vn, pad], axis=0).astype(bf16)
        ko_ref[e, 0:buf - s_len, :] = kc[s_len:, :]
        ko_ref[e, buf - s_len:, :] = kn
        vo_ref[e, 0:buf - s_len, :] = vc[s_len:, :]
        vo_ref[e, buf - s_len:, :] = vn
        qe = q_ref[rs, :]
        qbig = jnp.concatenate([qe[:, g * B_KV:(g + 1) * B_KV] * qscale[h]
                                for h in range(B_KV_HEADS) for g in range(B_GROUP)], axis=0).astype(bf16)
        r = _swa_softmax_pv(_dot_nt(qbig, kpad) + bias, sink, vpad)
        for g in range(B_GROUP):
            og = jnp.zeros((s_len, B_KV), f32)
            for h in range(B_KV_HEADS):
                blk = (h * B_GROUP + g) * s_len
                og = jnp.where(lane_head == h, r[blk:blk + s_len, :], og)
            o_ref[rs, g * B_KV:(g + 1) * B_KV] = og.astype(bf16)
        return carry

    lax.fori_loop(0, SWA_SB, body, 0, unroll=True)


def _swa_sample(proj, sinks, k_cache, v_cache):
    t = SWA_SB * DEC_SEQ
    kcol = B_Q // B_KV
    return pl.pallas_call(
        _swa_sample_kernel,
        grid=(DEC_BATCH // SWA_SB,),
        in_specs=[pl.BlockSpec(memory_space=pltpu.SMEM),
                  pl.BlockSpec((t, B_Q), lambda i: (i, 0)),
                  pl.BlockSpec((t, B_KV), lambda i: (i, kcol)),
                  pl.BlockSpec((t, B_KV), lambda i: (i, kcol + 1)),
                  pl.BlockSpec((SWA_SB, WINDOW, B_KV), lambda i: (i, 0, 0)),
                  pl.BlockSpec((SWA_SB, WINDOW, B_KV), lambda i: (i, 0, 0))],
        out_specs=[pl.BlockSpec((t, B_Q), lambda i: (i, 0)),
                   pl.BlockSpec((SWA_SB, WINDOW, B_KV), lambda i: (i, 0, 0)),
                   pl.BlockSpec((SWA_SB, WINDOW, B_KV), lambda i: (i, 0, 0))],
        out_shape=[jax.ShapeDtypeStruct((N_S, B_Q), bf16),
                   jax.ShapeDtypeStruct((DEC_BATCH, WINDOW, B_KV), f32),
                   jax.ShapeDtypeStruct((DEC_BATCH, WINDOW, B_KV), f32)],
        compiler_params=_params("parallel"),
        name="swa_sample",
    )(sinks, proj, proj, proj, k_cache, v_cache)


def _rg_conv_group(x8, p8, wc_ref, bc_ref):
    row = lax.broadcasted_iota(jnp.int32, x8.shape, 0)
    u = bc_ref[...] + wc_ref[CONV_W - 1] * x8
    for d in range(1, CONV_W):
        sh = pltpu.roll(jnp.where(row >= SUBLANES - d, p8, x8), d, 0)
        u = u + wc_ref[CONV_W - 1 - d] * sh
    return u


def _rg_scan_group(a8, b8, h_prev8):
    first = lax.broadcasted_iota(jnp.int32, a8.shape, 0) == 0
    b = b8 + jnp.where(first, a8 * pltpu.roll(h_prev8, 1, 0), 0.0)
    a = jnp.where(first, 0.0, a8)
    for d in (1, 2, 4):
        b = a * pltpu.roll(b, d, 0) + b
        if d < SUBLANES // 2:
            a = a * pltpu.roll(a, d, 0)
    return b


def _rg_gate_rows(ra, rx, neg_rate, u):
    z = _sigmoid(ra) * neg_rate
    a = jnp.exp(-z)
    return a, _sqrt_nonneg(jnp.tanh(z) * (a * a + 1.0)) * _sigmoid(rx) * u


def _rg_gates(u, gate, wa_ref, ba_ref, wx_ref, bx_ref, lam_ref):
    ub = u.astype(bf16)
    ra = jnp.concatenate([_dot(ub[:, n * C_BW:(n + 1) * C_BW], wa_ref[n]) for n in range(C_BLOCKS)], axis=1)
    rx = jnp.concatenate([_dot(ub[:, n * C_BW:(n + 1) * C_BW], wx_ref[n]) for n in range(C_BLOCKS)], axis=1)
    a, bterm = _rg_gate_rows(ra + ba_ref[0:1, :], rx + bx_ref[0:1, :], LRU_C * _softplus(-lam_ref[0:1, :]), u)
    return a, bterm, _gelu_tanh(gate)


def _rg_prompt_kernel(p_ref, wc_ref, bc_ref, wa_ref, ba_ref, wx_ref, bx_ref, lam_ref,
                      y_ref, h_ref, u_s, a_s, b_s, xc_s, hc_s):
    j = pl.program_id(1)
    ng = RG_T // SUBLANES
    out_rows = 2 * SUBLANES

    @pl.when(j == 0)
    def _():
        xc_s[...] = jnp.zeros_like(xc_s)
        hc_s[...] = jnp.zeros_like(hc_s)

    def group(gidx, n=SUBLANES):
        return pl.ds(pl.multiple_of(gidx * n, n), n)

    def conv_body(gidx, p8):
        x8 = p_ref[group(gidx), 0:C_WIDTH]
        u_s[group(gidx), :] = _rg_conv_group(x8, p8, wc_ref, bc_ref)
        return x8

    xc_s[...] = lax.fori_loop(0, ng, conv_body, xc_s[...], unroll=2)
    ub = u_s[...].astype(bf16)
    for n in range(C_BLOCKS):
        cols = slice(n * C_BW, (n + 1) * C_BW)
        a_s[:, cols] = _dot(ub[:, cols], wa_ref[n])
        b_s[:, cols] = _dot(ub[:, cols], wx_ref[n])
    neg_rate = LRU_C * _softplus(-lam_ref[...])

    def gate_body(gidx, carry):
        rows = group(gidx)
        a, bterm = _rg_gate_rows(a_s[rows, :] + ba_ref[...], b_s[rows, :] + bx_ref[...], neg_rate, u_s[rows, :])
        a_s[rows, :] = a
        b_s[rows, :] = bterm
        return carry

    lax.fori_loop(0, ng, gate_body, 0, unroll=4)

    def scan_body(gidx, carry):
        rows = group(gidx, out_rows)
        lo = pl.ds(pl.multiple_of(gidx * out_rows, out_rows), SUBLANES)
        hi = pl.ds(pl.multiple_of(gidx * out_rows + SUBLANES, SUBLANES), SUBLANES)
        h_lo = _rg_scan_group(a_s[lo, :], b_s[lo, :], carry)
        h_hi = _rg_scan_group(a_s[hi, :], b_s[hi, :], h_lo)
        h16 = jnp.concatenate([h_lo, h_hi], axis=0)
        y_ref[rows, :] = (h16 * _gelu_tanh(p_ref[rows, C_WIDTH:])).astype(bf16)
        return h_hi

    h_last = lax.fori_loop(0, RG_T // out_rows, scan_body, hc_s[...], unroll=2)
    hc_s[...] = h_last

    @pl.when(j == pl.num_programs(1) - 1)
    def _():
        h_ref[0] = h_last[SUBLANES - 1:SUBLANES, :]


def _rg_weight_specs():
    rows = (SUBLANES, C_WIDTH)
    return [_const_spec((CONV_W,) + rows), _const_spec(rows),
            _const_spec((C_BLOCKS, C_BW, C_BW)), _const_spec(rows),
            _const_spec((C_BLOCKS, C_BW, C_BW)), _const_spec(rows),
            _const_spec(rows)]


def _rg_prompt(proj, weights):
    nt = SEQ // RG_T
    return pl.pallas_call(
        _rg_prompt_kernel,
        grid=(BATCH, nt),
        in_specs=[pl.BlockSpec((RG_T, 2 * C_WIDTH), lambda b, j: (b * nt + j, 0))] + _rg_weight_specs(),
        out_specs=[pl.BlockSpec((RG_T, C_WIDTH), lambda b, j: (b * nt + j, 0)),
                   pl.BlockSpec((1, 1, C_WIDTH), lambda b, j: (b, 0, 0))],
        out_shape=[jax.ShapeDtypeStruct((N_P, C_WIDTH), bf16),
                   jax.ShapeDtypeStruct((BATCH, 1, C_WIDTH), f32)],
        scratch_shapes=[pltpu.VMEM((RG_T, C_WIDTH), f32),
                        pltpu.VMEM((RG_T, C_WIDTH), f32),
                        pltpu.VMEM((RG_T, C_WIDTH), f32),
                        pltpu.VMEM((SUBLANES, C_WIDTH), f32),
                        pltpu.VMEM((SUBLANES, C_WIDTH), f32)],
        compiler_params=_params("parallel", "arbitrary"),
        name="rg_prompt",
    )(proj, *weights)


def _rg_sample_kernel(p_ref, cv_ref, h0_ref, wc_ref, bc_ref, wa_ref, ba_ref, wx_ref, bx_ref, lam_ref,
                      y_ref, h_ref, u_s, a_s, b_s):
    def conv_body(gidx, carry):
        rows = pl.ds(pl.multiple_of(gidx * SUBLANES, SUBLANES), SUBLANES)
        u_s[rows, :] = _rg_conv_group(p_ref[rows, 0:C_WIDTH], cv_ref[rows, :], wc_ref, bc_ref)
        return carry

    lax.fori_loop(0, DEC_BATCH, conv_body, 0)
    a, bterm, gg = _rg_gates(u_s[...], p_ref[:, C_WIDTH:], wa_ref, ba_ref, wx_ref, bx_ref, lam_ref)
    a_s[...] = a
    b_s[...] = bterm

    def scan_body(gidx, carry):
        rows = pl.ds(pl.multiple_of(gidx * SUBLANES, SUBLANES), SUBLANES)
        h0 = jnp.broadcast_to(h0_ref[pl.ds(gidx, 1), :], (SUBLANES, C_WIDTH))
        h8 = _rg_scan_group(a_s[rows, :], b_s[rows, :], h0)
        u_s[rows, :] = h8
        h_ref[pl.ds(gidx, 1), :] = h8[SUBLANES - 1:SUBLANES, :]
        return carry

    lax.fori_loop(0, DEC_BATCH, scan_body, 0)
    y_ref[...] = (u_s[...] * gg).astype(bf16)


def _rg_sample(proj, conv_pad, h0, weights):
    return pl.pallas_call(
        _rg_sample_kernel,
        grid=(1,),
        in_specs=[_const_spec((N_S, 2 * C_WIDTH)),
                  _const_spec((N_S, C_WIDTH)),
                  _const_spec((DEC_BATCH, C_WIDTH))] + _rg_weight_specs(),
        out_specs=[_const_spec((N_S, C_WIDTH)), _const_spec((DEC_BATCH, C_WIDTH))],
        out_shape=[jax.ShapeDtypeStruct((N_S, C_WIDTH), bf16),
                   jax.ShapeDtypeStruct((DEC_BATCH, C_WIDTH), f32)],
        scratch_shapes=[pltpu.VMEM((N_S, C_WIDTH), f32),
                        pltpu.VMEM((N_S, C_WIDTH), f32),
                        pltpu.VMEM((N_S, C_WIDTH), f32)],
        compiler_params=_params("arbitrary"),
        name="rg_sample",
    )(proj, conv_pad, h0, *weights)


def _mlp_tile(x_ref, a_ref, o_ref, wo_ref, bo_ref, g_ref, wup_ref, wdn_ref, gf_ref, x1_s, xn_s, final):
    x1_s[...] = x_ref[...] + _dot(a_ref[...], wo_ref[...]) + bo_ref[...]
    xn_s[...] = _rms_bf16(x1_s[...], g_ref[...])
    for c in range(0, D_FF, FF_CHUNK):
        hmid = jnp.maximum(_dot(xn_s[...], wup_ref[:, c:c + FF_CHUNK]), 0.0)
        x1_s[...] += _dot((hmid * hmid).astype(bf16), wdn_ref[c:c + FF_CHUNK, :])
    if final:
        x1 = x1_s[...]
        y = x1 * lax.rsqrt(jnp.mean(x1 * x1, axis=-1, keepdims=True) + NORM_EPS)
        o_ref[...] = y * gf_ref[...]
    else:
        o_ref[...] = x1_s[...]


def _out_mlp_kernel(xp_ref, xs_ref, ap_ref, as_ref, wo_ref, bo_ref, g_ref, wup_ref, wdn_ref, gf_ref, *rest,
                    final, cast_next):
    if cast_next:
        nup_ref, ndn_ref, op_ref, os_ref, nup_out, ndn_out, x1_s, xn_s = rest
    else:
        op_ref, os_ref, x1_s, xn_s = rest

    def body(x_ref, a_ref, o_ref):
        _mlp_tile(x_ref, a_ref, o_ref, wo_ref, bo_ref, g_ref, wup_ref, wdn_ref, gf_ref, x1_s, xn_s, final)

    i = pl.program_id(0)

    @pl.when(i < NP_TILES)
    def _():
        body(xp_ref, ap_ref, op_ref)
        if cast_next:
            nup_out[...] = nup_ref[...].astype(bf16)
            ndn_out[...] = ndn_ref[...].astype(bf16)

    @pl.when(i >= NP_TILES)
    def _():
        body(xs_ref, as_ref, os_ref)


def _out_mlp(x_p, x_s, a_p, a_s, wo, bo, g, w_up, w_down, g_final, final, next_weights=None):
    single = pl.Buffered(1)
    wspec = lambda shape: pl.BlockSpec(shape, lambda i: (0, 0), pipeline_mode=single)
    in_specs = [_p_spec(D_MODEL), _s_spec(D_MODEL), _p_spec(D_MODEL), _s_spec(D_MODEL),
                wspec((D_MODEL, D_MODEL)), wspec((1, D_MODEL)), wspec((1, D_MODEL)),
                wspec((D_MODEL, D_FF)), wspec((D_FF, D_MODEL)), wspec((1, D_MODEL))]
    out_specs = [_p_spec(D_MODEL), _s_spec(D_MODEL)]
    out_shape = [jax.ShapeDtypeStruct((N_P, D_MODEL), f32), jax.ShapeDtypeStruct((N_S, D_MODEL), f32)]
    args = [x_p, x_s, a_p, a_s, wo, bo, g, w_up, w_down, g_final]
    if next_weights is not None:
        extra = _mlp_weight_cast_specs(next_weights)
        in_specs, out_specs, out_shape, args = (in_specs + extra[0], out_specs + extra[1], out_shape + extra[2],
                                                args + extra[3])
    return pl.pallas_call(
        functools.partial(_out_mlp_kernel, final=final, cast_next=next_weights is not None),
        grid=(NP_TILES + NS_TILES,),
        in_specs=in_specs,
        out_specs=out_specs,
        out_shape=out_shape,
        scratch_shapes=[pltpu.VMEM((TM, D_MODEL), f32),
                        pltpu.VMEM((TM, D_MODEL), bf16)],
        compiler_params=_params("arbitrary", fuse_inputs=(len(args), (4,))),
        name="out_mlp",
    )(*args)


def _rope_tables():
    half = ROPE_DIM // 2
    inv = np.float32(ROPE_THETA) ** (-np.arange(0, ROPE_DIM, 2, dtype=np.float32) / np.float32(ROPE_DIM))
    lane = np.arange(LANES) % B_HD
    inv_lane = np.where(lane < ROPE_DIM, inv[lane % half], np.float32(0.0)).astype(np.float32)
    pos = np.concatenate([np.arange(SEQ), PAST_LEN + np.arange(TM) % DEC_SEQ]).astype(np.float32)
    ang = pos[:, None] * inv_lane[None, :]
    tab = np.concatenate([np.cos(ang), np.sin(ang)], axis=1).astype(np.float32)
    consts = np.zeros((SUBLANES, LANES), np.float32)
    consts[0] = np.where(lane < half, -1.0, 0.0)
    consts[1] = np.where((lane >= half) & (lane < ROPE_DIM), 1.0, 0.0)
    return jnp.asarray(tab), jnp.asarray(consts)


def _q_cols_to_group_major(w):
    lead = w.shape[:-1]
    return jnp.swapaxes(w.reshape(lead + (B_KV_HEADS, B_GROUP, B_HD)), -3, -2).reshape(lead + (B_Q,))


def _last_rows(arr, n_seq, seq_len, n_rows, col0, col1):
    return jnp.stack([arr[(s + 1) * seq_len - n_rows:(s + 1) * seq_len, col0:col1] for s in range(n_seq)])


def kernel(x_prompt, x_sample, state_mlstm_c, state_mlstm_n, state_mlstm_m, cache_swa_k, cache_swa_v,
           state_rglru_h, state_rglru_conv, norm_mix, norm_mlp, norm_final, w_mlp_up, w_mlp_down,
           w_mlstm_in, b_mlstm_i, b_mlstm_f, g_mlstm_head, w_mlstm_out, w_swa_qkv, b_swa_qkv, swa_sinks,
           w_swa_out, b_swa_out, w_rg_in, w_rg_conv, b_rg_conv, w_rg_a, b_rg_a, w_rg_x, b_rg_x, rg_lambda,
           w_rg_out):
    assert N_A == 2
    x_p = x_prompt.reshape(N_P, D_MODEL)
    x_s = x_sample.reshape(N_S, D_MODEL)
    zero_bias = jnp.zeros((1, D_MODEL), f32)
    row = lambda v: v.reshape(1, -1).astype(f32)
    w_up = w_down = None
    state_c = state_mlstm_c.astype(f32)
    n_gate = 2 * A_HEADS
    assert n_gate == SUBLANES
    w_mlstm_main = w_mlstm_in[:, :, :A_MAIN].astype(bf16)
    w_gate_cols = w_mlstm_in[:, :, A_MAIN:]
    w_mlstm_gate = jnp.pad(w_gate_cols, ((0, 0), (0, 0), (0, LANES - n_gate))).astype(bf16)
    outs = {k: [] for k in ("c_p", "n_p", "m_p", "n_s", "m_s", "k_p", "v_p", "k_s", "v_s",
                            "h_p", "cv_p", "h_s", "cv_s")}
    deferred = None
    c_s_all = None
    for i in range(DEPTH):
        kind, j = i % N_MIXERS, i // N_MIXERS
        g_mix = row(norm_mix[i])
        if kind == 0:
            b_gate = jnp.concatenate([b_mlstm_i[j], b_mlstm_f[j]]).astype(f32)
            res = _proj_mlstm(x_p, x_s, g_mix, w_mlstm_main, w_mlstm_gate,
                              jnp.pad(b_gate, (0, LANES - n_gate)).reshape(1, LANES), j,
                              next_weights=(w_mlp_up, w_mlp_down, 0) if i == 0 else None)
            main_p, main_s, gates_p, gates_s, gates_p_t, gates_s_t = res[:6]
            if i == 0:
                w_up, w_down = res[6], res[7]
            g_head = g_mlstm_head[j].astype(f32)
            a_p, c_p, n_p, mt_p = _mlstm_prompt(main_p.reshape(BATCH, SEQ, A_MAIN),
                                                gates_p.reshape(BATCH, SEQ, LANES), gates_p_t, g_head)
            a_p = a_p.reshape(N_P, A_HEADS * A_DV)
            mt_p = mt_p.reshape(N_P, LANES)
            m0_tok = jnp.pad(jnp.repeat(state_mlstm_m[j].astype(f32), DEC_SEQ, axis=0),
                             ((0, 0), (0, LANES - A_HEADS)))
            n0 = state_mlstm_n[j].astype(f32)
            if j == 0:
                a_s, n_s, mt_s, vdt0, kb0, sc0 = _mlstm_sample_first(main_s, gates_s, gates_s_t, m0_tok, state_c, n0,
                                                                     g_head)
                deferred = (vdt0, kb0, sc0)
            else:
                a_s, n_s, mt_s, c_s_all = _mlstm_sample_second(main_s, gates_s, gates_s_t, m0_tok, state_c, n0,
                                                               g_head, *deferred)
            outs["c_p"].append(c_p); outs["n_p"].append(n_p)
            outs["m_p"].append(_last_rows(mt_p, BATCH, SEQ, 1, 0, A_HEADS).reshape(BATCH, A_HEADS))
            outs["n_s"].append(n_s)
            outs["m_s"].append(mt_s[DEC_SEQ - 1::DEC_SEQ, :A_HEADS])
            wo, bo = w_mlstm_out[j], zero_bias
        elif kind == 1:
            rope_tab, rope_consts = _rope_tables()
            w_qkv, b_qkv = w_swa_qkv[j], b_swa_qkv[j]
            w_qkv = jnp.concatenate([_q_cols_to_group_major(w_qkv[:, :B_Q]), w_qkv[:, B_Q:]], axis=1)
            b_qkv = jnp.concatenate([_q_cols_to_group_major(b_qkv[:B_Q]), b_qkv[B_Q:]])
            proj_p, proj_s = _proj_swa(x_p, x_s, g_mix, w_qkv.astype(bf16), row(b_qkv), rope_tab, rope_consts)
            sinks = swa_sinks[j].astype(f32)
            a_p = _swa_prompt(proj_p, sinks)
            buf = cache_swa_k.shape[2]
            a_s, k_s, v_s = _swa_sample(proj_s, sinks,
                                        cache_swa_k[j].astype(f32).reshape(DEC_BATCH, buf, B_KV),
                                        cache_swa_v[j].astype(f32).reshape(DEC_BATCH, buf, B_KV))
            outs["k_p"].append(_last_rows(proj_p, BATCH, SEQ, WINDOW, B_Q, B_Q + B_KV)
                               .reshape(BATCH, WINDOW, B_KV_HEADS, B_HD))
            outs["v_p"].append(_last_rows(proj_p, BATCH, SEQ, WINDOW, B_Q + B_KV, B_IN)
                               .reshape(BATCH, WINDOW, B_KV_HEADS, B_HD))
            outs["k_s"].append(k_s.reshape(DEC_BATCH, buf, B_KV_HEADS, B_HD))
            outs["v_s"].append(v_s.reshape(DEC_BATCH, buf, B_KV_HEADS, B_HD))
            wo = jnp.swapaxes(w_swa_out[j].reshape(B_KV_HEADS, B_GROUP, B_HD, D_MODEL), 0, 1).reshape(B_Q, D_MODEL)
            bo = row(b_swa_out[j])
        else:
            proj_p, proj_s = _proj_rg(x_p, x_s, g_mix, w_rg_in[j].astype(bf16))
            rows8 = lambda v: jnp.broadcast_to(v.astype(f32)[..., None, :], v.shape[:-1] + (SUBLANES, C_WIDTH))
            weights = (rows8(w_rg_conv[j]), rows8(b_rg_conv[j]), w_rg_a[j].astype(bf16), rows8(b_rg_a[j]),
                       w_rg_x[j].astype(bf16), rows8(b_rg_x[j]), rows8(rg_lambda[j]))
            a_p, h_p = _rg_prompt(proj_p, weights)
            conv_pad = jnp.pad(state_rglru_conv[j].astype(f32),
                               ((0, 0), (SUBLANES - (CONV_W - 1), 0), (0, 0))).reshape(N_S, C_WIDTH)
            a_s, h_s = _rg_sample(proj_s, conv_pad, state_rglru_h[j].astype(f32), weights)
            outs["h_p"].append(h_p.reshape(BATCH, C_WIDTH))
            outs["cv_p"].append(_last_rows(proj_p, BATCH, SEQ, CONV_W - 1, 0, C_WIDTH))
            outs["h_s"].append(h_s)
            outs["cv_s"].append(proj_s[:, :C_WIDTH].reshape(DEC_BATCH, DEC_SEQ, C_WIDTH)[:, DEC_SEQ - (CONV_W - 1):])
            wo, bo = w_rg_out[j], zero_bias
        last = i == DEPTH - 1
        res = _out_mlp(x_p, x_s, a_p, a_s, wo.astype(bf16), bo, row(norm_mlp[i]), w_up, w_down, row(norm_final),
                       final=last, next_weights=None if last else (w_mlp_up, w_mlp_down, i + 1))
        x_p, x_s = res[0], res[1]
        if not last:
            w_up, w_down = res[2], res[3]
    st = {k: jnp.stack(v) for k, v in outs.items()}
    y_p = x_p.reshape(BATCH, SEQ, D_MODEL)
    y_s = x_s.reshape(DEC_BATCH, DEC_SEQ, D_MODEL)
    return (y_p, y_s, st["c_p"], st["n_p"], st["m_p"], c_s_all, st["n_s"], st["m_s"],
            st["k_p"], st["v_p"], st["k_s"], st["v_s"], st["h_p"], st["cv_p"], st["h_s"], st["cv_s"])
```
